```python
import math
import jax, jax.numpy as jnp
from jax import lax
import numpy as np

D_MODEL = 2048
BATCH = 8
SEQ = 8192
DEPTH = 1

N_META = 16
ATTN_HEADS = 8
ATTN_HEAD_DIM = 128
D_ATTN = ATTN_HEADS * ATTN_HEAD_DIM
D_SSM = D_MODEL // 2
SSM_GROUP = 16
SSM_GROUPS = D_SSM // SSM_GROUP
SSM_STATE = 64
D_FF = 5632
CONV_WIDTH = 3
Q_BLOCK = 128
EPS = 1e-6
IN_SPLITS = [D_ATTN, D_ATTN, D_ATTN, ATTN_HEADS, D_SSM, D_MODEL, D_MODEL]
N_IN = sum(IN_SPLITS)
IN_OFFSETS = [int(o) for o in np.cumsum(IN_SPLITS)[:-1]]

kernel_name = "hybrid_s5_forgetting_attn_convffn"


def rmsnorm(x, g):
    xf = x.astype(jnp.float32)
    y = xf * lax.rsqrt(jnp.mean(xf * xf, axis=-1, keepdims=True) + EPS)
    return (y * g.astype(jnp.float32)).astype(x.dtype)


def _fox_block(qb, Fq, qpos, k, v, Fk, kpos):
    s = jnp.einsum('bqhd,bkhd->bhqk', qb, k, preferred_element_type=jnp.float32) * (ATTN_HEAD_DIM ** -0.5)
    s = s + jnp.transpose(Fq, (0, 2, 1))[..., None] - jnp.transpose(Fk, (0, 2, 1))[:, :, None, :]
    mask = kpos[None, :] <= qpos[:, None]
    s = jnp.where(mask[None, None], s, -jnp.inf)
    p = jax.nn.softmax(s, axis=-1)
    return jnp.einsum('bhqk,bkhd->bqhd', p.astype(v.dtype), v)


def forgetting_attention(q, k, v, log_f):
    b, L, H, hd = q.shape
    F = jnp.cumsum(log_f, axis=1)
    pos = jnp.arange(L)
    out_meta = _fox_block(q[:, :N_META], F[:, :N_META], pos[:N_META],
                          k[:, :N_META], v[:, :N_META], F[:, :N_META], pos[:N_META])
    n_blk = (L - N_META) // Q_BLOCK
    qr = q[:, N_META:].reshape(b, n_blk, Q_BLOCK, H, hd).transpose(1, 0, 2, 3, 4)
    Fr = F[:, N_META:].reshape(b, n_blk, Q_BLOCK, H).transpose(1, 0, 2, 3)
    qpos = (N_META + jnp.arange(L - N_META)).reshape(n_blk, Q_BLOCK)
    out_r = lax.map(lambda a: _fox_block(a[0], a[1], a[2], k, v, F, pos), (qr, Fr, qpos))
    out_r = out_r.transpose(1, 0, 2, 3, 4).reshape(b, L - N_META, H, hd)
    return jnp.concatenate([out_meta, out_r], axis=1)


def s5_ssm(u, lam_re, lam_im, log_dt, b_re, b_im, c_re, c_im, d_skip):
    bsz, L, _ = u.shape
    f32 = jnp.float32
    uf = u.astype(f32).reshape(bsz, L, SSM_GROUPS, SSM_GROUP)
    dt = jnp.exp(log_dt.astype(f32))[:, None]
    lr = lam_re.astype(f32)
    li = lam_im.astype(f32)
    mag = jnp.exp(lr * dt)
    a_re = mag * jnp.cos(li * dt)
    a_im = mag * jnp.sin(li * dt)
    den = lr * lr + li * li
    nr = a_re - 1.0
    z_re = (nr * lr + a_im * li) / den
    z_im = (a_im * lr - nr * li) / den
    br = b_re.astype(f32)
    bi = b_im.astype(f32)
    bb_re = z_re[..., None] * br - z_im[..., None] * bi
    bb_im = z_re[..., None] * bi + z_im[..., None] * br
    bu_re = jnp.einsum('gpc,blgc->blgp', bb_re, uf)
    bu_im = jnp.einsum('gpc,blgc->blgp', bb_im, uf)
    at_re = jnp.broadcast_to(a_re, (1, L, SSM_GROUPS, SSM_STATE))
    at_im = jnp.broadcast_to(a_im, (1, L, SSM_GROUPS, SSM_STATE))

    def combine(e1, e2):
        ar1, ai1, br1, bi1 = e1
        ar2, ai2, br2, bi2 = e2
        return (ar2 * ar1 - ai2 * ai1,
                ar2 * ai1 + ai2 * ar1,
                ar2 * br1 - ai2 * bi1 + br2,
                ar2 * bi1 + ai2 * br1 + bi2)

    _, _, h_re, h_im = lax.associative_scan(combine, (at_re, at_im, bu_re, bu_im), axis=1)
    y = (jnp.einsum('gcp,blgp->blgc', c_re.astype(f32), h_re)
         - jnp.einsum('gcp,blgp->blgc', c_im.astype(f32), h_im))
    y = y.reshape(bsz, L, D_SSM) + d_skip.astype(f32) * u.astype(f32)
    return y.astype(u.dtype)


def conv_ffn(x, w_up, conv_w, conv_b, w_down):
    gu = x @ w_up
    g, u = jnp.split(gu, 2, axis=-1)
    L = g.shape[1]
    gp = jnp.pad(g, ((0, 0), (CONV_WIDTH - 1, 0), (0, 0)))
    gc = conv_b + conv_w[0] * gp[:, 0:L]
    for j in range(1, CONV_WIDTH):
        gc = gc + conv_w[j] * gp[:, j:j + L]
    return (jax.nn.silu(gc) * u) @ w_down


def mixer(n, w_in, b_f, lam_re, lam_im, log_dt, b_re, b_im, c_re, c_im, d_skip, w_glu, w_attn_o, w_out):
    bsz, L, _ = n.shape
    z = n @ w_in
    q, k, v, f, u, ga, gb = jnp.split(z, IN_OFFSETS, axis=-1)
    q = q.reshape(bsz, L, ATTN_HEADS, ATTN_HEAD_DIM)
    k = k.reshape(bsz, L, ATTN_HEADS, ATTN_HEAD_DIM)
    v = v.reshape(bsz, L, ATTN_HEADS, ATTN_HEAD_DIM)
    log_f = jax.nn.log_sigmoid(f.astype(jnp.float32) + b_f.astype(jnp.float32))
    attn = forgetting_attention(q, k, v, log_f).reshape(bsz, L, D_ATTN) @ w_attn_o
    y = s5_ssm(u, lam_re, lam_im, log_dt, b_re, b_im, c_re, c_im, d_skip)
    ya, yb = jnp.split(jax.nn.gelu(y) @ w_glu, 2, axis=-1)
    ssm_out = ya * jax.nn.sigmoid(yb)
    merged = jax.nn.sigmoid(ga) * ssm_out + jax.nn.sigmoid(gb) * attn
    return merged @ w_out


def _fwd_setup_inputs(seed: int = 0) -> dict:
    key = jax.random.key(seed)
    ks = jax.random.split(key, 24)
    nrm = lambda k, s, sc: jax.random.normal(k, s, jnp.float32) * sc
    Dp = DEPTH
    n_idx = jnp.arange(SSM_STATE, dtype=jnp.float32)
    return {
        "x": nrm(ks[0], (BATCH, SEQ, D_MODEL), 1.0),
        "meta": nrm(ks[1], (N_META, D_MODEL), 1.0),
        "g_mix": 1.0 + nrm(ks[2], (Dp, D_MODEL), 0.01),
        "w_in": nrm(ks[3], (Dp, D_MODEL, N_IN), D_MODEL ** -0.5),
        "b_f": jax.random.uniform(ks[4], (Dp, ATTN_HEADS), jnp.float32, 1.0, 6.0),
        "lam_re": -0.5 + nrm(ks[5], (Dp, SSM_GROUPS, SSM_STATE), 0.01),
        "lam_im": math.pi * n_idx + nrm(ks[6], (Dp, SSM_GROUPS, SSM_STATE), 0.01),
        "log_dt": jax.random.uniform(ks[7], (Dp, SSM_GROUPS), jnp.float32, math.log(1e-3), math.log(1e-1)),
        "b_re": nrm(ks[8], (Dp, SSM_GROUPS, SSM_STATE, SSM_GROUP), (2 * SSM_GROUP) ** -0.5),
        "b_im": nrm(ks[9], (Dp, SSM_GROUPS, SSM_STATE, SSM_GROUP), (2 * SSM_GROUP) ** -0.5),
        "c_re": nrm(ks[10], (Dp, SSM_GROUPS, SSM_GROUP, SSM_STATE), (2 * SSM_STATE) ** -0.5),
        "c_im": nrm(ks[11], (Dp, SSM_GROUPS, SSM_GROUP, SSM_STATE), (2 * SSM_STATE) ** -0.5),
        "d_skip": nrm(ks[12], (Dp, D_SSM), 1.0),
        "w_glu": nrm(ks[13], (Dp, D_SSM, 2 * D_MODEL), D_SSM ** -0.5),
        "w_attn_o": nrm(ks[14], (Dp, D_ATTN, D_MODEL), D_ATTN ** -0.5),
        "w_out": nrm(ks[15], (Dp, D_MODEL, D_MODEL), D_MODEL ** -0.5),
        "g_ffn": 1.0 + nrm(ks[16], (Dp, D_MODEL), 0.01),
        "w_up": nrm(ks[17], (Dp, D_MODEL, 2 * D_FF), D_MODEL ** -0.5),
        "conv_w": nrm(ks[18], (Dp, CONV_WIDTH, D_FF), CONV_WIDTH ** -0.5),
        "conv_b": nrm(ks[19], (Dp, D_FF), 0.01),
        "w_down": nrm(ks[20], (Dp, D_FF, D_MODEL), D_FF ** -0.5),
        "g_final": 1.0 + nrm(ks[21], (D_MODEL,), 0.01),
    }


def _fwd_reference(x, meta, g_mix, w_in, b_f, lam_re, lam_im, log_dt, b_re, b_im, c_re, c_im, d_skip,
              w_glu, w_attn_o, w_out, g_ffn, w_up, conv_w, conv_b, w_down, g_final):
    bsz = x.shape[0]
    m = jnp.broadcast_to(meta.astype(x.dtype)[None], (bsz, N_META, D_MODEL))
    h = jnp.concatenate([m, x], axis=1)
    for l in range(DEPTH):
        n = rmsnorm(h, g_mix[l])
        h = h + mixer(n, w_in[l], b_f[l], lam_re[l], lam_im[l], log_dt[l], b_re[l], b_im[l],
                      c_re[l], c_im[l], d_skip[l], w_glu[l], w_attn_o[l], w_out[l])
        n2 = rmsnorm(h, g_ffn[l])
        h = h + conv_ffn(n2, w_up[l], conv_w[l], conv_b[l], w_down[l])
    return rmsnorm(h, g_final)[:, N_META:]


import jax as _jax
import jax.numpy as _jnp

TWIN_FORMAT = 'train_step'
FWD_PARAMS = ['x', 'meta', 'g_mix', 'w_in', 'b_f', 'lam_re', 'lam_im', 'log_dt', 'b_re', 'b_im', 'c_re', 'c_im', 'd_skip', 'w_glu', 'w_attn_o', 'w_out', 'g_ffn', 'w_up', 'conv_w', 'conv_b', 'w_down', 'g_final']
TWIN_WEIGHTS = ['meta', 'g_mix', 'w_in', 'b_f', 'lam_re', 'lam_im', 'log_dt', 'b_re', 'b_im', 'c_re', 'c_im', 'd_skip', 'w_glu', 'w_attn_o', 'w_out', 'g_ffn', 'w_up', 'conv_w', 'conv_b', 'w_down', 'g_final']
TWIN_DIFF_INPUT = 'x'
TWIN_INPUTS = ['x', 'meta', 'g_mix', 'w_in', 'b_f', 'lam_re', 'lam_im', 'log_dt', 'b_re', 'b_im', 'c_re', 'c_im', 'd_skip', 'w_glu', 'w_attn_o', 'w_out', 'g_ffn', 'w_up', 'conv_w', 'conv_b', 'w_down', 'g_final', 'loss_target', 'm_meta', 'm_g_mix', 'm_w_in', 'm_b_f', 'm_lam_re', 'm_lam_im', 'm_log_dt', 'm_b_re', 'm_b_im', 'm_c_re', 'm_c_im', 'm_d_skip', 'm_w_glu', 'm_w_attn_o', 'm_w_out', 'm_g_ffn', 'm_w_up', 'm_conv_w', 'm_conv_b', 'm_w_down', 'm_g_final', 'v_meta', 'v_g_mix', 'v_w_in', 'v_b_f', 'v_lam_re', 'v_lam_im', 'v_log_dt', 'v_b_re', 'v_b_im', 'v_c_re', 'v_c_im', 'v_d_skip', 'v_w_glu', 'v_w_attn_o', 'v_w_out', 'v_g_ffn', 'v_w_up', 'v_conv_w', 'v_conv_b', 'v_w_down', 'v_g_final']
TWIN_OUTPUTS = ['loss', 'grad_x', 'grad_meta', 'grad_g_mix', 'grad_w_in', 'grad_b_f', 'grad_lam_re', 'grad_lam_im', 'grad_log_dt', 'grad_b_re', 'grad_b_im', 'grad_c_re', 'grad_c_im', 'grad_d_skip', 'grad_w_glu', 'grad_w_attn_o', 'grad_w_out', 'grad_g_ffn', 'grad_w_up', 'grad_conv_w', 'grad_conv_b', 'grad_w_down', 'grad_g_final', 'delta_meta', 'delta_g_mix', 'delta_w_in', 'delta_b_f', 'delta_lam_re', 'delta_lam_im', 'delta_log_dt', 'delta_b_re', 'delta_b_im', 'delta_c_re', 'delta_c_im', 'delta_d_skip', 'delta_w_glu', 'delta_w_attn_o', 'delta_w_out', 'delta_g_ffn', 'delta_w_up', 'delta_conv_w', 'delta_conv_b', 'delta_w_down', 'delta_g_final', 'new_m_meta', 'new_m_g_mix', 'new_m_w_in', 'new_m_b_f', 'new_m_lam_re', 'new_m_lam_im', 'new_m_log_dt', 'new_m_b_re', 'new_m_b_im', 'new_m_c_re', 'new_m_c_im', 'new_m_d_skip', 'new_m_w_glu', 'new_m_w_attn_o', 'new_m_w_out', 'new_m_g_ffn', 'new_m_w_up', 'new_m_conv_w', 'new_m_conv_b', 'new_m_w_down', 'new_m_g_final', 'new_v_meta', 'new_v_g_mix', 'new_v_w_in', 'new_v_b_f', 'new_v_lam_re', 'new_v_lam_im', 'new_v_log_dt', 'new_v_b_re', 'new_v_b_im', 'new_v_c_re', 'new_v_c_im', 'new_v_d_skip', 'new_v_w_glu', 'new_v_w_attn_o', 'new_v_w_out', 'new_v_g_ffn', 'new_v_w_up', 'new_v_conv_w', 'new_v_conv_b', 'new_v_w_down', 'new_v_g_final']
TWIN_LEAF_KINDS = {'loss': 'loss', 'grad_x': 'grad_x', 'grad_meta': 'grad_w', 'grad_g_mix': 'grad_w', 'grad_w_in': 'grad_w', 'grad_b_f': 'grad_w', 'grad_lam_re': 'grad_w', 'grad_lam_im': 'grad_w', 'grad_log_dt': 'grad_w', 'grad_b_re': 'grad_w', 'grad_b_im': 'grad_w', 'grad_c_re': 'grad_w', 'grad_c_im': 'grad_w', 'grad_d_skip': 'grad_w', 'grad_w_glu': 'grad_w', 'grad_w_attn_o': 'grad_w', 'grad_w_out': 'grad_w', 'grad_g_ffn': 'grad_w', 'grad_w_up': 'grad_w', 'grad_conv_w': 'grad_w', 'grad_conv_b': 'grad_w', 'grad_w_down': 'grad_w', 'grad_g_final': 'grad_w', 'delta_meta': 'delta_w', 'delta_g_mix': 'delta_w', 'delta_w_in': 'delta_w', 'delta_b_f': 'delta_w', 'delta_lam_re': 'delta_w', 'delta_lam_im': 'delta_w', 'delta_log_dt': 'delta_w', 'delta_b_re': 'delta_w', 'delta_b_im': 'delta_w', 'delta_c_re': 'delta_w', 'delta_c_im': 'delta_w', 'delta_d_skip': 'delta_w', 'delta_w_glu': 'delta_w', 'delta_w_attn_o': 'delta_w', 'delta_w_out': 'delta_w', 'delta_g_ffn': 'delta_w', 'delta_w_up': 'delta_w', 'delta_conv_w': 'delta_w', 'delta_conv_b': 'delta_w', 'delta_w_down': 'delta_w', 'delta_g_final': 'delta_w', 'new_m_meta': 'new_m', 'new_m_g_mix': 'new_m', 'new_m_w_in': 'new_m', 'new_m_b_f': 'new_m', 'new_m_lam_re': 'new_m', 'new_m_lam_im': 'new_m', 'new_m_log_dt': 'new_m', 'new_m_b_re': 'new_m', 'new_m_b_im': 'new_m', 'new_m_c_re': 'new_m', 'new_m_c_im': 'new_m', 'new_m_d_skip': 'new_m', 'new_m_w_glu': 'new_m', 'new_m_w_attn_o': 'new_m', 'new_m_w_out': 'new_m', 'new_m_g_ffn': 'new_m', 'new_m_w_up': 'new_m', 'new_m_conv_w': 'new_m', 'new_m_conv_b': 'new_m', 'new_m_w_down': 'new_m', 'new_m_g_final': 'new_m', 'new_v_meta': 'new_v', 'new_v_g_mix': 'new_v', 'new_v_w_in': 'new_v', 'new_v_b_f': 'new_v', 'new_v_lam_re': 'new_v', 'new_v_lam_im': 'new_v', 'new_v_log_dt': 'new_v', 'new_v_b_re': 'new_v', 'new_v_b_im': 'new_v', 'new_v_c_re': 'new_v', 'new_v_c_im': 'new_v', 'new_v_d_skip': 'new_v', 'new_v_w_glu': 'new_v', 'new_v_w_attn_o': 'new_v', 'new_v_w_out': 'new_v', 'new_v_g_ffn': 'new_v', 'new_v_w_up': 'new_v', 'new_v_conv_w': 'new_v', 'new_v_conv_b': 'new_v', 'new_v_w_down': 'new_v', 'new_v_g_final': 'new_v'}


def _forward(args):
    return _fwd_reference(*[args[k] for k in FWD_PARAMS])


def _output_shape():
    def fwd():
        inp = _fwd_setup_inputs(0)
        return _fwd_reference(*[inp[k] for k in FWD_PARAMS])
    out = _jax.eval_shape(fwd)
    return out.shape, out.dtype

N_MICROBATCH = 1
ADAM_LR = 0.001
ADAM_B1 = 0.9
ADAM_B2 = 0.999
ADAM_EPS = 1e-08
ADAM_WD = 0.01
ADAM_STEP = 10
PER_EXAMPLE_BATCH_AXIS = {'x': 0, 'loss_target': 0}
SHARED_INPUTS = []
_WEIGHT_DTYPES = {'meta': _jnp.float32, 'g_mix': _jnp.float32, 'w_in': _jnp.float32, 'b_f': _jnp.float32, 'lam_re': _jnp.float32, 'lam_im': _jnp.float32, 'log_dt': _jnp.float32, 'b_re': _jnp.float32, 'b_im': _jnp.float32, 'c_re': _jnp.float32, 'c_im': _jnp.float32, 'd_skip': _jnp.float32, 'w_glu': _jnp.float32, 'w_attn_o': _jnp.float32, 'w_out': _jnp.float32, 'g_ffn': _jnp.float32, 'w_up': _jnp.float32, 'conv_w': _jnp.float32, 'conv_b': _jnp.float32, 'w_down': _jnp.float32, 'g_final': _jnp.float32}
MOMENT_SCALE = {'meta': 1.829319e-03, 'g_mix': 4.930366e-02, 'w_in': 2.448697e-02, 'b_f': 2.103682e-01, 'lam_re': 1.950166e-03, 'lam_im': 1.948988e-03, 'log_dt': 2.218124e+00, 'b_re': 1.265488e-03, 'b_im': 1.279299e-03, 'c_re': 2.508617e-03, 'c_im': 2.545774e-03, 'd_skip': 4.281286e-02, 'w_glu': 1.913791e-02, 'w_attn_o': 2.550028e-02, 'w_out': 3.591177e-02, 'g_ffn': 9.138251e-02, 'w_up': 3.887934e-02, 'conv_w': 4.028946e-02, 'conv_b': 3.851895e-02, 'w_down': 6.343897e-02, 'g_final': 3.194339e+01}


def _to_microbatches(a, axis):
    t = _jnp.moveaxis(a, axis, 0)
    t = t.reshape((N_MICROBATCH, t.shape[0] // N_MICROBATCH) + t.shape[1:])
    return _jnp.moveaxis(t, 1, axis + 1)


def setup_inputs(seed: int = 0) -> dict:
    inp = _fwd_setup_inputs(seed)
    key = _jax.random.fold_in(_jax.random.key(seed), 7919)
    shape, _ = _output_shape()
    out = dict(inp)
    out["loss_target"] = _jax.random.normal(_jax.random.fold_in(key, 0), shape, _jnp.float32)
    for i, name in enumerate(TWIN_WEIGHTS):
        w = inp[name].astype(_jnp.float32)
        if MOMENT_SCALE is None:
            s = _jnp.sqrt(_jnp.mean(_jnp.square(w)) + 1e-30)
        else:
            s = MOMENT_SCALE[name]
        km, kv = _jax.random.split(_jax.random.fold_in(key, i + 1))
        out[name] = w
        out["m_" + name] = s * _jax.random.normal(km, w.shape, _jnp.float32)
        out["v_" + name] = (s * s) * _jax.random.uniform(kv, w.shape, _jnp.float32, 0.5, 1.5)
    if N_MICROBATCH > 1:
        for name, axis in PER_EXAMPLE_BATCH_AXIS.items():
            out[name] = _to_microbatches(out[name], axis)
    return {'x': out['x'], 'meta': out['meta'], 'g_mix': out['g_mix'], 'w_in': out['w_in'], 'b_f': out['b_f'], 'lam_re': out['lam_re'], 'lam_im': out['lam_im'], 'log_dt': out['log_dt'], 'b_re': out['b_re'], 'b_im': out['b_im'], 'c_re': out['c_re'], 'c_im': out['c_im'], 'd_skip': out['d_skip'], 'w_glu': out['w_glu'], 'w_attn_o': out['w_attn_o'], 'w_out': out['w_out'], 'g_ffn': out['g_ffn'], 'w_up': out['w_up'], 'conv_w': out['conv_w'], 'conv_b': out['conv_b'], 'w_down': out['w_down'], 'g_final': out['g_final'], 'loss_target': out['loss_target'], 'm_meta': out['m_meta'], 'm_g_mix': out['m_g_mix'], 'm_w_in': out['m_w_in'], 'm_b_f': out['m_b_f'], 'm_lam_re': out['m_lam_re'], 'm_lam_im': out['m_lam_im'], 'm_log_dt': out['m_log_dt'], 'm_b_re': out['m_b_re'], 'm_b_im': out['m_b_im'], 'm_c_re': out['m_c_re'], 'm_c_im': out['m_c_im'], 'm_d_skip': out['m_d_skip'], 'm_w_glu': out['m_w_glu'], 'm_w_attn_o': out['m_w_attn_o'], 'm_w_out': out['m_w_out'], 'm_g_ffn': out['m_g_ffn'], 'm_w_up': out['m_w_up'], 'm_conv_w': out['m_conv_w'], 'm_conv_b': out['m_conv_b'], 'm_w_down': out['m_w_down'], 'm_g_final': out['m_g_final'], 'v_meta': out['v_meta'], 'v_g_mix': out['v_g_mix'], 'v_w_in': out['v_w_in'], 'v_b_f': out['v_b_f'], 'v_lam_re': out['v_lam_re'], 'v_lam_im': out['v_lam_im'], 'v_log_dt': out['v_log_dt'], 'v_b_re': out['v_b_re'], 'v_b_im': out['v_b_im'], 'v_c_re': out['v_c_re'], 'v_c_im': out['v_c_im'], 'v_d_skip': out['v_d_skip'], 'v_w_glu': out['v_w_glu'], 'v_w_attn_o': out['v_w_attn_o'], 'v_w_out': out['v_w_out'], 'v_g_ffn': out['v_g_ffn'], 'v_w_up': out['v_w_up'], 'v_conv_w': out['v_conv_w'], 'v_conv_b': out['v_conv_b'], 'v_w_down': out['v_w_down'], 'v_g_final': out['v_g_final']}


def _loss(weights, diff, rest, loss_target):
    with _jax.named_scope("forward"):
        args = {**rest, TWIN_DIFF_INPUT: diff, **{k: w.astype(_WEIGHT_DTYPES[k]) for k, w in weights.items()}}
        y = _forward(args)
    with _jax.named_scope("loss_head"):
        err = _jnp.square(y.astype(_jnp.float32) - loss_target)
        return 0.5 * _jnp.sum(_jnp.mean(err, axis=-1)) if err.ndim else 0.5 * err


def _adamw(w, g, m, v):
    m = ADAM_B1 * m + (1.0 - ADAM_B1) * g
    v = ADAM_B2 * v + (1.0 - ADAM_B2) * _jnp.square(g)
    m_hat = m / (1.0 - ADAM_B1 ** ADAM_STEP)
    v_hat = v / (1.0 - ADAM_B2 ** ADAM_STEP)
    delta = -ADAM_LR * (m_hat / (_jnp.sqrt(v_hat) + ADAM_EPS) + ADAM_WD * w)
    return delta, m, v


def reference(x, meta, g_mix, w_in, b_f, lam_re, lam_im, log_dt, b_re, b_im, c_re, c_im, d_skip, w_glu, w_attn_o, w_out, g_ffn, w_up, conv_w, conv_b, w_down, g_final, loss_target, m_meta, m_g_mix, m_w_in, m_b_f, m_lam_re, m_lam_im, m_log_dt, m_b_re, m_b_im, m_c_re, m_c_im, m_d_skip, m_w_glu, m_w_attn_o, m_w_out, m_g_ffn, m_w_up, m_conv_w, m_conv_b, m_w_down, m_g_final, v_meta, v_g_mix, v_w_in, v_b_f, v_lam_re, v_lam_im, v_log_dt, v_b_re, v_b_im, v_c_re, v_c_im, v_d_skip, v_w_glu, v_w_attn_o, v_w_out, v_g_ffn, v_w_up, v_conv_w, v_conv_b, v_w_down, v_g_final):
    given = dict(x=x, meta=meta, g_mix=g_mix, w_in=w_in, b_f=b_f, lam_re=lam_re, lam_im=lam_im, log_dt=log_dt, b_re=b_re, b_im=b_im, c_re=c_re, c_im=c_im, d_skip=d_skip, w_glu=w_glu, w_attn_o=w_attn_o, w_out=w_out, g_ffn=g_ffn, w_up=w_up, conv_w=conv_w, conv_b=conv_b, w_down=w_down, g_final=g_final, loss_target=loss_target, m_meta=m_meta, m_g_mix=m_g_mix, m_w_in=m_w_in, m_b_f=m_b_f, m_lam_re=m_lam_re, m_lam_im=m_lam_im, m_log_dt=m_log_dt, m_b_re=m_b_re, m_b_im=m_b_im, m_c_re=m_c_re, m_c_im=m_c_im, m_d_skip=m_d_skip, m_w_glu=m_w_glu, m_w_attn_o=m_w_attn_o, m_w_out=m_w_out, m_g_ffn=m_g_ffn, m_w_up=m_w_up, m_conv_w=m_conv_w, m_conv_b=m_conv_b, m_w_down=m_w_down, m_g_final=m_g_final, v_meta=v_meta, v_g_mix=v_g_mix, v_w_in=v_w_in, v_b_f=v_b_f, v_lam_re=v_lam_re, v_lam_im=v_lam_im, v_log_dt=v_log_dt, v_b_re=v_b_re, v_b_im=v_b_im, v_c_re=v_c_re, v_c_im=v_c_im, v_d_skip=v_d_skip, v_w_glu=v_w_glu, v_w_attn_o=v_w_attn_o, v_w_out=v_w_out, v_g_ffn=v_g_ffn, v_w_up=v_w_up, v_conv_w=v_conv_w, v_conv_b=v_conv_b, v_w_down=v_w_down, v_g_final=v_g_final)
    weights = {n: given[n] for n in TWIN_WEIGHTS}
    shared = {n: given[n] for n in SHARED_INPUTS}
    per_example = {n: given[n] for n in ['x']}
    grad_fn = _jax.value_and_grad(_loss, argnums=(0, 1))

    def one_microbatch(ex, loss_target):
        ex = dict(ex)
        diff = ex.pop(TWIN_DIFF_INPUT)
        return grad_fn(weights, diff, {**shared, **ex}, loss_target)

    if N_MICROBATCH == 1:
        loss, (grad_w, grad_x) = one_microbatch(per_example, given["loss_target"])
    else:
        def body(carry, xs):
            loss_sum, grad_sum = carry
            l_k, (gw_k, gx_k) = one_microbatch(xs[0], xs[1])
            with _jax.named_scope("update"):
                return (loss_sum + l_k, _jax.tree.map(_jnp.add, grad_sum, gw_k)), gx_k

        init = (_jnp.zeros((), _jnp.float32), _jax.tree.map(_jnp.zeros_like, weights))
        (loss, grad_w), grad_x = _jax.lax.scan(body, init, (per_example, given["loss_target"]))
    with _jax.named_scope("update"):
        delta_w, new_m, new_v = {}, {}, {}
        for n in TWIN_WEIGHTS:
            delta_w[n], new_m[n], new_v[n] = _adamw(weights[n], grad_w[n], given["m_" + n], given["v_" + n])
    return (loss, grad_x, *[grad_w[n] for n in TWIN_WEIGHTS], *[delta_w[n] for n in TWIN_WEIGHTS],
            *[new_m[n] for n in TWIN_WEIGHTS], *[new_v[n] for n in TWIN_WEIGHTS])
```

```python
import functools
import math

import jax
import jax.numpy as jnp
import numpy as np
from jax import lax
from jax.experimental import pallas as pl
from jax.experimental.pallas import tpu as pltpu

F32 = jnp.float32
BF16 = jnp.bfloat16
MESH = pl.DeviceIdType.MESH

EPS = 1e-6
HEAD_DIM = 128
LANES = 128
NEG = -1e30
GELU_C = math.sqrt(2.0 / math.pi)
GELU_A = 0.044715
ADAM_LR, ADAM_B1, ADAM_B2, ADAM_EPS, ADAM_WD, ADAM_STEP = 0.001, 0.9, 0.999, 1e-08, 0.01, 10
V7X_VMEM_BYTES = 64 << 20
GROUPS_PER_BLOCK = 8
RS_PIECES = 4


def _tile(n, pref, mult=LANES):
    if n <= pref:
        return n
    t = (pref // mult) * mult
    while t >= mult:
        if n % t == 0:
            return t
        t -= mult
    raise ValueError(f"no tile for {n} <= {pref} (multiple of {mult})")


def _ctile(pref, *vals):
    g = 0
    for v in vals:
        g = math.gcd(g, v)
    return _tile(g, pref)


def _cparams(sem, est_bytes):
    limit = int(min(max(est_bytes * 1.25 + (4 << 20), 16 << 20), V7X_VMEM_BYTES - (8 << 20)))
    return pltpu.CompilerParams(dimension_semantics=sem, vmem_limit_bytes=limit)


def _sds(shape, dtype):
    return jax.ShapeDtypeStruct(tuple(shape), dtype)


def _sig(x):
    return 1.0 / (1.0 + jnp.exp(-x))


def _gelu(x):
    t = jnp.tanh(GELU_C * (x + GELU_A * x * x * x))
    return 0.5 * x * (1.0 + t)


def _gelu_grad(x):
    t = jnp.tanh(GELU_C * (x + GELU_A * x * x * x))
    return 0.5 * (1.0 + t) + 0.5 * x * (1.0 - t * t) * GELU_C * (1.0 + 3.0 * GELU_A * x * x)


def _mm(a, b, mode, M, N, K, out_dtype, name, *, res=None, a_off=(0, 0), b_off=(0, 0),
        tm=640, tn=1024, tk=2048):
    tm, tn, tk = _tile(M, tm, 8 if mode != "tn" else LANES), _tile(N, tn), _tile(K, tk, LANES if mode != "tn" else 8)
    nk = K // tk
    ar, ac = a_off
    br, bc = b_off
    if mode == "tn":
        assert ar % tk == 0 and ac % tm == 0
        a_spec = pl.BlockSpec((tk, tm), lambda i, j, k: (k + ar // tk, i + ac // tm))
        a_dims = 0
    else:
        assert ar % tm == 0 and ac % tk == 0
        a_spec = pl.BlockSpec((tm, tk), lambda i, j, k: (i + ar // tm, k + ac // tk))
        a_dims = 1
    if mode == "nt":
        assert br % tn == 0 and bc % tk == 0
        b_spec = pl.BlockSpec((tn, tk), lambda i, j, k: (j + br // tn, k + bc // tk))
        b_dims = 1
    else:
        assert br % tk == 0 and bc % tn == 0
        b_spec = pl.BlockSpec((tk, tn), lambda i, j, k: (k + br // tk, j + bc // tn))
        b_dims = 0
    dims = (((a_dims,), (b_dims,)), ((), ()))
    o_spec = pl.BlockSpec((tm, tn), lambda i, j, k: (i, j))
    has_res = res is not None

    def body(*refs):
        if has_res:
            a_ref, b_ref, r_ref, o_ref = refs[:4]
        else:
            a_ref, b_ref, o_ref = refs[:3]
            r_ref = None
        part = lax.dot_general(a_ref[...].astype(BF16), b_ref[...].astype(BF16), dims, preferred_element_type=F32)

        def finish(acc):
            if has_res:
                acc = r_ref[...] + acc
            o_ref[...] = acc.astype(o_ref.dtype)

        if nk == 1:
            finish(part)
        else:
            acc_ref = refs[-1]
            k = pl.program_id(2)

            @pl.when(k == 0)
            def _():
                acc_ref[...] = part

            @pl.when(k > 0)
            def _():
                acc_ref[...] += part

            @pl.when(k == nk - 1)
            def _():
                finish(acc_ref[...])

    in_specs = [a_spec, b_spec] + ([o_spec] if has_res else [])
    args = (a, b) + ((res,) if has_res else ())
    isz = lambda x: jnp.dtype(x.dtype).itemsize
    est = 2 * (tm * tk * isz(a) + tk * tn * isz(b) + tm * tn * jnp.dtype(out_dtype).itemsize) + tm * tn * 4 * (2 + 2 * has_res)
    return pl.pallas_call(
        body, grid=(M // tm, N // tn, nk), in_specs=in_specs, out_specs=o_spec,
        out_shape=_sds((M, N), out_dtype),
        scratch_shapes=[pltpu.VMEM((tm, tn), F32)] if nk > 1 else [],
        compiler_params=_cparams(("parallel", "parallel", "arbitrary"), est), name=name,
    )(*args)


def _rms_fwd(h, g, name):
    LP, D = h.shape
    tm = _tile(LP, 640, 8)

    def body(h_ref, g_ref, o_ref):
        x = h_ref[...]
        r = lax.rsqrt(jnp.mean(x * x, axis=-1, keepdims=True) + EPS)
        o_ref[...] = (x * r * g_ref[...]).astype(o_ref.dtype)

    row = pl.BlockSpec((tm, D), lambda i: (i, 0))
    return pl.pallas_call(
        body, grid=(LP // tm,), in_specs=[row, pl.BlockSpec((1, D), lambda i: (0, 0))], out_specs=row,
        out_shape=_sds((LP, D), BF16), compiler_params=_cparams(("parallel",), 2 * tm * D * 6), name=name,
    )(h, g)


def _rms_bwd(h, g, dn, dres, name):
    LP, D = h.shape
    tm = _tile(LP, 320, 8)
    nt = LP // tm

    def body(h_ref, g_ref, dn_ref, dres_ref, dh_ref, dg_ref):
        i = pl.program_id(0)
        x = h_ref[...]
        r = lax.rsqrt(jnp.mean(x * x, axis=-1, keepdims=True) + EPS)
        xh = x * r
        dn_v = dn_ref[...]
        dxh = dn_v * g_ref[...]
        dh_ref[...] = dres_ref[...] + r * (dxh - xh * jnp.mean(dxh * xh, axis=-1, keepdims=True))
        part = jnp.sum(dn_v * xh, axis=0, keepdims=True)

        @pl.when(i == 0)
        def _():
            dg_ref[...] = part

        @pl.when(i > 0)
        def _():
            dg_ref[...] += part

    row = pl.BlockSpec((tm, D), lambda i: (i, 0))
    vec = pl.BlockSpec((1, D), lambda i: (0, 0))
    return pl.pallas_call(
        body, grid=(nt,), in_specs=[row, vec, row, row], out_specs=[row, vec],
        out_shape=[_sds((LP, D), F32), _sds((1, D), F32)],
        compiler_params=_cparams(("arbitrary",), 2 * 4 * tm * D * 4), name=name,
    )(h, g, dn, dres)


def _final_loss(h, g, tgt, off, name):
    LP, D = h.shape
    tm = LANES
    assert off % tm == 0
    ob = off // tm
    nt = LP // tm

    def body(h_ref, g_ref, t_ref, dh_ref, dg_ref, loss_ref):
        i = pl.program_id(0)
        x = h_ref[...]
        r = lax.rsqrt(jnp.mean(x * x, axis=-1, keepdims=True) + EPS)
        xh = x * r
        gv = g_ref[...]
        e = xh * gv - t_ref[...]
        valid = i >= ob
        dy = jnp.where(valid, e * (1.0 / D), 0.0)
        lpart = jnp.where(valid, 0.5 * jnp.sum(jnp.mean(e * e, axis=-1, keepdims=True), axis=0, keepdims=True), 0.0)
        dxh = dy * gv
        dh_ref[...] = r * (dxh - xh * jnp.mean(dxh * xh, axis=-1, keepdims=True))
        gpart = jnp.sum(dy * xh, axis=0, keepdims=True)

        @pl.when(i == 0)
        def _():
            dg_ref[...] = gpart
            loss_ref[...] = jnp.broadcast_to(lpart, loss_ref.shape)

        @pl.when(i > 0)
        def _():
            dg_ref[...] += gpart
            loss_ref[...] += jnp.broadcast_to(lpart, loss_ref.shape)

    row = pl.BlockSpec((tm, D), lambda i: (i, 0))
    vec = pl.BlockSpec((1, D), lambda i: (0, 0))
    return pl.pallas_call(
        body, grid=(nt,),
        in_specs=[row, vec, pl.BlockSpec((tm, D), lambda i: (jnp.maximum(i - ob, 0), 0))],
        out_specs=[row, vec, pl.BlockSpec((1, LANES), lambda i: (0, 0))],
        out_shape=[_sds((LP, D), F32), _sds((1, D), F32), _sds((1, LANES), F32)],
        compiler_params=_cparams(("arbitrary",), 2 * 3 * tm * D * 4), name=name,
    )(h, g, tgt)


def _fgate_fwd(fpre, bias, pad, name):
    LP, W = fpre.shape

    def body(f_ref, b_ref, o_ref):
        row8 = lax.broadcasted_iota(jnp.int32, (8, W), 0)
        bv = b_ref[...]

        def step(g, carry):
            r0 = pl.multiple_of(g * 8, 8)
            x = f_ref[pl.ds(r0, 8), :] + bv
            lf = jnp.minimum(x, 0.0) - jnp.log(1.0 + jnp.exp(-jnp.abs(x)))
            lf = jnp.where(r0 + row8 >= pad, lf, 0.0)
            for k in (1, 2, 4):
                lf = lf + jnp.where(row8 >= k, pltpu.roll(lf, k, 0), 0.0)
            lf = lf + carry
            o_ref[pl.ds(r0, 8), :] = lf
            return jnp.broadcast_to(lf[7:8, :], (8, W))

        lax.fori_loop(0, LP // 8, step, jnp.zeros((8, W), F32))

    return pl.pallas_call(
        body, out_shape=_sds((LP, W), F32),
        compiler_params=_cparams(None, 3 * LP * W * 4), name=name,
    )(fpre, bias)


def _fgate_bwd(dF, fpre, bias, pad, name):
    LP, W = fpre.shape
    ng = LP // 8

    def body(d_ref, f_ref, b_ref, o_ref, db_ref):
        row8 = lax.broadcasted_iota(jnp.int32, (8, W), 0)
        bv = b_ref[...]

        def step(t, carry):
            run, acc = carry
            g = ng - 1 - t
            r0 = pl.multiple_of(g * 8, 8)
            x = d_ref[pl.ds(r0, 8), :]
            for k in (1, 2, 4):
                x = x + jnp.where(row8 < 8 - k, pltpu.roll(x, 8 - k, 0), 0.0)
            x = x + run
            df = x * _sig(-(f_ref[pl.ds(r0, 8), :] + bv))
            df = jnp.where(r0 + row8 >= pad, df, 0.0)
            o_ref[pl.ds(r0, 8), :] = df
            return jnp.broadcast_to(x[0:1, :], (8, W)), acc + df

        _, acc = lax.fori_loop(0, ng, step, (jnp.zeros((8, W), F32), jnp.zeros((8, W), F32)))
        db_ref[...] = jnp.sum(acc, axis=0, keepdims=True)

    return pl.pallas_call(
        body, out_shape=[_sds((LP, W), F32), _sds((1, W), F32)],
        compiler_params=_cparams(None, 4 * LP * W * 4), name=name,
    )(dF, fpre, bias)


def _attn_fwd(z, bias_t, H, pad, name):
    LP = z.shape[0]
    BQ = _tile(LP, 640)
    BK = LANES
    R = BQ // BK
    scale = HEAD_DIM ** -0.5
    NT = (((1,), (1,)), ((), ()))

    def body(q_ref, k_ref, v_ref, b_ref, o_ref, of_ref, lse_ref):
        qi = pl.program_id(1)
        q = q_ref[...]

        def tile(kb, carry, masked):
            m, l, acc, acc_lo = carry
            k0 = pl.multiple_of(kb * BK, BK)
            s = lax.dot_general(q, k_ref[pl.ds(k0, BK), :], NT, preferred_element_type=F32) * scale
            s = s + b_ref[:, pl.ds(k0, BK)]
            if masked:
                ri = lax.broadcasted_iota(jnp.int32, (BQ, BK), 0)
                ci = lax.broadcasted_iota(jnp.int32, (BQ, BK), 1)
                s = jnp.where(ri >= ci + (kb - qi * R) * BK, s, NEG)
            mn = jnp.maximum(m, jnp.max(s, axis=-1, keepdims=True))
            p = jnp.exp(s - mn)
            alpha = jnp.exp(m - mn)
            l = alpha * l + jnp.sum(p, axis=-1, keepdims=True)
            vk = v_ref[pl.ds(k0, BK), :]
            p_hi = p.astype(BF16)
            p_lo = (p - p_hi.astype(F32)).astype(BF16)
            acc = alpha * acc + jnp.dot(p_hi, vk, preferred_element_type=F32)
            acc_lo = alpha * acc_lo + jnp.dot(p_lo, vk, preferred_element_type=F32)
            return mn, l, acc, acc_lo

        zero = jnp.zeros((BQ, HEAD_DIM), F32)
        carry = (jnp.full((BQ, 1), NEG, F32), jnp.zeros((BQ, 1), F32), zero, zero)
        carry = lax.fori_loop(0, qi * R, lambda kb, c: tile(kb, c, False), carry)
        for d in range(R):
            carry = tile(qi * R + d, carry, True)
        m, l, acc, acc_lo = carry
        rows = qi * BQ + lax.broadcasted_iota(jnp.int32, (BQ, 1), 0)
        o = jnp.where(rows >= pad, (acc + acc_lo) / l, 0.0)
        o_ref[...] = o.astype(o_ref.dtype)
        of_ref[...] = o
        lse_ref[...] = jnp.broadcast_to(m + jnp.log(l), (BQ, LANES))

    in_specs = [
        pl.BlockSpec((BQ, HEAD_DIM), lambda h, i: (i, h)),
        pl.BlockSpec((LP, HEAD_DIM), lambda h, i: (0, H + h)),
        pl.BlockSpec((LP, HEAD_DIM), lambda h, i: (0, 2 * H + h)),
        pl.BlockSpec((None, 1, LP), lambda h, i: (h, 0, 0)),
    ]
    out_specs = [
        pl.BlockSpec((BQ, HEAD_DIM), lambda h, i: (i, h)),
        pl.BlockSpec((BQ, HEAD_DIM), lambda h, i: (i, h)),
        pl.BlockSpec((None, BQ, LANES), lambda h, i: (h, i, 0)),
    ]
    est = 2 * (2 * LP * HEAD_DIM * 2 + 8 * LP * 4) + 20 * BQ * LANES * 4
    return pl.pallas_call(
        body, grid=(H, LP // BQ), in_specs=in_specs, out_specs=out_specs,
        out_shape=[_sds((LP, H * HEAD_DIM), BF16), _sds((LP, H * HEAD_DIM), F32), _sds((H, LP, LANES), F32)],
        compiler_params=_cparams(("parallel", "arbitrary"), est), name=name,
    )(z, z, z, bias_t)


def _attn_delta(do, o, H, name):
    LP = do.shape[0]
    tm = _tile(LP, 640, 8)

    def body(do_ref, o_ref, d_ref):
        d = jnp.sum(do_ref[...].astype(F32) * o_ref[...].astype(F32), axis=-1, keepdims=True)
        d_ref[...] = jnp.broadcast_to(d, (tm, LANES))

    blk = pl.BlockSpec((tm, HEAD_DIM), lambda h, i: (i, h))
    return pl.pallas_call(
        body, grid=(H, LP // tm), in_specs=[blk, blk],
        out_specs=pl.BlockSpec((None, tm, LANES), lambda h, i: (h, i, 0)),
        out_shape=_sds((H, LP, LANES), F32),
        compiler_params=_cparams(("parallel", "parallel"), 8 * tm * LANES * 4), name=name,
    )(do, o)


def _attn_bwd(z, do, lse_t, delta_t, bias_b, H, name):
    LP = z.shape[0]
    BK = _tile(LP, 640)
    BQ = LANES
    R = BK // BQ
    nk = LP // BK
    nq = LP // BQ
    scale = HEAD_DIM ** -0.5
    NT = (((1,), (1,)), ((), ()))
    TN = (((0,), (0,)), ((), ()))

    def body(q_ref, k_ref, v_ref, do_ref, lse_ref, dl_ref, b_ref, dq_ref, dk_ref, dv_ref, db_ref, dq_acc):
        kj = pl.program_id(1)

        @pl.when(kj == 0)
        def _():
            dq_acc[...] = jnp.zeros_like(dq_acc)

        k = k_ref[...]
        v = v_ref[...]
        bcol = b_ref[:, 0:1]

        def tile(qc, carry, masked):
            dk, dv, db = carry
            q0 = pl.multiple_of(qc * BQ, BQ)
            q = q_ref[pl.ds(q0, BQ), :]
            dout = do_ref[pl.ds(q0, BQ), :]
            st = lax.dot_general(k, q, NT, preferred_element_type=F32) * scale + bcol
            if masked:
                ri = lax.broadcasted_iota(jnp.int32, (BK, BQ), 0)
                ci = lax.broadcasted_iota(jnp.int32, (BK, BQ), 1)
                st = jnp.where(ci + (qc - kj * R) * BQ >= ri, st, NEG)
            pt = jnp.exp(st - lse_ref[:, pl.ds(q0, BQ)])
            dv = dv + jnp.dot(pt.astype(BF16), dout, preferred_element_type=F32)
            dpt = lax.dot_general(v, dout, NT, preferred_element_type=F32)
            dst = pt * (dpt - dl_ref[:, pl.ds(q0, BQ)])
            db = db + jnp.sum(dst, axis=-1, keepdims=True)
            dsb = (dst * scale).astype(BF16)
            dk = dk + jnp.dot(dsb, q, preferred_element_type=F32)
            dq_acc[pl.ds(q0, BQ), :] += lax.dot_general(dsb, k, TN, preferred_element_type=F32)
            return dk, dv, db

        carry = (jnp.zeros((BK, HEAD_DIM), F32), jnp.zeros((BK, HEAD_DIM), F32), jnp.zeros((BK, 1), F32))
        for d in range(R):
            carry = tile(kj * R + d, carry, True)
        carry = lax.fori_loop((kj + 1) * R, nq, lambda qc, c: tile(qc, c, False), carry)
        dk, dv, db = carry
        dk_ref[...] = dk.astype(dk_ref.dtype)
        dv_ref[...] = dv.astype(dv_ref.dtype)
        db_ref[...] = jnp.broadcast_to(db, (BK, LANES))

        @pl.when(kj == nk - 1)
        def _():
            dq_ref[...] = dq_acc[...].astype(dq_ref.dtype)

    full = lambda c0: pl.BlockSpec((LP, HEAD_DIM), lambda h, j: (0, c0 + h))
    blk = lambda c0: pl.BlockSpec((BK, HEAD_DIM), lambda h, j: (j, c0 + h))
    vec = pl.BlockSpec((None, 1, LP), lambda h, j: (h, 0, 0))
    in_specs = [full(0), blk(H), blk(2 * H), full(0), vec, vec, pl.BlockSpec((None, BK, LANES), lambda h, j: (h, j, 0))]
    out_specs = [full(0), blk(0), blk(0), pl.BlockSpec((None, BK, LANES), lambda h, j: (h, j, 0))]
    est = 2 * (3 * LP * HEAD_DIM * 2 + 16 * LP * 4) + LP * HEAD_DIM * 4 + 24 * BK * LANES * 4
    return pl.pallas_call(
        body, grid=(H, nk), in_specs=in_specs, out_specs=out_specs,
        out_shape=[_sds((LP, H * HEAD_DIM), BF16)] * 3 + [_sds((H, LP, LANES), F32)],
        scratch_shapes=[pltpu.VMEM((LP, HEAD_DIM), F32)],
        compiler_params=_cparams(("parallel", "arbitrary"), est), name=name,
    )(z, z, z, do, lse_t, delta_t, bias_b)


def _ssm_disc(lr, li, ldt, br, bi):
    dt = jnp.exp(ldt)
    mag = jnp.exp(lr * dt)
    a_re = mag * jnp.cos(li * dt)
    a_im = mag * jnp.sin(li * dt)
    den = lr * lr + li * li
    nr = a_re - 1.0
    z_re = (nr * lr + a_im * li) / den
    z_im = (a_im * lr - nr * li) / den
    return a_re, a_im, z_re * br - z_im * bi, z_re * bi + z_im * br


def _ssm_prep(lr, li, ldt, br, bi, name):
    GP, C = br.shape

    def body(lr_ref, li_ref, ldt_ref, br_ref, bi_ref, ar_ref, ai_ref, bbr_ref, bbi_ref, pr_ref, pi_ref):
        a_re, a_im, bb_re, bb_im = _ssm_disc(lr_ref[...], li_ref[...], ldt_ref[...], br_ref[...], bi_ref[...])
        ar_ref[...] = a_re
        ai_ref[...] = a_im
        bbr_ref[...] = bb_re
        bbi_ref[...] = bb_im
        lane = lax.broadcasted_iota(jnp.int32, (tg, 8), 1)
        pr, pi_ = a_re, a_im
        accr = jnp.zeros((tg, 8), F32)
        acci = jnp.zeros((tg, 8), F32)
        for k in range(8):
            accr = jnp.where(lane == k, pr, accr)
            acci = jnp.where(lane == k, pi_, acci)
            pr, pi_ = pr * a_re - pi_ * a_im, pr * a_im + pi_ * a_re
        pr_ref[...] = accr
        pi_ref[...] = acci

    tg = _tile(GP, 512, 8)
    blk = lambda w: pl.BlockSpec((tg, w), lambda i: (i, 0))
    col = _sds((GP, 1), F32)
    return pl.pallas_call(
        body, grid=(GP // tg,), in_specs=[blk(1), blk(1), blk(1), blk(C), blk(C)],
        out_specs=[blk(1), blk(1), blk(C), blk(C), blk(8), blk(8)],
        out_shape=[col, col, _sds((GP, C), F32), _sds((GP, C), F32), _sds((GP, 8), F32), _sds((GP, 8), F32)],
        compiler_params=_cparams(("parallel",), 48 * tg * LANES * 4), name=name,
    )(lr, li, ldt, br, bi)


def _ssm_prep_bwd(lr, li, ldt, br, bi, dar, dai, dbbr, dbbi, name):
    GP, C = br.shape

    def body(lr_ref, li_ref, ldt_ref, br_ref, bi_ref, dar_ref, dai_ref, dbbr_ref, dbbi_ref,
             glr_ref, gli_ref, gldt_ref, gbr_ref, gbi_ref):
        _, vjp = jax.vjp(_ssm_disc, lr_ref[...], li_ref[...], ldt_ref[...], br_ref[...], bi_ref[...])
        glr, gli, gldt, gbr, gbi = vjp((dar_ref[...], dai_ref[...], dbbr_ref[...], dbbi_ref[...]))
        glr_ref[...] = glr
        gli_ref[...] = gli
        gldt_ref[...] = gldt
        gbr_ref[...] = gbr
        gbi_ref[...] = gbi

    tg = _tile(GP, 512, 8)
    blk = lambda w: pl.BlockSpec((tg, w), lambda i: (i, 0))
    col = _sds((GP, 1), F32)
    return pl.pallas_call(
        body, grid=(GP // tg,), in_specs=[blk(1), blk(1), blk(1), blk(C), blk(C), blk(1), blk(1), blk(C), blk(C)],
        out_specs=[blk(1), blk(1), blk(1), blk(C), blk(C)],
        out_shape=[col, col, col, _sds((GP, C), F32), _sds((GP, C), F32)],
        compiler_params=_cparams(("parallel",), 96 * tg * LANES * 4), name=name,
    )(lr, li, ldt, br, bi, dar, dai, dbbr, dbbi)


def _cmul_add(xr, xi, mr, mi, sr, si):
    return xr + mr * sr - mi * si, xi + mr * si + mi * sr


def _ssm_fwd(z, u_col0, coef, bbr, bbi, ccr, cci, dskip, name):
    LP = z.shape[0]
    NB, CB, S = bbr.shape
    TS = _tile(LP, 640, 8)
    nt = LP // TS

    def body(u_ref, coef_ref, bbr_ref, bbi_ref, ccr_ref, cci_ref, ds_ref, y_ref, yg_ref, hr_ref, hi_ref, bur, bui, carry):
        i = pl.program_id(1)

        @pl.when(i == 0)
        def _():
            carry[...] = jnp.zeros_like(carry)

        u = u_ref[...]
        bur[...] = jnp.dot(u, bbr_ref[...], preferred_element_type=F32)
        bui[...] = jnp.dot(u, bbi_ref[...], preferred_element_type=F32)

        def step(g, c):
            cr, ci = c
            r0 = pl.multiple_of(g * 8, 8)
            xr = bur[pl.ds(r0, 8), :]
            xi = bui[pl.ds(r0, 8), :]
            for n, k in enumerate((1, 2, 4)):
                xr, xi = _cmul_add(xr, xi, coef_ref[2 * n], coef_ref[2 * n + 1], pltpu.roll(xr, k, 0), pltpu.roll(xi, k, 0))
            xr, xi = _cmul_add(xr, xi, coef_ref[6], coef_ref[7], cr, ci)
            hr_ref[pl.ds(r0, 8), :] = xr
            hi_ref[pl.ds(r0, 8), :] = xi
            return jnp.broadcast_to(xr[7:8, :], (8, S)), jnp.broadcast_to(xi[7:8, :], (8, S))

        cr, ci = lax.fori_loop(0, TS // 8, step, (carry[0], carry[1]))
        carry[0] = cr
        carry[1] = ci
        y = (jnp.dot(hr_ref[...].astype(BF16), ccr_ref[...], preferred_element_type=F32)
             - jnp.dot(hi_ref[...].astype(BF16), cci_ref[...], preferred_element_type=F32)
             + ds_ref[...] * u.astype(F32))
        y_ref[...] = y
        yg_ref[...] = _gelu(y).astype(yg_ref.dtype)

    ucb = u_col0 // CB
    in_specs = [
        pl.BlockSpec((TS, CB), lambda j, i: (i, ucb + j)),
        pl.BlockSpec((None, 8, 8, S), lambda j, i: (j, 0, 0, 0)),
        pl.BlockSpec((None, CB, S), lambda j, i: (j, 0, 0)),
        pl.BlockSpec((None, CB, S), lambda j, i: (j, 0, 0)),
        pl.BlockSpec((None, S, CB), lambda j, i: (j, 0, 0)),
        pl.BlockSpec((None, S, CB), lambda j, i: (j, 0, 0)),
        pl.BlockSpec((1, CB), lambda j, i: (0, j)),
    ]
    yb = pl.BlockSpec((TS, CB), lambda j, i: (i, j))
    hb = pl.BlockSpec((TS, S), lambda j, i: (i, j))
    est = 2 * (2 * TS * S * 4 + 3 * TS * CB * 4 + 8 * 8 * S * 4 + 4 * CB * S * 2) + 3 * TS * S * 4
    return pl.pallas_call(
        body, grid=(NB, nt), in_specs=in_specs, out_specs=[yb, yb, hb, hb],
        out_shape=[_sds((LP, NB * CB), F32), _sds((LP, NB * CB), BF16), _sds((LP, NB * S), F32), _sds((LP, NB * S), F32)],
        scratch_shapes=[pltpu.VMEM((TS, S), F32), pltpu.VMEM((TS, S), F32), pltpu.VMEM((2, 8, S), F32)],
        compiler_params=_cparams(("parallel", "arbitrary"), est), name=name,
    )(z, coef, bbr, bbi, ccr, cci, dskip)


def _ssm_bwd(z, u_col0, dyg, y, hr, hi, coef_rev, bbr_t, bbi_t, ccr_t, cci_t, dskip, name):
    LP = z.shape[0]
    NB, S, CB = bbr_t.shape
    TS = _tile(LP, 640, 8)
    nt = LP // TS
    ng = TS // 8
    TN = (((0,), (0,)), ((), ()))

    def body(u_ref, dyg_ref, y_ref, hr_ref, hi_ref, tr_ref, ti_ref, coef_ref, bbr_ref, bbi_ref, ccr_ref, cci_ref, ds_ref,
             du_ref, dbbr_ref, dbbi_ref, dccr_ref, dcci_ref, dar_ref, dai_ref, dd_ref,
             gr, gi, carry, acc_bbr, acc_bbi, acc_ccr, acc_cci, acc_a, acc_d):
        i = pl.program_id(1)

        @pl.when(i == 0)
        def _():
            for ref in (carry, acc_bbr, acc_bbi, acc_ccr, acc_cci, acc_a, acc_d):
                ref[...] = jnp.zeros_like(ref)

        u = u_ref[...]
        dy = dyg_ref[...] * _gelu_grad(y_ref[...])
        dyb = dy.astype(BF16)
        gr[...] = jnp.dot(dyb, ccr_ref[...], preferred_element_type=F32)
        gi[...] = -jnp.dot(dyb, cci_ref[...], preferred_element_type=F32)
        row8 = lax.broadcasted_iota(jnp.int32, (8, S), 0)
        first_chunk = i == nt - 1
        tail_r = jnp.where(first_chunk, 0.0, tr_ref[...])
        tail_i = jnp.where(first_chunk, 0.0, ti_ref[...])

        def step(t, c):
            cr, ci, sar, sai = c
            g = ng - 1 - t
            r0 = pl.multiple_of(g * 8, 8)
            xr = gr[pl.ds(r0, 8), :]
            xi = gi[pl.ds(r0, 8), :]
            for n, k in enumerate((1, 2, 4)):
                xr, xi = _cmul_add(xr, xi, coef_ref[2 * n], coef_ref[2 * n + 1], pltpu.roll(xr, 8 - k, 0), pltpu.roll(xi, 8 - k, 0))
            xr, xi = _cmul_add(xr, xi, coef_ref[6], coef_ref[7], cr, ci)
            gr[pl.ds(r0, 8), :] = xr
            gi[pl.ds(r0, 8), :] = xi
            p0 = pl.multiple_of(jnp.maximum(g - 1, 0) * 8, 8)
            pr = jnp.where(g > 0, hr_ref[pl.ds(p0, 8), :], tail_r)
            pi_ = jnp.where(g > 0, hi_ref[pl.ds(p0, 8), :], tail_i)
            hpr = pltpu.roll(jnp.where(row8 == 7, pr, hr_ref[pl.ds(r0, 8), :]), 1, 0)
            hpi = pltpu.roll(jnp.where(row8 == 7, pi_, hi_ref[pl.ds(r0, 8), :]), 1, 0)
            sar = sar + xr * hpr + xi * hpi
            sai = sai + xi * hpr - xr * hpi
            return jnp.broadcast_to(xr[0:1, :], (8, S)), jnp.broadcast_to(xi[0:1, :], (8, S)), sar, sai

        zero = jnp.zeros((8, S), F32)
        cr, ci, sar, sai = lax.fori_loop(0, ng, step, (carry[0], carry[1], zero, zero))
        carry[0] = cr
        carry[1] = ci
        acc_a[0] += sar
        acc_a[1] += sai
        grb = gr[...].astype(BF16)
        gib = gi[...].astype(BF16)
        du = (jnp.dot(grb, bbr_ref[...], preferred_element_type=F32) + jnp.dot(gib, bbi_ref[...], preferred_element_type=F32)
              + ds_ref[...] * dy)
        du_ref[...] = du.astype(du_ref.dtype)
        acc_bbr[...] += lax.dot_general(u, grb, TN, preferred_element_type=F32)
        acc_bbi[...] += lax.dot_general(u, gib, TN, preferred_element_type=F32)
        acc_ccr[...] += lax.dot_general(hr_ref[...].astype(BF16), dyb, TN, preferred_element_type=F32)
        acc_cci[...] -= lax.dot_general(hi_ref[...].astype(BF16), dyb, TN, preferred_element_type=F32)
        acc_d[...] += jnp.sum(dy * u.astype(F32), axis=0, keepdims=True)

        @pl.when(i == nt - 1)
        def _():
            dbbr_ref[...] = acc_bbr[...]
            dbbi_ref[...] = acc_bbi[...]
            dccr_ref[...] = acc_ccr[...]
            dcci_ref[...] = acc_cci[...]
            dar_ref[...] = jnp.sum(acc_a[0], axis=0, keepdims=True)
            dai_ref[...] = jnp.sum(acc_a[1], axis=0, keepdims=True)
            dd_ref[...] = acc_d[...]

    ucb = u_col0 // CB
    rev = lambda i: nt - 1 - i
    tail = lambda j, i: (jnp.maximum(rev(i) * ng - 1, 0), j)
    yb = pl.BlockSpec((TS, CB), lambda j, i: (rev(i), j))
    hb = pl.BlockSpec((TS, S), lambda j, i: (rev(i), j))
    in_specs = [
        pl.BlockSpec((TS, CB), lambda j, i: (rev(i), ucb + j)), yb, yb, hb, hb,
        pl.BlockSpec((8, S), tail), pl.BlockSpec((8, S), tail),
        pl.BlockSpec((None, 8, 8, S), lambda j, i: (j, 0, 0, 0)),
        pl.BlockSpec((None, S, CB), lambda j, i: (j, 0, 0)),
        pl.BlockSpec((None, S, CB), lambda j, i: (j, 0, 0)),
        pl.BlockSpec((None, CB, S), lambda j, i: (j, 0, 0)),
        pl.BlockSpec((None, CB, S), lambda j, i: (j, 0, 0)),
        pl.BlockSpec((1, CB), lambda j, i: (0, j)),
    ]
    mat_cs = pl.BlockSpec((None, CB, S), lambda j, i: (j, 0, 0))
    mat_sc = pl.BlockSpec((None, S, CB), lambda j, i: (j, 0, 0))
    vec_s = pl.BlockSpec((None, 1, S), lambda j, i: (j, 0, 0))
    out_specs = [yb, mat_cs, mat_cs, mat_sc, mat_sc, vec_s, vec_s, pl.BlockSpec((1, CB), lambda j, i: (0, j))]
    out_shape = [_sds((LP, NB * CB), BF16), _sds((NB, CB, S), F32), _sds((NB, CB, S), F32), _sds((NB, S, CB), F32),
                 _sds((NB, S, CB), F32), _sds((NB, 1, S), F32), _sds((NB, 1, S), F32), _sds((1, NB * CB), F32)]
    scratch = [pltpu.VMEM((TS, S), F32), pltpu.VMEM((TS, S), F32), pltpu.VMEM((2, 8, S), F32),
               pltpu.VMEM((CB, S), F32), pltpu.VMEM((CB, S), F32), pltpu.VMEM((S, CB), F32), pltpu.VMEM((S, CB), F32),
               pltpu.VMEM((2, 8, S), F32), pltpu.VMEM((1, CB), F32)]
    est = 2 * (2 * TS * S * 4 + 4 * TS * CB * 4 + 8 * 8 * S * 4 + 12 * CB * S * 4) + 4 * TS * S * 4
    return pl.pallas_call(
        body, grid=(NB, nt), in_specs=in_specs, out_specs=out_specs, out_shape=out_shape, scratch_shapes=scratch,
        compiler_params=_cparams(("parallel", "arbitrary"), est), name=name,
    )(z, dyg, y, hr, hi, hr, hi, coef_rev, bbr_t, bbi_t, ccr_t, cci_t, dskip)


def _merge_fwd(yab, z, ao, D, ga0, gb0, name):
    LP = z.shape[0]
    tm = _tile(LP, 640, 8)
    tn = _ctile(512, D, ga0, gb0)
    nj = D // tn

    def body(ya_ref, yb_ref, ga_ref, gb_ref, ao_ref, o_ref):
        f = lambda r: r[...].astype(F32)
        ssm = f(ya_ref) * _sig(f(yb_ref))
        o_ref[...] = (_sig(f(ga_ref)) * ssm + _sig(f(gb_ref)) * f(ao_ref)).astype(o_ref.dtype)

    blk = lambda c0: pl.BlockSpec((tm, tn), lambda i, j: (i, c0 // tn + j))
    return pl.pallas_call(
        body, grid=(LP // tm, nj), in_specs=[blk(0), blk(D), blk(ga0), blk(gb0), blk(0)], out_specs=blk(0),
        out_shape=_sds((LP, D), BF16), compiler_params=_cparams(("parallel", "parallel"), 2 * 6 * tm * tn * 4), name=name,
    )(yab, yab, z, z, ao)


def _merge_bwd(dm, yab, z, ao, D, ga0, gb0, name):
    LP = z.shape[0]
    tm = _tile(LP, 640, 8)
    tn = _ctile(512, D, ga0, gb0)
    nj = D // tn

    def body(dm_ref, ya_ref, yb_ref, ga_ref, gb_ref, ao_ref, dya_ref, dyb_ref, dga_ref, dgb_ref, dao_ref):
        f = lambda r: r[...].astype(F32)
        dmv, ya, ao_v = f(dm_ref), f(ya_ref), f(ao_ref)
        sa, sb, sy = _sig(f(ga_ref)), _sig(f(gb_ref)), _sig(f(yb_ref))
        t = dmv * sa
        dya_ref[...] = (t * sy).astype(BF16)
        dyb_ref[...] = (t * ya * sy * (1.0 - sy)).astype(BF16)
        dga_ref[...] = (dmv * (ya * sy) * sa * (1.0 - sa)).astype(BF16)
        dgb_ref[...] = (dmv * ao_v * sb * (1.0 - sb)).astype(BF16)
        dao_ref[...] = (dmv * sb).astype(BF16)

    blk = lambda c0: pl.BlockSpec((tm, tn), lambda i, j: (i, c0 // tn + j))
    return pl.pallas_call(
        body, grid=(LP // tm, nj), in_specs=[blk(0), blk(0), blk(D), blk(ga0), blk(gb0), blk(0)], out_specs=[blk(0)] * 5,
        out_shape=[_sds((LP, D), BF16)] * 5, compiler_params=_cparams(("parallel", "parallel"), 2 * 11 * tm * tn * 4), name=name,
    )(dm, yab, yab, z, z, ao)


def _shift_down(x, halo, k, row8):
    s = pltpu.roll(x, k, 0)
    top = jnp.where(row8 < k, pltpu.roll(halo, k, 0), s[0:8])
    return jnp.concatenate([top, s[8:]], axis=0) if x.shape[0] > 8 else top


def _shift_up(x, halo, k, row8):
    tm = x.shape[0]
    s = pltpu.roll(x, tm - k, 0)
    bot = jnp.where(row8 >= 8 - k, pltpu.roll(halo, 8 - k, 0), s[tm - 8:])
    return jnp.concatenate([s[:tm - 8], bot], axis=0) if tm > 8 else bot


def _conv_gate(g, halo, w_ref, cb, row8):
    return cb + w_ref[0:1, :] * _shift_down(g, halo, 2, row8) + w_ref[1:2, :] * _shift_down(g, halo, 1, row8) + w_ref[2:3, :] * g


def _convact_fwd(gu, cw, cb, DFF, name):
    LP = gu.shape[0]
    tm = _tile(LP, 640, 8)
    tn = _tile(DFF, 512)
    nj = DFF // tn
    t8 = tm // 8

    def body(g_ref, h_ref, u_ref, w_ref, b_ref, o_ref):
        i = pl.program_id(0)
        row8 = lax.broadcasted_iota(jnp.int32, (8, tn), 0)
        g = g_ref[...].astype(F32)
        halo = jnp.where(i > 0, h_ref[...].astype(F32), 0.0)
        gc = _conv_gate(g, halo, w_ref, b_ref[...], row8)
        o_ref[...] = (gc * _sig(gc) * u_ref[...].astype(F32)).astype(o_ref.dtype)

    in_specs = [
        pl.BlockSpec((tm, tn), lambda i, j: (i, j)),
        pl.BlockSpec((8, tn), lambda i, j: (jnp.maximum(i * t8 - 1, 0), j)),
        pl.BlockSpec((tm, tn), lambda i, j: (i, nj + j)),
        pl.BlockSpec((3, tn), lambda i, j: (0, j)),
        pl.BlockSpec((1, tn), lambda i, j: (0, j)),
    ]
    return pl.pallas_call(
        body, grid=(LP // tm, nj), in_specs=in_specs, out_specs=pl.BlockSpec((tm, tn), lambda i, j: (i, j)),
        out_shape=_sds((LP, DFF), BF16), compiler_params=_cparams(("parallel", "parallel"), 2 * 8 * tm * tn * 4), name=name,
    )(gu, gu, gu, cw, cb)


def _convact_bwd(dact, gu, cw, cb, DFF, name):
    LP = gu.shape[0]
    tm = _tile(LP, 640, 8)
    tn = _tile(DFF, 512)
    nj = DFF // tn
    t8 = tm // 8

    def body(da_ref, g_ref, h_ref, u_ref, w_ref, b_ref, dgc_ref, du_ref):
        i = pl.program_id(0)
        row8 = lax.broadcasted_iota(jnp.int32, (8, tn), 0)
        g = g_ref[...].astype(F32)
        halo = jnp.where(i > 0, h_ref[...].astype(F32), 0.0)
        gc = _conv_gate(g, halo, w_ref, b_ref[...], row8)
        sg = _sig(gc)
        da = da_ref[...].astype(F32)
        du_ref[...] = (da * gc * sg).astype(du_ref.dtype)
        dgc_ref[...] = da * u_ref[...].astype(F32) * sg * (1.0 + gc * (1.0 - sg))

    blk = pl.BlockSpec((tm, tn), lambda i, j: (i, j))
    in_specs = [
        blk, blk,
        pl.BlockSpec((8, tn), lambda i, j: (jnp.maximum(i * t8 - 1, 0), j)),
        pl.BlockSpec((tm, tn), lambda i, j: (i, nj + j)),
        pl.BlockSpec((3, tn), lambda i, j: (0, j)),
        pl.BlockSpec((1, tn), lambda i, j: (0, j)),
    ]
    return pl.pallas_call(
        body, grid=(LP // tm, nj), in_specs=in_specs, out_specs=[blk, blk],
        out_shape=[_sds((LP, DFF), F32), _sds((LP, DFF), BF16)],
        compiler_params=_cparams(("parallel", "parallel"), 2 * 10 * tm * tn * 4), name=name,
    )(dact, gu, gu, gu, cw, cb)


def _conv_bwd(dgc, gu, cw, DFF, pad, name):
    LP = gu.shape[0]
    tm = _tile(LP, 640, 8)
    tn = _tile(DFF, 512)
    nj = DFF // tn
    t8 = tm // 8
    nt = LP // tm

    def body(d_ref, dn_ref, g_ref, h_ref, w_ref, dg_ref, dw_ref, db_ref):
        i = pl.program_id(1)
        row8 = lax.broadcasted_iota(jnp.int32, (8, tn), 0)
        d = d_ref[...]
        nxt = jnp.where(i < nt - 1, dn_ref[...], 0.0)
        dg = w_ref[2:3, :] * d + w_ref[1:2, :] * _shift_up(d, nxt, 1, row8) + w_ref[0:1, :] * _shift_up(d, nxt, 2, row8)
        rows = i * tm + lax.broadcasted_iota(jnp.int32, (tm, 1), 0)
        dg_ref[...] = jnp.where(rows >= pad, dg, 0.0).astype(dg_ref.dtype)
        g = g_ref[...].astype(F32)
        halo = jnp.where(i > 0, h_ref[...].astype(F32), 0.0)
        row3 = lax.broadcasted_iota(jnp.int32, (3, tn), 0)
        s0 = jnp.sum(d * _shift_down(g, halo, 2, row8), axis=0, keepdims=True)
        s1 = jnp.sum(d * _shift_down(g, halo, 1, row8), axis=0, keepdims=True)
        s2 = jnp.sum(d * g, axis=0, keepdims=True)
        dw = jnp.where(row3 == 0, s0, jnp.where(row3 == 1, s1, s2))
        dbp = jnp.sum(d, axis=0, keepdims=True)

        @pl.when(i == 0)
        def _():
            dw_ref[...] = dw
            db_ref[...] = dbp

        @pl.when(i > 0)
        def _():
            dw_ref[...] += dw
            db_ref[...] += dbp

    blk = pl.BlockSpec((tm, tn), lambda j, i: (i, j))
    in_specs = [
        blk,
        pl.BlockSpec((8, tn), lambda j, i: (jnp.minimum((i + 1) * t8, LP // 8 - 1), j)),
        blk,
        pl.BlockSpec((8, tn), lambda j, i: (jnp.maximum(i * t8 - 1, 0), j)),
        pl.BlockSpec((3, tn), lambda j, i: (0, j)),
    ]
    out_specs = [blk, pl.BlockSpec((3, tn), lambda j, i: (0, j)), pl.BlockSpec((1, tn), lambda j, i: (0, j))]
    return pl.pallas_call(
        body, grid=(nj, nt), in_specs=in_specs, out_specs=out_specs,
        out_shape=[_sds((LP, DFF), BF16), _sds((3, DFF), F32), _sds((1, DFF), F32)],
        compiler_params=_cparams(("parallel", "arbitrary"), 2 * 10 * tm * tn * 4), name=name,
    )(dgc, dgc, gu, gu, cw)


def _adamw_math(w, g, m, v):
    m = ADAM_B1 * m + (1.0 - ADAM_B1) * g
    v = ADAM_B2 * v + (1.0 - ADAM_B2) * (g * g)
    m_hat = m / (1.0 - ADAM_B1 ** ADAM_STEP)
    v_hat = v / (1.0 - ADAM_B2 ** ADAM_STEP)
    delta = -ADAM_LR * (m_hat / (jnp.sqrt(v_hat) + ADAM_EPS) + ADAM_WD * w)
    return delta, m, v


def _adamw(w, g, m, v, name):
    R, C = w.shape
    tm = R if R * C * 4 <= (1 << 20) else _tile(R, max(8, ((1 << 20) // (C * 4)) // 8 * 8), 8)

    def body(w_ref, g_ref, m_ref, v_ref, d_ref, mo_ref, vo_ref):
        d_ref[...], mo_ref[...], vo_ref[...] = _adamw_math(w_ref[...], g_ref[...], m_ref[...], v_ref[...])

    blk = pl.BlockSpec((tm, C), lambda i: (i, 0))
    return pl.pallas_call(
        body, grid=(R // tm,), in_specs=[blk] * 4, out_specs=[blk] * 3, out_shape=[_sds((R, C), F32)] * 3,
        compiler_params=_cparams(("parallel",), 2 * 7 * tm * (C + LANES) * 4), name=name,
    )(w, g, m, v)


def _sum_adamw(parts, w, m, v, name):
    n, R, C = parts.shape
    tm = _tile(R, 256, 8)

    def body(p_ref, w_ref, m_ref, v_ref, g_ref, d_ref, mo_ref, vo_ref):
        g = p_ref[0]
        for k in range(1, n):
            g = g + p_ref[k]
        g_ref[...] = g
        d_ref[...], mo_ref[...], vo_ref[...] = _adamw_math(w_ref[...], g, m_ref[...], v_ref[...])

    blk = pl.BlockSpec((tm, C), lambda i: (i, 0))
    return pl.pallas_call(
        body, grid=(R // tm,), in_specs=[pl.BlockSpec((n, tm, C), lambda i: (0, i, 0))] + [blk] * 3, out_specs=[blk] * 4,
        out_shape=[_sds((R, C), F32)] * 4,
        compiler_params=_cparams(("parallel",), 2 * (n + 7) * tm * C * 4), name=name,
    )(parts, w, m, v)


def _add2(a, b, name):
    n, R, C = a.shape
    tm = _tile(R, 320, 8)

    def body(a_ref, b_ref, o_ref):
        o_ref[...] = a_ref[...] + b_ref[...]

    blk = pl.BlockSpec((None, tm, C), lambda k, i: (k, i, 0))
    return pl.pallas_call(
        body, grid=(n, R // tm), in_specs=[blk, blk], out_specs=blk, out_shape=_sds((n, R, C), F32),
        compiler_params=_cparams(("parallel", "parallel"), 2 * 3 * tm * C * 4), name=name,
    )(a, b)


def _sum_parts(parts, name):
    n, R, C = parts.shape
    tm = _tile(R, 320, 8)

    def body(p_ref, o_ref):
        g = p_ref[0]
        for k in range(1, n):
            g = g + p_ref[k]
        o_ref[...] = g

    return pl.pallas_call(
        body, grid=(R // tm,), in_specs=[pl.BlockSpec((n, tm, C), lambda i: (0, i, 0))],
        out_specs=pl.BlockSpec((tm, C), lambda i: (i, 0)), out_shape=_sds((R, C), F32),
        compiler_params=_cparams(("parallel",), 2 * (n + 1) * tm * C * 4), name=name,
    )(parts)


def _place():
    return lax.axis_index("x"), lax.axis_index("y"), lax.axis_index("c")


def _other_chips(x, y):
    return [(1 - x, y), (x, 1 - y), (1 - x, 1 - y)]


HBM_SPEC = pl.BlockSpec(memory_space=pltpu.HBM)


def _gather_chips(p, name):
    R, W = p.shape
    HR = R // 2
    assert HR % 16 == 0

    def body(p_ref, g_ref, send_sems, recv_sems, local_sem):
        x, y, c = _place()
        chips = _other_chips(x, y)
        me = 2 * x + y
        r0 = pl.multiple_of(c * HR, 16)
        q0 = pl.multiple_of((1 - c) * HR, 16)

        def copy(k, src, chip, rows0, to):
            return pltpu.make_async_remote_copy(
                src_ref=src, dst_ref=g_ref.at[chip, pl.ds(rows0, HR), :],
                send_sem=send_sems.at[k], recv_sem=recv_sems.at[k], device_id=to, device_id_type=MESH)

        mine = pltpu.make_async_copy(p_ref, g_ref.at[me], local_sem)
        mine.start()
        first = [copy(k, p_ref.at[pl.ds(r0, HR), :], me, r0, (cx, cy, c)) for k, (cx, cy) in enumerate(chips)]
        for cp in first:
            cp.start()
        passed = []
        for k, (cx, cy) in enumerate(chips):
            chip = 2 * cx + cy
            copy(k, p_ref.at[pl.ds(r0, HR), :], chip, r0, (cx, cy, c)).wait_recv()
            fw = copy(3 + k, g_ref.at[chip, pl.ds(r0, HR), :], chip, r0, (x, y, 1 - c))
            fw.start()
            passed.append(fw)
        for k, (cx, cy) in enumerate(chips):
            copy(3 + k, p_ref.at[pl.ds(q0, HR), :], 2 * cx + cy, q0, (x, y, 1 - c)).wait_recv()
        for cp in first + passed:
            cp.wait_send()
        mine.wait()

    return pl.pallas_call(
        body, out_shape=_sds((4, R, W), p.dtype), in_specs=[HBM_SPEC], out_specs=HBM_SPEC,
        scratch_shapes=[pltpu.SemaphoreType.DMA((6,)), pltpu.SemaphoreType.DMA((6,)), pltpu.SemaphoreType.DMA(())],
        name=name,
    )(p)


def _sibling_halves(gr, name):
    n, R, W = gr.shape
    HR = R // 2
    assert HR % 8 == 0

    def body(g_ref, land_ref, send_sem, recv_sem):
        x, y, c = _place()
        q0 = pl.multiple_of((1 - c) * HR, 8)
        cp = pltpu.make_async_remote_copy(
            src_ref=g_ref.at[pl.ds(0, n), pl.ds(q0, HR), :], dst_ref=land_ref, send_sem=send_sem, recv_sem=recv_sem,
            device_id=(x, y, 1 - c), device_id_type=MESH)
        cp.start()
        cp.wait()

    return pl.pallas_call(
        body, out_shape=_sds((n, HR, W), gr.dtype), in_specs=[HBM_SPEC], out_specs=HBM_SPEC,
        scratch_shapes=[pltpu.SemaphoreType.DMA(()), pltpu.SemaphoreType.DMA(())], name=name,
    )(gr)


def _scatter_chips(s, name):
    n, HR, W = s.shape

    def body(s_ref, land_ref, send_sems, recv_sems, local_sem):
        x, y, c = _place()
        chips = _other_chips(x, y)
        me = 2 * x + y
        mine = pltpu.make_async_copy(s_ref.at[me], land_ref.at[3], local_sem)
        mine.start()
        sends = []
        for k, (cx, cy) in enumerate(chips):
            cp = pltpu.make_async_remote_copy(
                src_ref=s_ref.at[2 * cx + cy], dst_ref=land_ref.at[k], send_sem=send_sems.at[k], recv_sem=recv_sems.at[k],
                device_id=(cx, cy, c), device_id_type=MESH)
            cp.start()
            sends.append(cp)
        for cp in sends:
            cp.wait_recv()
        for cp in sends:
            cp.wait_send()
        mine.wait()

    return pl.pallas_call(
        body, out_shape=_sds((4, HR, W), s.dtype), in_specs=[HBM_SPEC], out_specs=HBM_SPEC,
        scratch_shapes=[pltpu.SemaphoreType.DMA((3,)), pltpu.SemaphoreType.DMA((3,)), pltpu.SemaphoreType.DMA(())],
        name=name,
    )(s)


def _sibling_join(half, name):
    HR, W = half.shape
    assert HR % 8 == 0

    def body(h_ref, o_ref, send_sem, recv_sem, local_sem):
        x, y, c = _place()
        r0 = pl.multiple_of(c * HR, 8)
        mine = pltpu.make_async_copy(h_ref, o_ref.at[pl.ds(r0, HR), :], local_sem)
        mine.start()
        cp = pltpu.make_async_remote_copy(
            src_ref=h_ref, dst_ref=o_ref.at[pl.ds(r0, HR), :], send_sem=send_sem, recv_sem=recv_sem,
            device_id=(x, y, 1 - c), device_id_type=MESH)
        cp.start()
        q0 = pl.multiple_of((1 - c) * HR, 8)
        pltpu.make_async_remote_copy(
            src_ref=h_ref, dst_ref=o_ref.at[pl.ds(q0, HR), :], send_sem=send_sem, recv_sem=recv_sem,
            device_id=(x, y, 1 - c), device_id_type=MESH).wait_recv()
        cp.wait_send()
        mine.wait()

    return pl.pallas_call(
        body, out_shape=_sds((2 * HR, W), half.dtype), in_specs=[HBM_SPEC], out_specs=HBM_SPEC,
        scratch_shapes=[pltpu.SemaphoreType.DMA(()), pltpu.SemaphoreType.DMA(()), pltpu.SemaphoreType.DMA(())],
        name=name,
    )(half)


def _gather_all(v, name):
    M, W = v.shape

    def body(v_ref, o_ref, send_sems, recv_sems, local_sem):
        x, y, c = _place()
        me, sibling = (x, y, c), (x, y, 1 - c)
        chips = _other_chips(x, y)

        def slot(px, py, pc):
            return o_ref.at[4 * px + 2 * py + pc]

        def copy(k, block, to, src=None):
            return pltpu.make_async_remote_copy(
                src_ref=slot(*block) if src is None else src, dst_ref=slot(*block),
                send_sem=send_sems.at[k], recv_sem=recv_sems.at[k], device_id=to, device_id_type=MESH)

        mine = pltpu.make_async_copy(v_ref, slot(*me), local_sem)
        mine.start()
        first = [copy(0, me, sibling, src=v_ref)]
        first += [copy(1 + j, me, (*chip, c), src=v_ref) for j, chip in enumerate(chips)]
        for cp in first:
            cp.start()
        passed = [copy(4 + j, (*chip, c), sibling) for j, chip in enumerate(chips)]
        for j, chip in enumerate(chips):
            copy(1 + j, (*chip, c), me).wait_recv()
            passed[j].start()
        copy(0, sibling, me).wait_recv()
        for j, chip in enumerate(chips):
            copy(4 + j, (*chip, 1 - c), me).wait_recv()
        for cp in first + passed:
            cp.wait_send()
        mine.wait()

    vm = pl.BlockSpec(memory_space=pltpu.VMEM)
    return pl.pallas_call(
        body, out_shape=_sds((8, M, W), v.dtype), in_specs=[vm], out_specs=vm,
        scratch_shapes=[pltpu.SemaphoreType.DMA((7,)), pltpu.SemaphoreType.DMA((7,)), pltpu.SemaphoreType.DMA(())],
        compiler_params=pltpu.CompilerParams(vmem_limit_bytes=int(min(10 * M * W * 4 + (8 << 20), V7X_VMEM_BYTES - (8 << 20)))),
        name=name,
    )(v)


def _rows_for(n_elems, width, mult=16):
    rows = -(-n_elems // width)
    return -(-rows // mult) * mult


def _pack_rows(arrs, width, total_rows, dtype):
    parts = []
    used = 0
    for a in arrs:
        rows = _rows_for(a.size, width)
        flat = a.reshape(-1).astype(dtype)
        flat = jnp.pad(flat, (0, rows * width - a.size))
        parts.append(flat.reshape(rows, width))
        used += rows
    if total_rows > used:
        parts.append(jnp.zeros((total_rows - used, width), dtype))
    return jnp.concatenate(parts, axis=0)


def _unpack_rows(packed, shapes, width):
    outs = []
    r = 0
    lead = packed.shape[:-2]
    for shp in shapes:
        n = int(np.prod(shp))
        rows = _rows_for(n, width)
        blk = packed[..., r:r + rows, :].reshape(lead + (rows * width,))[..., :n]
        outs.append(blk.reshape(lead + tuple(shp)))
        r += rows
    return outs


def _cols_to_chips(w):
    K, N = w.shape
    return w.reshape(K, 4, N // 4).transpose(1, 0, 2)


def _chips_to_cols(w):
    n4, K, n = w.shape
    return w.transpose(1, 0, 2).reshape(K, n4 * n)


def _block_diag(m, gpb):
    G, A, B = m.shape
    nb = G // gpb
    eye = jnp.eye(gpb, dtype=m.dtype)
    t = m.reshape(nb, gpb, A, B)[:, :, :, None, :] * eye[None, :, None, :, None]
    return t.reshape(nb, gpb * A, gpb * B)


def _block_diag_extract(m, gpb, A, B):
    nb = m.shape[0]
    t = m.reshape(nb, gpb, A, gpb, B)
    idx = jnp.arange(gpb)
    d = t[:, idx, :, idx, :]
    return d.transpose(1, 0, 2, 3).reshape(nb * gpb, A, B)


def kernel(x, meta, g_mix, w_in, b_f, lam_re, lam_im, log_dt, b_re, b_im, c_re, c_im, d_skip, w_glu, w_attn_o, w_out, g_ffn, w_up, conv_w, conv_b, w_down, g_final, loss_target, m_meta, m_g_mix, m_w_in, m_b_f, m_lam_re, m_lam_im, m_log_dt, m_b_re, m_b_im, m_c_re, m_c_im, m_d_skip, m_w_glu, m_w_attn_o, m_w_out, m_g_ffn, m_w_up, m_conv_w, m_conv_b, m_w_down, m_g_final, v_meta, v_g_mix, v_w_in, v_b_f, v_lam_re, v_lam_im, v_log_dt, v_b_re, v_b_im, v_c_re, v_c_im, v_d_skip, v_w_glu, v_w_attn_o, v_w_out, v_g_ffn, v_w_up, v_conv_w, v_conv_b, v_w_down, v_g_final):
    args = dict(locals())
    L, D = x.shape[1], x.shape[2]
    NM = meta.shape[0]
    H = b_f.shape[1]
    DA = H * HEAD_DIM
    G, P, C = b_re.shape[1:]
    DS, GP = G * C, G * P
    DFF = conv_b.shape[1]
    PAD = (-NM) % LANES
    OFF = PAD + NM
    LP = OFF + L
    NZ = 3 * DA + DS + 2 * D
    U0, GA0, GB0 = 3 * DA, 3 * DA + DS, 3 * DA + DS + D
    NB = G // GROUPS_PER_BLOCK
    chip = 2 * lax.axis_index("x") + lax.axis_index("y")

    sharded = ["w_in", "w_glu", "w_attn_o", "w_out", "w_up", "conv_w", "w_down"]
    local = {n: args[n][0] for n in sharded}
    meta_bits = lax.bitcast_convert_type(meta, BF16)
    shard_shapes = [local[n].shape for n in sharded] + [meta_bits.shape]
    used_rows = sum(_rows_for(int(np.prod(s)), D) for s in shard_shapes)
    row_unit = 2 * RS_PIECES * 16
    R = -(-used_rows // row_unit) * row_unit
    packed_w = _pack_rows([local[n] for n in sharded] + [meta_bits], D, R, BF16)
    gathered = _gather_chips(packed_w, "gather_weights")
    parts = _unpack_rows(gathered, shard_shapes, D)
    wg = dict(zip(sharded, parts[:-1]))
    meta_full = _chips_to_cols(lax.bitcast_convert_type(parts[-1], F32))
    w_in_f = _chips_to_cols(wg["w_in"])
    w_f = jnp.pad(w_in_f[:, 3 * DA:3 * DA + H], ((0, 0), (0, LANES - H)))
    w_zf = jnp.concatenate([w_in_f[:, :3 * DA], w_in_f[:, 3 * DA + H:], w_f], axis=1)
    w_glu_f = _chips_to_cols(wg["w_glu"])
    w_ao_f = _chips_to_cols(wg["w_attn_o"])
    w_out_f = wg["w_out"].reshape(D, D)
    w_up_f = _chips_to_cols(wg["w_up"])
    conv_w_f = _chips_to_cols(wg["conv_w"]).astype(F32)
    w_down_f = wg["w_down"].reshape(DFF, D)

    col = lambda a: a.reshape(GP, 1)
    lr_c, li_c = col(lam_re[0]), col(lam_im[0])
    ldt_c = jnp.repeat(log_dt[0], P).reshape(GP, 1)
    br2, bi2 = b_re[0].reshape(GP, C), b_im[0].reshape(GP, C)
    a_re, a_im, bb_re, bb_im, pw_re, pw_im = _ssm_prep(lr_c, li_c, ldt_c, br2, bi2, "ssm_prep")
    S = GROUPS_PER_BLOCK * P
    CB = GROUPS_PER_BLOCK * C
    pw_r = pw_re.T.reshape(8, NB, S).transpose(1, 0, 2)
    pw_i = pw_im.T.reshape(8, NB, S).transpose(1, 0, 2)
    row8 = jnp.arange(8)[None, :, None]

    def masked_power(pw, k, keep):
        return jnp.where(keep, pw[:, k - 1][:, None, :], 0.0)

    coef = jnp.stack(
        [f(pw, k, row8 >= k) for k in (1, 2, 4) for f, pw in ((masked_power, pw_r), (masked_power, pw_i))]
        + [pw_r, pw_i], axis=1)
    coef_rev = jnp.stack(
        [f(pw, k, row8 < 8 - k) for k in (1, 2, 4) for f, pw in ((masked_power, pw_r), (masked_power, -pw_i))]
        + [pw_r[:, ::-1], -pw_i[:, ::-1]], axis=1)
    bd = lambda m: _block_diag(m, GROUPS_PER_BLOCK)
    bbr3, bbi3 = bb_re.reshape(G, P, C), bb_im.reshape(G, P, C)
    bbr_cs = bd(bbr3.transpose(0, 2, 1)).astype(BF16)
    bbi_cs = bd(bbi3.transpose(0, 2, 1)).astype(BF16)
    bbr_sc = bd(bbr3).astype(BF16)
    bbi_sc = bd(bbi3).astype(BF16)
    ccr_sc = bd(c_re[0].transpose(0, 2, 1)).astype(BF16)
    cci_sc = bd(c_im[0].transpose(0, 2, 1)).astype(BF16)
    ccr_cs = bd(c_re[0]).astype(BF16)
    cci_cs = bd(c_im[0]).astype(BF16)

    h0 = jnp.concatenate([jnp.zeros((PAD, D), F32), meta_full, x[0]], axis=0)
    n1 = _rms_fwd(h0, g_mix, "rms_mix")
    z = _mm(n1, w_zf, "nn", LP, NZ, D, BF16, "in_proj")
    fpre = _mm(n1, w_zf, "nn", LP, LANES, D, F32, "in_proj_f", b_off=(0, NZ))
    bf_pad = jnp.pad(b_f, ((0, 0), (0, LANES - H)))
    fcum = _fgate_fwd(fpre, bf_pad, PAD, "fgate_fwd")
    key_bias = jnp.where(jnp.arange(LP)[:, None] >= PAD, -fcum, NEG)
    bias_t = key_bias.T[:H].reshape(H, 1, LP)
    bias_b = jnp.broadcast_to(key_bias.T[:H][:, :, None], (H, LP, LANES))
    attn, attn_f32, lse_b = _attn_fwd(z, bias_t, H, PAD, "attn_fwd")
    ao = _mm(attn, w_ao_f, "nn", LP, D, DA, BF16, "attn_out")
    y, yg, hs_re, hs_im = _ssm_fwd(z, U0, coef, bbr_cs, bbi_cs, ccr_sc, cci_sc, d_skip, "ssm_fwd")
    yab = _mm(yg, w_glu_f, "nn", LP, 2 * D, DS, BF16, "glu_proj")
    merged = _merge_fwd(yab, z, ao, D, GA0, GB0, "merge_fwd")
    h1 = _mm(merged, w_out_f, "nn", LP, D, D, F32, "out_proj", res=h0)
    n2 = _rms_fwd(h1, g_ffn, "rms_ffn")
    gu = _mm(n2, w_up_f, "nn", LP, 2 * DFF, D, BF16, "up_proj")
    act = _convact_fwd(gu, conv_w_f, conv_b, DFF, "convact_fwd")
    h2 = _mm(act, w_down_f, "nn", LP, D, DFF, F32, "down_proj", res=h1, tk=1408)
    dh2, dg_final, loss_v = _final_loss(h2, g_final.reshape(1, D), loss_target[0], OFF, "final_loss")
    loss = lax.psum(loss_v[0, 0], ("x", "y", "c"))

    dact = _mm(dh2, w_down_f, "nt", LP, DFF, D, BF16, "down_bwd_x")
    dw_down = _mm(act, dh2, "tn", DFF, D, LP, F32, "down_bwd_w", tm=512, tk=640)
    dgc, du_ffn = _convact_bwd(dact, gu, conv_w_f, conv_b, DFF, "convact_bwd")
    dg_ffn_in, dconv_w, dconv_b = _conv_bwd(dgc, gu, conv_w_f, DFF, PAD, "conv_bwd")
    dn2 = _mm(dg_ffn_in, w_up_f, "nt", LP, D, DFF, F32, "up_bwd_x_g", tk=1408)
    dn2 = _mm(du_ffn, w_up_f, "nt", LP, D, DFF, F32, "up_bwd_x_u", res=dn2, b_off=(0, DFF), tk=1408)
    dw_up_g = _mm(n2, dg_ffn_in, "tn", D, DFF, LP, F32, "up_bwd_w_g", tm=1024, tn=512, tk=640)
    dw_up_u = _mm(n2, du_ffn, "tn", D, DFF, LP, F32, "up_bwd_w_u", tm=1024, tn=512, tk=640)
    dh1, dg_ffn = _rms_bwd(h1, g_ffn, dn2, dh2, "rms_ffn_bwd")

    dmerged = _mm(dh1, w_out_f, "nt", LP, D, D, F32, "out_bwd_x")
    dw_out = _mm(merged, dh1, "tn", D, D, LP, F32, "out_bwd_w", tm=1024, tk=640)
    dya, dyb, dga, dgb, dao = _merge_bwd(dmerged, yab, z, ao, D, GA0, GB0, "merge_bwd")
    dattn = _mm(dao, w_ao_f, "nt", LP, DA, D, BF16, "attn_out_bwd_x")
    dw_ao = _mm(attn, dao, "tn", DA, D, LP, F32, "attn_out_bwd_w", tm=1024, tk=640)
    dyg = _mm(dya, w_glu_f, "nt", LP, DS, D, F32, "glu_bwd_x_a")
    dyg = _mm(dyb, w_glu_f, "nt", LP, DS, D, F32, "glu_bwd_x_b", res=dyg, b_off=(0, D))
    dw_glu_a = _mm(yg, dya, "tn", DS, D, LP, F32, "glu_bwd_w_a", tm=1024, tk=640)
    dw_glu_b = _mm(yg, dyb, "tn", DS, D, LP, F32, "glu_bwd_w_b", tm=1024, tk=640)
    (du_ssm, dbbr_d, dbbi_d, dccr_d, dcci_d, dar_b, dai_b, dd_skip) = _ssm_bwd(
        z, U0, dyg, y, hs_re, hs_im, coef_rev, bbr_sc, bbi_sc, ccr_cs, cci_cs, d_skip, "ssm_bwd")
    delta_b = _attn_delta(dattn, attn_f32, H, "attn_delta")
    lse_t = lse_b[:, :, 0].reshape(H, 1, LP)
    delta_t = delta_b[:, :, 0].reshape(H, 1, LP)
    dq, dk, dv, dbias_b = _attn_bwd(z, dattn, lse_t, delta_t, bias_b, H, "attn_bwd")
    dF = jnp.pad(-dbias_b[:, :, 0].T, ((0, 0), (0, LANES - H)))
    dfpre, db_f = _fgate_bwd(dF, fpre, bf_pad, PAD, "fgate_bwd")
    dz = jnp.concatenate([dq, dk, dv, du_ssm, dga, dgb, dfpre.astype(BF16)], axis=1)
    dn1 = _mm(dz, w_zf, "nt", LP, D, NZ + LANES, F32, "in_bwd_x", tk=1664)
    dw_zf = _mm(n1, dz, "tn", D, NZ + LANES, LP, F32, "in_bwd_w", tm=1024, tn=640, tk=640)
    dh0, dg_mix = _rms_bwd(h0, g_mix, dn1, dh1, "rms_mix_bwd")
    grad_x = dh0[OFF:][None]
    dmeta_full = dh0[PAD:OFF]

    ext = lambda m, A, B: _block_diag_extract(m, GROUPS_PER_BLOCK, A, B)
    dbb_re = ext(dbbr_d, C, P).transpose(0, 2, 1).reshape(GP, C)
    dbb_im = ext(dbbi_d, C, P).transpose(0, 2, 1).reshape(GP, C)
    dc_re = ext(dccr_d, P, C).transpose(0, 2, 1)[None]
    dc_im = ext(dcci_d, P, C).transpose(0, 2, 1)[None]
    glr, gli, gldt, gbr, gbi = _ssm_prep_bwd(lr_c, li_c, ldt_c, br2, bi2, dar_b.reshape(GP, 1), dai_b.reshape(GP, 1),
                                             dbb_re, dbb_im, "ssm_prep_bwd")
    small_grads = {
        "g_mix": dg_mix, "b_f": db_f[:, :H], "lam_re": glr.reshape(1, G, P), "lam_im": gli.reshape(1, G, P),
        "log_dt": gldt.reshape(G, P).sum(axis=1)[None], "b_re": gbr.reshape(1, G, P, C), "b_im": gbi.reshape(1, G, P, C),
        "c_re": dc_re, "c_im": dc_im, "d_skip": dd_skip, "g_ffn": dg_ffn, "conv_b": dconv_b, "g_final": dg_final.reshape(D),
    }

    small = list(small_grads)
    small_shapes = [args[n].shape for n in small]
    srows = sum(_rows_for(int(np.prod(s)), LANES, 8) for s in small_shapes)
    srows_pad = -(-srows // 256) * 256

    def pack_small(arrs):
        parts_ = []
        for a in arrs:
            rows = _rows_for(a.size, LANES, 8)
            parts_.append(jnp.pad(a.reshape(-1), (0, rows * LANES - a.size)).reshape(rows, LANES))
        if srows_pad > srows:
            parts_.append(jnp.zeros((srows_pad - srows, LANES), F32))
        return jnp.concatenate(parts_, axis=0)

    def unpack_small(p):
        outs, r = [], 0
        for shp in small_shapes:
            n = int(np.prod(shp))
            rows = _rows_for(n, LANES, 8)
            outs.append(p[r:r + rows].reshape(-1)[:n].reshape(shp))
            r += rows
        return outs

    g_parts = _gather_all(pack_small([small_grads[n] for n in small]), "gather_small_grads")
    sm = _sum_adamw(g_parts, pack_small([args[n] for n in small]), pack_small([args["m_" + n] for n in small]),
                    pack_small([args["v_" + n] for n in small]), "small_adamw")
    sg, sd, smm, svv = (dict(zip(small, unpack_small(p))) for p in sm)

    dw_in_f = jnp.concatenate([dw_zf[:, :3 * DA], dw_zf[:, NZ:NZ + H], dw_zf[:, 3 * DA:NZ]], axis=1)
    full_grads = [
        _cols_to_chips(dw_in_f),
        _cols_to_chips(jnp.concatenate([dw_glu_a, dw_glu_b], axis=1)),
        _cols_to_chips(dw_ao),
        dw_out.reshape(4, D // 4, D),
        _cols_to_chips(jnp.concatenate([dw_up_g, dw_up_u], axis=1)),
        _cols_to_chips(dconv_w),
        dw_down.reshape(4, DFF // 4, D),
        _cols_to_chips(dmeta_full),
    ]
    grad_shapes = [a.shape[1:] for a in full_grads]
    packed_g = jnp.stack([_pack_rows([a[q] for a in full_grads], D, R, F32) for q in range(4)], axis=0)
    RP = R // RS_PIECES
    pieces = []
    for p_ in range(RS_PIECES):
        gr = packed_g[:, p_ * RP:(p_ + 1) * RP, :]
        got = _sibling_halves(gr, f"rs_sibling_{p_}")
        c_me = lax.axis_index("c")
        mine_half = lax.dynamic_slice_in_dim(gr, c_me * (RP // 2), RP // 2, axis=1)
        s_half = _add2(mine_half, got, f"rs_add_{p_}")
        contrib = _scatter_chips(s_half, f"rs_scatter_{p_}")
        final_half = _sum_parts(contrib, f"rs_sum_{p_}")
        pieces.append(_sibling_join(final_half, f"rs_join_{p_}"))
    shard_g = jnp.concatenate(pieces, axis=0)
    g_shards = _unpack_rows(shard_g, grad_shapes, D)
    big = sharded + ["meta"]
    bg, bd_, bm, bv = {}, {}, {}, {}
    for n, g in zip(big, g_shards):
        w_l = args[n]
        shp = w_l.shape
        two = (lambda a: a.reshape(shp[-2], shp[-1]))
        d_, m_, v_ = _adamw(two(w_l), two(g), two(args["m_" + n]), two(args["v_" + n]), "adamw_" + n)
        bg[n], bd_[n], bm[n], bv[n] = g.reshape(shp), d_.reshape(shp), m_.reshape(shp), v_.reshape(shp)

    order = ["meta", "g_mix", "w_in", "b_f", "lam_re", "lam_im", "log_dt", "b_re", "b_im", "c_re", "c_im", "d_skip",
             "w_glu", "w_attn_o", "w_out", "g_ffn", "w_up", "conv_w", "conv_b", "w_down", "g_final"]
    pick = lambda bigd, smalld, n: bigd[n] if n in bigd else smalld[n]
    outs = [loss, grad_x]
    for bigd, smalld in ((bg, sg), (bd_, sd), (bm, smm), (bv, svv)):
        outs += [pick(bigd, smalld, n) for n in order]
    return tuple(outs)
```

```python
import functools
import math

import jax
import jax.numpy as jnp
import numpy as np
from jax import lax
from jax.experimental import pallas as pl
from jax.experimental.pallas import tpu as pltpu

F32 = jnp.float32
BF16 = jnp.bfloat16
MESH = pl.DeviceIdType.MESH

EPS = 1e-6
HEAD_DIM = 128
LANES = 128
NEG = -1e30
GELU_C = math.sqrt(2.0 / math.pi)
GELU_A = 0.044715
ADAM_LR, ADAM_B1, ADAM_B2, ADAM_EPS, ADAM_WD, ADAM_STEP = 0.001, 0.9, 0.999, 1e-08, 0.01, 10
V7X_VMEM_BYTES = 64 << 20
GROUPS_PER_BLOCK = 8


def _tile(n, pref, mult=LANES):
    if n <= pref:
        return n
    t = (pref // mult) * mult
    while t >= mult:
        if n % t == 0:
            return t
        t -= mult
    raise ValueError(f"no tile for {n} <= {pref} (multiple of {mult})")


def _ctile(pref, *vals):
    g = 0
    for v in vals:
        g = math.gcd(g, v)
    return _tile(g, pref)


def _cparams(sem, est_bytes):
    limit = int(min(max(est_bytes * 1.25 + (4 << 20), 16 << 20), V7X_VMEM_BYTES - (8 << 20)))
    return pltpu.CompilerParams(dimension_semantics=sem, vmem_limit_bytes=limit)


def _sds(shape, dtype):
    return jax.ShapeDtypeStruct(tuple(shape), dtype)


def _sig(x):
    return 1.0 / (1.0 + jnp.exp(-x))


def _gelu(x):
    t = jnp.tanh(GELU_C * (x + GELU_A * x * x * x))
    return 0.5 * x * (1.0 + t)


def _gelu_grad(x):
    t = jnp.tanh(GELU_C * (x + GELU_A * x * x * x))
    return 0.5 * (1.0 + t) + 0.5 * x * (1.0 - t * t) * GELU_C * (1.0 + 3.0 * GELU_A * x * x)


def _mm(a, b, mode, M, N, K, out_dtype, name, *, res=None, a_off=(0, 0), b_off=(0, 0),
        tm=640, tn=1024, tk=2048, b_chips=None, out_chips=None, out_into=None):
    tm, tn, tk = _tile(M, tm, 8 if mode != "tn" else LANES), _tile(N, tn), _tile(K, tk, LANES if mode != "tn" else 8)
    if b_chips is not None and mode == "nt":
        tk = _ctile(tk, tk, b_chips)
    if b_chips is not None and mode != "nt":
        tn = _ctile(tn, tn, b_chips)
    if out_chips is not None:
        tn = _ctile(tn, tn, out_chips)
    nk = K // tk
    ar, ac = a_off
    br, bc = b_off
    if mode == "tn":
        assert ar % tk == 0 and ac % tm == 0
        a_spec = pl.BlockSpec((tk, tm), lambda i, j, k: (k + ar // tk, i + ac // tm))
        a_dims = 0
    else:
        assert ar % tm == 0 and ac % tk == 0
        a_spec = pl.BlockSpec((tm, tk), lambda i, j, k: (i + ar // tm, k + ac // tk))
        a_dims = 1
    if mode == "nt":
        assert br % tn == 0 and bc % tk == 0
        if b_chips is None:
            b_spec = pl.BlockSpec((tn, tk), lambda i, j, k: (j + br // tn, k + bc // tk))
        else:
            per = b_chips // tk
            b_spec = pl.BlockSpec((None, tn, tk), lambda i, j, k: ((k + bc // tk) // per, j + br // tn, (k + bc // tk) % per))
        b_dims = 1
    else:
        assert br % tk == 0 and bc % tn == 0
        if b_chips is None:
            b_spec = pl.BlockSpec((tk, tn), lambda i, j, k: (k + br // tk, j + bc // tn))
        else:
            per = b_chips // tn
            b_spec = pl.BlockSpec((None, tk, tn), lambda i, j, k: ((j + bc // tn) // per, k + br // tk, (j + bc // tn) % per))
        b_dims = 0
    dims = (((a_dims,), (b_dims,)), ((), ()))
    if out_chips is None:
        o_spec = pl.BlockSpec((tm, tn), lambda i, j, k: (i, j))
        o_shape = _sds((M, N), out_dtype)
    else:
        per_o = out_chips // tn
        chip0 = 0 if out_into is None else out_into[1]
        o_spec = pl.BlockSpec((None, tm, tn), lambda i, j, k: (chip0 + j // per_o, i, j % per_o))
        o_shape = _sds((N // out_chips if out_into is None else 4, M, out_chips), out_dtype)
    has_res = res is not None
    has_into = out_into is not None

    def body(*refs):
        if has_res:
            a_ref, b_ref, r_ref, o_ref = refs[:4]
        elif has_into:
            a_ref, b_ref, _, o_ref = refs[:4]
            r_ref = None
        else:
            a_ref, b_ref, o_ref = refs[:3]
            r_ref = None
        part = lax.dot_general(a_ref[...].astype(BF16), b_ref[...].astype(BF16), dims, preferred_element_type=F32)

        def finish(acc):
            if has_res:
                acc = r_ref[...] + acc
            o_ref[...] = acc.astype(o_ref.dtype)

        if nk == 1:
            finish(part)
        else:
            acc_ref = refs[-1]
            k = pl.program_id(2)

            @pl.when(k == 0)
            def _():
                acc_ref[...] = part

            @pl.when(k > 0)
            def _():
                acc_ref[...] += part

            @pl.when(k == nk - 1)
            def _():
                finish(acc_ref[...])

    in_specs = [a_spec, b_spec] + ([o_spec] if has_res else []) + ([pl.BlockSpec(memory_space=pl.ANY)] if has_into else [])
    args = (a, b) + ((res,) if has_res else ()) + ((out_into[0],) if has_into else ())
    isz = lambda x: jnp.dtype(x.dtype).itemsize
    est = 2 * (tm * tk * isz(a) + tk * tn * isz(b) + tm * tn * jnp.dtype(out_dtype).itemsize) + tm * tn * 4 * (2 + 2 * has_res)
    return pl.pallas_call(
        body, grid=(M // tm, N // tn, nk), in_specs=in_specs, out_specs=o_spec, out_shape=o_shape,
        scratch_shapes=[pltpu.VMEM((tm, tn), F32)] if nk > 1 else [],
        input_output_aliases={2: 0} if has_into else {},
        compiler_params=_cparams(("parallel", "parallel", "arbitrary"), est), name=name,
    )(*args)


def _rms_fwd(h, g, name):
    LP, D = h.shape
    tm = _tile(LP, 640, 8)

    def body(h_ref, g_ref, o_ref):
        x = h_ref[...]
        r = lax.rsqrt(jnp.mean(x * x, axis=-1, keepdims=True) + EPS)
        o_ref[...] = (x * r * g_ref[...]).astype(o_ref.dtype)

    row = pl.BlockSpec((tm, D), lambda i: (i, 0))
    return pl.pallas_call(
        body, grid=(LP // tm,), in_specs=[row, pl.BlockSpec((1, D), lambda i: (0, 0))], out_specs=row,
        out_shape=_sds((LP, D), BF16), compiler_params=_cparams(("parallel",), 2 * tm * D * 6), name=name,
    )(h, g)


def _rms_bwd(h, g, dn, dres, name):
    LP, D = h.shape
    tm = _tile(LP, 320, 8)
    nt = LP // tm

    def body(h_ref, g_ref, dn_ref, dres_ref, dh_ref, dg_ref):
        i = pl.program_id(0)
        x = h_ref[...]
        r = lax.rsqrt(jnp.mean(x * x, axis=-1, keepdims=True) + EPS)
        xh = x * r
        dn_v = dn_ref[...]
        dxh = dn_v * g_ref[...]
        dh_ref[...] = dres_ref[...] + r * (dxh - xh * jnp.mean(dxh * xh, axis=-1, keepdims=True))
        part = jnp.sum(dn_v * xh, axis=0, keepdims=True)

        @pl.when(i == 0)
        def _():
            dg_ref[...] = part

        @pl.when(i > 0)
        def _():
            dg_ref[...] += part

    row = pl.BlockSpec((tm, D), lambda i: (i, 0))
    vec = pl.BlockSpec((1, D), lambda i: (0, 0))
    return pl.pallas_call(
        body, grid=(nt,), in_specs=[row, vec, row, row], out_specs=[row, vec],
        out_shape=[_sds((LP, D), F32), _sds((1, D), F32)],
        compiler_params=_cparams(("arbitrary",), 2 * 4 * tm * D * 4), name=name,
    )(h, g, dn, dres)


def _final_loss(h, g, tgt, off, name):
    LP, D = h.shape
    tm = LANES
    assert off % tm == 0
    ob = off // tm
    nt = LP // tm

    def body(h_ref, g_ref, t_ref, dh_ref, dg_ref, loss_ref):
        i = pl.program_id(0)
        x = h_ref[...]
        r = lax.rsqrt(jnp.mean(x * x, axis=-1, keepdims=True) + EPS)
        xh = x * r
        gv = g_ref[...]
        e = xh * gv - t_ref[...]
        valid = i >= ob
        dy = jnp.where(valid, e * (1.0 / D), 0.0)
        lpart = jnp.where(valid, 0.5 * jnp.sum(jnp.mean(e * e, axis=-1, keepdims=True), axis=0, keepdims=True), 0.0)
        dxh = dy * gv
        dh_ref[...] = r * (dxh - xh * jnp.mean(dxh * xh, axis=-1, keepdims=True))
        gpart = jnp.sum(dy * xh, axis=0, keepdims=True)

        @pl.when(i == 0)
        def _():
            dg_ref[...] = gpart
            loss_ref[...] = jnp.broadcast_to(lpart, loss_ref.shape)

        @pl.when(i > 0)
        def _():
            dg_ref[...] += gpart
            loss_ref[...] += jnp.broadcast_to(lpart, loss_ref.shape)

    row = pl.BlockSpec((tm, D), lambda i: (i, 0))
    vec = pl.BlockSpec((1, D), lambda i: (0, 0))
    return pl.pallas_call(
        body, grid=(nt,),
        in_specs=[row, vec, pl.BlockSpec((tm, D), lambda i: (jnp.maximum(i - ob, 0), 0))],
        out_specs=[row, vec, pl.BlockSpec((1, LANES), lambda i: (0, 0))],
        out_shape=[_sds((LP, D), F32), _sds((1, D), F32), _sds((1, LANES), F32)],
        compiler_params=_cparams(("arbitrary",), 2 * 3 * tm * D * 4), name=name,
    )(h, g, tgt)


def _fgate_fwd(fpre, bias, pad, name):
    LP, W = fpre.shape

    def body(f_ref, b_ref, o_ref):
        row8 = lax.broadcasted_iota(jnp.int32, (8, W), 0)
        bv = b_ref[...]

        def step(g, carry):
            r0 = pl.multiple_of(g * 8, 8)
            x = f_ref[pl.ds(r0, 8), :] + bv
            lf = jnp.minimum(x, 0.0) - jnp.log(1.0 + jnp.exp(-jnp.abs(x)))
            lf = jnp.where(r0 + row8 >= pad, lf, 0.0)
            for k in (1, 2, 4):
                lf = lf + jnp.where(row8 >= k, pltpu.roll(lf, k, 0), 0.0)
            lf = lf + carry
            o_ref[pl.ds(r0, 8), :] = lf
            return jnp.broadcast_to(lf[7:8, :], (8, W))

        lax.fori_loop(0, LP // 8, step, jnp.zeros((8, W), F32))

    return pl.pallas_call(
        body, out_shape=_sds((LP, W), F32),
        compiler_params=_cparams(None, 3 * LP * W * 4), name=name,
    )(fpre, bias)


def _fgate_bwd(dF, fpre, bias, pad, name):
    LP, W = fpre.shape
    ng = LP // 8

    def body(d_ref, f_ref, b_ref, o_ref, db_ref):
        row8 = lax.broadcasted_iota(jnp.int32, (8, W), 0)
        bv = b_ref[...]

        def step(t, carry):
            run, acc = carry
            g = ng - 1 - t
            r0 = pl.multiple_of(g * 8, 8)
            x = d_ref[pl.ds(r0, 8), :]
            for k in (1, 2, 4):
                x = x + jnp.where(row8 < 8 - k, pltpu.roll(x, 8 - k, 0), 0.0)
            x = x + run
            df = x * _sig(-(f_ref[pl.ds(r0, 8), :] + bv))
            df = jnp.where(r0 + row8 >= pad, df, 0.0)
            o_ref[pl.ds(r0, 8), :] = df
            return jnp.broadcast_to(x[0:1, :], (8, W)), acc + df

        _, acc = lax.fori_loop(0, ng, step, (jnp.zeros((8, W), F32), jnp.zeros((8, W), F32)))
        db_ref[...] = jnp.sum(acc, axis=0, keepdims=True)

    return pl.pallas_call(
        body, out_shape=[_sds((LP, W), F32), _sds((1, W), F32)],
        compiler_params=_cparams(None, 4 * LP * W * 4), name=name,
    )(dF, fpre, bias)


def _attn_fwd(z, bias_t, H, pad, name):
    LP = z.shape[0]
    BQ = BK = _tile(LP, 640)
    scale = HEAD_DIM ** -0.5
    NT = (((1,), (1,)), ((), ()))

    def body(q_ref, k_ref, v_ref, b_ref, o_ref, of_ref, lse_ref):
        qi = pl.program_id(1)
        q = q_ref[...]

        def tile(kb, carry, masked):
            m, l, acc = carry
            k0 = pl.multiple_of(kb * BK, BK)
            s = lax.dot_general(q, k_ref[pl.ds(k0, BK), :], NT, preferred_element_type=F32) * scale
            s = s + b_ref[:, pl.ds(k0, BK)]
            if masked:
                ri = lax.broadcasted_iota(jnp.int32, (BQ, BK), 0)
                ci = lax.broadcasted_iota(jnp.int32, (BQ, BK), 1)
                s = jnp.where(ri >= ci, s, NEG)
            mn = jnp.maximum(m, jnp.max(s, axis=-1, keepdims=True))
            p = jnp.exp(s - mn)
            alpha = jnp.exp(m - mn)
            l = alpha * l + jnp.sum(p, axis=-1, keepdims=True)
            vk = v_ref[pl.ds(k0, BK), :]
            p_hi = p.astype(BF16)
            p_lo = (p - p_hi.astype(F32)).astype(BF16)
            pv = jnp.dot(p_hi, vk, preferred_element_type=F32) + jnp.dot(p_lo, vk, preferred_element_type=F32)
            return mn, l, alpha * acc + pv

        carry = (jnp.full((BQ, 1), NEG, F32), jnp.zeros((BQ, 1), F32), jnp.zeros((BQ, HEAD_DIM), F32))
        carry = lax.fori_loop(0, qi, lambda kb, c: tile(kb, c, False), carry)
        m, l, acc = tile(qi, carry, True)
        rows = qi * BQ + lax.broadcasted_iota(jnp.int32, (BQ, 1), 0)
        o = jnp.where(rows >= pad, acc / l, 0.0)
        o_ref[...] = o.astype(o_ref.dtype)
        of_ref[...] = o
        lse_ref[...] = jnp.broadcast_to(m + jnp.log(l), (BQ, LANES))

    in_specs = [
        pl.BlockSpec((BQ, HEAD_DIM), lambda h, i: (i, h)),
        pl.BlockSpec((LP, HEAD_DIM), lambda h, i: (0, H + h)),
        pl.BlockSpec((LP, HEAD_DIM), lambda h, i: (0, 2 * H + h)),
        pl.BlockSpec((None, 1, LP), lambda h, i: (h, 0, 0)),
    ]
    out_specs = [
        pl.BlockSpec((BQ, HEAD_DIM), lambda h, i: (i, h)),
        pl.BlockSpec((BQ, HEAD_DIM), lambda h, i: (i, h)),
        pl.BlockSpec((None, BQ, LANES), lambda h, i: (h, i, 0)),
    ]
    est = 2 * (2 * LP * HEAD_DIM * 2 + 8 * LP * 4) + 20 * BQ * LANES * 4 + 8 * BQ * BK * 4
    return pl.pallas_call(
        body, grid=(H, LP // BQ), in_specs=in_specs, out_specs=out_specs,
        out_shape=[_sds((LP, H * HEAD_DIM), BF16), _sds((LP, H * HEAD_DIM), F32), _sds((H, LP, LANES), F32)],
        compiler_params=_cparams(("parallel", "arbitrary"), est), name=name,
    )(z, z, z, bias_t)


def _attn_delta(do, o, H, name):
    LP = do.shape[0]
    tm = _tile(LP, 640, 8)

    def body(do_ref, o_ref, d_ref):
        d = jnp.sum(do_ref[...].astype(F32) * o_ref[...].astype(F32), axis=-1, keepdims=True)
        d_ref[...] = jnp.broadcast_to(d, (tm, LANES))

    blk = pl.BlockSpec((tm, HEAD_DIM), lambda h, i: (i, h))
    return pl.pallas_call(
        body, grid=(H, LP // tm), in_specs=[blk, blk],
        out_specs=pl.BlockSpec((None, tm, LANES), lambda h, i: (h, i, 0)),
        out_shape=_sds((H, LP, LANES), F32),
        compiler_params=_cparams(("parallel", "parallel"), 8 * tm * LANES * 4), name=name,
    )(do, o)


def _attn_bwd(z, do, lse_t, delta_t, bias_b, H, name):
    LP = z.shape[0]
    BK = BQ = _tile(LP, 640)
    nk = nq = LP // BK
    scale = HEAD_DIM ** -0.5
    NT = (((1,), (1,)), ((), ()))
    TN = (((0,), (0,)), ((), ()))

    def body(q_ref, k_ref, v_ref, do_ref, lse_ref, dl_ref, b_ref, dq_ref, dk_ref, dv_ref, db_ref, dq_acc):
        kj = pl.program_id(1)

        @pl.when(kj == 0)
        def _():
            dq_acc[...] = jnp.zeros_like(dq_acc)

        k = k_ref[...]
        v = v_ref[...]
        bcol = b_ref[:, 0:1]

        def tile(qc, carry, masked):
            dk, dv, db = carry
            q0 = pl.multiple_of(qc * BQ, BQ)
            q = q_ref[pl.ds(q0, BQ), :]
            dout = do_ref[pl.ds(q0, BQ), :]
            st = lax.dot_general(k, q, NT, preferred_element_type=F32) * scale + bcol
            if masked:
                ri = lax.broadcasted_iota(jnp.int32, (BK, BQ), 0)
                ci = lax.broadcasted_iota(jnp.int32, (BK, BQ), 1)
                st = jnp.where(ci >= ri, st, NEG)
            pt = jnp.exp(st - lse_ref[:, pl.ds(q0, BQ)])
            dv = dv + jnp.dot(pt.astype(BF16), dout, preferred_element_type=F32)
            dpt = lax.dot_general(v, dout, NT, preferred_element_type=F32)
            dst = pt * (dpt - dl_ref[:, pl.ds(q0, BQ)])
            db = db + jnp.sum(dst, axis=-1, keepdims=True)
            dsb = (dst * scale).astype(BF16)
            dk = dk + jnp.dot(dsb, q, preferred_element_type=F32)
            dq_acc[pl.ds(q0, BQ), :] += lax.dot_general(dsb, k, TN, preferred_element_type=F32)
            return dk, dv, db

        carry = (jnp.zeros((BK, HEAD_DIM), F32), jnp.zeros((BK, HEAD_DIM), F32), jnp.zeros((BK, 1), F32))
        carry = tile(kj, carry, True)
        dk, dv, db = lax.fori_loop(kj + 1, nq, lambda qc, c: tile(qc, c, False), carry)
        dk_ref[...] = dk.astype(dk_ref.dtype)
        dv_ref[...] = dv.astype(dv_ref.dtype)
        db_ref[...] = jnp.broadcast_to(db, (BK, LANES))

        @pl.when(kj == nk - 1)
        def _():
            dq_ref[...] = dq_acc[...].astype(dq_ref.dtype)

    full = lambda c0: pl.BlockSpec((LP, HEAD_DIM), lambda h, j: (0, c0 + h))
    blk = lambda c0: pl.BlockSpec((BK, HEAD_DIM), lambda h, j: (j, c0 + h))
    vec = pl.BlockSpec((None, 1, LP), lambda h, j: (h, 0, 0))
    in_specs = [full(0), blk(H), blk(2 * H), full(0), vec, vec, pl.BlockSpec((None, BK, LANES), lambda h, j: (h, j, 0))]
    out_specs = [full(0), blk(0), blk(0), pl.BlockSpec((None, BK, LANES), lambda h, j: (h, j, 0))]
    est = 2 * (3 * LP * HEAD_DIM * 2 + 16 * LP * 4) + LP * HEAD_DIM * 4 + 24 * BK * LANES * 4 + 10 * BK * BQ * 4
    return pl.pallas_call(
        body, grid=(H, nk), in_specs=in_specs, out_specs=out_specs,
        out_shape=[_sds((LP, H * HEAD_DIM), BF16)] * 3 + [_sds((H, LP, LANES), F32)],
        scratch_shapes=[pltpu.VMEM((LP, HEAD_DIM), F32)],
        compiler_params=_cparams(("parallel", "arbitrary"), est), name=name,
    )(z, z, z, do, lse_t, delta_t, bias_b)


def _ssm_disc(lr, li, ldt, br, bi):
    dt = jnp.exp(ldt)
    mag = jnp.exp(lr * dt)
    a_re = mag * jnp.cos(li * dt)
    a_im = mag * jnp.sin(li * dt)
    den = lr * lr + li * li
    nr = a_re - 1.0
    z_re = (nr * lr + a_im * li) / den
    z_im = (a_im * lr - nr * li) / den
    return a_re, a_im, z_re * br - z_im * bi, z_re * bi + z_im * br


def _ssm_prep(lr, li, ldt, br, bi, name):
    GP, C = br.shape

    def body(lr_ref, li_ref, ldt_ref, br_ref, bi_ref, ar_ref, ai_ref, bbr_ref, bbi_ref, pr_ref, pi_ref):
        a_re, a_im, bb_re, bb_im = _ssm_disc(lr_ref[...], li_ref[...], ldt_ref[...], br_ref[...], bi_ref[...])
        ar_ref[...] = a_re
        ai_ref[...] = a_im
        bbr_ref[...] = bb_re
        bbi_ref[...] = bb_im
        lane = lax.broadcasted_iota(jnp.int32, (tg, 8), 1)
        pr, pi_ = a_re, a_im
        accr = jnp.zeros((tg, 8), F32)
        acci = jnp.zeros((tg, 8), F32)
        for k in range(8):
            accr = jnp.where(lane == k, pr, accr)
            acci = jnp.where(lane == k, pi_, acci)
            pr, pi_ = pr * a_re - pi_ * a_im, pr * a_im + pi_ * a_re
        pr_ref[...] = accr
        pi_ref[...] = acci

    tg = _tile(GP, 512, 8)
    blk = lambda w: pl.BlockSpec((tg, w), lambda i: (i, 0))
    col = _sds((GP, 1), F32)
    return pl.pallas_call(
        body, grid=(GP // tg,), in_specs=[blk(1), blk(1), blk(1), blk(C), blk(C)],
        out_specs=[blk(1), blk(1), blk(C), blk(C), blk(8), blk(8)],
        out_shape=[col, col, _sds((GP, C), F32), _sds((GP, C), F32), _sds((GP, 8), F32), _sds((GP, 8), F32)],
        compiler_params=_cparams(("parallel",), 48 * tg * LANES * 4), name=name,
    )(lr, li, ldt, br, bi)


def _ssm_prep_bwd(lr, li, ldt, br, bi, dar, dai, dbbr, dbbi, name):
    GP, C = br.shape

    def body(lr_ref, li_ref, ldt_ref, br_ref, bi_ref, dar_ref, dai_ref, dbbr_ref, dbbi_ref,
             glr_ref, gli_ref, gldt_ref, gbr_ref, gbi_ref):
        _, vjp = jax.vjp(_ssm_disc, lr_ref[...], li_ref[...], ldt_ref[...], br_ref[...], bi_ref[...])
        glr, gli, gldt, gbr, gbi = vjp((dar_ref[...], dai_ref[...], dbbr_ref[...], dbbi_ref[...]))
        glr_ref[...] = glr
        gli_ref[...] = gli
        gldt_ref[...] = gldt
        gbr_ref[...] = gbr
        gbi_ref[...] = gbi

    tg = _tile(GP, 512, 8)
    blk = lambda w: pl.BlockSpec((tg, w), lambda i: (i, 0))
    col = _sds((GP, 1), F32)
    return pl.pallas_call(
        body, grid=(GP // tg,), in_specs=[blk(1), blk(1), blk(1), blk(C), blk(C), blk(1), blk(1), blk(C), blk(C)],
        out_specs=[blk(1), blk(1), blk(1), blk(C), blk(C)],
        out_shape=[col, col, col, _sds((GP, C), F32), _sds((GP, C), F32)],
        compiler_params=_cparams(("parallel",), 96 * tg * LANES * 4), name=name,
    )(lr, li, ldt, br, bi, dar, dai, dbbr, dbbi)


def _cmul_add(xr, xi, mr, mi, sr, si):
    return xr + mr * sr - mi * si, xi + mr * si + mi * sr


def _ssm_fwd(z, u_col0, coef, bbr, bbi, ccr, cci, dskip, name):
    LP = z.shape[0]
    NB, CB, S = bbr.shape
    TS = _tile(LP, 640, 8)
    nt = LP // TS

    def body(u_ref, coef_ref, bbr_ref, bbi_ref, ccr_ref, cci_ref, ds_ref, y_ref, yg_ref, hr_ref, hi_ref, bur, bui, carry):
        i = pl.program_id(1)

        @pl.when(i == 0)
        def _():
            carry[...] = jnp.zeros_like(carry)

        u = u_ref[...]
        bur[...] = jnp.dot(u, bbr_ref[...], preferred_element_type=F32)
        bui[...] = jnp.dot(u, bbi_ref[...], preferred_element_type=F32)

        def step(g, c):
            cr, ci = c
            r0 = pl.multiple_of(g * 8, 8)
            xr = bur[pl.ds(r0, 8), :]
            xi = bui[pl.ds(r0, 8), :]
            for n, k in enumerate((1, 2, 4)):
                xr, xi = _cmul_add(xr, xi, coef_ref[2 * n], coef_ref[2 * n + 1], pltpu.roll(xr, k, 0), pltpu.roll(xi, k, 0))
            xr, xi = _cmul_add(xr, xi, coef_ref[6], coef_ref[7], cr, ci)
            hr_ref[pl.ds(r0, 8), :] = xr
            hi_ref[pl.ds(r0, 8), :] = xi
            return jnp.broadcast_to(xr[7:8, :], (8, S)), jnp.broadcast_to(xi[7:8, :], (8, S))

        cr, ci = lax.fori_loop(0, TS // 8, step, (carry[0], carry[1]))
        carry[0] = cr
        carry[1] = ci
        y = (jnp.dot(hr_ref[...].astype(BF16), ccr_ref[...], preferred_element_type=F32)
             - jnp.dot(hi_ref[...].astype(BF16), cci_ref[...], preferred_element_type=F32)
             + ds_ref[...] * u.astype(F32))
        y_ref[...] = y
        yg_ref[...] = _gelu(y).astype(yg_ref.dtype)

    ucb = u_col0 // CB
    in_specs = [
        pl.BlockSpec((TS, CB), lambda j, i: (i, ucb + j)),
        pl.BlockSpec((None, 8, 8, S), lambda j, i: (j, 0, 0, 0)),
        pl.BlockSpec((None, CB, S), lambda j, i: (j, 0, 0)),
        pl.BlockSpec((None, CB, S), lambda j, i: (j, 0, 0)),
        pl.BlockSpec((None, S, CB), lambda j, i: (j, 0, 0)),
        pl.BlockSpec((None, S, CB), lambda j, i: (j, 0, 0)),
        pl.BlockSpec((1, CB), lambda j, i: (0, j)),
    ]
    yb = pl.BlockSpec((TS, CB), lambda j, i: (i, j))
    hb = pl.BlockSpec((TS, S), lambda j, i: (i, j))
    est = 2 * (2 * TS * S * 4 + 3 * TS * CB * 4 + 8 * 8 * S * 4 + 4 * CB * S * 2) + 3 * TS * S * 4
    return pl.pallas_call(
        body, grid=(NB, nt), in_specs=in_specs, out_specs=[yb, yb, hb, hb],
        out_shape=[_sds((LP, NB * CB), F32), _sds((LP, NB * CB), BF16), _sds((LP, NB * S), F32), _sds((LP, NB * S), F32)],
        scratch_shapes=[pltpu.VMEM((TS, S), F32), pltpu.VMEM((TS, S), F32), pltpu.VMEM((2, 8, S), F32)],
        compiler_params=_cparams(("parallel", "arbitrary"), est), name=name,
    )(z, coef, bbr, bbi, ccr, cci, dskip)


def _ssm_bwd(z, u_col0, dyg, y, hr, hi, coef_rev, bbr_t, bbi_t, ccr_t, cci_t, dskip, name):
    LP = z.shape[0]
    NB, S, CB = bbr_t.shape
    TS = _tile(LP, 640, 8)
    nt = LP // TS
    ng = TS // 8
    TN = (((0,), (0,)), ((), ()))

    def body(u_ref, dyg_ref, y_ref, hr_ref, hi_ref, tr_ref, ti_ref, coef_ref, bbr_ref, bbi_ref, ccr_ref, cci_ref, ds_ref,
             du_ref, dbbr_ref, dbbi_ref, dccr_ref, dcci_ref, dar_ref, dai_ref, dd_ref,
             gr, gi, carry, acc_bbr, acc_bbi, acc_ccr, acc_cci, acc_a, acc_d):
        i = pl.program_id(1)

        @pl.when(i == 0)
        def _():
            for ref in (carry, acc_bbr, acc_bbi, acc_ccr, acc_cci, acc_a, acc_d):
                ref[...] = jnp.zeros_like(ref)

        u = u_ref[...]
        dy = dyg_ref[...] * _gelu_grad(y_ref[...])
        dyb = dy.astype(BF16)
        gr[...] = jnp.dot(dyb, ccr_ref[...], preferred_element_type=F32)
        gi[...] = -jnp.dot(dyb, cci_ref[...], preferred_element_type=F32)
        row8 = lax.broadcasted_iota(jnp.int32, (8, S), 0)
        first_chunk = i == nt - 1
        tail_r = jnp.where(first_chunk, 0.0, tr_ref[...])
        tail_i = jnp.where(first_chunk, 0.0, ti_ref[...])

        def step(t, c):
            cr, ci, sar, sai = c
            g = ng - 1 - t
            r0 = pl.multiple_of(g * 8, 8)
            xr = gr[pl.ds(r0, 8), :]
            xi = gi[pl.ds(r0, 8), :]
            for n, k in enumerate((1, 2, 4)):
                xr, xi = _cmul_add(xr, xi, coef_ref[2 * n], coef_ref[2 * n + 1], pltpu.roll(xr, 8 - k, 0), pltpu.roll(xi, 8 - k, 0))
            xr, xi = _cmul_add(xr, xi, coef_ref[6], coef_ref[7], cr, ci)
            gr[pl.ds(r0, 8), :] = xr
            gi[pl.ds(r0, 8), :] = xi
            p0 = pl.multiple_of(jnp.maximum(g - 1, 0) * 8, 8)
            pr = jnp.where(g > 0, hr_ref[pl.ds(p0, 8), :], tail_r)
            pi_ = jnp.where(g > 0, hi_ref[pl.ds(p0, 8), :], tail_i)
            hpr = pltpu.roll(jnp.where(row8 == 7, pr, hr_ref[pl.ds(r0, 8), :]), 1, 0)
            hpi = pltpu.roll(jnp.where(row8 == 7, pi_, hi_ref[pl.ds(r0, 8), :]), 1, 0)
            sar = sar + xr * hpr + xi * hpi
            sai = sai + xi * hpr - xr * hpi
            return jnp.broadcast_to(xr[0:1, :], (8, S)), jnp.broadcast_to(xi[0:1, :], (8, S)), sar, sai

        zero = jnp.zeros((8, S), F32)
        cr, ci, sar, sai = lax.fori_loop(0, ng, step, (carry[0], carry[1], zero, zero))
        carry[0] = cr
        carry[1] = ci
        acc_a[0] += sar
        acc_a[1] += sai
        grb = gr[...].astype(BF16)
        gib = gi[...].astype(BF16)
        du = (jnp.dot(grb, bbr_ref[...], preferred_element_type=F32) + jnp.dot(gib, bbi_ref[...], preferred_element_type=F32)
              + ds_ref[...] * dy)
        du_ref[...] = du.astype(du_ref.dtype)
        acc_bbr[...] += lax.dot_general(u, grb, TN, preferred_element_type=F32)
        acc_bbi[...] += lax.dot_general(u, gib, TN, preferred_element_type=F32)
        acc_ccr[...] += lax.dot_general(hr_ref[...].astype(BF16), dyb, TN, preferred_element_type=F32)
        acc_cci[...] -= lax.dot_general(hi_ref[...].astype(BF16), dyb, TN, preferred_element_type=F32)
        acc_d[...] += jnp.sum(dy * u.astype(F32), axis=0, keepdims=True)

        @pl.when(i == nt - 1)
        def _():
            dbbr_ref[...] = acc_bbr[...]
            dbbi_ref[...] = acc_bbi[...]
            dccr_ref[...] = acc_ccr[...]
            dcci_ref[...] = acc_cci[...]
            dar_ref[...] = jnp.sum(acc_a[0], axis=0, keepdims=True)
            dai_ref[...] = jnp.sum(acc_a[1], axis=0, keepdims=True)
            dd_ref[...] = acc_d[...]

    ucb = u_col0 // CB
    rev = lambda i: nt - 1 - i
    tail = lambda j, i: (jnp.maximum(rev(i) * ng - 1, 0), j)
    yb = pl.BlockSpec((TS, CB), lambda j, i: (rev(i), j))
    hb = pl.BlockSpec((TS, S), lambda j, i: (rev(i), j))
    in_specs = [
        pl.BlockSpec((TS, CB), lambda j, i: (rev(i), ucb + j)), yb, yb, hb, hb,
        pl.BlockSpec((8, S), tail), pl.BlockSpec((8, S), tail),
        pl.BlockSpec((None, 8, 8, S), lambda j, i: (j, 0, 0, 0)),
        pl.BlockSpec((None, S, CB), lambda j, i: (j, 0, 0)),
        pl.BlockSpec((None, S, CB), lambda j, i: (j, 0, 0)),
        pl.BlockSpec((None, CB, S), lambda j, i: (j, 0, 0)),
        pl.BlockSpec((None, CB, S), lambda j, i: (j, 0, 0)),
        pl.BlockSpec((1, CB), lambda j, i: (0, j)),
    ]
    mat_cs = pl.BlockSpec((None, CB, S), lambda j, i: (j, 0, 0))
    mat_sc = pl.BlockSpec((None, S, CB), lambda j, i: (j, 0, 0))
    vec_s = pl.BlockSpec((None, 1, S), lambda j, i: (j, 0, 0))
    out_specs = [yb, mat_cs, mat_cs, mat_sc, mat_sc, vec_s, vec_s, pl.BlockSpec((1, CB), lambda j, i: (0, j))]
    out_shape = [_sds((LP, NB * CB), BF16), _sds((NB, CB, S), F32), _sds((NB, CB, S), F32), _sds((NB, S, CB), F32),
                 _sds((NB, S, CB), F32), _sds((NB, 1, S), F32), _sds((NB, 1, S), F32), _sds((1, NB * CB), F32)]
    scratch = [pltpu.VMEM((TS, S), F32), pltpu.VMEM((TS, S), F32), pltpu.VMEM((2, 8, S), F32),
               pltpu.VMEM((CB, S), F32), pltpu.VMEM((CB, S), F32), pltpu.VMEM((S, CB), F32), pltpu.VMEM((S, CB), F32),
               pltpu.VMEM((2, 8, S), F32), pltpu.VMEM((1, CB), F32)]
    est = 2 * (2 * TS * S * 4 + 4 * TS * CB * 4 + 8 * 8 * S * 4 + 12 * CB * S * 4) + 4 * TS * S * 4
    return pl.pallas_call(
        body, grid=(NB, nt), in_specs=in_specs, out_specs=out_specs, out_shape=out_shape, scratch_shapes=scratch,
        compiler_params=_cparams(("parallel", "arbitrary"), est), name=name,
    )(z, dyg, y, hr, hi, hr, hi, coef_rev, bbr_t, bbi_t, ccr_t, cci_t, dskip)


def _merge_fwd(yab, z, ao, D, ga0, gb0, name):
    LP = z.shape[0]
    tm = _tile(LP, 640, 8)
    tn = _ctile(512, D, ga0, gb0)
    nj = D // tn

    def body(ya_ref, yb_ref, ga_ref, gb_ref, ao_ref, o_ref):
        f = lambda r: r[...].astype(F32)
        ssm = f(ya_ref) * _sig(f(yb_ref))
        o_ref[...] = (_sig(f(ga_ref)) * ssm + _sig(f(gb_ref)) * f(ao_ref)).astype(o_ref.dtype)

    blk = lambda c0: pl.BlockSpec((tm, tn), lambda i, j: (i, c0 // tn + j))
    return pl.pallas_call(
        body, grid=(LP // tm, nj), in_specs=[blk(0), blk(D), blk(ga0), blk(gb0), blk(0)], out_specs=blk(0),
        out_shape=_sds((LP, D), BF16), compiler_params=_cparams(("parallel", "parallel"), 2 * 6 * tm * tn * 4), name=name,
    )(yab, yab, z, z, ao)


def _merge_bwd(dm, yab, z, ao, D, ga0, gb0, name):
    LP = z.shape[0]
    tm = _tile(LP, 640, 8)
    tn = _ctile(512, D, ga0, gb0)
    nj = D // tn

    def body(dm_ref, ya_ref, yb_ref, ga_ref, gb_ref, ao_ref, dya_ref, dyb_ref, dga_ref, dgb_ref, dao_ref):
        f = lambda r: r[...].astype(F32)
        dmv, ya, ao_v = f(dm_ref), f(ya_ref), f(ao_ref)
        sa, sb, sy = _sig(f(ga_ref)), _sig(f(gb_ref)), _sig(f(yb_ref))
        t = dmv * sa
        dya_ref[...] = (t * sy).astype(BF16)
        dyb_ref[...] = (t * ya * sy * (1.0 - sy)).astype(BF16)
        dga_ref[...] = (dmv * (ya * sy) * sa * (1.0 - sa)).astype(BF16)
        dgb_ref[...] = (dmv * ao_v * sb * (1.0 - sb)).astype(BF16)
        dao_ref[...] = (dmv * sb).astype(BF16)

    blk = lambda c0: pl.BlockSpec((tm, tn), lambda i, j: (i, c0 // tn + j))
    return pl.pallas_call(
        body, grid=(LP // tm, nj), in_specs=[blk(0), blk(0), blk(D), blk(ga0), blk(gb0), blk(0)], out_specs=[blk(0)] * 5,
        out_shape=[_sds((LP, D), BF16)] * 5, compiler_params=_cparams(("parallel", "parallel"), 2 * 11 * tm * tn * 4), name=name,
    )(dm, yab, yab, z, z, ao)


def _shift_down(x, halo, k, row8):
    s = pltpu.roll(x, k, 0)
    top = jnp.where(row8 < k, pltpu.roll(halo, k, 0), s[0:8])
    return jnp.concatenate([top, s[8:]], axis=0) if x.shape[0] > 8 else top


def _shift_up(x, halo, k, row8):
    tm = x.shape[0]
    s = pltpu.roll(x, tm - k, 0)
    bot = jnp.where(row8 >= 8 - k, pltpu.roll(halo, 8 - k, 0), s[tm - 8:])
    return jnp.concatenate([s[:tm - 8], bot], axis=0) if tm > 8 else bot


def _conv_gate(g, halo, w_ref, cb, row8):
    return cb + w_ref[0:1, :] * _shift_down(g, halo, 2, row8) + w_ref[1:2, :] * _shift_down(g, halo, 1, row8) + w_ref[2:3, :] * g


def _convact_fwd(gu, cw, cb, DFF, name):
    LP = gu.shape[0]
    tm = _tile(LP, 640, 8)
    tn = _tile(DFF, 512)
    nj = DFF // tn
    t8 = tm // 8

    def body(g_ref, h_ref, u_ref, w_ref, b_ref, o_ref):
        i = pl.program_id(0)
        row8 = lax.broadcasted_iota(jnp.int32, (8, tn), 0)
        g = g_ref[...].astype(F32)
        halo = jnp.where(i > 0, h_ref[...].astype(F32), 0.0)
        gc = _conv_gate(g, halo, w_ref, b_ref[...], row8)
        o_ref[...] = (gc * _sig(gc) * u_ref[...].astype(F32)).astype(o_ref.dtype)

    in_specs = [
        pl.BlockSpec((tm, tn), lambda i, j: (i, j)),
        pl.BlockSpec((8, tn), lambda i, j: (jnp.maximum(i * t8 - 1, 0), j)),
        pl.BlockSpec((tm, tn), lambda i, j: (i, nj + j)),
        pl.BlockSpec((3, tn), lambda i, j: (0, j)),
        pl.BlockSpec((1, tn), lambda i, j: (0, j)),
    ]
    return pl.pallas_call(
        body, grid=(LP // tm, nj), in_specs=in_specs, out_specs=pl.BlockSpec((tm, tn), lambda i, j: (i, j)),
        out_shape=_sds((LP, DFF), BF16), compiler_params=_cparams(("parallel", "parallel"), 2 * 8 * tm * tn * 4), name=name,
    )(gu, gu, gu, cw, cb)


def _convact_bwd(dact, gu, cw, cb, DFF, name):
    LP = gu.shape[0]
    tm = _tile(LP, 640, 8)
    tn = _tile(DFF, 512)
    nj = DFF // tn
    t8 = tm // 8

    def body(da_ref, g_ref, h_ref, u_ref, w_ref, b_ref, dgc_ref, du_ref):
        i = pl.program_id(0)
        row8 = lax.broadcasted_iota(jnp.int32, (8, tn), 0)
        g = g_ref[...].astype(F32)
        halo = jnp.where(i > 0, h_ref[...].astype(F32), 0.0)
        gc = _conv_gate(g, halo, w_ref, b_ref[...], row8)
        sg = _sig(gc)
        da = da_ref[...].astype(F32)
        du_ref[...] = (da * gc * sg).astype(du_ref.dtype)
        dgc_ref[...] = da * u_ref[...].astype(F32) * sg * (1.0 + gc * (1.0 - sg))

    blk = pl.BlockSpec((tm, tn), lambda i, j: (i, j))
    in_specs = [
        blk, blk,
        pl.BlockSpec((8, tn), lambda i, j: (jnp.maximum(i * t8 - 1, 0), j)),
        pl.BlockSpec((tm, tn), lambda i, j: (i, nj + j)),
        pl.BlockSpec((3, tn), lambda i, j: (0, j)),
        pl.BlockSpec((1, tn), lambda i, j: (0, j)),
    ]
    return pl.pallas_call(
        body, grid=(LP // tm, nj), in_specs=in_specs, out_specs=[blk, blk],
        out_shape=[_sds((LP, DFF), F32), _sds((LP, DFF), BF16)],
        compiler_params=_cparams(("parallel", "parallel"), 2 * 10 * tm * tn * 4), name=name,
    )(dact, gu, gu, gu, cw, cb)


def _conv_bwd(dgc, gu, cw, DFF, pad, name):
    LP = gu.shape[0]
    tm = _tile(LP, 640, 8)
    tn = _tile(DFF, 512)
    nj = DFF // tn
    t8 = tm // 8
    nt = LP // tm

    def body(d_ref, dn_ref, g_ref, h_ref, w_ref, dg_ref, dw_ref, db_ref):
        i = pl.program_id(1)
        row8 = lax.broadcasted_iota(jnp.int32, (8, tn), 0)
        d = d_ref[...]
        nxt = jnp.where(i < nt - 1, dn_ref[...], 0.0)
        dg = w_ref[2:3, :] * d + w_ref[1:2, :] * _shift_up(d, nxt, 1, row8) + w_ref[0:1, :] * _shift_up(d, nxt, 2, row8)
        rows = i * tm + lax.broadcasted_iota(jnp.int32, (tm, 1), 0)
        dg_ref[...] = jnp.where(rows >= pad, dg, 0.0).astype(dg_ref.dtype)
        g = g_ref[...].astype(F32)
        halo = jnp.where(i > 0, h_ref[...].astype(F32), 0.0)
        row3 = lax.broadcasted_iota(jnp.int32, (3, tn), 0)
        s0 = jnp.sum(d * _shift_down(g, halo, 2, row8), axis=0, keepdims=True)
        s1 = jnp.sum(d * _shift_down(g, halo, 1, row8), axis=0, keepdims=True)
        s2 = jnp.sum(d * g, axis=0, keepdims=True)
        dw = jnp.where(row3 == 0, s0, jnp.where(row3 == 1, s1, s2))
        dbp = jnp.sum(d, axis=0, keepdims=True)

        @pl.when(i == 0)
        def _():
            dw_ref[...] = dw
            db_ref[...] = dbp

        @pl.when(i > 0)
        def _():
            dw_ref[...] += dw
            db_ref[...] += dbp

    blk = pl.BlockSpec((tm, tn), lambda j, i: (i, j))
    in_specs = [
        blk,
        pl.BlockSpec((8, tn), lambda j, i: (jnp.minimum((i + 1) * t8, LP // 8 - 1), j)),
        blk,
        pl.BlockSpec((8, tn), lambda j, i: (jnp.maximum(i * t8 - 1, 0), j)),
        pl.BlockSpec((3, tn), lambda j, i: (0, j)),
    ]
    out_specs = [blk, pl.BlockSpec((3, tn), lambda j, i: (0, j)), pl.BlockSpec((1, tn), lambda j, i: (0, j))]
    return pl.pallas_call(
        body, grid=(nj, nt), in_specs=in_specs, out_specs=out_specs,
        out_shape=[_sds((LP, DFF), BF16), _sds((3, DFF), F32), _sds((1, DFF), F32)],
        compiler_params=_cparams(("parallel", "arbitrary"), 2 * 10 * tm * tn * 4), name=name,
    )(dgc, dgc, gu, gu, cw)


def _adamw_math(w, g, m, v):
    m = ADAM_B1 * m + (1.0 - ADAM_B1) * g
    v = ADAM_B2 * v + (1.0 - ADAM_B2) * (g * g)
    m_hat = m / (1.0 - ADAM_B1 ** ADAM_STEP)
    v_hat = v / (1.0 - ADAM_B2 ** ADAM_STEP)
    delta = -ADAM_LR * (m_hat / (jnp.sqrt(v_hat) + ADAM_EPS) + ADAM_WD * w)
    return delta, m, v


def _adamw(w, g, m, v, name):
    R, C = w.shape
    tm = R if R * C * 4 <= (1 << 20) else _tile(R, max(8, ((1 << 20) // (C * 4)) // 8 * 8), 8)

    def body(w_ref, g_ref, m_ref, v_ref, d_ref, mo_ref, vo_ref):
        d_ref[...], mo_ref[...], vo_ref[...] = _adamw_math(w_ref[...], g_ref[...], m_ref[...], v_ref[...])

    blk = pl.BlockSpec((tm, C), lambda i: (i, 0))
    return pl.pallas_call(
        body, grid=(R // tm,), in_specs=[blk] * 4, out_specs=[blk] * 3, out_shape=[_sds((R, C), F32)] * 3,
        compiler_params=_cparams(("parallel",), 2 * 7 * tm * (C + LANES) * 4), name=name,
    )(w, g, m, v)


def _sum_adamw(parts, w, m, v, name):
    n, R, C = parts.shape
    tm = _tile(R, 256, 8)

    def body(p_ref, w_ref, m_ref, v_ref, g_ref, d_ref, mo_ref, vo_ref):
        g = p_ref[0]
        for k in range(1, n):
            g = g + p_ref[k]
        g_ref[...] = g
        d_ref[...], mo_ref[...], vo_ref[...] = _adamw_math(w_ref[...], g, m_ref[...], v_ref[...])

    blk = pl.BlockSpec((tm, C), lambda i: (i, 0))
    return pl.pallas_call(
        body, grid=(R // tm,), in_specs=[pl.BlockSpec((n, tm, C), lambda i: (0, i, 0))] + [blk] * 3, out_specs=[blk] * 4,
        out_shape=[_sds((R, C), F32)] * 4,
        compiler_params=_cparams(("parallel",), 2 * (n + 7) * tm * C * 4), name=name,
    )(parts, w, m, v)


def _add_half(g, got, c_idx, name):
    n, R, C = g.shape
    HR = R // 2
    tm = _tile(HR, max(8, ((1 << 20) // (C * 4)) // 8 * 8), 8)
    nb = HR // tm

    def body(c_ref, g_ref, t_ref, o_ref):
        o_ref[...] = (g_ref[...] + t_ref[...]).astype(o_ref.dtype)

    grid_spec = pltpu.PrefetchScalarGridSpec(
        num_scalar_prefetch=1, grid=(n, nb),
        in_specs=[pl.BlockSpec((None, tm, C), lambda k, i, c: (k, c[0] * nb + i, 0)),
                  pl.BlockSpec((None, tm, C), lambda k, i, c: (k, i, 0))],
        out_specs=pl.BlockSpec((None, tm, C), lambda k, i, c: (k, i, 0)))
    return pl.pallas_call(
        body, grid_spec=grid_spec, out_shape=_sds((n, HR, C), BF16),
        compiler_params=_cparams(("parallel", "parallel"), 2 * 3 * tm * (C + LANES) * 4), name=name,
    )(c_idx, g, got)


def _sum_half(g, got, land, chip_c, name):
    n, R, C = g.shape
    HR = R // 2
    tm = _tile(HR, max(8, ((1 << 20) // (C * 4)) // 8 * 8), 8)
    nb = HR // tm

    def body(s_ref, g_ref, t_ref, l_ref, o_ref):
        acc = g_ref[...] + t_ref[...]
        for k in range(3):
            acc = acc + l_ref[k].astype(F32)
        o_ref[...] = acc

    grid_spec = pltpu.PrefetchScalarGridSpec(
        num_scalar_prefetch=1, grid=(nb,),
        in_specs=[pl.BlockSpec((None, tm, C), lambda i, sc: (sc[0], sc[1] * nb + i, 0)),
                  pl.BlockSpec((None, tm, C), lambda i, sc: (sc[0], i, 0)),
                  pl.BlockSpec((3, tm, C), lambda i, sc: (0, i, 0))],
        out_specs=pl.BlockSpec((tm, C), lambda i, sc: (i, 0)))
    return pl.pallas_call(
        body, grid_spec=grid_spec, out_shape=_sds((HR, C), F32),
        compiler_params=_cparams(("parallel",), 2 * 6 * tm * (C + LANES) * 4), name=name,
    )(chip_c, g, got, land)


def _place():
    return lax.axis_index("x"), lax.axis_index("y"), lax.axis_index("c")


def _other_chips(x, y):
    return [(1 - x, y), (x, 1 - y), (1 - x, 1 - y)]


HBM_SPEC = pl.BlockSpec(memory_space=pltpu.HBM)


def _gather_chips(ps, name):
    nw = len(ps)
    halves = [p.shape[0] // 2 for p in ps]
    assert all(h % 16 == 0 for h in halves)

    def body(*refs):
        p_refs, g_refs = refs[:nw], refs[nw:2 * nw]
        send_sems, recv_sems, local_sems = refs[2 * nw:]
        x, y, c = _place()
        chips = _other_chips(x, y)
        me = 2 * x + y

        def copy(i, k, from_p, chip, half, to):
            HR = halves[i]
            r0 = pl.multiple_of(half * HR, 16)
            dst = g_refs[i].at[chip, pl.ds(r0, HR), :]
            src = p_refs[i].at[pl.ds(r0, HR), :] if from_p else dst
            return pltpu.make_async_remote_copy(
                src_ref=src, dst_ref=dst, send_sem=send_sems.at[6 * i + k], recv_sem=recv_sems.at[6 * i + k],
                device_id=to, device_id_type=MESH)

        locals_ = [pltpu.make_async_copy(p_refs[i], g_refs[i].at[me], local_sems.at[i]) for i in range(nw)]
        for cp in locals_:
            cp.start()
        first = [copy(i, k, True, me, c, (cx, cy, c)) for i in range(nw) for k, (cx, cy) in enumerate(chips)]
        for cp in first:
            cp.start()
        passed = []
        for i in range(nw):
            for k, (cx, cy) in enumerate(chips):
                copy(i, k, True, 2 * cx + cy, c, (cx, cy, c)).wait_recv()
                fw = copy(i, 3 + k, False, 2 * cx + cy, c, (x, y, 1 - c))
                fw.start()
                passed.append(fw)
        for i in range(nw):
            for k, (cx, cy) in enumerate(chips):
                copy(i, 3 + k, True, 2 * cx + cy, 1 - c, (x, y, 1 - c)).wait_recv()
        for cp in first + passed:
            cp.wait_send()
        for cp in locals_:
            cp.wait()

    return pl.pallas_call(
        body, out_shape=[_sds((4,) + p.shape, p.dtype) for p in ps], in_specs=[HBM_SPEC] * nw, out_specs=[HBM_SPEC] * nw,
        scratch_shapes=[pltpu.SemaphoreType.DMA((6 * nw,)), pltpu.SemaphoreType.DMA((6 * nw,)), pltpu.SemaphoreType.DMA((nw,))],
        name=name,
    )(*ps)


def _sibling_halves(gs, name):
    nw = len(gs)
    halves = [g.shape[1] // 2 for g in gs]
    assert all(h % 8 == 0 for h in halves)

    def body(*refs):
        g_refs, land_refs = refs[:nw], refs[nw:2 * nw]
        send_sems, recv_sems = refs[2 * nw:]
        x, y, c = _place()
        cps = []
        for i in range(nw):
            q0 = pl.multiple_of((1 - c) * halves[i], 8)
            cp = pltpu.make_async_remote_copy(
                src_ref=g_refs[i].at[pl.ds(0, 4), pl.ds(q0, halves[i]), :], dst_ref=land_refs[i],
                send_sem=send_sems.at[i], recv_sem=recv_sems.at[i], device_id=(x, y, 1 - c), device_id_type=MESH)
            cp.start()
            cps.append(cp)
        for cp in cps:
            cp.wait()

    return pl.pallas_call(
        body, out_shape=[_sds((4, h, g.shape[2]), g.dtype) for g, h in zip(gs, halves)],
        in_specs=[HBM_SPEC] * nw, out_specs=[HBM_SPEC] * nw,
        scratch_shapes=[pltpu.SemaphoreType.DMA((nw,)), pltpu.SemaphoreType.DMA((nw,))], name=name,
    )(*gs)


def _scatter_chips(ss, name):
    nw = len(ss)

    def body(*refs):
        s_refs, land_refs = refs[:nw], refs[nw:2 * nw]
        send_sems, recv_sems = refs[2 * nw:]
        x, y, c = _place()
        chips = _other_chips(x, y)
        sends = []
        for i in range(nw):
            for k, (cx, cy) in enumerate(chips):
                cp = pltpu.make_async_remote_copy(
                    src_ref=s_refs[i].at[2 * cx + cy], dst_ref=land_refs[i].at[k],
                    send_sem=send_sems.at[3 * i + k], recv_sem=recv_sems.at[3 * i + k],
                    device_id=(cx, cy, c), device_id_type=MESH)
                cp.start()
                sends.append(cp)
        for cp in sends:
            cp.wait_recv()
        for cp in sends:
            cp.wait_send()

    return pl.pallas_call(
        body, out_shape=[_sds((3,) + s.shape[1:], s.dtype) for s in ss], in_specs=[HBM_SPEC] * nw, out_specs=[HBM_SPEC] * nw,
        scratch_shapes=[pltpu.SemaphoreType.DMA((3 * nw,)), pltpu.SemaphoreType.DMA((3 * nw,))], name=name,
    )(*ss)


def _sibling_join(hs, name):
    nw = len(hs)
    assert all(h.shape[0] % 8 == 0 for h in hs)

    def body(*refs):
        h_refs, o_refs = refs[:nw], refs[nw:2 * nw]
        send_sems, recv_sems, local_sems = refs[2 * nw:]
        x, y, c = _place()
        locals_, sends = [], []
        for i in range(nw):
            HR = h_refs[i].shape[0]
            r0 = pl.multiple_of(c * HR, 8)
            mine = pltpu.make_async_copy(h_refs[i], o_refs[i].at[pl.ds(r0, HR), :], local_sems.at[i])
            mine.start()
            locals_.append(mine)
            cp = pltpu.make_async_remote_copy(
                src_ref=h_refs[i], dst_ref=o_refs[i].at[pl.ds(r0, HR), :], send_sem=send_sems.at[i], recv_sem=recv_sems.at[i],
                device_id=(x, y, 1 - c), device_id_type=MESH)
            cp.start()
            sends.append(cp)
        for i in range(nw):
            HR = h_refs[i].shape[0]
            q0 = pl.multiple_of((1 - c) * HR, 8)
            pltpu.make_async_remote_copy(
                src_ref=h_refs[i], dst_ref=o_refs[i].at[pl.ds(q0, HR), :], send_sem=send_sems.at[i], recv_sem=recv_sems.at[i],
                device_id=(x, y, 1 - c), device_id_type=MESH).wait_recv()
        for cp in sends:
            cp.wait_send()
        for cp in locals_:
            cp.wait()

    return pl.pallas_call(
        body, out_shape=[_sds((2 * h.shape[0], h.shape[1]), h.dtype) for h in hs], in_specs=[HBM_SPEC] * nw, out_specs=[HBM_SPEC] * nw,
        scratch_shapes=[pltpu.SemaphoreType.DMA((nw,)), pltpu.SemaphoreType.DMA((nw,)), pltpu.SemaphoreType.DMA((nw,))],
        name=name,
    )(*hs)


def _gather_all(v, name):
    M, W = v.shape

    def body(v_ref, o_ref, send_sems, recv_sems, local_sem):
        x, y, c = _place()
        me, sibling = (x, y, c), (x, y, 1 - c)
        chips = _other_chips(x, y)

        def slot(px, py, pc):
            return o_ref.at[4 * px + 2 * py + pc]

        def copy(k, block, to, src=None):
            return pltpu.make_async_remote_copy(
                src_ref=slot(*block) if src is None else src, dst_ref=slot(*block),
                send_sem=send_sems.at[k], recv_sem=recv_sems.at[k], device_id=to, device_id_type=MESH)

        mine = pltpu.make_async_copy(v_ref, slot(*me), local_sem)
        mine.start()
        first = [copy(0, me, sibling, src=v_ref)]
        first += [copy(1 + j, me, (*chip, c), src=v_ref) for j, chip in enumerate(chips)]
        for cp in first:
            cp.start()
        passed = [copy(4 + j, (*chip, c), sibling) for j, chip in enumerate(chips)]
        for j, chip in enumerate(chips):
            copy(1 + j, (*chip, c), me).wait_recv()
            passed[j].start()
        copy(0, sibling, me).wait_recv()
        for j, chip in enumerate(chips):
            copy(4 + j, (*chip, 1 - c), me).wait_recv()
        for cp in first + passed:
            cp.wait_send()
        mine.wait()

    vm = pl.BlockSpec(memory_space=pltpu.VMEM)
    return pl.pallas_call(
        body, out_shape=_sds((8, M, W), v.dtype), in_specs=[vm], out_specs=vm,
        scratch_shapes=[pltpu.SemaphoreType.DMA((7,)), pltpu.SemaphoreType.DMA((7,)), pltpu.SemaphoreType.DMA(())],
        compiler_params=pltpu.CompilerParams(vmem_limit_bytes=int(min(10 * M * W * 4 + (8 << 20), V7X_VMEM_BYTES - (8 << 20)))),
        name=name,
    )(v)


def _rows_for(n_elems, width, mult=8):
    rows = -(-n_elems // width)
    return -(-rows // mult) * mult


def _pack_small(arrs, total_rows):
    parts = []
    used = 0
    for a in arrs:
        rows = _rows_for(a.size, LANES)
        parts.append(jnp.pad(a.reshape(-1), (0, rows * LANES - a.size)).reshape(rows, LANES))
        used += rows
    if total_rows > used:
        parts.append(jnp.zeros((total_rows - used, LANES), F32))
    return jnp.concatenate(parts, axis=0)


def _unpack_small(p, shapes):
    outs, r = [], 0
    lead = p.shape[:-2]
    for shp in shapes:
        n = int(np.prod(shp))
        rows = _rows_for(n, LANES)
        outs.append(p[..., r:r + rows, :].reshape(lead + (rows * LANES,))[..., :n].reshape(lead + tuple(shp)))
        r += rows
    return outs


def _cols_to_chips(w):
    K, N = w.shape
    return w.reshape(K, 4, N // 4).transpose(1, 0, 2)


def _chips_to_cols(w):
    n4, K, n = w.shape
    return w.transpose(1, 0, 2).reshape(K, n4 * n)


def _block_diag(m, gpb):
    G, A, B = m.shape
    nb = G // gpb
    eye = jnp.eye(gpb, dtype=m.dtype)
    t = m.reshape(nb, gpb, A, B)[:, :, :, None, :] * eye[None, :, None, :, None]
    return t.reshape(nb, gpb * A, gpb * B)


def _block_diag_extract(m, gpb, A, B):
    nb = m.shape[0]
    t = m.reshape(nb, gpb, A, gpb, B)
    eye = jnp.eye(gpb, dtype=m.dtype)
    d = jnp.sum(t * eye[None, :, None, :, None], axis=3)
    return d.reshape(nb * gpb, A, B)


def kernel(x, meta, g_mix, w_in, b_f, lam_re, lam_im, log_dt, b_re, b_im, c_re, c_im, d_skip, w_glu, w_attn_o, w_out, g_ffn, w_up, conv_w, conv_b, w_down, g_final, loss_target, m_meta, m_g_mix, m_w_in, m_b_f, m_lam_re, m_lam_im, m_log_dt, m_b_re, m_b_im, m_c_re, m_c_im, m_d_skip, m_w_glu, m_w_attn_o, m_w_out, m_g_ffn, m_w_up, m_conv_w, m_conv_b, m_w_down, m_g_final, v_meta, v_g_mix, v_w_in, v_b_f, v_lam_re, v_lam_im, v_log_dt, v_b_re, v_b_im, v_c_re, v_c_im, v_d_skip, v_w_glu, v_w_attn_o, v_w_out, v_g_ffn, v_w_up, v_conv_w, v_conv_b, v_w_down, v_g_final):
    args = dict(locals())
    L, D = x.shape[1], x.shape[2]
    NM = meta.shape[0]
    H = b_f.shape[1]
    DA = H * HEAD_DIM
    G, P, C = b_re.shape[1:]
    DS, GP = G * C, G * P
    DFF = conv_b.shape[1]
    PAD = (-NM) % LANES
    OFF = PAD + NM
    LP = OFF + L
    NZ = 3 * DA + DS + 2 * D
    U0, GA0, GB0 = 3 * DA, 3 * DA + DS, 3 * DA + DS + D
    NB = G // GROUPS_PER_BLOCK
    chip = 2 * lax.axis_index("x") + lax.axis_index("y")
    core = lax.axis_index("c")

    big = ["w_in", "w_glu", "w_attn_o", "w_out", "w_up", "w_down"]
    local = {n: args[n][0] for n in big}
    gathered = dict(zip(big, _gather_chips([local[n].astype(BF16) for n in big], "gather_weights")))
    tiny_shapes = [conv_w.shape[1:], meta.shape]
    tiny_rows = sum(_rows_for(int(np.prod(sh)), LANES) for sh in tiny_shapes)
    tiny = _gather_all(_pack_small([conv_w[0], meta], tiny_rows), "gather_small_weights")[0::2]
    conv_w_c, meta_c = _unpack_small(tiny, tiny_shapes)
    conv_w_f = _chips_to_cols(conv_w_c)
    meta_full = _chips_to_cols(meta_c)
    w_in_f = _chips_to_cols(gathered["w_in"])
    w_f = jnp.pad(w_in_f[:, 3 * DA:3 * DA + H], ((0, 0), (0, LANES - H)))
    w_zf = jnp.concatenate([w_in_f[:, :3 * DA], w_in_f[:, 3 * DA + H:], w_f], axis=1)
    w_glu_c, w_ao_c, w_up_c = gathered["w_glu"], gathered["w_attn_o"], gathered["w_up"]
    N_GLU, N_AO, N_UP = w_glu_c.shape[2], w_ao_c.shape[2], w_up_c.shape[2]
    w_out_f = gathered["w_out"].reshape(D, D)
    w_down_f = gathered["w_down"].reshape(DFF, D)

    col = lambda a: a.reshape(GP, 1)
    lr_c, li_c = col(lam_re[0]), col(lam_im[0])
    ldt_c = jnp.repeat(log_dt[0], P).reshape(GP, 1)
    br2, bi2 = b_re[0].reshape(GP, C), b_im[0].reshape(GP, C)
    a_re, a_im, bb_re, bb_im, pw_re, pw_im = _ssm_prep(lr_c, li_c, ldt_c, br2, bi2, "ssm_prep")
    S = GROUPS_PER_BLOCK * P
    CB = GROUPS_PER_BLOCK * C
    pw_r = pw_re.T.reshape(8, NB, S).transpose(1, 0, 2)
    pw_i = pw_im.T.reshape(8, NB, S).transpose(1, 0, 2)
    row8 = jnp.arange(8)[None, :, None]

    def masked_power(pw, k, keep):
        return jnp.where(keep, pw[:, k - 1][:, None, :], 0.0)

    coef = jnp.stack(
        [masked_power(pw, k, row8 >= k) for k in (1, 2, 4) for pw in (pw_r, pw_i)] + [pw_r, pw_i], axis=1)
    coef_rev = jnp.stack(
        [masked_power(pw, k, row8 < 8 - k) for k in (1, 2, 4) for pw in (pw_r, -pw_i)]
        + [pw_r[:, ::-1], -pw_i[:, ::-1]], axis=1)
    bd = lambda m: _block_diag(m, GROUPS_PER_BLOCK)
    bbr3, bbi3 = bb_re.reshape(G, P, C), bb_im.reshape(G, P, C)
    bbr_cs = bd(bbr3.transpose(0, 2, 1)).astype(BF16)
    bbi_cs = bd(bbi3.transpose(0, 2, 1)).astype(BF16)
    bbr_sc = bd(bbr3).astype(BF16)
    bbi_sc = bd(bbi3).astype(BF16)
    ccr_sc = bd(c_re[0].transpose(0, 2, 1)).astype(BF16)
    cci_sc = bd(c_im[0].transpose(0, 2, 1)).astype(BF16)
    ccr_cs = bd(c_re[0]).astype(BF16)
    cci_cs = bd(c_im[0]).astype(BF16)

    h0 = jnp.concatenate([jnp.zeros((PAD, D), F32), meta_full, x[0]], axis=0)
    n1 = _rms_fwd(h0, g_mix, "rms_mix")
    z = _mm(n1, w_zf, "nn", LP, NZ, D, BF16, "in_proj")
    fpre = _mm(n1, w_zf, "nn", LP, LANES, D, F32, "in_proj_f", b_off=(0, NZ))
    bf_pad = jnp.pad(b_f, ((0, 0), (0, LANES - H)))
    fcum = _fgate_fwd(fpre, bf_pad, PAD, "fgate_fwd")
    key_bias = jnp.where(jnp.arange(LP)[:, None] >= PAD, -fcum, NEG)
    bias_t = key_bias.T[:H].reshape(H, 1, LP)
    bias_b = jnp.broadcast_to(key_bias.T[:H][:, :, None], (H, LP, LANES))
    attn, attn_f32, lse_b = _attn_fwd(z, bias_t, H, PAD, "attn_fwd")
    ao = _mm(attn, w_ao_c, "nn", LP, D, DA, BF16, "attn_out", b_chips=N_AO)
    y, yg, hs_re, hs_im = _ssm_fwd(z, U0, coef, bbr_cs, bbi_cs, ccr_sc, cci_sc, d_skip, "ssm_fwd")
    yab = _mm(yg, w_glu_c, "nn", LP, 2 * D, DS, BF16, "glu_proj", b_chips=N_GLU)
    merged = _merge_fwd(yab, z, ao, D, GA0, GB0, "merge_fwd")
    h1 = _mm(merged, w_out_f, "nn", LP, D, D, F32, "out_proj", res=h0)
    n2 = _rms_fwd(h1, g_ffn, "rms_ffn")
    gu = _mm(n2, w_up_c, "nn", LP, 2 * DFF, D, BF16, "up_proj", tn=1408, b_chips=N_UP)
    act = _convact_fwd(gu, conv_w_f, conv_b, DFF, "convact_fwd")
    h2 = _mm(act, w_down_f, "nn", LP, D, DFF, F32, "down_proj", res=h1, tk=1408)
    dh2, dg_final, loss_v = _final_loss(h2, g_final.reshape(1, D), loss_target[0], OFF, "final_loss")
    loss = lax.psum(loss_v[0, 0], ("x", "y", "c"))

    KW = dict(tm=512, tn=1024, tk=1664)
    dact = _mm(dh2, w_down_f, "nt", LP, DFF, D, BF16, "down_bwd_x")
    dw_down = _mm(act, dh2, "tn", DFF, D, LP, F32, "down_bwd_w", **KW)
    dgc, du_ffn = _convact_bwd(dact, gu, conv_w_f, conv_b, DFF, "convact_bwd")
    dg_ffn_in, dconv_w, dconv_b = _conv_bwd(dgc, gu, conv_w_f, DFF, PAD, "conv_bwd")
    dn2 = _mm(dg_ffn_in, w_up_c, "nt", LP, D, DFF, F32, "up_bwd_x_g", tk=1408, b_chips=N_UP)
    dn2 = _mm(du_ffn, w_up_c, "nt", LP, D, DFF, F32, "up_bwd_x_u", res=dn2, b_off=(0, DFF), tk=1408, b_chips=N_UP)
    dw_up = _mm(n2, dg_ffn_in, "tn", D, DFF, LP, F32, "up_bwd_w_g", tm=512, tn=1408, tk=1664, out_chips=N_UP,
                out_into=(jnp.zeros((4, D, N_UP), F32), 0))
    dw_up = _mm(n2, du_ffn, "tn", D, DFF, LP, F32, "up_bwd_w_u", tm=512, tn=1408, tk=1664, out_chips=N_UP,
                out_into=(dw_up, DFF // N_UP))
    dh1, dg_ffn = _rms_bwd(h1, g_ffn, dn2, dh2, "rms_ffn_bwd")

    dmerged = _mm(dh1, w_out_f, "nt", LP, D, D, F32, "out_bwd_x")
    dw_out = _mm(merged, dh1, "tn", D, D, LP, F32, "out_bwd_w", **KW)
    dya, dyb, dga, dgb, dao = _merge_bwd(dmerged, yab, z, ao, D, GA0, GB0, "merge_bwd")
    dattn = _mm(dao, w_ao_c, "nt", LP, DA, D, BF16, "attn_out_bwd_x", b_chips=N_AO)
    dw_ao = _mm(attn, dao, "tn", DA, D, LP, F32, "attn_out_bwd_w", out_chips=N_AO, **KW)
    dyg = _mm(dya, w_glu_c, "nt", LP, DS, D, F32, "glu_bwd_x_a", b_chips=N_GLU)
    dyg = _mm(dyb, w_glu_c, "nt", LP, DS, D, F32, "glu_bwd_x_b", res=dyg, b_off=(0, D), b_chips=N_GLU)
    dw_glu = _mm(yg, dya, "tn", DS, D, LP, F32, "glu_bwd_w_a", out_chips=N_GLU,
                 out_into=(jnp.zeros((4, DS, N_GLU), F32), 0), **KW)
    dw_glu = _mm(yg, dyb, "tn", DS, D, LP, F32, "glu_bwd_w_b", out_chips=N_GLU, out_into=(dw_glu, D // N_GLU), **KW)
    (du_ssm, dbbr_d, dbbi_d, dccr_d, dcci_d, dar_b, dai_b, dd_skip) = _ssm_bwd(
        z, U0, dyg, y, hs_re, hs_im, coef_rev, bbr_sc, bbi_sc, ccr_cs, cci_cs, d_skip, "ssm_bwd")
    delta_b = _attn_delta(dattn, attn_f32, H, "attn_delta")
    lse_t = lse_b[:, :, 0].reshape(H, 1, LP)
    delta_t = delta_b[:, :, 0].reshape(H, 1, LP)
    dq, dk, dv, dbias_b = _attn_bwd(z, dattn, lse_t, delta_t, bias_b, H, "attn_bwd")
    dF = jnp.pad(-dbias_b[:, :, 0].T, ((0, 0), (0, LANES - H)))
    dfpre, db_f = _fgate_bwd(dF, fpre, bf_pad, PAD, "fgate_bwd")
    dz = jnp.concatenate([dq, dk, dv, du_ssm, dga, dgb, dfpre.astype(BF16)], axis=1)
    dn1 = _mm(dz, w_zf, "nt", LP, D, NZ + LANES, F32, "in_bwd_x", tk=1664)
    dw_zf = _mm(n1, dz, "tn", D, NZ + LANES, LP, F32, "in_bwd_w", tm=512, tn=1664, tk=1664)
    dh0, dg_mix = _rms_bwd(h0, g_mix, dn1, dh1, "rms_mix_bwd")
    grad_x = dh0[OFF:][None]
    dmeta_full = dh0[PAD:OFF]

    ext = lambda m, A, B: _block_diag_extract(m, GROUPS_PER_BLOCK, A, B)
    dbb_re = ext(dbbr_d, C, P).transpose(0, 2, 1).reshape(GP, C)
    dbb_im = ext(dbbi_d, C, P).transpose(0, 2, 1).reshape(GP, C)
    dc_re = ext(dccr_d, P, C).transpose(0, 2, 1)[None]
    dc_im = ext(dcci_d, P, C).transpose(0, 2, 1)[None]
    glr, gli, gldt, gbr, gbi = _ssm_prep_bwd(lr_c, li_c, ldt_c, br2, bi2, dar_b.reshape(GP, 1), dai_b.reshape(GP, 1),
                                             dbb_re, dbb_im, "ssm_prep_bwd")
    small_grads = {
        "g_mix": dg_mix, "b_f": db_f[:, :H], "lam_re": glr.reshape(1, G, P), "lam_im": gli.reshape(1, G, P),
        "log_dt": gldt.reshape(G, P).sum(axis=1)[None], "b_re": gbr.reshape(1, G, P, C), "b_im": gbi.reshape(1, G, P, C),
        "c_re": dc_re, "c_im": dc_im, "d_skip": dd_skip, "g_ffn": dg_ffn, "conv_b": dconv_b, "g_final": dg_final.reshape(D),
    }

    small = list(small_grads)
    rider_grads = [dconv_w, dmeta_full]
    small_shapes = [args[n].shape for n in small] + [g.shape for g in rider_grads]
    srows = sum(_rows_for(int(np.prod(sh)), LANES) for sh in small_shapes)
    srows = -(-srows // 256) * 256
    zeros_like_riders = [jnp.zeros(g.shape, F32) for g in rider_grads]
    pack = lambda arrs: _pack_small(arrs, srows)
    g_parts = _gather_all(pack([small_grads[n] for n in small] + rider_grads), "gather_small_grads")
    sm = _sum_adamw(g_parts, pack([args[n] for n in small] + zeros_like_riders),
                    pack([args["m_" + n] for n in small] + zeros_like_riders),
                    pack([args["v_" + n] for n in small] + zeros_like_riders), "small_adamw")
    unpacked = [_unpack_small(p, small_shapes) for p in sm]
    sg, sd, smm, svv = (dict(zip(small, u[:len(small)])) for u in unpacked)
    dconv_w_sum, dmeta_sum = unpacked[0][len(small):]
    n_cw, n_me = conv_w.shape[2], meta.shape[1]
    rider = {"conv_w": lax.dynamic_slice_in_dim(dconv_w_sum, chip * n_cw, n_cw, axis=1)[None],
             "meta": lax.dynamic_slice_in_dim(dmeta_sum, chip * n_me, n_me, axis=1)}

    dw_in_f = jnp.concatenate([dw_zf[:, :3 * DA], dw_zf[:, NZ:NZ + H], dw_zf[:, 3 * DA:NZ]], axis=1)
    full_grads = [_cols_to_chips(dw_in_f), dw_glu, dw_ao, dw_out.reshape(4, D // 4, D), dw_up, dw_down.reshape(4, DFF // 4, D)]
    c_idx = core.reshape(1).astype(jnp.int32)
    chip_c = jnp.stack([chip, core]).astype(jnp.int32)
    got = _sibling_halves(full_grads, "rs_sibling")
    part = [_add_half(g, t, c_idx, "rs_add_" + n) for n, g, t in zip(big, full_grads, got)]
    land = _scatter_chips(part, "rs_scatter")
    halves = [_sum_half(g, t, l_, chip_c, "rs_sum_" + n) for n, g, t, l_ in zip(big, full_grads, got, land)]
    shard_grads = dict(zip(big, _sibling_join(halves, "rs_join")))
    shard_grads.update({n: g.reshape(g.shape[-2:]) for n, g in rider.items()})
    bg, bd_, bm, bv = {}, {}, {}, {}
    for n, g in shard_grads.items():
        shp = args[n].shape
        two = (lambda a: a.reshape(shp[-2], shp[-1]))
        d_, m_, v_ = _adamw(two(args[n]), g, two(args["m_" + n]), two(args["v_" + n]), "adamw_" + n)
        bg[n], bd_[n], bm[n], bv[n] = g.reshape(shp), d_.reshape(shp), m_.reshape(shp), v_.reshape(shp)

    order = ["meta", "g_mix", "w_in", "b_f", "lam_re", "lam_im", "log_dt", "b_re", "b_im", "c_re", "c_im", "d_skip",
             "w_glu", "w_attn_o", "w_out", "g_ffn", "w_up", "conv_w", "conv_b", "w_down", "g_final"]
    pick = lambda bigd, smalld, n: bigd[n] if n in bigd else smalld[n]
    outs = [loss, grad_x]
    for bigd, smalld in ((bg, sg), (bd_, sd), (bm, smm), (bv, svv)):
        outs += [pick(bigd, smalld, n) for n in order]
    return tuple(outs)
```

```python
import functools
import math

import jax
import jax.numpy as jnp
import numpy as np
from jax import lax
from jax.experimental import pallas as pl
from jax.experimental.pallas import tpu as pltpu

F32 = jnp.float32
BF16 = jnp.bfloat16
MESH = pl.DeviceIdType.MESH

EPS = 1e-6
HEAD_DIM = 128
LANES = 128
NEG = -1e30
GELU_C = math.sqrt(2.0 / math.pi)
GELU_A = 0.044715
ADAM_LR, ADAM_B1, ADAM_B2, ADAM_EPS, ADAM_WD, ADAM_STEP = 0.001, 0.9, 0.999, 1e-08, 0.01, 10
V7X_VMEM_BYTES = 64 << 20
GROUPS_PER_BLOCK = 8


def _tile(n, pref, mult=LANES):
    if n <= pref:
        return n
    t = (pref // mult) * mult
    while t >= mult:
        if n % t == 0:
            return t
        t -= mult
    raise ValueError(f"no tile for {n} <= {pref} (multiple of {mult})")


def _ctile(pref, *vals):
    g = 0
    for v in vals:
        g = math.gcd(g, v)
    return _tile(g, pref)


def _cparams(sem, est_bytes):
    limit = int(min(max(est_bytes * 1.25 + (4 << 20), 16 << 20), V7X_VMEM_BYTES - (8 << 20)))
    return pltpu.CompilerParams(dimension_semantics=sem, vmem_limit_bytes=limit)


def _sds(shape, dtype):
    return jax.ShapeDtypeStruct(tuple(shape), dtype)


def _sig(x):
    return 1.0 / (1.0 + jnp.exp(-x))


def _gelu(x):
    t = jnp.tanh(GELU_C * (x + GELU_A * x * x * x))
    return 0.5 * x * (1.0 + t)


def _gelu_grad(x):
    t = jnp.tanh(GELU_C * (x + GELU_A * x * x * x))
    return 0.5 * (1.0 + t) + 0.5 * x * (1.0 - t * t) * GELU_C * (1.0 + 3.0 * GELU_A * x * x)


def _mm(a, b, mode, M, N, K, out_dtype, name, *, res=None, a_off=(0, 0), b_off=(0, 0),
        tm=640, tn=1024, tk=2048, b_chips=None, out_chips=None, out_into=None):
    tm, tn, tk = _tile(M, tm, 8 if mode != "tn" else LANES), _tile(N, tn), _tile(K, tk, LANES if mode != "tn" else 8)
    if b_chips is not None and mode == "nt":
        tk = _ctile(tk, tk, b_chips)
    if b_chips is not None and mode != "nt":
        tn = _ctile(tn, tn, b_chips)
    if out_chips is not None:
        tn = _ctile(tn, tn, out_chips)
    nk = K // tk
    ar, ac = a_off
    br, bc = b_off
    if mode == "tn":
        assert ar % tk == 0 and ac % tm == 0
        a_spec = pl.BlockSpec((tk, tm), lambda i, j, k: (k + ar // tk, i + ac // tm))
        a_dims = 0
    else:
        assert ar % tm == 0 and ac % tk == 0
        a_spec = pl.BlockSpec((tm, tk), lambda i, j, k: (i + ar // tm, k + ac // tk))
        a_dims = 1
    if mode == "nt":
        assert br % tn == 0 and bc % tk == 0
        if b_chips is None:
            b_spec = pl.BlockSpec((tn, tk), lambda i, j, k: (j + br // tn, k + bc // tk))
        else:
            per = b_chips // tk
            b_spec = pl.BlockSpec((None, tn, tk), lambda i, j, k: ((k + bc // tk) // per, j + br // tn, (k + bc // tk) % per))
        b_dims = 1
    else:
        assert br % tk == 0 and bc % tn == 0
        if b_chips is None:
            b_spec = pl.BlockSpec((tk, tn), lambda i, j, k: (k + br // tk, j + bc // tn))
        else:
            per = b_chips // tn
            b_spec = pl.BlockSpec((None, tk, tn), lambda i, j, k: ((j + bc // tn) // per, k + br // tk, (j + bc // tn) % per))
        b_dims = 0
    dims = (((a_dims,), (b_dims,)), ((), ()))
    if out_chips is None:
        o_spec = pl.BlockSpec((tm, tn), lambda i, j, k: (i, j))
        o_shape = _sds((M, N), out_dtype)
    else:
        per_o = out_chips // tn
        chip0 = 0 if out_into is None else out_into[1]
        o_spec = pl.BlockSpec((None, tm, tn), lambda i, j, k: (chip0 + j // per_o, i, j % per_o))
        o_shape = _sds((N // out_chips if out_into is None else 4, M, out_chips), out_dtype)
    has_res = res is not None
    has_into = out_into is not None

    def body(*refs):
        if has_res:
            a_ref, b_ref, r_ref, o_ref = refs[:4]
        elif has_into:
            a_ref, b_ref, _, o_ref = refs[:4]
            r_ref = None
        else:
            a_ref, b_ref, o_ref = refs[:3]
            r_ref = None
        part = lax.dot_general(a_ref[...].astype(BF16), b_ref[...].astype(BF16), dims, preferred_element_type=F32)

        def finish(acc):
            if has_res:
                acc = r_ref[...] + acc
            o_ref[...] = acc.astype(o_ref.dtype)

        if nk == 1:
            finish(part)
        else:
            acc_ref = refs[-1]
            k = pl.program_id(2)

            @pl.when(k == 0)
            def _():
                acc_ref[...] = part

            @pl.when(k > 0)
            def _():
                acc_ref[...] += part

            @pl.when(k == nk - 1)
            def _():
                finish(acc_ref[...])

    in_specs = [a_spec, b_spec] + ([o_spec] if has_res else []) + ([pl.BlockSpec(memory_space=pl.ANY)] if has_into else [])
    args = (a, b) + ((res,) if has_res else ()) + ((out_into[0],) if has_into else ())
    isz = lambda x: jnp.dtype(x.dtype).itemsize
    est = 2 * (tm * tk * isz(a) + tk * tn * isz(b) + tm * tn * jnp.dtype(out_dtype).itemsize) + tm * tn * 4 * (2 + 2 * has_res)
    return pl.pallas_call(
        body, grid=(M // tm, N // tn, nk), in_specs=in_specs, out_specs=o_spec, out_shape=o_shape,
        scratch_shapes=[pltpu.VMEM((tm, tn), F32)] if nk > 1 else [],
        input_output_aliases={2: 0} if has_into else {},
        compiler_params=_cparams(("parallel", "parallel", "arbitrary"), est), name=name,
    )(*args)


def _rms_fwd(h, g, name):
    LP, D = h.shape
    tm = _tile(LP, 640, 8)

    def body(h_ref, g_ref, o_ref):
        x = h_ref[...]
        r = lax.rsqrt(jnp.mean(x * x, axis=-1, keepdims=True) + EPS)
        o_ref[...] = (x * r * g_ref[...]).astype(o_ref.dtype)

    row = pl.BlockSpec((tm, D), lambda i: (i, 0))
    return pl.pallas_call(
        body, grid=(LP // tm,), in_specs=[row, pl.BlockSpec((1, D), lambda i: (0, 0))], out_specs=row,
        out_shape=_sds((LP, D), BF16), compiler_params=_cparams(("parallel",), 2 * tm * D * 6), name=name,
    )(h, g)


def _rms_bwd(h, g, dn, dres, name):
    LP, D = h.shape
    tm = _tile(LP, 320, 8)
    nt = LP // tm

    def body(h_ref, g_ref, dn_ref, dres_ref, dh_ref, dg_ref):
        i = pl.program_id(0)
        x = h_ref[...]
        r = lax.rsqrt(jnp.mean(x * x, axis=-1, keepdims=True) + EPS)
        xh = x * r
        dn_v = dn_ref[...]
        dxh = dn_v * g_ref[...]
        dh_ref[...] = dres_ref[...] + r * (dxh - xh * jnp.mean(dxh * xh, axis=-1, keepdims=True))
        part = jnp.sum(dn_v * xh, axis=0, keepdims=True)

        @pl.when(i == 0)
        def _():
            dg_ref[...] = part

        @pl.when(i > 0)
        def _():
            dg_ref[...] += part

    row = pl.BlockSpec((tm, D), lambda i: (i, 0))
    vec = pl.BlockSpec((1, D), lambda i: (0, 0))
    return pl.pallas_call(
        body, grid=(nt,), in_specs=[row, vec, row, row], out_specs=[row, vec],
        out_shape=[_sds((LP, D), F32), _sds((1, D), F32)],
        compiler_params=_cparams(("arbitrary",), 2 * 4 * tm * D * 4), name=name,
    )(h, g, dn, dres)


def _final_loss(h, g, tgt, off, name):
    LP, D = h.shape
    tm = LANES
    assert off % tm == 0
    ob = off // tm
    nt = LP // tm

    def body(h_ref, g_ref, t_ref, dh_ref, dg_ref, loss_ref):
        i = pl.program_id(0)
        x = h_ref[...]
        r = lax.rsqrt(jnp.mean(x * x, axis=-1, keepdims=True) + EPS)
        xh = x * r
        gv = g_ref[...]
        e = xh * gv - t_ref[...]
        valid = i >= ob
        dy = jnp.where(valid, e * (1.0 / D), 0.0)
        lpart = jnp.where(valid, 0.5 * jnp.sum(jnp.mean(e * e, axis=-1, keepdims=True), axis=0, keepdims=True), 0.0)
        dxh = dy * gv
        dh_ref[...] = r * (dxh - xh * jnp.mean(dxh * xh, axis=-1, keepdims=True))
        gpart = jnp.sum(dy * xh, axis=0, keepdims=True)

        @pl.when(i == 0)
        def _():
            dg_ref[...] = gpart
            loss_ref[...] = jnp.broadcast_to(lpart, loss_ref.shape)

        @pl.when(i > 0)
        def _():
            dg_ref[...] += gpart
            loss_ref[...] += jnp.broadcast_to(lpart, loss_ref.shape)

    row = pl.BlockSpec((tm, D), lambda i: (i, 0))
    vec = pl.BlockSpec((1, D), lambda i: (0, 0))
    return pl.pallas_call(
        body, grid=(nt,),
        in_specs=[row, vec, pl.BlockSpec((tm, D), lambda i: (jnp.maximum(i - ob, 0), 0))],
        out_specs=[row, vec, pl.BlockSpec((1, LANES), lambda i: (0, 0))],
        out_shape=[_sds((LP, D), F32), _sds((1, D), F32), _sds((1, LANES), F32)],
        compiler_params=_cparams(("arbitrary",), 2 * 3 * tm * D * 4), name=name,
    )(h, g, tgt)


def _fgate_fwd(fpre, bias, pad, name):
    LP, W = fpre.shape

    def body(f_ref, b_ref, o_ref):
        row8 = lax.broadcasted_iota(jnp.int32, (8, W), 0)
        bv = b_ref[...]

        def step(g, carry):
            r0 = pl.multiple_of(g * 8, 8)
            x = f_ref[pl.ds(r0, 8), :] + bv
            lf = jnp.minimum(x, 0.0) - jnp.log(1.0 + jnp.exp(-jnp.abs(x)))
            lf = jnp.where(r0 + row8 >= pad, lf, 0.0)
            for k in (1, 2, 4):
                lf = lf + jnp.where(row8 >= k, pltpu.roll(lf, k, 0), 0.0)
            lf = lf + carry
            o_ref[pl.ds(r0, 8), :] = lf
            return jnp.broadcast_to(lf[7:8, :], (8, W))

        lax.fori_loop(0, LP // 8, step, jnp.zeros((8, W), F32))

    return pl.pallas_call(
        body, out_shape=_sds((LP, W), F32),
        compiler_params=_cparams(None, 3 * LP * W * 4), name=name,
    )(fpre, bias)


def _fgate_bwd(dF, fpre, bias, pad, name):
    LP, W = fpre.shape
    ng = LP // 8

    def body(d_ref, f_ref, b_ref, o_ref, db_ref):
        row8 = lax.broadcasted_iota(jnp.int32, (8, W), 0)
        bv = b_ref[...]

        def step(t, carry):
            run, acc = carry
            g = ng - 1 - t
            r0 = pl.multiple_of(g * 8, 8)
            x = d_ref[pl.ds(r0, 8), :]
            for k in (1, 2, 4):
                x = x + jnp.where(row8 < 8 - k, pltpu.roll(x, 8 - k, 0), 0.0)
            x = x + run
            df = x * _sig(-(f_ref[pl.ds(r0, 8), :] + bv))
            df = jnp.where(r0 + row8 >= pad, df, 0.0)
            o_ref[pl.ds(r0, 8), :] = df
            return jnp.broadcast_to(x[0:1, :], (8, W)), acc + df

        _, acc = lax.fori_loop(0, ng, step, (jnp.zeros((8, W), F32), jnp.zeros((8, W), F32)))
        db_ref[...] = jnp.sum(acc, axis=0, keepdims=True)

    return pl.pallas_call(
        body, out_shape=[_sds((LP, W), F32), _sds((1, W), F32)],
        compiler_params=_cparams(None, 4 * LP * W * 4), name=name,
    )(dF, fpre, bias)


def _col_to_row(col):
    n = col.shape[0]
    return jnp.transpose(jnp.broadcast_to(col, (n, LANES)))[0:1, :]


def _row_to_col(row):
    n = row.shape[1]
    return jnp.transpose(jnp.broadcast_to(row, (LANES, n)))[:, 0:1]


def _attn_fwd(z, bias_t, H, pad, name):
    LP = z.shape[0]
    BQ = BK = _tile(LP, 640)
    scale = HEAD_DIM ** -0.5
    NT = (((1,), (1,)), ((), ()))

    def body(q_ref, k_ref, v_ref, b_ref, o_ref, of_ref, lse_ref):
        qi = pl.program_id(1)
        q = q_ref[...]

        def tile(kb, carry, masked):
            m, l, acc = carry
            k0 = pl.multiple_of(kb * BK, BK)
            s = lax.dot_general(q, k_ref[pl.ds(k0, BK), :], NT, preferred_element_type=F32) * scale
            s = s + b_ref[:, pl.ds(k0, BK)]
            if masked:
                ri = lax.broadcasted_iota(jnp.int32, (BQ, BK), 0)
                ci = lax.broadcasted_iota(jnp.int32, (BQ, BK), 1)
                s = jnp.where(ri >= ci, s, NEG)
            mn = jnp.maximum(m, jnp.max(s, axis=-1, keepdims=True))
            p = jnp.exp(s - mn)
            alpha = jnp.exp(m - mn)
            l = alpha * l + jnp.sum(p, axis=-1, keepdims=True)
            vk = v_ref[pl.ds(k0, BK), :]
            p_hi = p.astype(BF16)
            p_lo = (p - p_hi.astype(F32)).astype(BF16)
            pv = jnp.dot(p_hi, vk, preferred_element_type=F32) + jnp.dot(p_lo, vk, preferred_element_type=F32)
            return mn, l, alpha * acc + pv

        carry = (jnp.full((BQ, 1), NEG, F32), jnp.zeros((BQ, 1), F32), jnp.zeros((BQ, HEAD_DIM), F32))
        carry = lax.fori_loop(0, qi, lambda kb, c: tile(kb, c, False), carry)
        m, l, acc = tile(qi, carry, True)
        rows = qi * BQ + lax.broadcasted_iota(jnp.int32, (BQ, 1), 0)
        o = jnp.where(rows >= pad, acc / l, 0.0)
        o_ref[...] = o.astype(o_ref.dtype)
        of_ref[...] = o
        lse_ref[...] = _col_to_row(m + jnp.log(l))

    in_specs = [
        pl.BlockSpec((BQ, HEAD_DIM), lambda h, i: (i, h)),
        pl.BlockSpec((LP, HEAD_DIM), lambda h, i: (0, H + h)),
        pl.BlockSpec((LP, HEAD_DIM), lambda h, i: (0, 2 * H + h)),
        pl.BlockSpec((None, 1, LP), lambda h, i: (h, 0, 0)),
    ]
    out_specs = [
        pl.BlockSpec((BQ, HEAD_DIM), lambda h, i: (i, h)),
        pl.BlockSpec((BQ, HEAD_DIM), lambda h, i: (i, h)),
        pl.BlockSpec((None, 1, BQ), lambda h, i: (h, 0, i)),
    ]
    est = 2 * (2 * LP * HEAD_DIM * 2 + 8 * LP * 4) + 20 * BQ * LANES * 4 + 8 * BQ * BK * 4
    return pl.pallas_call(
        body, grid=(H, LP // BQ), in_specs=in_specs, out_specs=out_specs,
        out_shape=[_sds((LP, H * HEAD_DIM), BF16), _sds((LP, H * HEAD_DIM), F32), _sds((H, 1, LP), F32)],
        compiler_params=_cparams(("parallel", "arbitrary"), est), name=name,
    )(z, z, z, bias_t)


def _attn_delta(do, o, H, name):
    LP = do.shape[0]
    tm = _tile(LP, 640)

    def body(do_ref, o_ref, d_ref):
        d_ref[...] = _col_to_row(jnp.sum(do_ref[...].astype(F32) * o_ref[...].astype(F32), axis=-1, keepdims=True))

    blk = pl.BlockSpec((tm, HEAD_DIM), lambda h, i: (i, h))
    return pl.pallas_call(
        body, grid=(H, LP // tm), in_specs=[blk, blk],
        out_specs=pl.BlockSpec((None, 1, tm), lambda h, i: (h, 0, i)),
        out_shape=_sds((H, 1, LP), F32),
        compiler_params=_cparams(("parallel", "parallel"), 8 * tm * LANES * 4), name=name,
    )(do, o)


def _attn_bwd(z, do, lse_t, delta_t, bias_t, H, name):
    LP = z.shape[0]
    BK = BQ = _tile(LP, 640)
    nk = nq = LP // BK
    scale = HEAD_DIM ** -0.5
    NT = (((1,), (1,)), ((), ()))
    TN = (((0,), (0,)), ((), ()))

    def body(q_ref, k_ref, v_ref, do_ref, lse_ref, dl_ref, b_ref, dq_ref, dk_ref, dv_ref, db_ref, dq_acc):
        kj = pl.program_id(1)

        @pl.when(kj == 0)
        def _():
            dq_acc[...] = jnp.zeros_like(dq_acc)

        k = k_ref[...]
        v = v_ref[...]
        bcol = _row_to_col(b_ref[:, pl.ds(pl.multiple_of(kj * BK, BK), BK)])

        def tile(qc, carry, masked):
            dk, dv, db = carry
            q0 = pl.multiple_of(qc * BQ, BQ)
            q = q_ref[pl.ds(q0, BQ), :]
            dout = do_ref[pl.ds(q0, BQ), :]
            st = lax.dot_general(k, q, NT, preferred_element_type=F32) * scale + bcol
            if masked:
                ri = lax.broadcasted_iota(jnp.int32, (BK, BQ), 0)
                ci = lax.broadcasted_iota(jnp.int32, (BK, BQ), 1)
                st = jnp.where(ci >= ri, st, NEG)
            pt = jnp.exp(st - lse_ref[:, pl.ds(q0, BQ)])
            dv = dv + jnp.dot(pt.astype(BF16), dout, preferred_element_type=F32)
            dpt = lax.dot_general(v, dout, NT, preferred_element_type=F32)
            dst = pt * (dpt - dl_ref[:, pl.ds(q0, BQ)])
            db = db + jnp.sum(dst, axis=-1, keepdims=True)
            dsb = (dst * scale).astype(BF16)
            dk = dk + jnp.dot(dsb, q, preferred_element_type=F32)
            dq_acc[pl.ds(q0, BQ), :] += lax.dot_general(dsb, k, TN, preferred_element_type=F32)
            return dk, dv, db

        carry = (jnp.zeros((BK, HEAD_DIM), F32), jnp.zeros((BK, HEAD_DIM), F32), jnp.zeros((BK, 1), F32))
        carry = tile(kj, carry, True)
        dk, dv, db = lax.fori_loop(kj + 1, nq, lambda qc, c: tile(qc, c, False), carry)
        dk_ref[...] = dk.astype(dk_ref.dtype)
        dv_ref[...] = dv.astype(dv_ref.dtype)
        db_ref[...] = _col_to_row(db)

        @pl.when(kj == nk - 1)
        def _():
            dq_ref[...] = dq_acc[...].astype(dq_ref.dtype)

    full = lambda c0: pl.BlockSpec((LP, HEAD_DIM), lambda h, j: (0, c0 + h))
    blk = lambda c0: pl.BlockSpec((BK, HEAD_DIM), lambda h, j: (j, c0 + h))
    vec = pl.BlockSpec((None, 1, LP), lambda h, j: (h, 0, 0))
    in_specs = [full(0), blk(H), blk(2 * H), full(0), vec, vec, vec]
    out_specs = [full(0), blk(0), blk(0), pl.BlockSpec((None, 1, BK), lambda h, j: (h, 0, j))]
    est = 2 * (3 * LP * HEAD_DIM * 2 + 16 * LP * 4) + LP * HEAD_DIM * 4 + 24 * BK * LANES * 4 + 10 * BK * BQ * 4
    return pl.pallas_call(
        body, grid=(H, nk), in_specs=in_specs, out_specs=out_specs,
        out_shape=[_sds((LP, H * HEAD_DIM), BF16)] * 3 + [_sds((H, 1, LP), F32)],
        scratch_shapes=[pltpu.VMEM((LP, HEAD_DIM), F32)],
        compiler_params=_cparams(("parallel", "arbitrary"), est), name=name,
    )(z, z, z, do, lse_t, delta_t, bias_t)


def _ssm_disc(lr, li, ldt, br, bi):
    dt = jnp.exp(ldt)
    mag = jnp.exp(lr * dt)
    a_re = mag * jnp.cos(li * dt)
    a_im = mag * jnp.sin(li * dt)
    den = lr * lr + li * li
    nr = a_re - 1.0
    z_re = (nr * lr + a_im * li) / den
    z_im = (a_im * lr - nr * li) / den
    return a_re, a_im, z_re * br - z_im * bi, z_re * bi + z_im * br


def _ssm_prep(lr, li, ldt, br, bi, name):
    GP, C = br.shape

    def body(lr_ref, li_ref, ldt_ref, br_ref, bi_ref, ar_ref, ai_ref, bbr_ref, bbi_ref, pr_ref, pi_ref):
        a_re, a_im, bb_re, bb_im = _ssm_disc(lr_ref[...], li_ref[...], ldt_ref[...], br_ref[...], bi_ref[...])
        ar_ref[...] = a_re
        ai_ref[...] = a_im
        bbr_ref[...] = bb_re
        bbi_ref[...] = bb_im
        lane = lax.broadcasted_iota(jnp.int32, (tg, 8), 1)
        pr, pi_ = a_re, a_im
        accr = jnp.zeros((tg, 8), F32)
        acci = jnp.zeros((tg, 8), F32)
        for k in range(8):
            accr = jnp.where(lane == k, pr, accr)
            acci = jnp.where(lane == k, pi_, acci)
            pr, pi_ = pr * a_re - pi_ * a_im, pr * a_im + pi_ * a_re
        pr_ref[...] = accr
        pi_ref[...] = acci

    tg = _tile(GP, 512, 8)
    blk = lambda w: pl.BlockSpec((tg, w), lambda i: (i, 0))
    col = _sds((GP, 1), F32)
    return pl.pallas_call(
        body, grid=(GP // tg,), in_specs=[blk(1), blk(1), blk(1), blk(C), blk(C)],
        out_specs=[blk(1), blk(1), blk(C), blk(C), blk(8), blk(8)],
        out_shape=[col, col, _sds((GP, C), F32), _sds((GP, C), F32), _sds((GP, 8), F32), _sds((GP, 8), F32)],
        compiler_params=_cparams(("parallel",), 48 * tg * LANES * 4), name=name,
    )(lr, li, ldt, br, bi)


def _ssm_prep_bwd(lr, li, ldt, br, bi, dar, dai, dbbr, dbbi, name):
    GP, C = br.shape

    def body(lr_ref, li_ref, ldt_ref, br_ref, bi_ref, dar_ref, dai_ref, dbbr_ref, dbbi_ref,
             glr_ref, gli_ref, gldt_ref, gbr_ref, gbi_ref):
        _, vjp = jax.vjp(_ssm_disc, lr_ref[...], li_ref[...], ldt_ref[...], br_ref[...], bi_ref[...])
        glr, gli, gldt, gbr, gbi = vjp((dar_ref[...], dai_ref[...], dbbr_ref[...], dbbi_ref[...]))
        glr_ref[...] = glr
        gli_ref[...] = gli
        gldt_ref[...] = gldt
        gbr_ref[...] = gbr
        gbi_ref[...] = gbi

    tg = _tile(GP, 512, 8)
    blk = lambda w: pl.BlockSpec((tg, w), lambda i: (i, 0))
    col = _sds((GP, 1), F32)
    return pl.pallas_call(
        body, grid=(GP // tg,), in_specs=[blk(1), blk(1), blk(1), blk(C), blk(C), blk(1), blk(1), blk(C), blk(C)],
        out_specs=[blk(1), blk(1), blk(1), blk(C), blk(C)],
        out_shape=[col, col, col, _sds((GP, C), F32), _sds((GP, C), F32)],
        compiler_params=_cparams(("parallel",), 96 * tg * LANES * 4), name=name,
    )(lr, li, ldt, br, bi, dar, dai, dbbr, dbbi)


def _cmul_add(xr, xi, mr, mi, sr, si):
    return xr + mr * sr - mi * si, xi + mr * si + mi * sr


def _ssm_fwd(z, u_col0, coef, bbr, bbi, ccr, cci, dskip, name):
    LP = z.shape[0]
    NB, CB, S = bbr.shape
    TS = _tile(LP, 640, 8)
    nt = LP // TS

    def body(u_ref, coef_ref, bbr_ref, bbi_ref, ccr_ref, cci_ref, ds_ref, y_ref, yg_ref, hr_ref, hi_ref, bur, bui, carry):
        i = pl.program_id(1)

        @pl.when(i == 0)
        def _():
            carry[...] = jnp.zeros_like(carry)

        u = u_ref[...]
        bur[...] = jnp.dot(u, bbr_ref[...], preferred_element_type=F32)
        bui[...] = jnp.dot(u, bbi_ref[...], preferred_element_type=F32)

        def step(g, c):
            cr, ci = c
            r0 = pl.multiple_of(g * 8, 8)
            xr = bur[pl.ds(r0, 8), :]
            xi = bui[pl.ds(r0, 8), :]
            for n, k in enumerate((1, 2, 4)):
                xr, xi = _cmul_add(xr, xi, coef_ref[2 * n], coef_ref[2 * n + 1], pltpu.roll(xr, k, 0), pltpu.roll(xi, k, 0))
            xr, xi = _cmul_add(xr, xi, coef_ref[6], coef_ref[7], cr, ci)
            hr_ref[pl.ds(r0, 8), :] = xr
            hi_ref[pl.ds(r0, 8), :] = xi
            return jnp.broadcast_to(xr[7:8, :], (8, S)), jnp.broadcast_to(xi[7:8, :], (8, S))

        cr, ci = lax.fori_loop(0, TS // 8, step, (carry[0], carry[1]))
        carry[0] = cr
        carry[1] = ci
        y = (jnp.dot(hr_ref[...].astype(BF16), ccr_ref[...], preferred_element_type=F32)
             - jnp.dot(hi_ref[...].astype(BF16), cci_ref[...], preferred_element_type=F32)
             + ds_ref[...] * u.astype(F32))
        y_ref[...] = y
        yg_ref[...] = _gelu(y).astype(yg_ref.dtype)

    ucb = u_col0 // CB
    in_specs = [
        pl.BlockSpec((TS, CB), lambda j, i: (i, ucb + j)),
        pl.BlockSpec((None, 8, 8, S), lambda j, i: (j, 0, 0, 0)),
        pl.BlockSpec((None, CB, S), lambda j, i: (j, 0, 0)),
        pl.BlockSpec((None, CB, S), lambda j, i: (j, 0, 0)),
        pl.BlockSpec((None, S, CB), lambda j, i: (j, 0, 0)),
        pl.BlockSpec((None, S, CB), lambda j, i: (j, 0, 0)),
        pl.BlockSpec((1, CB), lambda j, i: (0, j)),
    ]
    yb = pl.BlockSpec((TS, CB), lambda j, i: (i, j))
    hb = pl.BlockSpec((TS, S), lambda j, i: (i, j))
    est = 2 * (2 * TS * S * 4 + 3 * TS * CB * 4 + 8 * 8 * S * 4 + 4 * CB * S * 2) + 3 * TS * S * 4
    return pl.pallas_call(
        body, grid=(NB, nt), in_specs=in_specs, out_specs=[yb, yb, hb, hb],
        out_shape=[_sds((LP, NB * CB), F32), _sds((LP, NB * CB), BF16), _sds((LP, NB * S), F32), _sds((LP, NB * S), F32)],
        scratch_shapes=[pltpu.VMEM((TS, S), F32), pltpu.VMEM((TS, S), F32), pltpu.VMEM((2, 8, S), F32)],
        compiler_params=_cparams(("parallel", "arbitrary"), est), name=name,
    )(z, coef, bbr, bbi, ccr, cci, dskip)


def _ssm_bwd(z, u_col0, dyg, y, hr, hi, coef_rev, bbr_t, bbi_t, ccr_t, cci_t, dskip, name):
    LP = z.shape[0]
    NB, S, CB = bbr_t.shape
    TS = _tile(LP, 640, 8)
    nt = LP // TS
    ng = TS // 8
    TN = (((0,), (0,)), ((), ()))

    def body(u_ref, dyg_ref, y_ref, hr_ref, hi_ref, tr_ref, ti_ref, coef_ref, bbr_ref, bbi_ref, ccr_ref, cci_ref, ds_ref,
             du_ref, dbbr_ref, dbbi_ref, dccr_ref, dcci_ref, dar_ref, dai_ref, dd_ref,
             gr, gi, carry, acc_bbr, acc_bbi, acc_ccr, acc_cci, acc_a, acc_d):
        i = pl.program_id(1)

        @pl.when(i == 0)
        def _():
            for ref in (carry, acc_bbr, acc_bbi, acc_ccr, acc_cci, acc_a, acc_d):
                ref[...] = jnp.zeros_like(ref)

        u = u_ref[...]
        dy = dyg_ref[...] * _gelu_grad(y_ref[...])
        dyb = dy.astype(BF16)
        gr[...] = jnp.dot(dyb, ccr_ref[...], preferred_element_type=F32)
        gi[...] = -jnp.dot(dyb, cci_ref[...], preferred_element_type=F32)
        row8 = lax.broadcasted_iota(jnp.int32, (8, S), 0)
        first_chunk = i == nt - 1
        tail_r = jnp.where(first_chunk, 0.0, tr_ref[...])
        tail_i = jnp.where(first_chunk, 0.0, ti_ref[...])

        def step(t, c):
            cr, ci, sar, sai = c
            g = ng - 1 - t
            r0 = pl.multiple_of(g * 8, 8)
            xr = gr[pl.ds(r0, 8), :]
            xi = gi[pl.ds(r0, 8), :]
            for n, k in enumerate((1, 2, 4)):
                xr, xi = _cmul_add(xr, xi, coef_ref[2 * n], coef_ref[2 * n + 1], pltpu.roll(xr, 8 - k, 0), pltpu.roll(xi, 8 - k, 0))
            xr, xi = _cmul_add(xr, xi, coef_ref[6], coef_ref[7], cr, ci)
            gr[pl.ds(r0, 8), :] = xr
            gi[pl.ds(r0, 8), :] = xi
            p0 = pl.multiple_of(jnp.maximum(g - 1, 0) * 8, 8)
            pr = jnp.where(g > 0, hr_ref[pl.ds(p0, 8), :], tail_r)
            pi_ = jnp.where(g > 0, hi_ref[pl.ds(p0, 8), :], tail_i)
            hpr = pltpu.roll(jnp.where(row8 == 7, pr, hr_ref[pl.ds(r0, 8), :]), 1, 0)
            hpi = pltpu.roll(jnp.where(row8 == 7, pi_, hi_ref[pl.ds(r0, 8), :]), 1, 0)
            sar = sar + xr * hpr + xi * hpi
            sai = sai + xi * hpr - xr * hpi
            return jnp.broadcast_to(xr[0:1, :], (8, S)), jnp.broadcast_to(xi[0:1, :], (8, S)), sar, sai

        zero = jnp.zeros((8, S), F32)
        cr, ci, sar, sai = lax.fori_loop(0, ng, step, (carry[0], carry[1], zero, zero))
        carry[0] = cr
        carry[1] = ci
        acc_a[0] += sar
        acc_a[1] += sai
        grb = gr[...].astype(BF16)
        gib = gi[...].astype(BF16)
        du = (jnp.dot(grb, bbr_ref[...], preferred_element_type=F32) + jnp.dot(gib, bbi_ref[...], preferred_element_type=F32)
              + ds_ref[...] * dy)
        du_ref[...] = du.astype(du_ref.dtype)
        acc_bbr[...] += lax.dot_general(u, grb, TN, preferred_element_type=F32)
        acc_bbi[...] += lax.dot_general(u, gib, TN, preferred_element_type=F32)
        acc_ccr[...] += lax.dot_general(hr_ref[...].astype(BF16), dyb, TN, preferred_element_type=F32)
        acc_cci[...] -= lax.dot_general(hi_ref[...].astype(BF16), dyb, TN, preferred_element_type=F32)
        acc_d[...] += jnp.sum(dy * u.astype(F32), axis=0, keepdims=True)

        @pl.when(i == nt - 1)
        def _():
            dbbr_ref[...] = acc_bbr[...]
            dbbi_ref[...] = acc_bbi[...]
            dccr_ref[...] = acc_ccr[...]
            dcci_ref[...] = acc_cci[...]
            dar_ref[...] = jnp.sum(acc_a[0], axis=0, keepdims=True)
            dai_ref[...] = jnp.sum(acc_a[1], axis=0, keepdims=True)
            dd_ref[...] = acc_d[...]

    ucb = u_col0 // CB
    rev = lambda i: nt - 1 - i
    tail = lambda j, i: (jnp.maximum(rev(i) * ng - 1, 0), j)
    yb = pl.BlockSpec((TS, CB), lambda j, i: (rev(i), j))
    hb = pl.BlockSpec((TS, S), lambda j, i: (rev(i), j))
    in_specs = [
        pl.BlockSpec((TS, CB), lambda j, i: (rev(i), ucb + j)), yb, yb, hb, hb,
        pl.BlockSpec((8, S), tail), pl.BlockSpec((8, S), tail),
        pl.BlockSpec((None, 8, 8, S), lambda j, i: (j, 0, 0, 0)),
        pl.BlockSpec((None, S, CB), lambda j, i: (j, 0, 0)),
        pl.BlockSpec((None, S, CB), lambda j, i: (j, 0, 0)),
        pl.BlockSpec((None, CB, S), lambda j, i: (j, 0, 0)),
        pl.BlockSpec((None, CB, S), lambda j, i: (j, 0, 0)),
        pl.BlockSpec((1, CB), lambda j, i: (0, j)),
    ]
    mat_cs = pl.BlockSpec((None, CB, S), lambda j, i: (j, 0, 0))
    mat_sc = pl.BlockSpec((None, S, CB), lambda j, i: (j, 0, 0))
    vec_s = pl.BlockSpec((None, 1, S), lambda j, i: (j, 0, 0))
    out_specs = [yb, mat_cs, mat_cs, mat_sc, mat_sc, vec_s, vec_s, pl.BlockSpec((1, CB), lambda j, i: (0, j))]
    out_shape = [_sds((LP, NB * CB), BF16), _sds((NB, CB, S), F32), _sds((NB, CB, S), F32), _sds((NB, S, CB), F32),
                 _sds((NB, S, CB), F32), _sds((NB, 1, S), F32), _sds((NB, 1, S), F32), _sds((1, NB * CB), F32)]
    scratch = [pltpu.VMEM((TS, S), F32), pltpu.VMEM((TS, S), F32), pltpu.VMEM((2, 8, S), F32),
               pltpu.VMEM((CB, S), F32), pltpu.VMEM((CB, S), F32), pltpu.VMEM((S, CB), F32), pltpu.VMEM((S, CB), F32),
               pltpu.VMEM((2, 8, S), F32), pltpu.VMEM((1, CB), F32)]
    est = 2 * (2 * TS * S * 4 + 4 * TS * CB * 4 + 8 * 8 * S * 4 + 12 * CB * S * 4) + 4 * TS * S * 4
    return pl.pallas_call(
        body, grid=(NB, nt), in_specs=in_specs, out_specs=out_specs, out_shape=out_shape, scratch_shapes=scratch,
        compiler_params=_cparams(("parallel", "arbitrary"), est), name=name,
    )(z, dyg, y, hr, hi, hr, hi, coef_rev, bbr_t, bbi_t, ccr_t, cci_t, dskip)


def _merge_fwd(yab, z, ao, D, ga0, gb0, name):
    LP = z.shape[0]
    tm = _tile(LP, 640, 8)
    tn = _ctile(512, D, ga0, gb0)
    nj = D // tn

    def body(ya_ref, yb_ref, ga_ref, gb_ref, ao_ref, o_ref):
        f = lambda r: r[...].astype(F32)
        ssm = f(ya_ref) * _sig(f(yb_ref))
        o_ref[...] = (_sig(f(ga_ref)) * ssm + _sig(f(gb_ref)) * f(ao_ref)).astype(o_ref.dtype)

    blk = lambda c0: pl.BlockSpec((tm, tn), lambda i, j: (i, c0 // tn + j))
    return pl.pallas_call(
        body, grid=(LP // tm, nj), in_specs=[blk(0), blk(D), blk(ga0), blk(gb0), blk(0)], out_specs=blk(0),
        out_shape=_sds((LP, D), BF16), compiler_params=_cparams(("parallel", "parallel"), 2 * 6 * tm * tn * 4), name=name,
    )(yab, yab, z, z, ao)


def _merge_bwd(dm, yab, z, ao, D, ga0, gb0, name):
    LP = z.shape[0]
    tm = _tile(LP, 640, 8)
    tn = _ctile(512, D, ga0, gb0)
    nj = D // tn

    def body(dm_ref, ya_ref, yb_ref, ga_ref, gb_ref, ao_ref, dya_ref, dyb_ref, dga_ref, dgb_ref, dao_ref):
        f = lambda r: r[...].astype(F32)
        dmv, ya, ao_v = f(dm_ref), f(ya_ref), f(ao_ref)
        sa, sb, sy = _sig(f(ga_ref)), _sig(f(gb_ref)), _sig(f(yb_ref))
        t = dmv * sa
        dya_ref[...] = (t * sy).astype(BF16)
        dyb_ref[...] = (t * ya * sy * (1.0 - sy)).astype(BF16)
        dga_ref[...] = (dmv * (ya * sy) * sa * (1.0 - sa)).astype(BF16)
        dgb_ref[...] = (dmv * ao_v * sb * (1.0 - sb)).astype(BF16)
        dao_ref[...] = (dmv * sb).astype(BF16)

    blk = lambda c0: pl.BlockSpec((tm, tn), lambda i, j: (i, c0 // tn + j))
    return pl.pallas_call(
        body, grid=(LP // tm, nj), in_specs=[blk(0), blk(0), blk(D), blk(ga0), blk(gb0), blk(0)], out_specs=[blk(0)] * 5,
        out_shape=[_sds((LP, D), BF16)] * 5, compiler_params=_cparams(("parallel", "parallel"), 2 * 11 * tm * tn * 4), name=name,
    )(dm, yab, yab, z, z, ao)


def _shift_down(x, halo, k, row8):
    s = pltpu.roll(x, k, 0)
    top = jnp.where(row8 < k, pltpu.roll(halo, k, 0), s[0:8])
    return jnp.concatenate([top, s[8:]], axis=0) if x.shape[0] > 8 else top


def _shift_up(x, halo, k, row8):
    tm = x.shape[0]
    s = pltpu.roll(x, tm - k, 0)
    bot = jnp.where(row8 >= 8 - k, pltpu.roll(halo, 8 - k, 0), s[tm - 8:])
    return jnp.concatenate([s[:tm - 8], bot], axis=0) if tm > 8 else bot


def _conv_gate(g, halo, w_ref, cb, row8):
    return cb + w_ref[0:1, :] * _shift_down(g, halo, 2, row8) + w_ref[1:2, :] * _shift_down(g, halo, 1, row8) + w_ref[2:3, :] * g


def _convact_fwd(gu, cw, cb, DFF, name):
    LP = gu.shape[0]
    tm = _tile(LP, 640, 8)
    tn = _tile(DFF, 512)
    nj = DFF // tn
    t8 = tm // 8

    def body(g_ref, h_ref, u_ref, w_ref, b_ref, o_ref):
        i = pl.program_id(0)
        row8 = lax.broadcasted_iota(jnp.int32, (8, tn), 0)
        g = g_ref[...].astype(F32)
        halo = jnp.where(i > 0, h_ref[...].astype(F32), 0.0)
        gc = _conv_gate(g, halo, w_ref, b_ref[...], row8)
        o_ref[...] = (gc * _sig(gc) * u_ref[...].astype(F32)).astype(o_ref.dtype)

    in_specs = [
        pl.BlockSpec((tm, tn), lambda i, j: (i, j)),
        pl.BlockSpec((8, tn), lambda i, j: (jnp.maximum(i * t8 - 1, 0), j)),
        pl.BlockSpec((tm, tn), lambda i, j: (i, nj + j)),
        pl.BlockSpec((3, tn), lambda i, j: (0, j)),
        pl.BlockSpec((1, tn), lambda i, j: (0, j)),
    ]
    return pl.pallas_call(
        body, grid=(LP // tm, nj), in_specs=in_specs, out_specs=pl.BlockSpec((tm, tn), lambda i, j: (i, j)),
        out_shape=_sds((LP, DFF), BF16), compiler_params=_cparams(("parallel", "parallel"), 2 * 8 * tm * tn * 4), name=name,
    )(gu, gu, gu, cw, cb)


def _convact_bwd(dact, gu, cw, cb, DFF, name):
    LP = gu.shape[0]
    tm = _tile(LP, 640, 8)
    tn = _tile(DFF, 512)
    nj = DFF // tn
    t8 = tm // 8

    def body(da_ref, g_ref, h_ref, u_ref, w_ref, b_ref, dgc_ref, du_ref):
        i = pl.program_id(0)
        row8 = lax.broadcasted_iota(jnp.int32, (8, tn), 0)
        g = g_ref[...].astype(F32)
        halo = jnp.where(i > 0, h_ref[...].astype(F32), 0.0)
        gc = _conv_gate(g, halo, w_ref, b_ref[...], row8)
        sg = _sig(gc)
        da = da_ref[...].astype(F32)
        du_ref[...] = (da * gc * sg).astype(du_ref.dtype)
        dgc_ref[...] = da * u_ref[...].astype(F32) * sg * (1.0 + gc * (1.0 - sg))

    blk = pl.BlockSpec((tm, tn), lambda i, j: (i, j))
    in_specs = [
        blk, blk,
        pl.BlockSpec((8, tn), lambda i, j: (jnp.maximum(i * t8 - 1, 0), j)),
        pl.BlockSpec((tm, tn), lambda i, j: (i, nj + j)),
        pl.BlockSpec((3, tn), lambda i, j: (0, j)),
        pl.BlockSpec((1, tn), lambda i, j: (0, j)),
    ]
    return pl.pallas_call(
        body, grid=(LP // tm, nj), in_specs=in_specs, out_specs=[blk, blk],
        out_shape=[_sds((LP, DFF), F32), _sds((LP, DFF), BF16)],
        compiler_params=_cparams(("parallel", "parallel"), 2 * 10 * tm * tn * 4), name=name,
    )(dact, gu, gu, gu, cw, cb)


def _conv_bwd(dgc, gu, cw, DFF, pad, name):
    LP = gu.shape[0]
    tm = _tile(LP, 640, 8)
    tn = _tile(DFF, 512)
    nj = DFF // tn
    t8 = tm // 8
    nt = LP // tm

    def body(d_ref, dn_ref, g_ref, h_ref, w_ref, dg_ref, dw_ref, db_ref):
        i = pl.program_id(1)
        row8 = lax.broadcasted_iota(jnp.int32, (8, tn), 0)
        d = d_ref[...]
        nxt = jnp.where(i < nt - 1, dn_ref[...], 0.0)
        dg = w_ref[2:3, :] * d + w_ref[1:2, :] * _shift_up(d, nxt, 1, row8) + w_ref[0:1, :] * _shift_up(d, nxt, 2, row8)
        rows = i * tm + lax.broadcasted_iota(jnp.int32, (tm, 1), 0)
        dg_ref[...] = jnp.where(rows >= pad, dg, 0.0).astype(dg_ref.dtype)
        g = g_ref[...].astype(F32)
        halo = jnp.where(i > 0, h_ref[...].astype(F32), 0.0)
        row3 = lax.broadcasted_iota(jnp.int32, (3, tn), 0)
        s0 = jnp.sum(d * _shift_down(g, halo, 2, row8), axis=0, keepdims=True)
        s1 = jnp.sum(d * _shift_down(g, halo, 1, row8), axis=0, keepdims=True)
        s2 = jnp.sum(d * g, axis=0, keepdims=True)
        dw = jnp.where(row3 == 0, s0, jnp.where(row3 == 1, s1, s2))
        dbp = jnp.sum(d, axis=0, keepdims=True)

        @pl.when(i == 0)
        def _():
            dw_ref[...] = dw
            db_ref[...] = dbp

        @pl.when(i > 0)
        def _():
            dw_ref[...] += dw
            db_ref[...] += dbp

    blk = pl.BlockSpec((tm, tn), lambda j, i: (i, j))
    in_specs = [
        blk,
        pl.BlockSpec((8, tn), lambda j, i: (jnp.minimum((i + 1) * t8, LP // 8 - 1), j)),
        blk,
        pl.BlockSpec((8, tn), lambda j, i: (jnp.maximum(i * t8 - 1, 0), j)),
        pl.BlockSpec((3, tn), lambda j, i: (0, j)),
    ]
    out_specs = [blk, pl.BlockSpec((3, tn), lambda j, i: (0, j)), pl.BlockSpec((1, tn), lambda j, i: (0, j))]
    return pl.pallas_call(
        body, grid=(nj, nt), in_specs=in_specs, out_specs=out_specs,
        out_shape=[_sds((LP, DFF), BF16), _sds((3, DFF), F32), _sds((1, DFF), F32)],
        compiler_params=_cparams(("parallel", "arbitrary"), 2 * 10 * tm * tn * 4), name=name,
    )(dgc, dgc, gu, gu, cw)


def _adamw_math(w, g, m, v):
    m = ADAM_B1 * m + (1.0 - ADAM_B1) * g
    v = ADAM_B2 * v + (1.0 - ADAM_B2) * (g * g)
    m_hat = m / (1.0 - ADAM_B1 ** ADAM_STEP)
    v_hat = v / (1.0 - ADAM_B2 ** ADAM_STEP)
    delta = -ADAM_LR * (m_hat / (jnp.sqrt(v_hat) + ADAM_EPS) + ADAM_WD * w)
    return delta, m, v


def _adamw(w, g, m, v, name):
    R, C = w.shape
    tm = R if R * C * 4 <= (1 << 20) else _tile(R, max(8, ((1 << 20) // (C * 4)) // 8 * 8), 8)

    def body(w_ref, g_ref, m_ref, v_ref, d_ref, mo_ref, vo_ref):
        d_ref[...], mo_ref[...], vo_ref[...] = _adamw_math(w_ref[...], g_ref[...], m_ref[...], v_ref[...])

    blk = pl.BlockSpec((tm, C), lambda i: (i, 0))
    return pl.pallas_call(
        body, grid=(R // tm,), in_specs=[blk] * 4, out_specs=[blk] * 3, out_shape=[_sds((R, C), F32)] * 3,
        compiler_params=_cparams(("parallel",), 2 * 7 * tm * (C + LANES) * 4), name=name,
    )(w, g, m, v)


def _sum_adamw(parts, w, m, v, name):
    n, R, C = parts.shape
    tm = _tile(R, 256, 8)

    def body(p_ref, w_ref, m_ref, v_ref, g_ref, d_ref, mo_ref, vo_ref):
        g = p_ref[0]
        for k in range(1, n):
            g = g + p_ref[k]
        g_ref[...] = g
        d_ref[...], mo_ref[...], vo_ref[...] = _adamw_math(w_ref[...], g, m_ref[...], v_ref[...])

    blk = pl.BlockSpec((tm, C), lambda i: (i, 0))
    return pl.pallas_call(
        body, grid=(R // tm,), in_specs=[pl.BlockSpec((n, tm, C), lambda i: (0, i, 0))] + [blk] * 3, out_specs=[blk] * 4,
        out_shape=[_sds((R, C), F32)] * 4,
        compiler_params=_cparams(("parallel",), 2 * (n + 7) * tm * C * 4), name=name,
    )(parts, w, m, v)


def _add_half(g, got, c_idx, name):
    n, R, C = g.shape
    HR = R // 2
    tm = _tile(HR, max(8, ((1 << 20) // (C * 4)) // 8 * 8), 8)
    nb = HR // tm

    def body(c_ref, g_ref, t_ref, o_ref):
        o_ref[...] = (g_ref[...] + t_ref[...]).astype(o_ref.dtype)

    grid_spec = pltpu.PrefetchScalarGridSpec(
        num_scalar_prefetch=1, grid=(n, nb),
        in_specs=[pl.BlockSpec((None, tm, C), lambda k, i, c: (k, c[0] * nb + i, 0)),
                  pl.BlockSpec((None, tm, C), lambda k, i, c: (k, i, 0))],
        out_specs=pl.BlockSpec((None, tm, C), lambda k, i, c: (k, i, 0)))
    return pl.pallas_call(
        body, grid_spec=grid_spec, out_shape=_sds((n, HR, C), BF16),
        compiler_params=_cparams(("parallel", "parallel"), 2 * 3 * tm * (C + LANES) * 4), name=name,
    )(c_idx, g, got)


def _sum_half(g, got, land, chip_c, name):
    n, R, C = g.shape
    HR = R // 2
    tm = _tile(HR, max(8, ((1 << 20) // (C * 4)) // 8 * 8), 8)
    nb = HR // tm

    def body(s_ref, g_ref, t_ref, l_ref, o_ref):
        acc = g_ref[...] + t_ref[...]
        for k in range(3):
            acc = acc + l_ref[k].astype(F32)
        o_ref[...] = acc

    grid_spec = pltpu.PrefetchScalarGridSpec(
        num_scalar_prefetch=1, grid=(nb,),
        in_specs=[pl.BlockSpec((None, tm, C), lambda i, sc: (sc[0], sc[1] * nb + i, 0)),
                  pl.BlockSpec((None, tm, C), lambda i, sc: (sc[0], i, 0)),
                  pl.BlockSpec((3, tm, C), lambda i, sc: (0, i, 0))],
        out_specs=pl.BlockSpec((tm, C), lambda i, sc: (sc[1] * nb + i, 0)))
    return pl.pallas_call(
        body, grid_spec=grid_spec, out_shape=_sds((R, C), F32),
        compiler_params=_cparams(("parallel",), 2 * 6 * tm * (C + LANES) * 4), name=name,
    )(chip_c, g, got, land)


def _place():
    return lax.axis_index("x"), lax.axis_index("y"), lax.axis_index("c")


def _other_chips(x, y):
    return [(1 - x, y), (x, 1 - y), (1 - x, 1 - y)]


HBM_SPEC = pl.BlockSpec(memory_space=pltpu.HBM)


def _cast_slot(w, chip_idx, name):
    R, C = w.shape
    tm = _tile(R, max(16, ((1 << 20) // (C * 4)) // 16 * 16), 16)

    def body(s_ref, w_ref, o_ref):
        o_ref[...] = w_ref[...].astype(o_ref.dtype)

    grid_spec = pltpu.PrefetchScalarGridSpec(
        num_scalar_prefetch=1, grid=(R // tm,),
        in_specs=[pl.BlockSpec((tm, C), lambda i, sc: (i, 0))],
        out_specs=pl.BlockSpec((None, tm, C), lambda i, sc: (sc[0], i, 0)))
    return pl.pallas_call(
        body, grid_spec=grid_spec, out_shape=_sds((4, R, C), BF16),
        compiler_params=_cparams(("parallel",), 2 * 2 * tm * (C + LANES) * 4), name=name,
    )(chip_idx, w)


def _gather_chips(gs, name):
    nw = len(gs)
    halves = [g.shape[1] // 2 for g in gs]
    assert all(h % 16 == 0 for h in halves)

    def body(*refs):
        g_refs = refs[nw:2 * nw]
        send_sems, recv_sems = refs[2 * nw:]
        x, y, c = _place()
        chips = _other_chips(x, y)
        me = 2 * x + y

        def copy(i, k, chip, half, to):
            HR = halves[i]
            rows = g_refs[i].at[chip, pl.ds(pl.multiple_of(half * HR, 16), HR), :]
            return pltpu.make_async_remote_copy(
                src_ref=rows, dst_ref=rows, send_sem=send_sems.at[6 * i + k], recv_sem=recv_sems.at[6 * i + k],
                device_id=to, device_id_type=MESH)

        first = [copy(i, k, me, c, (cx, cy, c)) for i in range(nw) for k, (cx, cy) in enumerate(chips)]
        for cp in first:
            cp.start()
        passed = []
        for i in range(nw):
            for k, (cx, cy) in enumerate(chips):
                copy(i, k, 2 * cx + cy, c, (cx, cy, c)).wait_recv()
                fw = copy(i, 3 + k, 2 * cx + cy, c, (x, y, 1 - c))
                fw.start()
                passed.append(fw)
        for i in range(nw):
            for k, (cx, cy) in enumerate(chips):
                copy(i, 3 + k, 2 * cx + cy, 1 - c, (x, y, 1 - c)).wait_recv()
        for cp in first + passed:
            cp.wait_send()

    return pl.pallas_call(
        body, out_shape=[_sds(g.shape, g.dtype) for g in gs], in_specs=[HBM_SPEC] * nw, out_specs=[HBM_SPEC] * nw,
        scratch_shapes=[pltpu.SemaphoreType.DMA((6 * nw,)), pltpu.SemaphoreType.DMA((6 * nw,))],
        input_output_aliases={i: i for i in range(nw)}, name=name,
    )(*gs)


def _sibling_halves(gs, name):
    nw = len(gs)
    halves = [g.shape[1] // 2 for g in gs]
    assert all(h % 8 == 0 for h in halves)

    def body(*refs):
        g_refs, land_refs = refs[:nw], refs[nw:2 * nw]
        send_sems, recv_sems = refs[2 * nw:]
        x, y, c = _place()
        cps = []
        for i in range(nw):
            q0 = pl.multiple_of((1 - c) * halves[i], 8)
            cp = pltpu.make_async_remote_copy(
                src_ref=g_refs[i].at[pl.ds(0, 4), pl.ds(q0, halves[i]), :], dst_ref=land_refs[i],
                send_sem=send_sems.at[i], recv_sem=recv_sems.at[i], device_id=(x, y, 1 - c), device_id_type=MESH)
            cp.start()
            cps.append(cp)
        for cp in cps:
            cp.wait()

    return pl.pallas_call(
        body, out_shape=[_sds((4, h, g.shape[2]), g.dtype) for g, h in zip(gs, halves)],
        in_specs=[HBM_SPEC] * nw, out_specs=[HBM_SPEC] * nw,
        scratch_shapes=[pltpu.SemaphoreType.DMA((nw,)), pltpu.SemaphoreType.DMA((nw,))], name=name,
    )(*gs)


def _scatter_chips(ss, name):
    nw = len(ss)

    def body(*refs):
        s_refs, land_refs = refs[:nw], refs[nw:2 * nw]
        send_sems, recv_sems = refs[2 * nw:]
        x, y, c = _place()
        chips = _other_chips(x, y)
        sends = []
        for i in range(nw):
            for k, (cx, cy) in enumerate(chips):
                cp = pltpu.make_async_remote_copy(
                    src_ref=s_refs[i].at[2 * cx + cy], dst_ref=land_refs[i].at[k],
                    send_sem=send_sems.at[3 * i + k], recv_sem=recv_sems.at[3 * i + k],
                    device_id=(cx, cy, c), device_id_type=MESH)
                cp.start()
                sends.append(cp)
        for cp in sends:
            cp.wait_recv()
        for cp in sends:
            cp.wait_send()

    return pl.pallas_call(
        body, out_shape=[_sds((3,) + s.shape[1:], s.dtype) for s in ss], in_specs=[HBM_SPEC] * nw, out_specs=[HBM_SPEC] * nw,
        scratch_shapes=[pltpu.SemaphoreType.DMA((3 * nw,)), pltpu.SemaphoreType.DMA((3 * nw,))], name=name,
    )(*ss)


def _sibling_join(fs, name):
    nw = len(fs)
    assert all(f.shape[0] % 16 == 0 for f in fs)

    def body(*refs):
        o_refs = refs[nw:2 * nw]
        send_sems, recv_sems = refs[2 * nw:]
        x, y, c = _place()

        def copy(i, half):
            HR = o_refs[i].shape[0] // 2
            rows = o_refs[i].at[pl.ds(pl.multiple_of(half * HR, 8), HR), :]
            return pltpu.make_async_remote_copy(
                src_ref=rows, dst_ref=rows, send_sem=send_sems.at[i], recv_sem=recv_sems.at[i],
                device_id=(x, y, 1 - c), device_id_type=MESH)

        sends = [copy(i, c) for i in range(nw)]
        for cp in sends:
            cp.start()
        for i in range(nw):
            copy(i, 1 - c).wait_recv()
        for cp in sends:
            cp.wait_send()

    return pl.pallas_call(
        body, out_shape=[_sds(f.shape, f.dtype) for f in fs], in_specs=[HBM_SPEC] * nw, out_specs=[HBM_SPEC] * nw,
        scratch_shapes=[pltpu.SemaphoreType.DMA((nw,)), pltpu.SemaphoreType.DMA((nw,))],
        input_output_aliases={i: i for i in range(nw)}, name=name,
    )(*fs)


def _gather_all(v, name):
    M, W = v.shape

    def body(v_ref, o_ref, send_sems, recv_sems, local_sem):
        x, y, c = _place()
        me, sibling = (x, y, c), (x, y, 1 - c)
        chips = _other_chips(x, y)

        def slot(px, py, pc):
            return o_ref.at[4 * px + 2 * py + pc]

        def copy(k, block, to, src=None):
            return pltpu.make_async_remote_copy(
                src_ref=slot(*block) if src is None else src, dst_ref=slot(*block),
                send_sem=send_sems.at[k], recv_sem=recv_sems.at[k], device_id=to, device_id_type=MESH)

        mine = pltpu.make_async_copy(v_ref, slot(*me), local_sem)
        mine.start()
        first = [copy(0, me, sibling, src=v_ref)]
        first += [copy(1 + j, me, (*chip, c), src=v_ref) for j, chip in enumerate(chips)]
        for cp in first:
            cp.start()
        passed = [copy(4 + j, (*chip, c), sibling) for j, chip in enumerate(chips)]
        for j, chip in enumerate(chips):
            copy(1 + j, (*chip, c), me).wait_recv()
            passed[j].start()
        copy(0, sibling, me).wait_recv()
        for j, chip in enumerate(chips):
            copy(4 + j, (*chip, 1 - c), me).wait_recv()
        for cp in first + passed:
            cp.wait_send()
        mine.wait()

    vm = pl.BlockSpec(memory_space=pltpu.VMEM)
    return pl.pallas_call(
        body, out_shape=_sds((8, M, W), v.dtype), in_specs=[vm], out_specs=vm,
        scratch_shapes=[pltpu.SemaphoreType.DMA((7,)), pltpu.SemaphoreType.DMA((7,)), pltpu.SemaphoreType.DMA(())],
        compiler_params=pltpu.CompilerParams(vmem_limit_bytes=int(min(10 * M * W * 4 + (8 << 20), V7X_VMEM_BYTES - (8 << 20)))),
        name=name,
    )(v)


def _rows_for(n_elems, width, mult=8):
    rows = -(-n_elems // width)
    return -(-rows // mult) * mult


def _pack_small(arrs, total_rows):
    parts = []
    used = 0
    for a in arrs:
        rows = _rows_for(a.size, LANES)
        parts.append(jnp.pad(a.reshape(-1), (0, rows * LANES - a.size)).reshape(rows, LANES))
        used += rows
    if total_rows > used:
        parts.append(jnp.zeros((total_rows - used, LANES), F32))
    return jnp.concatenate(parts, axis=0)


def _unpack_small(p, shapes):
    outs, r = [], 0
    lead = p.shape[:-2]
    for shp in shapes:
        n = int(np.prod(shp))
        rows = _rows_for(n, LANES)
        outs.append(p[..., r:r + rows, :].reshape(lead + (rows * LANES,))[..., :n].reshape(lead + tuple(shp)))
        r += rows
    return outs


def _cols_to_chips(w):
    K, N = w.shape
    return w.reshape(K, 4, N // 4).transpose(1, 0, 2)


def _chips_to_cols(w):
    n4, K, n = w.shape
    return w.transpose(1, 0, 2).reshape(K, n4 * n)


def _block_diag(m, gpb):
    G, A, B = m.shape
    nb = G // gpb
    eye = jnp.eye(gpb, dtype=m.dtype)
    t = m.reshape(nb, gpb, A, B)[:, :, :, None, :] * eye[None, :, None, :, None]
    return t.reshape(nb, gpb * A, gpb * B)


def _block_diag_extract(m, gpb, A, B):
    nb = m.shape[0]
    t = m.reshape(nb, gpb, A, gpb, B)
    eye = jnp.eye(gpb, dtype=m.dtype)
    d = jnp.sum(t * eye[None, :, None, :, None], axis=3)
    return d.reshape(nb * gpb, A, B)


def kernel(x, meta, g_mix, w_in, b_f, lam_re, lam_im, log_dt, b_re, b_im, c_re, c_im, d_skip, w_glu, w_attn_o, w_out, g_ffn, w_up, conv_w, conv_b, w_down, g_final, loss_target, m_meta, m_g_mix, m_w_in, m_b_f, m_lam_re, m_lam_im, m_log_dt, m_b_re, m_b_im, m_c_re, m_c_im, m_d_skip, m_w_glu, m_w_attn_o, m_w_out, m_g_ffn, m_w_up, m_conv_w, m_conv_b, m_w_down, m_g_final, v_meta, v_g_mix, v_w_in, v_b_f, v_lam_re, v_lam_im, v_log_dt, v_b_re, v_b_im, v_c_re, v_c_im, v_d_skip, v_w_glu, v_w_attn_o, v_w_out, v_g_ffn, v_w_up, v_conv_w, v_conv_b, v_w_down, v_g_final):
    args = dict(locals())
    L, D = x.shape[1], x.shape[2]
    NM = meta.shape[0]
    H = b_f.shape[1]
    DA = H * HEAD_DIM
    G, P, C = b_re.shape[1:]
    DS, GP = G * C, G * P
    DFF = conv_b.shape[1]
    PAD = (-NM) % LANES
    OFF = PAD + NM
    LP = OFF + L
    NZ = 3 * DA + DS + 2 * D
    U0, GA0, GB0 = 3 * DA, 3 * DA + DS, 3 * DA + DS + D
    NB = G // GROUPS_PER_BLOCK
    chip = 2 * lax.axis_index("x") + lax.axis_index("y")
    core = lax.axis_index("c")

    big = ["w_in", "w_glu", "w_attn_o", "w_out", "w_up", "w_down"]
    local = {n: args[n][0] for n in big}
    chip_idx = chip.reshape(1).astype(jnp.int32)
    slots = [_cast_slot(local[n], chip_idx, "cast_" + n) for n in big]
    gathered = dict(zip(big, _gather_chips(slots, "gather_weights")))
    tiny_shapes = [conv_w.shape[1:], meta.shape]
    tiny_rows = sum(_rows_for(int(np.prod(sh)), LANES) for sh in tiny_shapes)
    tiny = _gather_all(_pack_small([conv_w[0], meta], tiny_rows), "gather_small_weights")[0::2]
    conv_w_c, meta_c = _unpack_small(tiny, tiny_shapes)
    conv_w_f = _chips_to_cols(conv_w_c)
    meta_full = _chips_to_cols(meta_c)
    w_in_f = _chips_to_cols(gathered["w_in"])
    w_f = jnp.pad(w_in_f[:, 3 * DA:3 * DA + H], ((0, 0), (0, LANES - H)))
    w_zf = jnp.concatenate([w_in_f[:, :3 * DA], w_in_f[:, 3 * DA + H:], w_f], axis=1)
    w_glu_c, w_ao_c, w_up_c = gathered["w_glu"], gathered["w_attn_o"], gathered["w_up"]
    N_GLU, N_AO, N_UP = w_glu_c.shape[2], w_ao_c.shape[2], w_up_c.shape[2]
    w_out_f = gathered["w_out"].reshape(D, D)
    w_down_f = gathered["w_down"].reshape(DFF, D)

    col = lambda a: a.reshape(GP, 1)
    lr_c, li_c = col(lam_re[0]), col(lam_im[0])
    ldt_c = jnp.repeat(log_dt[0], P).reshape(GP, 1)
    br2, bi2 = b_re[0].reshape(GP, C), b_im[0].reshape(GP, C)
    a_re, a_im, bb_re, bb_im, pw_re, pw_im = _ssm_prep(lr_c, li_c, ldt_c, br2, bi2, "ssm_prep")
    S = GROUPS_PER_BLOCK * P
    CB = GROUPS_PER_BLOCK * C
    pw_r = pw_re.T.reshape(8, NB, S).transpose(1, 0, 2)
    pw_i = pw_im.T.reshape(8, NB, S).transpose(1, 0, 2)
    row8 = jnp.arange(8)[None, :, None]

    def masked_power(pw, k, keep):
        return jnp.where(keep, pw[:, k - 1][:, None, :], 0.0)

    coef = jnp.stack(
        [masked_power(pw, k, row8 >= k) for k in (1, 2, 4) for pw in (pw_r, pw_i)] + [pw_r, pw_i], axis=1)
    coef_rev = jnp.stack(
        [masked_power(pw, k, row8 < 8 - k) for k in (1, 2, 4) for pw in (pw_r, -pw_i)]
        + [pw_r[:, ::-1], -pw_i[:, ::-1]], axis=1)
    bd = lambda m: _block_diag(m, GROUPS_PER_BLOCK)
    bbr3, bbi3 = bb_re.reshape(G, P, C), bb_im.reshape(G, P, C)
    bbr_cs = bd(bbr3.transpose(0, 2, 1)).astype(BF16)
    bbi_cs = bd(bbi3.transpose(0, 2, 1)).astype(BF16)
    bbr_sc = bd(bbr3).astype(BF16)
    bbi_sc = bd(bbi3).astype(BF16)
    ccr_sc = bd(c_re[0].transpose(0, 2, 1)).astype(BF16)
    cci_sc = bd(c_im[0].transpose(0, 2, 1)).astype(BF16)
    ccr_cs = bd(c_re[0]).astype(BF16)
    cci_cs = bd(c_im[0]).astype(BF16)

    h0 = jnp.concatenate([jnp.zeros((PAD, D), F32), meta_full, x[0]], axis=0)
    n1 = _rms_fwd(h0, g_mix, "rms_mix")
    z = _mm(n1, w_zf, "nn", LP, NZ, D, BF16, "in_proj")
    fpre = _mm(n1, w_zf, "nn", LP, LANES, D, F32, "in_proj_f", b_off=(0, NZ))
    bf_pad = jnp.pad(b_f, ((0, 0), (0, LANES - H)))
    fcum = _fgate_fwd(fpre, bf_pad, PAD, "fgate_fwd")
    key_bias = jnp.where(jnp.arange(LP)[:, None] >= PAD, -fcum, NEG)
    bias_t = key_bias.T[:H].reshape(H, 1, LP)
    attn, attn_f32, lse_t = _attn_fwd(z, bias_t, H, PAD, "attn_fwd")
    ao = _mm(attn, w_ao_c, "nn", LP, D, DA, BF16, "attn_out", b_chips=N_AO)
    y, yg, hs_re, hs_im = _ssm_fwd(z, U0, coef, bbr_cs, bbi_cs, ccr_sc, cci_sc, d_skip, "ssm_fwd")
    yab = _mm(yg, w_glu_c, "nn", LP, 2 * D, DS, BF16, "glu_proj", b_chips=N_GLU)
    merged = _merge_fwd(yab, z, ao, D, GA0, GB0, "merge_fwd")
    h1 = _mm(merged, w_out_f, "nn", LP, D, D, F32, "out_proj", res=h0)
    n2 = _rms_fwd(h1, g_ffn, "rms_ffn")
    gu = _mm(n2, w_up_c, "nn", LP, 2 * DFF, D, BF16, "up_proj", tn=1408, b_chips=N_UP)
    act = _convact_fwd(gu, conv_w_f, conv_b, DFF, "convact_fwd")
    h2 = _mm(act, w_down_f, "nn", LP, D, DFF, F32, "down_proj", res=h1, tn=512, tk=DFF)
    dh2, dg_final, loss_v = _final_loss(h2, g_final.reshape(1, D), loss_target[0], OFF, "final_loss")
    loss = lax.psum(loss_v[0, 0], ("x", "y", "c"))

    KW = dict(tm=512, tn=512, tk=LP)
    dact = _mm(dh2, w_down_f, "nt", LP, DFF, D, BF16, "down_bwd_x")
    dw_down = _mm(act, dh2, "tn", DFF, D, LP, F32, "down_bwd_w", **KW)
    dgc, du_ffn = _convact_bwd(dact, gu, conv_w_f, conv_b, DFF, "convact_bwd")
    dg_ffn_in, dconv_w, dconv_b = _conv_bwd(dgc, gu, conv_w_f, DFF, PAD, "conv_bwd")
    dn2 = _mm(dg_ffn_in, w_up_c, "nt", LP, D, DFF, F32, "up_bwd_x_g", tn=512, tk=N_UP, b_chips=N_UP)
    dn2 = _mm(du_ffn, w_up_c, "nt", LP, D, DFF, F32, "up_bwd_x_u", res=dn2, b_off=(0, DFF), tn=512, tk=N_UP, b_chips=N_UP)
    dw_up = _mm(n2, dg_ffn_in, "tn", D, DFF, LP, F32, "up_bwd_w_g", tm=512, tn=256, tk=LP, out_chips=N_UP,
                out_into=(jnp.zeros((4, D, N_UP), F32), 0))
    dw_up = _mm(n2, du_ffn, "tn", D, DFF, LP, F32, "up_bwd_w_u", tm=512, tn=256, tk=LP, out_chips=N_UP,
                out_into=(dw_up, DFF // N_UP))
    dh1, dg_ffn = _rms_bwd(h1, g_ffn, dn2, dh2, "rms_ffn_bwd")

    dmerged = _mm(dh1, w_out_f, "nt", LP, D, D, F32, "out_bwd_x")
    dw_out = _mm(merged, dh1, "tn", D, D, LP, F32, "out_bwd_w", **KW)
    dya, dyb, dga, dgb, dao = _merge_bwd(dmerged, yab, z, ao, D, GA0, GB0, "merge_bwd")
    dattn = _mm(dao, w_ao_c, "nt", LP, DA, D, BF16, "attn_out_bwd_x", b_chips=N_AO)
    dw_ao = _mm(attn, dao, "tn", DA, D, LP, F32, "attn_out_bwd_w", out_chips=N_AO, **KW)
    dyg = _mm(dya, w_glu_c, "nt", LP, DS, D, F32, "glu_bwd_x_a", b_chips=N_GLU)
    dyg = _mm(dyb, w_glu_c, "nt", LP, DS, D, F32, "glu_bwd_x_b", res=dyg, b_off=(0, D), b_chips=N_GLU)
    dw_glu = _mm(yg, dya, "tn", DS, D, LP, F32, "glu_bwd_w_a", out_chips=N_GLU,
                 out_into=(jnp.zeros((4, DS, N_GLU), F32), 0), **KW)
    dw_glu = _mm(yg, dyb, "tn", DS, D, LP, F32, "glu_bwd_w_b", out_chips=N_GLU, out_into=(dw_glu, D // N_GLU), **KW)
    (du_ssm, dbbr_d, dbbi_d, dccr_d, dcci_d, dar_b, dai_b, dd_skip) = _ssm_bwd(
        z, U0, dyg, y, hs_re, hs_im, coef_rev, bbr_sc, bbi_sc, ccr_cs, cci_cs, d_skip, "ssm_bwd")
    delta_t = _attn_delta(dattn, attn_f32, H, "attn_delta")
    dq, dk, dv, dbias_t = _attn_bwd(z, dattn, lse_t, delta_t, bias_t, H, "attn_bwd")
    dF = jnp.pad(-dbias_t[:, 0, :].T, ((0, 0), (0, LANES - H)))
    dfpre, db_f = _fgate_bwd(dF, fpre, bf_pad, PAD, "fgate_bwd")
    dz = jnp.concatenate([dq, dk, dv, du_ssm, dga, dgb, dfpre.astype(BF16)], axis=1)
    dn1 = _mm(dz, w_zf, "nt", LP, D, NZ + LANES, F32, "in_bwd_x", tm=320, tn=512, tk=NZ + LANES)
    dw_zf = _mm(n1, dz, "tn", D, NZ + LANES, LP, F32, "in_bwd_w", tm=256, tn=640, tk=LP)
    dh0, dg_mix = _rms_bwd(h0, g_mix, dn1, dh1, "rms_mix_bwd")
    grad_x = dh0[OFF:][None]
    dmeta_full = dh0[PAD:OFF]

    ext = lambda m, A, B: _block_diag_extract(m, GROUPS_PER_BLOCK, A, B)
    dbb_re = ext(dbbr_d, C, P).transpose(0, 2, 1).reshape(GP, C)
    dbb_im = ext(dbbi_d, C, P).transpose(0, 2, 1).reshape(GP, C)
    dc_re = ext(dccr_d, P, C).transpose(0, 2, 1)[None]
    dc_im = ext(dcci_d, P, C).transpose(0, 2, 1)[None]
    glr, gli, gldt, gbr, gbi = _ssm_prep_bwd(lr_c, li_c, ldt_c, br2, bi2, dar_b.reshape(GP, 1), dai_b.reshape(GP, 1),
                                             dbb_re, dbb_im, "ssm_prep_bwd")
    small_grads = {
        "g_mix": dg_mix, "b_f": db_f[:, :H], "lam_re": glr.reshape(1, G, P), "lam_im": gli.reshape(1, G, P),
        "log_dt": gldt.reshape(G, P).sum(axis=1)[None], "b_re": gbr.reshape(1, G, P, C), "b_im": gbi.reshape(1, G, P, C),
        "c_re": dc_re, "c_im": dc_im, "d_skip": dd_skip, "g_ffn": dg_ffn, "conv_b": dconv_b, "g_final": dg_final.reshape(D),
    }

    small = list(small_grads)
    rider_grads = [dconv_w, dmeta_full]
    small_shapes = [args[n].shape for n in small] + [g.shape for g in rider_grads]
    srows = sum(_rows_for(int(np.prod(sh)), LANES) for sh in small_shapes)
    srows = -(-srows // 256) * 256
    zeros_like_riders = [jnp.zeros(g.shape, F32) for g in rider_grads]
    pack = lambda arrs: _pack_small(arrs, srows)
    g_parts = _gather_all(pack([small_grads[n] for n in small] + rider_grads), "gather_small_grads")
    sm = _sum_adamw(g_parts, pack([args[n] for n in small] + zeros_like_riders),
                    pack([args["m_" + n] for n in small] + zeros_like_riders),
                    pack([args["v_" + n] for n in small] + zeros_like_riders), "small_adamw")
    unpacked = [_unpack_small(p, small_shapes) for p in sm]
    sg, sd, smm, svv = (dict(zip(small, u[:len(small)])) for u in unpacked)
    dconv_w_sum, dmeta_sum = unpacked[0][len(small):]
    n_cw, n_me = conv_w.shape[2], meta.shape[1]
    rider = {"conv_w": lax.dynamic_slice_in_dim(dconv_w_sum, chip * n_cw, n_cw, axis=1)[None],
             "meta": lax.dynamic_slice_in_dim(dmeta_sum, chip * n_me, n_me, axis=1)}

    dw_in_f = jnp.concatenate([dw_zf[:, :3 * DA], dw_zf[:, NZ:NZ + H], dw_zf[:, 3 * DA:NZ]], axis=1)
    full_grads = [_cols_to_chips(dw_in_f), dw_glu, dw_ao, dw_out.reshape(4, D // 4, D), dw_up, dw_down.reshape(4, DFF // 4, D)]
    c_idx = core.reshape(1).astype(jnp.int32)
    chip_c = jnp.stack([chip, core]).astype(jnp.int32)
    got = _sibling_halves(full_grads, "rs_sibling")
    part = [_add_half(g, t, c_idx, "rs_add_" + n) for n, g, t in zip(big, full_grads, got)]
    land = _scatter_chips(part, "rs_scatter")
    halves = [_sum_half(g, t, l_, chip_c, "rs_sum_" + n) for n, g, t, l_ in zip(big, full_grads, got, land)]
    shard_grads = dict(zip(big, _sibling_join(halves, "rs_join")))
    shard_grads.update({n: g.reshape(g.shape[-2:]) for n, g in rider.items()})
    bg, bd_, bm, bv = {}, {}, {}, {}
    for n, g in shard_grads.items():
        shp = args[n].shape
        two = (lambda a: a.reshape(shp[-2], shp[-1]))
        d_, m_, v_ = _adamw(two(args[n]), g, two(args["m_" + n]), two(args["v_" + n]), "adamw_" + n)
        bg[n], bd_[n], bm[n], bv[n] = g.reshape(shp), d_.reshape(shp), m_.reshape(shp), v_.reshape(shp)

    order = ["meta", "g_mix", "w_in", "b_f", "lam_re", "lam_im", "log_dt", "b_re", "b_im", "c_re", "c_im", "d_skip",
             "w_glu", "w_attn_o", "w_out", "g_ffn", "w_up", "conv_w", "conv_b", "w_down", "g_final"]
    pick = lambda bigd, smalld, n: bigd[n] if n in bigd else smalld[n]
    outs = [loss, grad_x]
    for bigd, smalld in ((bg, sg), (bd_, sd), (bm, smm), (bv, svv)):
        outs += [pick(bigd, smalld, n) for n in order]
    return tuple(outs)
```

```python
import functools
import math

import jax
import jax.numpy as jnp
import numpy as np
from jax import lax
from jax.experimental import pallas as pl
from jax.experimental.pallas import tpu as pltpu

F32 = jnp.float32
BF16 = jnp.bfloat16
MESH = pl.DeviceIdType.MESH

EPS = 1e-6
HEAD_DIM = 128
LANES = 128
NEG = -1e30
GELU_C = math.sqrt(2.0 / math.pi)
GELU_A = 0.044715
ADAM_LR, ADAM_B1, ADAM_B2, ADAM_EPS, ADAM_WD, ADAM_STEP = 0.001, 0.9, 0.999, 1e-08, 0.01, 10
V7X_VMEM_BYTES = 64 << 20
GROUPS_PER_BLOCK = 8


def _tile(n, pref, mult=LANES):
    if n <= pref:
        return n
    t = (pref // mult) * mult
    while t >= mult:
        if n % t == 0:
            return t
        t -= mult
    raise ValueError(f"no tile for {n} <= {pref} (multiple of {mult})")


def _ctile(pref, *vals):
    g = 0
    for v in vals:
        g = math.gcd(g, v)
    return _tile(g, pref)


def _cparams(sem, est_bytes):
    limit = int(min(max(est_bytes * 1.25 + (4 << 20), 16 << 20), V7X_VMEM_BYTES - (8 << 20)))
    return pltpu.CompilerParams(dimension_semantics=sem, vmem_limit_bytes=limit)


def _sds(shape, dtype):
    return jax.ShapeDtypeStruct(tuple(shape), dtype)


def _sig(x):
    return 1.0 / (1.0 + jnp.exp(-x))


def _gelu(x):
    t = jnp.tanh(GELU_C * (x + GELU_A * x * x * x))
    return 0.5 * x * (1.0 + t)


def _gelu_grad(x):
    t = jnp.tanh(GELU_C * (x + GELU_A * x * x * x))
    return 0.5 * (1.0 + t) + 0.5 * x * (1.0 - t * t) * GELU_C * (1.0 + 3.0 * GELU_A * x * x)


def _mm(a, b, mode, M, N, K, out_dtype, name, *, res=None, a_off=(0, 0), b_off=(0, 0),
        tm=640, tn=1024, tk=2048, b_chips=None, out_chips=None, out_into=None):
    tm, tn, tk = _tile(M, tm, 8 if mode != "tn" else LANES), _tile(N, tn), _tile(K, tk, LANES if mode != "tn" else 8)
    if b_chips is not None and mode == "nt":
        tk = _ctile(tk, tk, b_chips)
    if b_chips is not None and mode != "nt":
        tn = _ctile(tn, tn, b_chips)
    if out_chips is not None:
        tn = _ctile(tn, tn, out_chips)
    nk = K // tk
    ar, ac = a_off
    br, bc = b_off
    if mode == "tn":
        assert ar % tk == 0 and ac % tm == 0
        a_spec = pl.BlockSpec((tk, tm), lambda i, j, k: (k + ar // tk, i + ac // tm))
        a_dims = 0
    else:
        assert ar % tm == 0 and ac % tk == 0
        a_spec = pl.BlockSpec((tm, tk), lambda i, j, k: (i + ar // tm, k + ac // tk))
        a_dims = 1
    if mode == "nt":
        assert br % tn == 0 and bc % tk == 0
        if b_chips is None:
            b_spec = pl.BlockSpec((tn, tk), lambda i, j, k: (j + br // tn, k + bc // tk))
        else:
            per = b_chips // tk
            b_spec = pl.BlockSpec((None, tn, tk), lambda i, j, k: ((k + bc // tk) // per, j + br // tn, (k + bc // tk) % per))
        b_dims = 1
    else:
        assert br % tk == 0 and bc % tn == 0
        if b_chips is None:
            b_spec = pl.BlockSpec((tk, tn), lambda i, j, k: (k + br // tk, j + bc // tn))
        else:
            per = b_chips // tn
            b_spec = pl.BlockSpec((None, tk, tn), lambda i, j, k: ((j + bc // tn) // per, k + br // tk, (j + bc // tn) % per))
        b_dims = 0
    dims = (((a_dims,), (b_dims,)), ((), ()))
    if out_chips is None:
        o_spec = pl.BlockSpec((tm, tn), lambda i, j, k: (i, j))
        o_shape = _sds((M, N), out_dtype)
    else:
        per_o = out_chips // tn
        chip0 = 0 if out_into is None else out_into[1]
        o_spec = pl.BlockSpec((None, tm, tn), lambda i, j, k: (chip0 + j // per_o, i, j % per_o))
        o_shape = _sds((N // out_chips if out_into is None else 4, M, out_chips), out_dtype)
    has_res = res is not None
    has_into = out_into is not None

    def body(*refs):
        if has_res:
            a_ref, b_ref, r_ref, o_ref = refs[:4]
        elif has_into:
            a_ref, b_ref, _, o_ref = refs[:4]
            r_ref = None
        else:
            a_ref, b_ref, o_ref = refs[:3]
            r_ref = None
        part = lax.dot_general(a_ref[...].astype(BF16), b_ref[...].astype(BF16), dims, preferred_element_type=F32)

        def finish(acc):
            if has_res:
                acc = r_ref[...] + acc
            o_ref[...] = acc.astype(o_ref.dtype)

        if nk == 1:
            finish(part)
        else:
            acc_ref = refs[-1]
            k = pl.program_id(2)

            @pl.when(k == 0)
            def _():
                acc_ref[...] = part

            @pl.when(k > 0)
            def _():
                acc_ref[...] += part

            @pl.when(k == nk - 1)
            def _():
                finish(acc_ref[...])

    in_specs = [a_spec, b_spec] + ([o_spec] if has_res else []) + ([pl.BlockSpec(memory_space=pl.ANY)] if has_into else [])
    args = (a, b) + ((res,) if has_res else ()) + ((out_into[0],) if has_into else ())
    isz = lambda x: jnp.dtype(x.dtype).itemsize
    est = 2 * (tm * tk * isz(a) + tk * tn * isz(b) + tm * tn * jnp.dtype(out_dtype).itemsize) + tm * tn * 4 * (2 + 2 * has_res)
    return pl.pallas_call(
        body, grid=(M // tm, N // tn, nk), in_specs=in_specs, out_specs=o_spec, out_shape=o_shape,
        scratch_shapes=[pltpu.VMEM((tm, tn), F32)] if nk > 1 else [],
        input_output_aliases={2: 0} if has_into else {},
        compiler_params=_cparams(("parallel", "parallel", "arbitrary"), est), name=name,
    )(*args)


def _rms_fwd(h, g, name):
    LP, D = h.shape
    tm = _tile(LP, 640, 8)

    def body(h_ref, g_ref, o_ref):
        x = h_ref[...]
        r = lax.rsqrt(jnp.mean(x * x, axis=-1, keepdims=True) + EPS)
        o_ref[...] = (x * r * g_ref[...]).astype(o_ref.dtype)

    row = pl.BlockSpec((tm, D), lambda i: (i, 0))
    return pl.pallas_call(
        body, grid=(LP // tm,), in_specs=[row, pl.BlockSpec((1, D), lambda i: (0, 0))], out_specs=row,
        out_shape=_sds((LP, D), BF16), compiler_params=_cparams(("parallel",), 2 * tm * D * 6), name=name,
    )(h, g)


def _rms_bwd(h, g, dn, dres, name):
    LP, D = h.shape
    tm = _tile(LP, 320, 8)
    nt = LP // tm

    def body(h_ref, g_ref, dn_ref, dres_ref, dh_ref, dg_ref):
        i = pl.program_id(0)
        x = h_ref[...]
        r = lax.rsqrt(jnp.mean(x * x, axis=-1, keepdims=True) + EPS)
        xh = x * r
        dn_v = dn_ref[...]
        dxh = dn_v * g_ref[...]
        dh_ref[...] = dres_ref[...] + r * (dxh - xh * jnp.mean(dxh * xh, axis=-1, keepdims=True))
        part = jnp.sum(dn_v * xh, axis=0, keepdims=True)

        @pl.when(i == 0)
        def _():
            dg_ref[...] = part

        @pl.when(i > 0)
        def _():
            dg_ref[...] += part

    row = pl.BlockSpec((tm, D), lambda i: (i, 0))
    vec = pl.BlockSpec((1, D), lambda i: (0, 0))
    return pl.pallas_call(
        body, grid=(nt,), in_specs=[row, vec, row, row], out_specs=[row, vec],
        out_shape=[_sds((LP, D), F32), _sds((1, D), F32)],
        compiler_params=_cparams(("arbitrary",), 2 * 4 * tm * D * 4), name=name,
    )(h, g, dn, dres)


def _final_loss(h, g, tgt, off, name):
    LP, D = h.shape
    tm = LANES
    assert off % tm == 0
    ob = off // tm
    nt = LP // tm

    def body(h_ref, g_ref, t_ref, dh_ref, dg_ref, loss_ref):
        i = pl.program_id(0)
        x = h_ref[...]
        r = lax.rsqrt(jnp.mean(x * x, axis=-1, keepdims=True) + EPS)
        xh = x * r
        gv = g_ref[...]
        e = xh * gv - t_ref[...]
        valid = i >= ob
        dy = jnp.where(valid, e * (1.0 / D), 0.0)
        lpart = jnp.where(valid, 0.5 * jnp.sum(jnp.mean(e * e, axis=-1, keepdims=True), axis=0, keepdims=True), 0.0)
        dxh = dy * gv
        dh_ref[...] = r * (dxh - xh * jnp.mean(dxh * xh, axis=-1, keepdims=True))
        gpart = jnp.sum(dy * xh, axis=0, keepdims=True)

        @pl.when(i == 0)
        def _():
            dg_ref[...] = gpart
            loss_ref[...] = jnp.broadcast_to(lpart, loss_ref.shape)

        @pl.when(i > 0)
        def _():
            dg_ref[...] += gpart
            loss_ref[...] += jnp.broadcast_to(lpart, loss_ref.shape)

    row = pl.BlockSpec((tm, D), lambda i: (i, 0))
    vec = pl.BlockSpec((1, D), lambda i: (0, 0))
    return pl.pallas_call(
        body, grid=(nt,),
        in_specs=[row, vec, pl.BlockSpec((tm, D), lambda i: (jnp.maximum(i - ob, 0), 0))],
        out_specs=[row, vec, pl.BlockSpec((1, LANES), lambda i: (0, 0))],
        out_shape=[_sds((LP, D), F32), _sds((1, D), F32), _sds((1, LANES), F32)],
        compiler_params=_cparams(("arbitrary",), 2 * 3 * tm * D * 4), name=name,
    )(h, g, tgt)


def _fgate_fwd(fpre, bias, pad, name):
    LP, W = fpre.shape

    def body(f_ref, b_ref, o_ref):
        row8 = lax.broadcasted_iota(jnp.int32, (8, W), 0)
        bv = b_ref[...]

        def step(g, carry):
            r0 = pl.multiple_of(g * 8, 8)
            x = f_ref[pl.ds(r0, 8), :] + bv
            lf = jnp.minimum(x, 0.0) - jnp.log(1.0 + jnp.exp(-jnp.abs(x)))
            lf = jnp.where(r0 + row8 >= pad, lf, 0.0)
            for k in (1, 2, 4):
                lf = lf + jnp.where(row8 >= k, pltpu.roll(lf, k, 0), 0.0)
            lf = lf + carry
            o_ref[pl.ds(r0, 8), :] = lf
            return jnp.broadcast_to(lf[7:8, :], (8, W))

        lax.fori_loop(0, LP // 8, step, jnp.zeros((8, W), F32))

    return pl.pallas_call(
        body, out_shape=_sds((LP, W), F32),
        compiler_params=_cparams(None, 3 * LP * W * 4), name=name,
    )(fpre, bias)


def _fgate_bwd(dF, fpre, bias, pad, name):
    LP, W = fpre.shape
    ng = LP // 8

    def body(d_ref, f_ref, b_ref, o_ref, db_ref):
        row8 = lax.broadcasted_iota(jnp.int32, (8, W), 0)
        bv = b_ref[...]

        def step(t, carry):
            run, acc = carry
            g = ng - 1 - t
            r0 = pl.multiple_of(g * 8, 8)
            x = d_ref[pl.ds(r0, 8), :]
            for k in (1, 2, 4):
                x = x + jnp.where(row8 < 8 - k, pltpu.roll(x, 8 - k, 0), 0.0)
            x = x + run
            df = x * _sig(-(f_ref[pl.ds(r0, 8), :] + bv))
            df = jnp.where(r0 + row8 >= pad, df, 0.0)
            o_ref[pl.ds(r0, 8), :] = df
            return jnp.broadcast_to(x[0:1, :], (8, W)), acc + df

        _, acc = lax.fori_loop(0, ng, step, (jnp.zeros((8, W), F32), jnp.zeros((8, W), F32)))
        db_ref[...] = jnp.sum(acc, axis=0, keepdims=True)

    return pl.pallas_call(
        body, out_shape=[_sds((LP, W), F32), _sds((1, W), F32)],
        compiler_params=_cparams(None, 4 * LP * W * 4), name=name,
    )(dF, fpre, bias)


def _place():
    return lax.axis_index("x"), lax.axis_index("y"), lax.axis_index("c")


def _other_chips(x, y):
    return [(1 - x, y), (x, 1 - y), (1 - x, 1 - y)]


def _gather_phases(g_refs, send_sems, recv_sems):
    nw = len(g_refs)
    x, y, c = _place()
    chips = _other_chips(x, y)
    me = 2 * x + y

    def copy(i, k, chip, half, to):
        HR = g_refs[i].shape[1] // 2
        rows = g_refs[i].at[chip, pl.ds(pl.multiple_of(half * HR, 16), HR), :]
        return pltpu.make_async_remote_copy(
            src_ref=rows, dst_ref=rows, send_sem=send_sems.at[6 * i + k], recv_sem=recv_sems.at[6 * i + k],
            device_id=to, device_id_type=MESH)

    pairs = [(i, k, cx, cy) for i in range(nw) for k, (cx, cy) in enumerate(chips)]

    def start():
        for i, k, cx, cy in pairs:
            copy(i, k, me, c, (cx, cy, c)).start()

    def forward():
        for i, k, cx, cy in pairs:
            copy(i, k, 2 * cx + cy, c, (cx, cy, c)).wait_recv()
            copy(i, 3 + k, 2 * cx + cy, c, (x, y, 1 - c)).start()

    def finish():
        for i, k, cx, cy in pairs:
            copy(i, 3 + k, 2 * cx + cy, 1 - c, (x, y, 1 - c)).wait_recv()
        for i, k, cx, cy in pairs:
            copy(i, k, me, c, (cx, cy, c)).wait_send()
            copy(i, 3 + k, 2 * cx + cy, c, (x, y, 1 - c)).wait_send()

    return start, forward, finish


def _scatter_phases(s_refs, land_refs, send_sems, recv_sems):
    x, y, c = _place()
    chips = _other_chips(x, y)

    def copy(i, k, cx, cy):
        return pltpu.make_async_remote_copy(
            src_ref=s_refs[i].at[2 * cx + cy], dst_ref=land_refs[i].at[k],
            send_sem=send_sems.at[3 * i + k], recv_sem=recv_sems.at[3 * i + k],
            device_id=(cx, cy, c), device_id_type=MESH)

    pairs = [(i, k, cx, cy) for i in range(len(s_refs)) for k, (cx, cy) in enumerate(chips)]

    def start():
        for p in pairs:
            copy(*p).start()

    def finish():
        for p in pairs:
            copy(*p).wait_recv()
        for p in pairs:
            copy(*p).wait_send()

    return start, finish


HBM_SPEC = pl.BlockSpec(memory_space=pltpu.HBM)


def _col_to_row(col):
    n = col.shape[0]
    return jnp.transpose(jnp.broadcast_to(col, (n, LANES)))[0:1, :]


def _row_to_col(row):
    n = row.shape[1]
    return jnp.transpose(jnp.broadcast_to(row, (LANES, n)))[:, 0:1]


def _attn_fwd(z, bias_t, H, pad, name, gather=()):
    LP = z.shape[0]
    BQ = BK = _tile(LP, 640)
    scale = HEAD_DIM ** -0.5
    NT = (((1,), (1,)), ((), ()))

    nw = len(gather)
    nq = LP // BQ

    def body(*refs):
        q_ref, k_ref, v_ref, b_ref = refs[:4]
        o_ref, of_ref, lse_ref = refs[4 + nw:7 + nw]
        hd = pl.program_id(0)
        qi = pl.program_id(1)
        if nw:
            start, forward, finish = _gather_phases(refs[7 + nw:7 + 2 * nw], *refs[7 + 2 * nw:])
            pl.when((hd == 0) & (qi == 0))(start)
            pl.when((hd == H // 2) & (qi == 0))(forward)
        q = q_ref[...]

        def tile(kb, carry, masked):
            m, l, acc = carry
            k0 = pl.multiple_of(kb * BK, BK)
            s = lax.dot_general(q, k_ref[pl.ds(k0, BK), :], NT, preferred_element_type=F32) * scale
            s = s + b_ref[:, pl.ds(k0, BK)]
            if masked:
                ri = lax.broadcasted_iota(jnp.int32, (BQ, BK), 0)
                ci = lax.broadcasted_iota(jnp.int32, (BQ, BK), 1)
                s = jnp.where(ri >= ci, s, NEG)
            mn = jnp.maximum(m, jnp.max(s, axis=-1, keepdims=True))
            p = jnp.exp(s - mn)
            alpha = jnp.exp(m - mn)
            l = alpha * l + jnp.sum(p, axis=-1, keepdims=True)
            vk = v_ref[pl.ds(k0, BK), :]
            p_hi = p.astype(BF16)
            p_lo = (p - p_hi.astype(F32)).astype(BF16)
            pv = jnp.dot(p_hi, vk, preferred_element_type=F32) + jnp.dot(p_lo, vk, preferred_element_type=F32)
            return mn, l, alpha * acc + pv

        carry = (jnp.full((BQ, 1), NEG, F32), jnp.zeros((BQ, 1), F32), jnp.zeros((BQ, HEAD_DIM), F32))
        carry = lax.fori_loop(0, qi, lambda kb, c: tile(kb, c, False), carry)
        m, l, acc = tile(qi, carry, True)
        rows = qi * BQ + lax.broadcasted_iota(jnp.int32, (BQ, 1), 0)
        o = jnp.where(rows >= pad, acc / l, 0.0)
        o_ref[...] = o.astype(o_ref.dtype)
        of_ref[...] = o
        lse_ref[...] = _col_to_row(m + jnp.log(l))
        if nw:
            pl.when((hd == H - 1) & (qi == nq - 1))(finish)

    in_specs = [
        pl.BlockSpec((BQ, HEAD_DIM), lambda h, i: (i, h)),
        pl.BlockSpec((LP, HEAD_DIM), lambda h, i: (0, H + h)),
        pl.BlockSpec((LP, HEAD_DIM), lambda h, i: (0, 2 * H + h)),
        pl.BlockSpec((None, 1, LP), lambda h, i: (h, 0, 0)),
    ]
    out_specs = [
        pl.BlockSpec((BQ, HEAD_DIM), lambda h, i: (i, h)),
        pl.BlockSpec((BQ, HEAD_DIM), lambda h, i: (i, h)),
        pl.BlockSpec((None, 1, BQ), lambda h, i: (h, 0, i)),
    ]
    est = 2 * (2 * LP * HEAD_DIM * 2 + 8 * LP * 4) + 20 * BQ * LANES * 4 + 8 * BQ * BK * 4
    sems = [pltpu.SemaphoreType.DMA((6 * nw,)), pltpu.SemaphoreType.DMA((6 * nw,))] if nw else []
    return pl.pallas_call(
        body, grid=(H, nq), in_specs=in_specs + [HBM_SPEC] * nw, out_specs=out_specs + [HBM_SPEC] * nw,
        out_shape=[_sds((LP, H * HEAD_DIM), BF16), _sds((LP, H * HEAD_DIM), F32), _sds((H, 1, LP), F32)]
        + [_sds(g.shape, g.dtype) for g in gather],
        scratch_shapes=sems, input_output_aliases={4 + i: 3 + i for i in range(nw)},
        compiler_params=_cparams(("arbitrary", "arbitrary"), est), name=name,
    )(z, z, z, bias_t, *gather)


def _attn_delta(do, o, H, name):
    LP = do.shape[0]
    tm = _tile(LP, 640)

    def body(do_ref, o_ref, d_ref):
        d_ref[...] = _col_to_row(jnp.sum(do_ref[...].astype(F32) * o_ref[...].astype(F32), axis=-1, keepdims=True))

    blk = pl.BlockSpec((tm, HEAD_DIM), lambda h, i: (i, h))
    return pl.pallas_call(
        body, grid=(H, LP // tm), in_specs=[blk, blk],
        out_specs=pl.BlockSpec((None, 1, tm), lambda h, i: (h, 0, i)),
        out_shape=_sds((H, 1, LP), F32),
        compiler_params=_cparams(("parallel", "parallel"), 8 * tm * LANES * 4), name=name,
    )(do, o)


def _attn_bwd(z, do, lse_t, delta_t, bias_t, H, name, scatter=()):
    LP = z.shape[0]
    BK = BQ = _tile(LP, 640)
    nk = nq = LP // BK
    scale = HEAD_DIM ** -0.5
    NT = (((1,), (1,)), ((), ()))
    TN = (((0,), (0,)), ((), ()))

    nw = len(scatter)

    def body(*refs):
        q_ref, k_ref, v_ref, do_ref, lse_ref, dl_ref, b_ref = refs[:7]
        dq_ref, dk_ref, dv_ref, db_ref = refs[7 + nw:11 + nw]
        dq_acc = refs[11 + 2 * nw]
        hd = pl.program_id(0)
        kj = pl.program_id(1)
        if nw:
            start, finish = _scatter_phases(refs[7:7 + nw], refs[11 + nw:11 + 2 * nw], *refs[12 + 2 * nw:])
            pl.when((hd == 0) & (kj == 0))(start)

        @pl.when(kj == 0)
        def _():
            dq_acc[...] = jnp.zeros_like(dq_acc)

        k = k_ref[...]
        v = v_ref[...]
        bcol = _row_to_col(b_ref[:, pl.ds(pl.multiple_of(kj * BK, BK), BK)])

        def tile(qc, carry, masked):
            dk, dv, db = carry
            q0 = pl.multiple_of(qc * BQ, BQ)
            q = q_ref[pl.ds(q0, BQ), :]
            dout = do_ref[pl.ds(q0, BQ), :]
            st = lax.dot_general(k, q, NT, preferred_element_type=F32) * scale + bcol
            if masked:
                ri = lax.broadcasted_iota(jnp.int32, (BK, BQ), 0)
                ci = lax.broadcasted_iota(jnp.int32, (BK, BQ), 1)
                st = jnp.where(ci >= ri, st, NEG)
            pt = jnp.exp(st - lse_ref[:, pl.ds(q0, BQ)])
            dv = dv + jnp.dot(pt.astype(BF16), dout, preferred_element_type=F32)
            dpt = lax.dot_general(v, dout, NT, preferred_element_type=F32)
            dst = pt * (dpt - dl_ref[:, pl.ds(q0, BQ)])
            db = db + jnp.sum(dst, axis=-1, keepdims=True)
            dsb = (dst * scale).astype(BF16)
            dk = dk + jnp.dot(dsb, q, preferred_element_type=F32)
            dq_acc[pl.ds(q0, BQ), :] += lax.dot_general(dsb, k, TN, preferred_element_type=F32)
            return dk, dv, db

        carry = (jnp.zeros((BK, HEAD_DIM), F32), jnp.zeros((BK, HEAD_DIM), F32), jnp.zeros((BK, 1), F32))
        carry = tile(kj, carry, True)
        dk, dv, db = lax.fori_loop(kj + 1, nq, lambda qc, c: tile(qc, c, False), carry)
        dk_ref[...] = dk.astype(dk_ref.dtype)
        dv_ref[...] = dv.astype(dv_ref.dtype)
        db_ref[...] = _col_to_row(db)

        @pl.when(kj == nk - 1)
        def _():
            dq_ref[...] = dq_acc[...].astype(dq_ref.dtype)

        if nw:
            pl.when((hd == H - 1) & (kj == nk - 1))(finish)

    full = lambda c0: pl.BlockSpec((LP, HEAD_DIM), lambda h, j: (0, c0 + h))
    blk = lambda c0: pl.BlockSpec((BK, HEAD_DIM), lambda h, j: (j, c0 + h))
    vec = pl.BlockSpec((None, 1, LP), lambda h, j: (h, 0, 0))
    in_specs = [full(0), blk(H), blk(2 * H), full(0), vec, vec, vec]
    out_specs = [full(0), blk(0), blk(0), pl.BlockSpec((None, 1, BK), lambda h, j: (h, 0, j))]
    est = 2 * (3 * LP * HEAD_DIM * 2 + 16 * LP * 4) + LP * HEAD_DIM * 4 + 24 * BK * LANES * 4 + 10 * BK * BQ * 4
    sems = [pltpu.SemaphoreType.DMA((3 * nw,)), pltpu.SemaphoreType.DMA((3 * nw,))] if nw else []
    return pl.pallas_call(
        body, grid=(H, nk), in_specs=in_specs + [HBM_SPEC] * nw, out_specs=out_specs + [HBM_SPEC] * nw,
        out_shape=[_sds((LP, H * HEAD_DIM), BF16)] * 3 + [_sds((H, 1, LP), F32)]
        + [_sds((3,) + p.shape[1:], p.dtype) for p in scatter],
        scratch_shapes=[pltpu.VMEM((LP, HEAD_DIM), F32)] + sems,
        compiler_params=_cparams(("arbitrary", "arbitrary"), est), name=name,
    )(z, z, z, do, lse_t, delta_t, bias_t, *scatter)


def _ssm_disc(lr, li, ldt, br, bi):
    dt = jnp.exp(ldt)
    mag = jnp.exp(lr * dt)
    a_re = mag * jnp.cos(li * dt)
    a_im = mag * jnp.sin(li * dt)
    den = lr * lr + li * li
    nr = a_re - 1.0
    z_re = (nr * lr + a_im * li) / den
    z_im = (a_im * lr - nr * li) / den
    return a_re, a_im, z_re * br - z_im * bi, z_re * bi + z_im * br


def _ssm_prep(lr, li, ldt, br, bi, name):
    GP, C = br.shape

    def body(lr_ref, li_ref, ldt_ref, br_ref, bi_ref, ar_ref, ai_ref, bbr_ref, bbi_ref, pr_ref, pi_ref):
        a_re, a_im, bb_re, bb_im = _ssm_disc(lr_ref[...], li_ref[...], ldt_ref[...], br_ref[...], bi_ref[...])
        ar_ref[...] = a_re
        ai_ref[...] = a_im
        bbr_ref[...] = bb_re
        bbi_ref[...] = bb_im
        lane = lax.broadcasted_iota(jnp.int32, (tg, 8), 1)
        pr, pi_ = a_re, a_im
        accr = jnp.zeros((tg, 8), F32)
        acci = jnp.zeros((tg, 8), F32)
        for k in range(8):
            accr = jnp.where(lane == k, pr, accr)
            acci = jnp.where(lane == k, pi_, acci)
            pr, pi_ = pr * a_re - pi_ * a_im, pr * a_im + pi_ * a_re
        pr_ref[...] = accr
        pi_ref[...] = acci

    tg = _tile(GP, 512, 8)
    blk = lambda w: pl.BlockSpec((tg, w), lambda i: (i, 0))
    col = _sds((GP, 1), F32)
    return pl.pallas_call(
        body, grid=(GP // tg,), in_specs=[blk(1), blk(1), blk(1), blk(C), blk(C)],
        out_specs=[blk(1), blk(1), blk(C), blk(C), blk(8), blk(8)],
        out_shape=[col, col, _sds((GP, C), F32), _sds((GP, C), F32), _sds((GP, 8), F32), _sds((GP, 8), F32)],
        compiler_params=_cparams(("parallel",), 48 * tg * LANES * 4), name=name,
    )(lr, li, ldt, br, bi)


def _ssm_prep_bwd(lr, li, ldt, br, bi, dar, dai, dbbr, dbbi, name):
    GP, C = br.shape

    def body(lr_ref, li_ref, ldt_ref, br_ref, bi_ref, dar_ref, dai_ref, dbbr_ref, dbbi_ref,
             glr_ref, gli_ref, gldt_ref, gbr_ref, gbi_ref):
        _, vjp = jax.vjp(_ssm_disc, lr_ref[...], li_ref[...], ldt_ref[...], br_ref[...], bi_ref[...])
        glr, gli, gldt, gbr, gbi = vjp((dar_ref[...], dai_ref[...], dbbr_ref[...], dbbi_ref[...]))
        glr_ref[...] = glr
        gli_ref[...] = gli
        gldt_ref[...] = gldt
        gbr_ref[...] = gbr
        gbi_ref[...] = gbi

    tg = _tile(GP, 512, 8)
    blk = lambda w: pl.BlockSpec((tg, w), lambda i: (i, 0))
    col = _sds((GP, 1), F32)
    return pl.pallas_call(
        body, grid=(GP // tg,), in_specs=[blk(1), blk(1), blk(1), blk(C), blk(C), blk(1), blk(1), blk(C), blk(C)],
        out_specs=[blk(1), blk(1), blk(1), blk(C), blk(C)],
        out_shape=[col, col, col, _sds((GP, C), F32), _sds((GP, C), F32)],
        compiler_params=_cparams(("parallel",), 96 * tg * LANES * 4), name=name,
    )(lr, li, ldt, br, bi, dar, dai, dbbr, dbbi)


def _cmul_add(xr, xi, mr, mi, sr, si):
    return xr + mr * sr - mi * si, xi + mr * si + mi * sr


def _ssm_fwd(z, u_col0, coef, bbr, bbi, ccr, cci, dskip, name):
    LP = z.shape[0]
    NB, CB, S = bbr.shape
    TS = _tile(LP, 640, 8)
    nt = LP // TS

    def body(u_ref, coef_ref, bbr_ref, bbi_ref, ccr_ref, cci_ref, ds_ref, y_ref, yg_ref, hr_ref, hi_ref, bur, bui, carry):
        i = pl.program_id(1)

        @pl.when(i == 0)
        def _():
            carry[...] = jnp.zeros_like(carry)

        u = u_ref[...]
        bur[...] = jnp.dot(u, bbr_ref[...], preferred_element_type=F32)
        bui[...] = jnp.dot(u, bbi_ref[...], preferred_element_type=F32)

        def step(g, c):
            cr, ci = c
            r0 = pl.multiple_of(g * 8, 8)
            xr = bur[pl.ds(r0, 8), :]
            xi = bui[pl.ds(r0, 8), :]
            for n, k in enumerate((1, 2, 4)):
                xr, xi = _cmul_add(xr, xi, coef_ref[2 * n], coef_ref[2 * n + 1], pltpu.roll(xr, k, 0), pltpu.roll(xi, k, 0))
            xr, xi = _cmul_add(xr, xi, coef_ref[6], coef_ref[7], cr, ci)
            hr_ref[pl.ds(r0, 8), :] = xr
            hi_ref[pl.ds(r0, 8), :] = xi
            return jnp.broadcast_to(xr[7:8, :], (8, S)), jnp.broadcast_to(xi[7:8, :], (8, S))

        cr, ci = lax.fori_loop(0, TS // 8, step, (carry[0], carry[1]))
        carry[0] = cr
        carry[1] = ci
        y = (jnp.dot(hr_ref[...].astype(BF16), ccr_ref[...], preferred_element_type=F32)
             - jnp.dot(hi_ref[...].astype(BF16), cci_ref[...], preferred_element_type=F32)
             + ds_ref[...] * u.astype(F32))
        y_ref[...] = y
        yg_ref[...] = _gelu(y).astype(yg_ref.dtype)

    ucb = u_col0 // CB
    in_specs = [
        pl.BlockSpec((TS, CB), lambda j, i: (i, ucb + j)),
        pl.BlockSpec((None, 8, 8, S), lambda j, i: (j, 0, 0, 0)),
        pl.BlockSpec((None, CB, S), lambda j, i: (j, 0, 0)),
        pl.BlockSpec((None, CB, S), lambda j, i: (j, 0, 0)),
        pl.BlockSpec((None, S, CB), lambda j, i: (j, 0, 0)),
        pl.BlockSpec((None, S, CB), lambda j, i: (j, 0, 0)),
        pl.BlockSpec((1, CB), lambda j, i: (0, j)),
    ]
    yb = pl.BlockSpec((TS, CB), lambda j, i: (i, j))
    hb = pl.BlockSpec((TS, S), lambda j, i: (i, j))
    est = 2 * (2 * TS * S * 4 + 3 * TS * CB * 4 + 8 * 8 * S * 4 + 4 * CB * S * 2) + 3 * TS * S * 4
    return pl.pallas_call(
        body, grid=(NB, nt), in_specs=in_specs, out_specs=[yb, yb, hb, hb],
        out_shape=[_sds((LP, NB * CB), F32), _sds((LP, NB * CB), BF16), _sds((LP, NB * S), F32), _sds((LP, NB * S), F32)],
        scratch_shapes=[pltpu.VMEM((TS, S), F32), pltpu.VMEM((TS, S), F32), pltpu.VMEM((2, 8, S), F32)],
        compiler_params=_cparams(("parallel", "arbitrary"), est), name=name,
    )(z, coef, bbr, bbi, ccr, cci, dskip)


def _ssm_bwd(z, u_col0, dyg, y, hr, hi, coef_rev, bbr_t, bbi_t, ccr_t, cci_t, dskip, name):
    LP = z.shape[0]
    NB, S, CB = bbr_t.shape
    TS = _tile(LP, 640, 8)
    nt = LP // TS
    ng = TS // 8
    TN = (((0,), (0,)), ((), ()))

    def body(u_ref, dyg_ref, y_ref, hr_ref, hi_ref, tr_ref, ti_ref, coef_ref, bbr_ref, bbi_ref, ccr_ref, cci_ref, ds_ref,
             du_ref, dbbr_ref, dbbi_ref, dccr_ref, dcci_ref, dar_ref, dai_ref, dd_ref,
             gr, gi, carry, acc_bbr, acc_bbi, acc_ccr, acc_cci, acc_a, acc_d):
        i = pl.program_id(1)

        @pl.when(i == 0)
        def _():
            for ref in (carry, acc_bbr, acc_bbi, acc_ccr, acc_cci, acc_a, acc_d):
                ref[...] = jnp.zeros_like(ref)

        u = u_ref[...]
        dy = dyg_ref[...] * _gelu_grad(y_ref[...])
        dyb = dy.astype(BF16)
        gr[...] = jnp.dot(dyb, ccr_ref[...], preferred_element_type=F32)
        gi[...] = -jnp.dot(dyb, cci_ref[...], preferred_element_type=F32)
        row8 = lax.broadcasted_iota(jnp.int32, (8, S), 0)
        first_chunk = i == nt - 1
        tail_r = jnp.where(first_chunk, 0.0, tr_ref[...])
        tail_i = jnp.where(first_chunk, 0.0, ti_ref[...])

        def step(t, c):
            cr, ci, sar, sai = c
            g = ng - 1 - t
            r0 = pl.multiple_of(g * 8, 8)
            xr = gr[pl.ds(r0, 8), :]
            xi = gi[pl.ds(r0, 8), :]
            for n, k in enumerate((1, 2, 4)):
                xr, xi = _cmul_add(xr, xi, coef_ref[2 * n], coef_ref[2 * n + 1], pltpu.roll(xr, 8 - k, 0), pltpu.roll(xi, 8 - k, 0))
            xr, xi = _cmul_add(xr, xi, coef_ref[6], coef_ref[7], cr, ci)
            gr[pl.ds(r0, 8), :] = xr
            gi[pl.ds(r0, 8), :] = xi
            p0 = pl.multiple_of(jnp.maximum(g - 1, 0) * 8, 8)
            pr = jnp.where(g > 0, hr_ref[pl.ds(p0, 8), :], tail_r)
            pi_ = jnp.where(g > 0, hi_ref[pl.ds(p0, 8), :], tail_i)
            hpr = pltpu.roll(jnp.where(row8 == 7, pr, hr_ref[pl.ds(r0, 8), :]), 1, 0)
            hpi = pltpu.roll(jnp.where(row8 == 7, pi_, hi_ref[pl.ds(r0, 8), :]), 1, 0)
            sar = sar + xr * hpr + xi * hpi
            sai = sai + xi * hpr - xr * hpi
            return jnp.broadcast_to(xr[0:1, :], (8, S)), jnp.broadcast_to(xi[0:1, :], (8, S)), sar, sai

        zero = jnp.zeros((8, S), F32)
        cr, ci, sar, sai = lax.fori_loop(0, ng, step, (carry[0], carry[1], zero, zero))
        carry[0] = cr
        carry[1] = ci
        acc_a[0] += sar
        acc_a[1] += sai
        grb = gr[...].astype(BF16)
        gib = gi[...].astype(BF16)
        du = (jnp.dot(grb, bbr_ref[...], preferred_element_type=F32) + jnp.dot(gib, bbi_ref[...], preferred_element_type=F32)
              + ds_ref[...] * dy)
        du_ref[...] = du.astype(du_ref.dtype)
        acc_bbr[...] += lax.dot_general(u, grb, TN, preferred_element_type=F32)
        acc_bbi[...] += lax.dot_general(u, gib, TN, preferred_element_type=F32)
        acc_ccr[...] += lax.dot_general(hr_ref[...].astype(BF16), dyb, TN, preferred_element_type=F32)
        acc_cci[...] -= lax.dot_general(hi_ref[...].astype(BF16), dyb, TN, preferred_element_type=F32)
        acc_d[...] += jnp.sum(dy * u.astype(F32), axis=0, keepdims=True)

        @pl.when(i == nt - 1)
        def _():
            dbbr_ref[...] = acc_bbr[...]
            dbbi_ref[...] = acc_bbi[...]
            dccr_ref[...] = acc_ccr[...]
            dcci_ref[...] = acc_cci[...]
            dar_ref[...] = jnp.sum(acc_a[0], axis=0, keepdims=True)
            dai_ref[...] = jnp.sum(acc_a[1], axis=0, keepdims=True)
            dd_ref[...] = acc_d[...]

    ucb = u_col0 // CB
    rev = lambda i: nt - 1 - i
    tail = lambda j, i: (jnp.maximum(rev(i) * ng - 1, 0), j)
    yb = pl.BlockSpec((TS, CB), lambda j, i: (rev(i), j))
    hb = pl.BlockSpec((TS, S), lambda j, i: (rev(i), j))
    in_specs = [
        pl.BlockSpec((TS, CB), lambda j, i: (rev(i), ucb + j)), yb, yb, hb, hb,
        pl.BlockSpec((8, S), tail), pl.BlockSpec((8, S), tail),
        pl.BlockSpec((None, 8, 8, S), lambda j, i: (j, 0, 0, 0)),
        pl.BlockSpec((None, S, CB), lambda j, i: (j, 0, 0)),
        pl.BlockSpec((None, S, CB), lambda j, i: (j, 0, 0)),
        pl.BlockSpec((None, CB, S), lambda j, i: (j, 0, 0)),
        pl.BlockSpec((None, CB, S), lambda j, i: (j, 0, 0)),
        pl.BlockSpec((1, CB), lambda j, i: (0, j)),
    ]
    mat_cs = pl.BlockSpec((None, CB, S), lambda j, i: (j, 0, 0))
    mat_sc = pl.BlockSpec((None, S, CB), lambda j, i: (j, 0, 0))
    vec_s = pl.BlockSpec((None, 1, S), lambda j, i: (j, 0, 0))
    out_specs = [yb, mat_cs, mat_cs, mat_sc, mat_sc, vec_s, vec_s, pl.BlockSpec((1, CB), lambda j, i: (0, j))]
    out_shape = [_sds((LP, NB * CB), BF16), _sds((NB, CB, S), F32), _sds((NB, CB, S), F32), _sds((NB, S, CB), F32),
                 _sds((NB, S, CB), F32), _sds((NB, 1, S), F32), _sds((NB, 1, S), F32), _sds((1, NB * CB), F32)]
    scratch = [pltpu.VMEM((TS, S), F32), pltpu.VMEM((TS, S), F32), pltpu.VMEM((2, 8, S), F32),
               pltpu.VMEM((CB, S), F32), pltpu.VMEM((CB, S), F32), pltpu.VMEM((S, CB), F32), pltpu.VMEM((S, CB), F32),
               pltpu.VMEM((2, 8, S), F32), pltpu.VMEM((1, CB), F32)]
    est = 2 * (2 * TS * S * 4 + 4 * TS * CB * 4 + 8 * 8 * S * 4 + 12 * CB * S * 4) + 4 * TS * S * 4
    return pl.pallas_call(
        body, grid=(NB, nt), in_specs=in_specs, out_specs=out_specs, out_shape=out_shape, scratch_shapes=scratch,
        compiler_params=_cparams(("parallel", "arbitrary"), est), name=name,
    )(z, dyg, y, hr, hi, hr, hi, coef_rev, bbr_t, bbi_t, ccr_t, cci_t, dskip)


def _merge_fwd(yab, z, ao, D, ga0, gb0, name):
    LP = z.shape[0]
    tm = _tile(LP, 640, 8)
    tn = _ctile(512, D, ga0, gb0)
    nj = D // tn

    def body(ya_ref, yb_ref, ga_ref, gb_ref, ao_ref, o_ref):
        f = lambda r: r[...].astype(F32)
        ssm = f(ya_ref) * _sig(f(yb_ref))
        o_ref[...] = (_sig(f(ga_ref)) * ssm + _sig(f(gb_ref)) * f(ao_ref)).astype(o_ref.dtype)

    blk = lambda c0: pl.BlockSpec((tm, tn), lambda i, j: (i, c0 // tn + j))
    return pl.pallas_call(
        body, grid=(LP // tm, nj), in_specs=[blk(0), blk(D), blk(ga0), blk(gb0), blk(0)], out_specs=blk(0),
        out_shape=_sds((LP, D), BF16), compiler_params=_cparams(("parallel", "parallel"), 2 * 6 * tm * tn * 4), name=name,
    )(yab, yab, z, z, ao)


def _merge_bwd(dm, yab, z, ao, D, ga0, gb0, name):
    LP = z.shape[0]
    tm = _tile(LP, 640, 8)
    tn = _ctile(512, D, ga0, gb0)
    nj = D // tn

    def body(dm_ref, ya_ref, yb_ref, ga_ref, gb_ref, ao_ref, dya_ref, dyb_ref, dga_ref, dgb_ref, dao_ref):
        f = lambda r: r[...].astype(F32)
        dmv, ya, ao_v = f(dm_ref), f(ya_ref), f(ao_ref)
        sa, sb, sy = _sig(f(ga_ref)), _sig(f(gb_ref)), _sig(f(yb_ref))
        t = dmv * sa
        dya_ref[...] = (t * sy).astype(BF16)
        dyb_ref[...] = (t * ya * sy * (1.0 - sy)).astype(BF16)
        dga_ref[...] = (dmv * (ya * sy) * sa * (1.0 - sa)).astype(BF16)
        dgb_ref[...] = (dmv * ao_v * sb * (1.0 - sb)).astype(BF16)
        dao_ref[...] = (dmv * sb).astype(BF16)

    blk = lambda c0: pl.BlockSpec((tm, tn), lambda i, j: (i, c0 // tn + j))
    return pl.pallas_call(
        body, grid=(LP // tm, nj), in_specs=[blk(0), blk(0), blk(D), blk(ga0), blk(gb0), blk(0)], out_specs=[blk(0)] * 5,
        out_shape=[_sds((LP, D), BF16)] * 5, compiler_params=_cparams(("parallel", "parallel"), 2 * 11 * tm * tn * 4), name=name,
    )(dm, yab, yab, z, z, ao)


def _shift_down(x, halo, k, row8):
    s = pltpu.roll(x, k, 0)
    top = jnp.where(row8 < k, pltpu.roll(halo, k, 0), s[0:8])
    return jnp.concatenate([top, s[8:]], axis=0) if x.shape[0] > 8 else top


def _shift_up(x, halo, k, row8):
    tm = x.shape[0]
    s = pltpu.roll(x, tm - k, 0)
    bot = jnp.where(row8 >= 8 - k, pltpu.roll(halo, 8 - k, 0), s[tm - 8:])
    return jnp.concatenate([s[:tm - 8], bot], axis=0) if tm > 8 else bot


def _conv_gate(g, halo, w_ref, cb, row8):
    return cb + w_ref[0:1, :] * _shift_down(g, halo, 2, row8) + w_ref[1:2, :] * _shift_down(g, halo, 1, row8) + w_ref[2:3, :] * g


def _convact_fwd(gu, cw, cb, DFF, name):
    LP = gu.shape[0]
    tm = _tile(LP, 640, 8)
    tn = _tile(DFF, 512)
    nj = DFF // tn
    t8 = tm // 8

    def body(g_ref, h_ref, u_ref, w_ref, b_ref, o_ref):
        i = pl.program_id(0)
        row8 = lax.broadcasted_iota(jnp.int32, (8, tn), 0)
        g = g_ref[...].astype(F32)
        halo = jnp.where(i > 0, h_ref[...].astype(F32), 0.0)
        gc = _conv_gate(g, halo, w_ref, b_ref[...], row8)
        o_ref[...] = (gc * _sig(gc) * u_ref[...].astype(F32)).astype(o_ref.dtype)

    in_specs = [
        pl.BlockSpec((tm, tn), lambda i, j: (i, j)),
        pl.BlockSpec((8, tn), lambda i, j: (jnp.maximum(i * t8 - 1, 0), j)),
        pl.BlockSpec((tm, tn), lambda i, j: (i, nj + j)),
        pl.BlockSpec((3, tn), lambda i, j: (0, j)),
        pl.BlockSpec((1, tn), lambda i, j: (0, j)),
    ]
    return pl.pallas_call(
        body, grid=(LP // tm, nj), in_specs=in_specs, out_specs=pl.BlockSpec((tm, tn), lambda i, j: (i, j)),
        out_shape=_sds((LP, DFF), BF16), compiler_params=_cparams(("parallel", "parallel"), 2 * 8 * tm * tn * 4), name=name,
    )(gu, gu, gu, cw, cb)


def _convact_bwd(dact, gu, cw, cb, DFF, name):
    LP = gu.shape[0]
    tm = _tile(LP, 640, 8)
    tn = _tile(DFF, 512)
    nj = DFF // tn
    t8 = tm // 8

    def body(da_ref, g_ref, h_ref, u_ref, w_ref, b_ref, dgc_ref, du_ref):
        i = pl.program_id(0)
        row8 = lax.broadcasted_iota(jnp.int32, (8, tn), 0)
        g = g_ref[...].astype(F32)
        halo = jnp.where(i > 0, h_ref[...].astype(F32), 0.0)
        gc = _conv_gate(g, halo, w_ref, b_ref[...], row8)
        sg = _sig(gc)
        da = da_ref[...].astype(F32)
        du_ref[...] = (da * gc * sg).astype(du_ref.dtype)
        dgc_ref[...] = da * u_ref[...].astype(F32) * sg * (1.0 + gc * (1.0 - sg))

    blk = pl.BlockSpec((tm, tn), lambda i, j: (i, j))
    in_specs = [
        blk, blk,
        pl.BlockSpec((8, tn), lambda i, j: (jnp.maximum(i * t8 - 1, 0), j)),
        pl.BlockSpec((tm, tn), lambda i, j: (i, nj + j)),
        pl.BlockSpec((3, tn), lambda i, j: (0, j)),
        pl.BlockSpec((1, tn), lambda i, j: (0, j)),
    ]
    return pl.pallas_call(
        body, grid=(LP // tm, nj), in_specs=in_specs, out_specs=[blk, blk],
        out_shape=[_sds((LP, DFF), F32), _sds((LP, DFF), BF16)],
        compiler_params=_cparams(("parallel", "parallel"), 2 * 10 * tm * tn * 4), name=name,
    )(dact, gu, gu, gu, cw, cb)


def _conv_bwd(dgc, gu, cw, DFF, pad, name):
    LP = gu.shape[0]
    tm = _tile(LP, 640, 8)
    tn = _tile(DFF, 512)
    nj = DFF // tn
    t8 = tm // 8
    nt = LP // tm

    def body(d_ref, dn_ref, g_ref, h_ref, w_ref, dg_ref, dw_ref, db_ref):
        i = pl.program_id(1)
        row8 = lax.broadcasted_iota(jnp.int32, (8, tn), 0)
        d = d_ref[...]
        nxt = jnp.where(i < nt - 1, dn_ref[...], 0.0)
        dg = w_ref[2:3, :] * d + w_ref[1:2, :] * _shift_up(d, nxt, 1, row8) + w_ref[0:1, :] * _shift_up(d, nxt, 2, row8)
        rows = i * tm + lax.broadcasted_iota(jnp.int32, (tm, 1), 0)
        dg_ref[...] = jnp.where(rows >= pad, dg, 0.0).astype(dg_ref.dtype)
        g = g_ref[...].astype(F32)
        halo = jnp.where(i > 0, h_ref[...].astype(F32), 0.0)
        row3 = lax.broadcasted_iota(jnp.int32, (3, tn), 0)
        s0 = jnp.sum(d * _shift_down(g, halo, 2, row8), axis=0, keepdims=True)
        s1 = jnp.sum(d * _shift_down(g, halo, 1, row8), axis=0, keepdims=True)
        s2 = jnp.sum(d * g, axis=0, keepdims=True)
        dw = jnp.where(row3 == 0, s0, jnp.where(row3 == 1, s1, s2))
        dbp = jnp.sum(d, axis=0, keepdims=True)

        @pl.when(i == 0)
        def _():
            dw_ref[...] = dw
            db_ref[...] = dbp

        @pl.when(i > 0)
        def _():
            dw_ref[...] += dw
            db_ref[...] += dbp

    blk = pl.BlockSpec((tm, tn), lambda j, i: (i, j))
    in_specs = [
        blk,
        pl.BlockSpec((8, tn), lambda j, i: (jnp.minimum((i + 1) * t8, LP // 8 - 1), j)),
        blk,
        pl.BlockSpec((8, tn), lambda j, i: (jnp.maximum(i * t8 - 1, 0), j)),
        pl.BlockSpec((3, tn), lambda j, i: (0, j)),
    ]
    out_specs = [blk, pl.BlockSpec((3, tn), lambda j, i: (0, j)), pl.BlockSpec((1, tn), lambda j, i: (0, j))]
    return pl.pallas_call(
        body, grid=(nj, nt), in_specs=in_specs, out_specs=out_specs,
        out_shape=[_sds((LP, DFF), BF16), _sds((3, DFF), F32), _sds((1, DFF), F32)],
        compiler_params=_cparams(("parallel", "arbitrary"), 2 * 10 * tm * tn * 4), name=name,
    )(dgc, dgc, gu, gu, cw)


def _adamw_math(w, g, m, v):
    m = ADAM_B1 * m + (1.0 - ADAM_B1) * g
    v = ADAM_B2 * v + (1.0 - ADAM_B2) * (g * g)
    m_hat = m / (1.0 - ADAM_B1 ** ADAM_STEP)
    v_hat = v / (1.0 - ADAM_B2 ** ADAM_STEP)
    delta = -ADAM_LR * (m_hat / (jnp.sqrt(v_hat) + ADAM_EPS) + ADAM_WD * w)
    return delta, m, v


def _adamw(w, g, m, v, name):
    R, C = w.shape
    tm = R if R * C * 4 <= (1 << 20) else _tile(R, max(8, ((1 << 20) // (C * 4)) // 8 * 8), 8)

    def body(w_ref, g_ref, m_ref, v_ref, d_ref, mo_ref, vo_ref):
        d_ref[...], mo_ref[...], vo_ref[...] = _adamw_math(w_ref[...], g_ref[...], m_ref[...], v_ref[...])

    blk = pl.BlockSpec((tm, C), lambda i: (i, 0))
    return pl.pallas_call(
        body, grid=(R // tm,), in_specs=[blk] * 4, out_specs=[blk] * 3, out_shape=[_sds((R, C), F32)] * 3,
        compiler_params=_cparams(("parallel",), 2 * 7 * tm * (C + LANES) * 4), name=name,
    )(w, g, m, v)


def _sum_adamw(parts, w, m, v, name):
    n, R, C = parts.shape
    tm = _tile(R, 256, 8)

    def body(p_ref, w_ref, m_ref, v_ref, g_ref, d_ref, mo_ref, vo_ref):
        g = p_ref[0]
        for k in range(1, n):
            g = g + p_ref[k]
        g_ref[...] = g
        d_ref[...], mo_ref[...], vo_ref[...] = _adamw_math(w_ref[...], g, m_ref[...], v_ref[...])

    blk = pl.BlockSpec((tm, C), lambda i: (i, 0))
    return pl.pallas_call(
        body, grid=(R // tm,), in_specs=[pl.BlockSpec((n, tm, C), lambda i: (0, i, 0))] + [blk] * 3, out_specs=[blk] * 4,
        out_shape=[_sds((R, C), F32)] * 4,
        compiler_params=_cparams(("parallel",), 2 * (n + 7) * tm * C * 4), name=name,
    )(parts, w, m, v)


def _add_half(g, got, c_idx, name):
    n, R, C = g.shape
    HR = R // 2
    tm = _tile(HR, max(8, ((1 << 20) // (C * 4)) // 8 * 8), 8)
    nb = HR // tm

    def body(c_ref, g_ref, t_ref, o_ref):
        o_ref[...] = (g_ref[...] + t_ref[...]).astype(o_ref.dtype)

    grid_spec = pltpu.PrefetchScalarGridSpec(
        num_scalar_prefetch=1, grid=(n, nb),
        in_specs=[pl.BlockSpec((None, tm, C), lambda k, i, c: (k, c[0] * nb + i, 0)),
                  pl.BlockSpec((None, tm, C), lambda k, i, c: (k, i, 0))],
        out_specs=pl.BlockSpec((None, tm, C), lambda k, i, c: (k, i, 0)))
    return pl.pallas_call(
        body, grid_spec=grid_spec, out_shape=_sds((n, HR, C), BF16),
        compiler_params=_cparams(("parallel", "parallel"), 2 * 3 * tm * (C + LANES) * 4), name=name,
    )(c_idx, g, got)


def _sum_half(g, got, land, chip_c, name):
    n, R, C = g.shape
    HR = R // 2
    tm = _tile(HR, max(8, ((1 << 20) // (C * 4)) // 8 * 8), 8)
    nb = HR // tm

    def body(s_ref, g_ref, t_ref, l_ref, o_ref):
        acc = g_ref[...] + t_ref[...]
        for k in range(3):
            acc = acc + l_ref[k].astype(F32)
        o_ref[...] = acc

    grid_spec = pltpu.PrefetchScalarGridSpec(
        num_scalar_prefetch=1, grid=(nb,),
        in_specs=[pl.BlockSpec((None, tm, C), lambda i, sc: (sc[0], sc[1] * nb + i, 0)),
                  pl.BlockSpec((None, tm, C), lambda i, sc: (sc[0], i, 0)),
                  pl.BlockSpec((3, tm, C), lambda i, sc: (0, i, 0))],
        out_specs=pl.BlockSpec((tm, C), lambda i, sc: (sc[1] * nb + i, 0)))
    return pl.pallas_call(
        body, grid_spec=grid_spec, out_shape=_sds((R, C), F32),
        compiler_params=_cparams(("parallel",), 2 * 6 * tm * (C + LANES) * 4), name=name,
    )(chip_c, g, got, land)


def _cast_slot(w, chip_idx, name):
    R, C = w.shape
    tm = _tile(R, max(16, ((1 << 20) // (C * 4)) // 16 * 16), 16)

    def body(s_ref, w_ref, o_ref):
        o_ref[...] = w_ref[...].astype(o_ref.dtype)

    grid_spec = pltpu.PrefetchScalarGridSpec(
        num_scalar_prefetch=1, grid=(R // tm,),
        in_specs=[pl.BlockSpec((tm, C), lambda i, sc: (i, 0))],
        out_specs=pl.BlockSpec((None, tm, C), lambda i, sc: (sc[0], i, 0)))
    return pl.pallas_call(
        body, grid_spec=grid_spec, out_shape=_sds((4, R, C), BF16),
        compiler_params=_cparams(("parallel",), 2 * 2 * tm * (C + LANES) * 4), name=name,
    )(chip_idx, w)


def _gather_chips(gs, name):
    nw = len(gs)

    def body(*refs):
        start, forward, finish = _gather_phases(refs[nw:2 * nw], *refs[2 * nw:])
        start()
        forward()
        finish()

    return pl.pallas_call(
        body, out_shape=[_sds(g.shape, g.dtype) for g in gs], in_specs=[HBM_SPEC] * nw, out_specs=[HBM_SPEC] * nw,
        scratch_shapes=[pltpu.SemaphoreType.DMA((6 * nw,)), pltpu.SemaphoreType.DMA((6 * nw,))],
        input_output_aliases={i: i for i in range(nw)}, name=name,
    )(*gs)


def _sibling_halves(gs, name):
    nw = len(gs)
    halves = [g.shape[1] // 2 for g in gs]
    assert all(h % 8 == 0 for h in halves)

    def body(*refs):
        g_refs, land_refs = refs[:nw], refs[nw:2 * nw]
        send_sems, recv_sems = refs[2 * nw:]
        x, y, c = _place()
        cps = []
        for i in range(nw):
            q0 = pl.multiple_of((1 - c) * halves[i], 8)
            cp = pltpu.make_async_remote_copy(
                src_ref=g_refs[i].at[pl.ds(0, 4), pl.ds(q0, halves[i]), :], dst_ref=land_refs[i],
                send_sem=send_sems.at[i], recv_sem=recv_sems.at[i], device_id=(x, y, 1 - c), device_id_type=MESH)
            cp.start()
            cps.append(cp)
        for cp in cps:
            cp.wait()

    return pl.pallas_call(
        body, out_shape=[_sds((4, h, g.shape[2]), g.dtype) for g, h in zip(gs, halves)],
        in_specs=[HBM_SPEC] * nw, out_specs=[HBM_SPEC] * nw,
        scratch_shapes=[pltpu.SemaphoreType.DMA((nw,)), pltpu.SemaphoreType.DMA((nw,))], name=name,
    )(*gs)


def _scatter_chips(ss, name):
    nw = len(ss)

    def body(*refs):
        start, finish = _scatter_phases(refs[:nw], refs[nw:2 * nw], *refs[2 * nw:])
        start()
        finish()

    return pl.pallas_call(
        body, out_shape=[_sds((3,) + s_.shape[1:], s_.dtype) for s_ in ss], in_specs=[HBM_SPEC] * nw, out_specs=[HBM_SPEC] * nw,
        scratch_shapes=[pltpu.SemaphoreType.DMA((3 * nw,)), pltpu.SemaphoreType.DMA((3 * nw,))], name=name,
    )(*ss)


def _sibling_join(fs, name):
    nw = len(fs)
    assert all(f.shape[0] % 16 == 0 for f in fs)

    def body(*refs):
        o_refs = refs[nw:2 * nw]
        send_sems, recv_sems = refs[2 * nw:]
        x, y, c = _place()

        def copy(i, half):
            HR = o_refs[i].shape[0] // 2
            rows = o_refs[i].at[pl.ds(pl.multiple_of(half * HR, 8), HR), :]
            return pltpu.make_async_remote_copy(
                src_ref=rows, dst_ref=rows, send_sem=send_sems.at[i], recv_sem=recv_sems.at[i],
                device_id=(x, y, 1 - c), device_id_type=MESH)

        sends = [copy(i, c) for i in range(nw)]
        for cp in sends:
            cp.start()
        for i in range(nw):
            copy(i, 1 - c).wait_recv()
        for cp in sends:
            cp.wait_send()

    return pl.pallas_call(
        body, out_shape=[_sds(f.shape, f.dtype) for f in fs], in_specs=[HBM_SPEC] * nw, out_specs=[HBM_SPEC] * nw,
        scratch_shapes=[pltpu.SemaphoreType.DMA((nw,)), pltpu.SemaphoreType.DMA((nw,))],
        input_output_aliases={i: i for i in range(nw)}, name=name,
    )(*fs)


def _gather_all(v, name):
    M, W = v.shape

    def body(v_ref, o_ref, send_sems, recv_sems, local_sem):
        x, y, c = _place()
        me, sibling = (x, y, c), (x, y, 1 - c)
        chips = _other_chips(x, y)

        def slot(px, py, pc):
            return o_ref.at[4 * px + 2 * py + pc]

        def copy(k, block, to, src=None):
            return pltpu.make_async_remote_copy(
                src_ref=slot(*block) if src is None else src, dst_ref=slot(*block),
                send_sem=send_sems.at[k], recv_sem=recv_sems.at[k], device_id=to, device_id_type=MESH)

        mine = pltpu.make_async_copy(v_ref, slot(*me), local_sem)
        mine.start()
        first = [copy(0, me, sibling, src=v_ref)]
        first += [copy(1 + j, me, (*chip, c), src=v_ref) for j, chip in enumerate(chips)]
        for cp in first:
            cp.start()
        passed = [copy(4 + j, (*chip, c), sibling) for j, chip in enumerate(chips)]
        for j, chip in enumerate(chips):
            copy(1 + j, (*chip, c), me).wait_recv()
            passed[j].start()
        copy(0, sibling, me).wait_recv()
        for j, chip in enumerate(chips):
            copy(4 + j, (*chip, 1 - c), me).wait_recv()
        for cp in first + passed:
            cp.wait_send()
        mine.wait()

    vm = pl.BlockSpec(memory_space=pltpu.VMEM)
    return pl.pallas_call(
        body, out_shape=_sds((8, M, W), v.dtype), in_specs=[vm], out_specs=vm,
        scratch_shapes=[pltpu.SemaphoreType.DMA((7,)), pltpu.SemaphoreType.DMA((7,)), pltpu.SemaphoreType.DMA(())],
        compiler_params=pltpu.CompilerParams(vmem_limit_bytes=int(min(10 * M * W * 4 + (8 << 20), V7X_VMEM_BYTES - (8 << 20)))),
        name=name,
    )(v)


def _rows_for(n_elems, width, mult=8):
    rows = -(-n_elems // width)
    return -(-rows // mult) * mult


def _pack_small(arrs, total_rows):
    parts = []
    used = 0
    for a in arrs:
        rows = _rows_for(a.size, LANES)
        parts.append(jnp.pad(a.reshape(-1), (0, rows * LANES - a.size)).reshape(rows, LANES))
        used += rows
    if total_rows > used:
        parts.append(jnp.zeros((total_rows - used, LANES), F32))
    return jnp.concatenate(parts, axis=0)


def _unpack_small(p, shapes):
    outs, r = [], 0
    lead = p.shape[:-2]
    for shp in shapes:
        n = int(np.prod(shp))
        rows = _rows_for(n, LANES)
        outs.append(p[..., r:r + rows, :].reshape(lead + (rows * LANES,))[..., :n].reshape(lead + tuple(shp)))
        r += rows
    return outs


def _cols_to_chips(w):
    K, N = w.shape
    return w.reshape(K, 4, N // 4).transpose(1, 0, 2)


def _chips_to_cols(w):
    n4, K, n = w.shape
    return w.transpose(1, 0, 2).reshape(K, n4 * n)


def _block_diag(m, gpb):
    G, A, B = m.shape
    nb = G // gpb
    eye = jnp.eye(gpb, dtype=m.dtype)
    t = m.reshape(nb, gpb, A, B)[:, :, :, None, :] * eye[None, :, None, :, None]
    return t.reshape(nb, gpb * A, gpb * B)


def _block_diag_extract(m, gpb, A, B):
    nb = m.shape[0]
    t = m.reshape(nb, gpb, A, gpb, B)
    eye = jnp.eye(gpb, dtype=m.dtype)
    d = jnp.sum(t * eye[None, :, None, :, None], axis=3)
    return d.reshape(nb * gpb, A, B)


def kernel(x, meta, g_mix, w_in, b_f, lam_re, lam_im, log_dt, b_re, b_im, c_re, c_im, d_skip, w_glu, w_attn_o, w_out, g_ffn, w_up, conv_w, conv_b, w_down, g_final, loss_target, m_meta, m_g_mix, m_w_in, m_b_f, m_lam_re, m_lam_im, m_log_dt, m_b_re, m_b_im, m_c_re, m_c_im, m_d_skip, m_w_glu, m_w_attn_o, m_w_out, m_g_ffn, m_w_up, m_conv_w, m_conv_b, m_w_down, m_g_final, v_meta, v_g_mix, v_w_in, v_b_f, v_lam_re, v_lam_im, v_log_dt, v_b_re, v_b_im, v_c_re, v_c_im, v_d_skip, v_w_glu, v_w_attn_o, v_w_out, v_g_ffn, v_w_up, v_conv_w, v_conv_b, v_w_down, v_g_final):
    args = dict(locals())
    L, D = x.shape[1], x.shape[2]
    NM = meta.shape[0]
    H = b_f.shape[1]
    DA = H * HEAD_DIM
    G, P, C = b_re.shape[1:]
    DS, GP = G * C, G * P
    DFF = conv_b.shape[1]
    PAD = (-NM) % LANES
    OFF = PAD + NM
    LP = OFF + L
    NZ = 3 * DA + DS + 2 * D
    U0, GA0, GB0 = 3 * DA, 3 * DA + DS, 3 * DA + DS + D
    NB = G // GROUPS_PER_BLOCK
    chip = 2 * lax.axis_index("x") + lax.axis_index("y")
    core = lax.axis_index("c")

    big = ["w_in", "w_glu", "w_attn_o", "w_out", "w_up", "w_down"]
    local = {n: args[n][0] for n in big}
    chip_idx = chip.reshape(1).astype(jnp.int32)
    slots = {n: _cast_slot(local[n], chip_idx, "cast_" + n) for n in big}
    gathered = {"w_in": _gather_chips([slots["w_in"]], "gather_w_in")[0]}
    tiny_shapes = [conv_w.shape[1:], meta.shape]
    tiny_rows = sum(_rows_for(int(np.prod(sh)), LANES) for sh in tiny_shapes)
    tiny = _gather_all(_pack_small([conv_w[0], meta], tiny_rows), "gather_small_weights")[0::2]
    conv_w_c, meta_c = _unpack_small(tiny, tiny_shapes)
    conv_w_f = _chips_to_cols(conv_w_c)
    meta_full = _chips_to_cols(meta_c)
    w_in_f = _chips_to_cols(gathered["w_in"])
    w_f = jnp.pad(w_in_f[:, 3 * DA:3 * DA + H], ((0, 0), (0, LANES - H)))
    w_zf = jnp.concatenate([w_in_f[:, :3 * DA], w_in_f[:, 3 * DA + H:], w_f], axis=1)
    N_GLU, N_AO, N_UP = (slots[n].shape[2] for n in ("w_glu", "w_attn_o", "w_up"))

    col = lambda a: a.reshape(GP, 1)
    lr_c, li_c = col(lam_re[0]), col(lam_im[0])
    ldt_c = jnp.repeat(log_dt[0], P).reshape(GP, 1)
    br2, bi2 = b_re[0].reshape(GP, C), b_im[0].reshape(GP, C)
    a_re, a_im, bb_re, bb_im, pw_re, pw_im = _ssm_prep(lr_c, li_c, ldt_c, br2, bi2, "ssm_prep")
    S = GROUPS_PER_BLOCK * P
    CB = GROUPS_PER_BLOCK * C
    pw_r = pw_re.T.reshape(8, NB, S).transpose(1, 0, 2)
    pw_i = pw_im.T.reshape(8, NB, S).transpose(1, 0, 2)
    row8 = jnp.arange(8)[None, :, None]

    def masked_power(pw, k, keep):
        return jnp.where(keep, pw[:, k - 1][:, None, :], 0.0)

    coef = jnp.stack(
        [masked_power(pw, k, row8 >= k) for k in (1, 2, 4) for pw in (pw_r, pw_i)] + [pw_r, pw_i], axis=1)
    coef_rev = jnp.stack(
        [masked_power(pw, k, row8 < 8 - k) for k in (1, 2, 4) for pw in (pw_r, -pw_i)]
        + [pw_r[:, ::-1], -pw_i[:, ::-1]], axis=1)
    bd = lambda m: _block_diag(m, GROUPS_PER_BLOCK)
    bbr3, bbi3 = bb_re.reshape(G, P, C), bb_im.reshape(G, P, C)
    bbr_cs = bd(bbr3.transpose(0, 2, 1)).astype(BF16)
    bbi_cs = bd(bbi3.transpose(0, 2, 1)).astype(BF16)
    bbr_sc = bd(bbr3).astype(BF16)
    bbi_sc = bd(bbi3).astype(BF16)
    ccr_sc = bd(c_re[0].transpose(0, 2, 1)).astype(BF16)
    cci_sc = bd(c_im[0].transpose(0, 2, 1)).astype(BF16)
    ccr_cs = bd(c_re[0]).astype(BF16)
    cci_cs = bd(c_im[0]).astype(BF16)

    h0 = jnp.concatenate([jnp.zeros((PAD, D), F32), meta_full, x[0]], axis=0)
    n1 = _rms_fwd(h0, g_mix, "rms_mix")
    z = _mm(n1, w_zf, "nn", LP, NZ, D, BF16, "in_proj")
    fpre = _mm(n1, w_zf, "nn", LP, LANES, D, F32, "in_proj_f", b_off=(0, NZ))
    bf_pad = jnp.pad(b_f, ((0, 0), (0, LANES - H)))
    fcum = _fgate_fwd(fpre, bf_pad, PAD, "fgate_fwd")
    key_bias = jnp.where(jnp.arange(LP)[:, None] >= PAD, -fcum, NEG)
    bias_t = key_bias.T[:H].reshape(H, 1, LP)
    attn, attn_f32, lse_t, *rest = _attn_fwd(z, bias_t, H, PAD, "attn_fwd", gather=[slots[n] for n in big[1:]])
    gathered.update(zip(big[1:], rest))
    w_glu_c, w_ao_c, w_up_c = gathered["w_glu"], gathered["w_attn_o"], gathered["w_up"]
    w_out_f = gathered["w_out"].reshape(D, D)
    w_down_f = gathered["w_down"].reshape(DFF, D)
    ao = _mm(attn, w_ao_c, "nn", LP, D, DA, BF16, "attn_out", b_chips=N_AO)
    y, yg, hs_re, hs_im = _ssm_fwd(z, U0, coef, bbr_cs, bbi_cs, ccr_sc, cci_sc, d_skip, "ssm_fwd")
    yab = _mm(yg, w_glu_c, "nn", LP, 2 * D, DS, BF16, "glu_proj", b_chips=N_GLU)
    merged = _merge_fwd(yab, z, ao, D, GA0, GB0, "merge_fwd")
    h1 = _mm(merged, w_out_f, "nn", LP, D, D, F32, "out_proj", res=h0)
    n2 = _rms_fwd(h1, g_ffn, "rms_ffn")
    gu = _mm(n2, w_up_c, "nn", LP, 2 * DFF, D, BF16, "up_proj", tn=1408, b_chips=N_UP)
    act = _convact_fwd(gu, conv_w_f, conv_b, DFF, "convact_fwd")
    h2 = _mm(act, w_down_f, "nn", LP, D, DFF, F32, "down_proj", res=h1, tn=512, tk=DFF)
    dh2, dg_final, loss_v = _final_loss(h2, g_final.reshape(1, D), loss_target[0], OFF, "final_loss")
    loss = lax.psum(loss_v[0, 0], ("x", "y", "c"))

    KW = dict(tm=512, tn=512, tk=LP)
    dact = _mm(dh2, w_down_f, "nt", LP, DFF, D, BF16, "down_bwd_x")
    dw_down = _mm(act, dh2, "tn", DFF, D, LP, F32, "down_bwd_w", **KW)
    dgc, du_ffn = _convact_bwd(dact, gu, conv_w_f, conv_b, DFF, "convact_bwd")
    dg_ffn_in, dconv_w, dconv_b = _conv_bwd(dgc, gu, conv_w_f, DFF, PAD, "conv_bwd")
    dn2 = _mm(dg_ffn_in, w_up_c, "nt", LP, D, DFF, F32, "up_bwd_x_g", tn=512, tk=N_UP, b_chips=N_UP)
    dn2 = _mm(du_ffn, w_up_c, "nt", LP, D, DFF, F32, "up_bwd_x_u", res=dn2, b_off=(0, DFF), tn=512, tk=N_UP, b_chips=N_UP)
    dw_up = _mm(n2, dg_ffn_in, "tn", D, DFF, LP, F32, "up_bwd_w_g", tm=512, tn=256, tk=LP, out_chips=N_UP,
                out_into=(jnp.zeros((4, D, N_UP), F32), 0))
    dw_up = _mm(n2, du_ffn, "tn", D, DFF, LP, F32, "up_bwd_w_u", tm=512, tn=256, tk=LP, out_chips=N_UP,
                out_into=(dw_up, DFF // N_UP))
    dh1, dg_ffn = _rms_bwd(h1, g_ffn, dn2, dh2, "rms_ffn_bwd")
    c_idx = core.reshape(1).astype(jnp.int32)
    chip_c = jnp.stack([chip, core]).astype(jnp.int32)
    ffn = ["w_up", "w_down"]
    ffn_grads = [dw_up, dw_down.reshape(4, DFF // 4, D)]
    ffn_got = _sibling_halves(ffn_grads, "rs_sibling_ffn")
    ffn_part = [_add_half(g, t, c_idx, "rs_add_" + n) for n, g, t in zip(ffn, ffn_grads, ffn_got)]

    dmerged = _mm(dh1, w_out_f, "nt", LP, D, D, F32, "out_bwd_x")
    dw_out = _mm(merged, dh1, "tn", D, D, LP, F32, "out_bwd_w", **KW)
    dya, dyb, dga, dgb, dao = _merge_bwd(dmerged, yab, z, ao, D, GA0, GB0, "merge_bwd")
    dattn = _mm(dao, w_ao_c, "nt", LP, DA, D, BF16, "attn_out_bwd_x", b_chips=N_AO)
    dw_ao = _mm(attn, dao, "tn", DA, D, LP, F32, "attn_out_bwd_w", out_chips=N_AO, **KW)
    dyg = _mm(dya, w_glu_c, "nt", LP, DS, D, F32, "glu_bwd_x_a", b_chips=N_GLU)
    dyg = _mm(dyb, w_glu_c, "nt", LP, DS, D, F32, "glu_bwd_x_b", res=dyg, b_off=(0, D), b_chips=N_GLU)
    dw_glu = _mm(yg, dya, "tn", DS, D, LP, F32, "glu_bwd_w_a", out_chips=N_GLU,
                 out_into=(jnp.zeros((4, DS, N_GLU), F32), 0), **KW)
    dw_glu = _mm(yg, dyb, "tn", DS, D, LP, F32, "glu_bwd_w_b", out_chips=N_GLU, out_into=(dw_glu, D // N_GLU), **KW)
    (du_ssm, dbbr_d, dbbi_d, dccr_d, dcci_d, dar_b, dai_b, dd_skip) = _ssm_bwd(
        z, U0, dyg, y, hs_re, hs_im, coef_rev, bbr_sc, bbi_sc, ccr_cs, cci_cs, d_skip, "ssm_bwd")
    delta_t = _attn_delta(dattn, attn_f32, H, "attn_delta")
    dq, dk, dv, dbias_t, *ffn_land = _attn_bwd(z, dattn, lse_t, delta_t, bias_t, H, "attn_bwd", scatter=ffn_part)
    dF = jnp.pad(-dbias_t[:, 0, :].T, ((0, 0), (0, LANES - H)))
    dfpre, db_f = _fgate_bwd(dF, fpre, bf_pad, PAD, "fgate_bwd")
    dz = jnp.concatenate([dq, dk, dv, du_ssm, dga, dgb, dfpre.astype(BF16)], axis=1)
    dn1 = _mm(dz, w_zf, "nt", LP, D, NZ + LANES, F32, "in_bwd_x", tm=640, tn=256, tk=NZ + LANES)
    dw_zf = _mm(n1, dz, "tn", D, NZ + LANES, LP, F32, "in_bwd_w", tm=512, tn=640, tk=LP)
    dh0, dg_mix = _rms_bwd(h0, g_mix, dn1, dh1, "rms_mix_bwd")
    grad_x = dh0[OFF:][None]
    dmeta_full = dh0[PAD:OFF]

    ext = lambda m, A, B: _block_diag_extract(m, GROUPS_PER_BLOCK, A, B)
    dbb_re = ext(dbbr_d, C, P).transpose(0, 2, 1).reshape(GP, C)
    dbb_im = ext(dbbi_d, C, P).transpose(0, 2, 1).reshape(GP, C)
    dc_re = ext(dccr_d, P, C).transpose(0, 2, 1)[None]
    dc_im = ext(dcci_d, P, C).transpose(0, 2, 1)[None]
    glr, gli, gldt, gbr, gbi = _ssm_prep_bwd(lr_c, li_c, ldt_c, br2, bi2, dar_b.reshape(GP, 1), dai_b.reshape(GP, 1),
                                             dbb_re, dbb_im, "ssm_prep_bwd")
    small_grads = {
        "g_mix": dg_mix, "b_f": db_f[:, :H], "lam_re": glr.reshape(1, G, P), "lam_im": gli.reshape(1, G, P),
        "log_dt": gldt.reshape(G, P).sum(axis=1)[None], "b_re": gbr.reshape(1, G, P, C), "b_im": gbi.reshape(1, G, P, C),
        "c_re": dc_re, "c_im": dc_im, "d_skip": dd_skip, "g_ffn": dg_ffn, "conv_b": dconv_b, "g_final": dg_final.reshape(D),
    }

    small = list(small_grads)
    rider_grads = [dconv_w, dmeta_full]
    small_shapes = [args[n].shape for n in small] + [g.shape for g in rider_grads]
    srows = sum(_rows_for(int(np.prod(sh)), LANES) for sh in small_shapes)
    srows = -(-srows // 256) * 256
    zeros_like_riders = [jnp.zeros(g.shape, F32) for g in rider_grads]
    pack = lambda arrs: _pack_small(arrs, srows)
    g_parts = _gather_all(pack([small_grads[n] for n in small] + rider_grads), "gather_small_grads")
    sm = _sum_adamw(g_parts, pack([args[n] for n in small] + zeros_like_riders),
                    pack([args["m_" + n] for n in small] + zeros_like_riders),
                    pack([args["v_" + n] for n in small] + zeros_like_riders), "small_adamw")
    unpacked = [_unpack_small(p, small_shapes) for p in sm]
    sg, sd, smm, svv = (dict(zip(small, u[:len(small)])) for u in unpacked)
    dconv_w_sum, dmeta_sum = unpacked[0][len(small):]
    n_cw, n_me = conv_w.shape[2], meta.shape[1]
    rider = {"conv_w": lax.dynamic_slice_in_dim(dconv_w_sum, chip * n_cw, n_cw, axis=1)[None],
             "meta": lax.dynamic_slice_in_dim(dmeta_sum, chip * n_me, n_me, axis=1)}

    dw_in_f = jnp.concatenate([dw_zf[:, :3 * DA], dw_zf[:, NZ:NZ + H], dw_zf[:, 3 * DA:NZ]], axis=1)
    mix = ["w_in", "w_glu", "w_attn_o", "w_out"]
    mix_grads = [_cols_to_chips(dw_in_f), dw_glu, dw_ao, dw_out.reshape(4, D // 4, D)]
    mix_got = _sibling_halves(mix_grads, "rs_sibling_mix")
    mix_part = [_add_half(g, t, c_idx, "rs_add_" + n) for n, g, t in zip(mix, mix_grads, mix_got)]
    mix_land = _scatter_chips(mix_part, "rs_scatter_mix")
    big = mix + ffn
    halves = [_sum_half(g, t, l_, chip_c, "rs_sum_" + n) for n, g, t, l_ in
              zip(big, mix_grads + ffn_grads, list(mix_got) + list(ffn_got), list(mix_land) + list(ffn_land))]
    shard_grads = dict(zip(big, _sibling_join(halves, "rs_join")))
    shard_grads.update({n: g.reshape(g.shape[-2:]) for n, g in rider.items()})
    bg, bd_, bm, bv = {}, {}, {}, {}
    for n, g in shard_grads.items():
        shp = args[n].shape
        two = (lambda a: a.reshape(shp[-2], shp[-1]))
        d_, m_, v_ = _adamw(two(args[n]), g, two(args["m_" + n]), two(args["v_" + n]), "adamw_" + n)
        bg[n], bd_[n], bm[n], bv[n] = g.reshape(shp), d_.reshape(shp), m_.reshape(shp), v_.reshape(shp)

    order = ["meta", "g_mix", "w_in", "b_f", "lam_re", "lam_im", "log_dt", "b_re", "b_im", "c_re", "c_im", "d_skip",
             "w_glu", "w_attn_o", "w_out", "g_ffn", "w_up", "conv_w", "conv_b", "w_down", "g_final"]
    pick = lambda bigd, smalld, n: bigd[n] if n in bigd else smalld[n]
    outs = [loss, grad_x]
    for bigd, smalld in ((bg, sg), (bd_, sd), (bm, smm), (bv, svv)):
        outs += [pick(bigd, smalld, n) for n in order]
    return tuple(outs)
```

```python
import functools
import math

import jax
import jax.numpy as jnp
import numpy as np
from jax import lax
from jax.experimental import pallas as pl
from jax.experimental.pallas import tpu as pltpu

F32 = jnp.float32
BF16 = jnp.bfloat16
MESH = pl.DeviceIdType.MESH

EPS = 1e-6
HEAD_DIM = 128
LANES = 128
NEG = -1e30
GELU_C = math.sqrt(2.0 / math.pi)
GELU_A = 0.044715
ADAM_LR, ADAM_B1, ADAM_B2, ADAM_EPS, ADAM_WD, ADAM_STEP = 0.001, 0.9, 0.999, 1e-08, 0.01, 10
V7X_VMEM_BYTES = 64 << 20
GROUPS_PER_BLOCK = 8


def _tile(n, pref, mult=LANES):
    if n <= pref:
        return n
    t = (pref // mult) * mult
    while t >= mult:
        if n % t == 0:
            return t
        t -= mult
    raise ValueError(f"no tile for {n} <= {pref} (multiple of {mult})")


def _ctile(pref, *vals):
    g = 0
    for v in vals:
        g = math.gcd(g, v)
    return _tile(g, pref)


def _cparams(sem, est_bytes):
    limit = int(min(max(est_bytes * 1.25 + (4 << 20), 16 << 20), V7X_VMEM_BYTES - (8 << 20)))
    return pltpu.CompilerParams(dimension_semantics=sem, vmem_limit_bytes=limit)


def _sds(shape, dtype):
    return jax.ShapeDtypeStruct(tuple(shape), dtype)


def _sig(x):
    return 0.5 * jnp.tanh(0.5 * x) + 0.5


def _sig_tail(x):
    return 1.0 / (1.0 + jnp.exp(-x))


def _gelu(x):
    t = jnp.tanh(GELU_C * (x + GELU_A * x * x * x))
    return 0.5 * x * (1.0 + t)


def _gelu_grad(x):
    t = jnp.tanh(GELU_C * (x + GELU_A * x * x * x))
    return 0.5 * (1.0 + t) + 0.5 * x * (1.0 - t * t) * GELU_C * (1.0 + 3.0 * GELU_A * x * x)


def _mm(a, b, mode, M, N, K, out_dtype, name, *, res=None, a_off=(0, 0), b_off=(0, 0),
        tm=640, tn=1024, tk=2048, b_chips=None, out_chips=None, out_into=None):
    tm, tn, tk = _tile(M, tm, 8 if mode != "tn" else LANES), _tile(N, tn), _tile(K, tk, LANES if mode != "tn" else 8)
    if b_chips is not None and mode == "nt":
        tk = _ctile(tk, tk, b_chips)
    if b_chips is not None and mode != "nt":
        tn = _ctile(tn, tn, b_chips)
    if out_chips is not None:
        tn = _ctile(tn, tn, out_chips)
    nk = K // tk
    ar, ac = a_off
    br, bc = b_off
    if mode == "tn":
        assert ar % tk == 0 and ac % tm == 0
        a_spec = pl.BlockSpec((tk, tm), lambda i, j, k: (k + ar // tk, i + ac // tm))
        a_dims = 0
    else:
        assert ar % tm == 0 and ac % tk == 0
        a_spec = pl.BlockSpec((tm, tk), lambda i, j, k: (i + ar // tm, k + ac // tk))
        a_dims = 1
    if mode == "nt":
        assert br % tn == 0 and bc % tk == 0
        if b_chips is None:
            b_spec = pl.BlockSpec((tn, tk), lambda i, j, k: (j + br // tn, k + bc // tk))
        else:
            per = b_chips // tk
            b_spec = pl.BlockSpec((None, tn, tk), lambda i, j, k: ((k + bc // tk) // per, j + br // tn, (k + bc // tk) % per))
        b_dims = 1
    else:
        assert br % tk == 0 and bc % tn == 0
        if b_chips is None:
            b_spec = pl.BlockSpec((tk, tn), lambda i, j, k: (k + br // tk, j + bc // tn))
        else:
            per = b_chips // tn
            b_spec = pl.BlockSpec((None, tk, tn), lambda i, j, k: ((j + bc // tn) // per, k + br // tk, (j + bc // tn) % per))
        b_dims = 0
    dims = (((a_dims,), (b_dims,)), ((), ()))
    if out_chips is None:
        o_spec = pl.BlockSpec((tm, tn), lambda i, j, k: (i, j))
        o_shape = _sds((M, N), out_dtype)
    else:
        per_o = out_chips // tn
        chip0 = 0 if out_into is None else out_into[1]
        o_spec = pl.BlockSpec((None, tm, tn), lambda i, j, k: (chip0 + j // per_o, i, j % per_o))
        o_shape = _sds((N // out_chips if out_into is None else 4, M, out_chips), out_dtype)
    has_res = res is not None
    has_into = out_into is not None

    def body(*refs):
        if has_res:
            a_ref, b_ref, r_ref, o_ref = refs[:4]
        elif has_into:
            a_ref, b_ref, _, o_ref = refs[:4]
            r_ref = None
        else:
            a_ref, b_ref, o_ref = refs[:3]
            r_ref = None
        part = lax.dot_general(a_ref[...].astype(BF16), b_ref[...].astype(BF16), dims, preferred_element_type=F32)

        def finish(acc):
            if has_res:
                acc = r_ref[...] + acc
            o_ref[...] = acc.astype(o_ref.dtype)

        if nk == 1:
            finish(part)
        else:
            acc_ref = refs[-1]
            k = pl.program_id(2)

            @pl.when(k == 0)
            def _():
                acc_ref[...] = part

            @pl.when(k > 0)
            def _():
                acc_ref[...] += part

            @pl.when(k == nk - 1)
            def _():
                finish(acc_ref[...])

    in_specs = [a_spec, b_spec] + ([o_spec] if has_res else []) + ([pl.BlockSpec(memory_space=pl.ANY)] if has_into else [])
    args = (a, b) + ((res,) if has_res else ()) + ((out_into[0],) if has_into else ())
    isz = lambda x: jnp.dtype(x.dtype).itemsize
    est = 2 * (tm * tk * isz(a) + tk * tn * isz(b) + tm * tn * jnp.dtype(out_dtype).itemsize) + tm * tn * 4 * (2 + 2 * has_res)
    return pl.pallas_call(
        body, grid=(M // tm, N // tn, nk), in_specs=in_specs, out_specs=o_spec, out_shape=o_shape,
        scratch_shapes=[pltpu.VMEM((tm, tn), F32)] if nk > 1 else [],
        input_output_aliases={2: 0} if has_into else {},
        compiler_params=_cparams(("parallel", "parallel", "arbitrary"), est), name=name,
    )(*args)


def _rms_fwd(h, g, name):
    LP, D = h.shape
    tm = _tile(LP, 640, 8)

    def body(h_ref, g_ref, o_ref):
        x = h_ref[...]
        r = lax.rsqrt(jnp.mean(x * x, axis=-1, keepdims=True) + EPS)
        o_ref[...] = (x * r * g_ref[...]).astype(o_ref.dtype)

    row = pl.BlockSpec((tm, D), lambda i: (i, 0))
    return pl.pallas_call(
        body, grid=(LP // tm,), in_specs=[row, pl.BlockSpec((1, D), lambda i: (0, 0))], out_specs=row,
        out_shape=_sds((LP, D), BF16), compiler_params=_cparams(("parallel",), 2 * tm * D * 6), name=name,
    )(h, g)


def _rms_bwd(h, g, dn, dres, name):
    LP, D = h.shape
    tm = _tile(LP, 320, 8)
    nt = LP // tm

    def body(h_ref, g_ref, dn_ref, dres_ref, dh_ref, dg_ref):
        i = pl.program_id(0)
        x = h_ref[...]
        r = lax.rsqrt(jnp.mean(x * x, axis=-1, keepdims=True) + EPS)
        xh = x * r
        dn_v = dn_ref[...]
        dxh = dn_v * g_ref[...]
        dh_ref[...] = dres_ref[...] + r * (dxh - xh * jnp.mean(dxh * xh, axis=-1, keepdims=True))
        part = jnp.sum(dn_v * xh, axis=0, keepdims=True)

        @pl.when(i == 0)
        def _():
            dg_ref[...] = part

        @pl.when(i > 0)
        def _():
            dg_ref[...] += part

    row = pl.BlockSpec((tm, D), lambda i: (i, 0))
    vec = pl.BlockSpec((1, D), lambda i: (0, 0))
    return pl.pallas_call(
        body, grid=(nt,), in_specs=[row, vec, row, row], out_specs=[row, vec],
        out_shape=[_sds((LP, D), F32), _sds((1, D), F32)],
        compiler_params=_cparams(("arbitrary",), 2 * 4 * tm * D * 4), name=name,
    )(h, g, dn, dres)


def _final_loss(h, g, tgt, off, name):
    LP, D = h.shape
    tm = LANES
    assert off % tm == 0
    ob = off // tm
    nt = LP // tm

    def body(h_ref, g_ref, t_ref, dh_ref, dg_ref, loss_ref):
        i = pl.program_id(0)
        x = h_ref[...]
        r = lax.rsqrt(jnp.mean(x * x, axis=-1, keepdims=True) + EPS)
        xh = x * r
        gv = g_ref[...]
        e = xh * gv - t_ref[...]
        valid = i >= ob
        dy = jnp.where(valid, e * (1.0 / D), 0.0)
        lpart = jnp.where(valid, 0.5 * jnp.sum(jnp.mean(e * e, axis=-1, keepdims=True), axis=0, keepdims=True), 0.0)
        dxh = dy * gv
        dh_ref[...] = r * (dxh - xh * jnp.mean(dxh * xh, axis=-1, keepdims=True))
        gpart = jnp.sum(dy * xh, axis=0, keepdims=True)

        @pl.when(i == 0)
        def _():
            dg_ref[...] = gpart
            loss_ref[...] = jnp.broadcast_to(lpart, loss_ref.shape)

        @pl.when(i > 0)
        def _():
            dg_ref[...] += gpart
            loss_ref[...] += jnp.broadcast_to(lpart, loss_ref.shape)

    row = pl.BlockSpec((tm, D), lambda i: (i, 0))
    vec = pl.BlockSpec((1, D), lambda i: (0, 0))
    return pl.pallas_call(
        body, grid=(nt,),
        in_specs=[row, vec, pl.BlockSpec((tm, D), lambda i: (jnp.maximum(i - ob, 0), 0))],
        out_specs=[row, vec, pl.BlockSpec((1, LANES), lambda i: (0, 0))],
        out_shape=[_sds((LP, D), F32), _sds((1, D), F32), _sds((1, LANES), F32)],
        compiler_params=_cparams(("arbitrary",), 2 * 3 * tm * D * 4), name=name,
    )(h, g, tgt)


def _fgate_fwd(fpre, bias, pad, name):
    LP, W = fpre.shape

    def body(f_ref, b_ref, o_ref):
        row8 = lax.broadcasted_iota(jnp.int32, (8, W), 0)
        bv = b_ref[...]

        def step(g, carry):
            r0 = pl.multiple_of(g * 8, 8)
            x = f_ref[pl.ds(r0, 8), :] + bv
            lf = jnp.minimum(x, 0.0) - jnp.log(1.0 + jnp.exp(-jnp.abs(x)))
            lf = jnp.where(r0 + row8 >= pad, lf, 0.0)
            for k in (1, 2, 4):
                lf = lf + jnp.where(row8 >= k, pltpu.roll(lf, k, 0), 0.0)
            lf = lf + carry
            o_ref[pl.ds(r0, 8), :] = lf
            return jnp.broadcast_to(lf[7:8, :], (8, W))

        lax.fori_loop(0, LP // 8, step, jnp.zeros((8, W), F32))

    return pl.pallas_call(
        body, out_shape=_sds((LP, W), F32),
        compiler_params=_cparams(None, 3 * LP * W * 4), name=name,
    )(fpre, bias)


def _fgate_bwd(dF, fpre, bias, pad, name):
    LP, W = fpre.shape
    ng = LP // 8

    def body(d_ref, f_ref, b_ref, o_ref, db_ref):
        row8 = lax.broadcasted_iota(jnp.int32, (8, W), 0)
        bv = b_ref[...]

        def step(t, carry):
            run, acc = carry
            g = ng - 1 - t
            r0 = pl.multiple_of(g * 8, 8)
            x = d_ref[pl.ds(r0, 8), :]
            for k in (1, 2, 4):
                x = x + jnp.where(row8 < 8 - k, pltpu.roll(x, 8 - k, 0), 0.0)
            x = x + run
            df = x * _sig_tail(-(f_ref[pl.ds(r0, 8), :] + bv))
            df = jnp.where(r0 + row8 >= pad, df, 0.0)
            o_ref[pl.ds(r0, 8), :] = df
            return jnp.broadcast_to(x[0:1, :], (8, W)), acc + df

        _, acc = lax.fori_loop(0, ng, step, (jnp.zeros((8, W), F32), jnp.zeros((8, W), F32)))
        db_ref[...] = jnp.sum(acc, axis=0, keepdims=True)

    return pl.pallas_call(
        body, out_shape=[_sds((LP, W), F32), _sds((1, W), F32)],
        compiler_params=_cparams(None, 4 * LP * W * 4), name=name,
    )(dF, fpre, bias)


def _place():
    return lax.axis_index("x"), lax.axis_index("y"), lax.axis_index("c")


def _other_chips(x, y):
    return [(1 - x, y), (x, 1 - y), (1 - x, 1 - y)]


def _gather_phases(g_refs, send_sems, recv_sems):
    nw = len(g_refs)
    x, y, c = _place()
    chips = _other_chips(x, y)
    me = 2 * x + y

    def copy(i, k, chip, half, to):
        HR = g_refs[i].shape[1] // 2
        rows = g_refs[i].at[chip, pl.ds(pl.multiple_of(half * HR, 16), HR), :]
        return pltpu.make_async_remote_copy(
            src_ref=rows, dst_ref=rows, send_sem=send_sems.at[6 * i + k], recv_sem=recv_sems.at[6 * i + k],
            device_id=to, device_id_type=MESH)

    pairs = [(i, k, cx, cy) for i in range(nw) for k, (cx, cy) in enumerate(chips)]

    def start():
        for i, k, cx, cy in pairs:
            copy(i, k, me, c, (cx, cy, c)).start()

    def forward():
        for i, k, cx, cy in pairs:
            copy(i, k, 2 * cx + cy, c, (cx, cy, c)).wait_recv()
            copy(i, 3 + k, 2 * cx + cy, c, (x, y, 1 - c)).start()

    def finish():
        for i, k, cx, cy in pairs:
            copy(i, 3 + k, 2 * cx + cy, 1 - c, (x, y, 1 - c)).wait_recv()
        for i, k, cx, cy in pairs:
            copy(i, k, me, c, (cx, cy, c)).wait_send()
            copy(i, 3 + k, 2 * cx + cy, c, (x, y, 1 - c)).wait_send()

    return start, forward, finish


def _scatter_phases(s_refs, land_refs, send_sems, recv_sems):
    x, y, c = _place()
    chips = _other_chips(x, y)

    def copy(i, k, cx, cy):
        return pltpu.make_async_remote_copy(
            src_ref=s_refs[i].at[2 * cx + cy], dst_ref=land_refs[i].at[k],
            send_sem=send_sems.at[3 * i + k], recv_sem=recv_sems.at[3 * i + k],
            device_id=(cx, cy, c), device_id_type=MESH)

    pairs = [(i, k, cx, cy) for i in range(len(s_refs)) for k, (cx, cy) in enumerate(chips)]

    def start():
        for p in pairs:
            copy(*p).start()

    def finish():
        for p in pairs:
            copy(*p).wait_recv()
        for p in pairs:
            copy(*p).wait_send()

    return start, finish


HBM_SPEC = pl.BlockSpec(memory_space=pltpu.HBM)


def _col_to_row(col):
    n = col.shape[0]
    return jnp.transpose(jnp.broadcast_to(col, (n, LANES)))[0:1, :]


def _row_to_col(row):
    n = row.shape[1]
    return jnp.transpose(jnp.broadcast_to(row, (LANES, n)))[:, 0:1]


def _attn_fwd(z, bias_t, H, pad, name, gather=()):
    LP = z.shape[0]
    BQ = BK = _tile(LP, 640)
    scale = HEAD_DIM ** -0.5
    NT = (((1,), (1,)), ((), ()))

    nw = len(gather)
    nq = LP // BQ

    def body(*refs):
        q_ref, k_ref, v_ref, b_ref = refs[:4]
        o_ref, of_ref, lse_ref = refs[4 + nw:7 + nw]
        hd = pl.program_id(0)
        qi = pl.program_id(1)
        if nw:
            start, forward, finish = _gather_phases(refs[7 + nw:7 + 2 * nw], *refs[7 + 2 * nw:])
            pl.when((hd == 0) & (qi == 0))(start)
            pl.when((hd == H // 2) & (qi == 0))(forward)
        q = q_ref[...]

        def tile(kb, carry, masked):
            m, l, acc = carry
            k0 = pl.multiple_of(kb * BK, BK)
            s = lax.dot_general(q, k_ref[pl.ds(k0, BK), :], NT, preferred_element_type=F32) * scale
            s = s + b_ref[:, pl.ds(k0, BK)]
            if masked:
                ri = lax.broadcasted_iota(jnp.int32, (BQ, BK), 0)
                ci = lax.broadcasted_iota(jnp.int32, (BQ, BK), 1)
                s = jnp.where(ri >= ci, s, NEG)
            mn = jnp.maximum(m, jnp.max(s, axis=-1, keepdims=True))
            p = jnp.exp(s - mn)
            alpha = jnp.exp(m - mn)
            l = alpha * l + jnp.sum(p, axis=-1, keepdims=True)
            vk = v_ref[pl.ds(k0, BK), :]
            p_hi = p.astype(BF16)
            p_lo = (p - p_hi.astype(F32)).astype(BF16)
            pv = jnp.dot(p_hi, vk, preferred_element_type=F32) + jnp.dot(p_lo, vk, preferred_element_type=F32)
            return mn, l, alpha * acc + pv

        carry = (jnp.full((BQ, 1), NEG, F32), jnp.zeros((BQ, 1), F32), jnp.zeros((BQ, HEAD_DIM), F32))
        carry = lax.fori_loop(0, qi, lambda kb, c: tile(kb, c, False), carry)
        m, l, acc = tile(qi, carry, True)
        rows = qi * BQ + lax.broadcasted_iota(jnp.int32, (BQ, 1), 0)
        o = jnp.where(rows >= pad, acc / l, 0.0)
        o_ref[...] = o.astype(o_ref.dtype)
        of_ref[...] = o
        lse_ref[...] = _col_to_row(m + jnp.log(l))
        if nw:
            pl.when((hd == H - 1) & (qi == nq - 1))(finish)

    in_specs = [
        pl.BlockSpec((BQ, HEAD_DIM), lambda h, i: (i, h)),
        pl.BlockSpec((LP, HEAD_DIM), lambda h, i: (0, H + h)),
        pl.BlockSpec((LP, HEAD_DIM), lambda h, i: (0, 2 * H + h)),
        pl.BlockSpec((None, 1, LP), lambda h, i: (h, 0, 0)),
    ]
    out_specs = [
        pl.BlockSpec((BQ, HEAD_DIM), lambda h, i: (i, h)),
        pl.BlockSpec((BQ, HEAD_DIM), lambda h, i: (i, h)),
        pl.BlockSpec((None, 1, BQ), lambda h, i: (h, 0, i)),
    ]
    est = 2 * (2 * LP * HEAD_DIM * 2 + 8 * LP * 4) + 20 * BQ * LANES * 4 + 8 * BQ * BK * 4
    sems = [pltpu.SemaphoreType.DMA((6 * nw,)), pltpu.SemaphoreType.DMA((6 * nw,))] if nw else []
    return pl.pallas_call(
        body, grid=(H, nq), in_specs=in_specs + [HBM_SPEC] * nw, out_specs=out_specs + [HBM_SPEC] * nw,
        out_shape=[_sds((LP, H * HEAD_DIM), BF16), _sds((LP, H * HEAD_DIM), F32), _sds((H, 1, LP), F32)]
        + [_sds(g.shape, g.dtype) for g in gather],
        scratch_shapes=sems, input_output_aliases={4 + i: 3 + i for i in range(nw)},
        compiler_params=_cparams(("arbitrary", "arbitrary"), est), name=name,
    )(z, z, z, bias_t, *gather)


def _attn_delta(do, o, H, name):
    LP = do.shape[0]
    tm = _tile(LP, 640)

    def body(do_ref, o_ref, d_ref):
        d_ref[...] = _col_to_row(jnp.sum(do_ref[...].astype(F32) * o_ref[...].astype(F32), axis=-1, keepdims=True))

    blk = pl.BlockSpec((tm, HEAD_DIM), lambda h, i: (i, h))
    return pl.pallas_call(
        body, grid=(H, LP // tm), in_specs=[blk, blk],
        out_specs=pl.BlockSpec((None, 1, tm), lambda h, i: (h, 0, i)),
        out_shape=_sds((H, 1, LP), F32),
        compiler_params=_cparams(("parallel", "parallel"), 8 * tm * LANES * 4), name=name,
    )(do, o)


def _attn_bwd(z, do, lse_t, delta_t, bias_t, H, name, scatter=()):
    LP = z.shape[0]
    BK = BQ = _tile(LP, 640)
    nk = nq = LP // BK
    scale = HEAD_DIM ** -0.5
    NT = (((1,), (1,)), ((), ()))
    TN = (((0,), (0,)), ((), ()))

    nw = len(scatter)

    def body(*refs):
        q_ref, k_ref, v_ref, do_ref, lse_ref, dl_ref, b_ref = refs[:7]
        dq_ref, dk_ref, dv_ref, db_ref = refs[7 + nw:11 + nw]
        dq_acc = refs[11 + 2 * nw]
        hd = pl.program_id(0)
        kj = pl.program_id(1)
        if nw:
            start, finish = _scatter_phases(refs[7:7 + nw], refs[11 + nw:11 + 2 * nw], *refs[12 + 2 * nw:])
            pl.when((hd == 0) & (kj == 0))(start)

        @pl.when(kj == 0)
        def _():
            dq_acc[...] = jnp.zeros_like(dq_acc)

        k = k_ref[...]
        v = v_ref[...]
        bcol = _row_to_col(b_ref[:, pl.ds(pl.multiple_of(kj * BK, BK), BK)])

        def tile(qc, carry, masked):
            dk, dv, db = carry
            q0 = pl.multiple_of(qc * BQ, BQ)
            q = q_ref[pl.ds(q0, BQ), :]
            dout = do_ref[pl.ds(q0, BQ), :]
            st = lax.dot_general(k, q, NT, preferred_element_type=F32) * scale + bcol
            if masked:
                ri = lax.broadcasted_iota(jnp.int32, (BK, BQ), 0)
                ci = lax.broadcasted_iota(jnp.int32, (BK, BQ), 1)
                st = jnp.where(ci >= ri, st, NEG)
            pt = jnp.exp(st - lse_ref[:, pl.ds(q0, BQ)])
            dv = dv + jnp.dot(pt.astype(BF16), dout, preferred_element_type=F32)
            dpt = lax.dot_general(v, dout, NT, preferred_element_type=F32)
            dst = pt * (dpt - dl_ref[:, pl.ds(q0, BQ)])
            db = db + jnp.sum(dst, axis=-1, keepdims=True)
            dsb = (dst * scale).astype(BF16)
            dk = dk + jnp.dot(dsb, q, preferred_element_type=F32)
            dq_acc[pl.ds(q0, BQ), :] += lax.dot_general(dsb, k, TN, preferred_element_type=F32)
            return dk, dv, db

        carry = (jnp.zeros((BK, HEAD_DIM), F32), jnp.zeros((BK, HEAD_DIM), F32), jnp.zeros((BK, 1), F32))
        carry = tile(kj, carry, True)
        dk, dv, db = lax.fori_loop(kj + 1, nq, lambda qc, c: tile(qc, c, False), carry)
        dk_ref[...] = dk.astype(dk_ref.dtype)
        dv_ref[...] = dv.astype(dv_ref.dtype)
        db_ref[...] = _col_to_row(db)

        @pl.when(kj == nk - 1)
        def _():
            dq_ref[...] = dq_acc[...].astype(dq_ref.dtype)

        if nw:
            pl.when((hd == H - 1) & (kj == nk - 1))(finish)

    full = lambda c0: pl.BlockSpec((LP, HEAD_DIM), lambda h, j: (0, c0 + h))
    blk = lambda c0: pl.BlockSpec((BK, HEAD_DIM), lambda h, j: (j, c0 + h))
    vec = pl.BlockSpec((None, 1, LP), lambda h, j: (h, 0, 0))
    in_specs = [full(0), blk(H), blk(2 * H), full(0), vec, vec, vec]
    out_specs = [full(0), blk(0), blk(0), pl.BlockSpec((None, 1, BK), lambda h, j: (h, 0, j))]
    est = 2 * (3 * LP * HEAD_DIM * 2 + 16 * LP * 4) + LP * HEAD_DIM * 4 + 24 * BK * LANES * 4 + 10 * BK * BQ * 4
    sems = [pltpu.SemaphoreType.DMA((3 * nw,)), pltpu.SemaphoreType.DMA((3 * nw,))] if nw else []
    return pl.pallas_call(
        body, grid=(H, nk), in_specs=in_specs + [HBM_SPEC] * nw, out_specs=out_specs + [HBM_SPEC] * nw,
        out_shape=[_sds((LP, H * HEAD_DIM), BF16)] * 3 + [_sds((H, 1, LP), F32)]
        + [_sds((3,) + p.shape[1:], p.dtype) for p in scatter],
        scratch_shapes=[pltpu.VMEM((LP, HEAD_DIM), F32)] + sems,
        compiler_params=_cparams(("arbitrary", "arbitrary"), est), name=name,
    )(z, z, z, do, lse_t, delta_t, bias_t, *scatter)


def _ssm_disc(lr, li, ldt, br, bi):
    dt = jnp.exp(ldt)
    mag = jnp.exp(lr * dt)
    a_re = mag * jnp.cos(li * dt)
    a_im = mag * jnp.sin(li * dt)
    den = lr * lr + li * li
    nr = a_re - 1.0
    z_re = (nr * lr + a_im * li) / den
    z_im = (a_im * lr - nr * li) / den
    return a_re, a_im, z_re * br - z_im * bi, z_re * bi + z_im * br


def _ssm_prep(lr, li, ldt, br, bi, name):
    GP, C = br.shape

    def body(lr_ref, li_ref, ldt_ref, br_ref, bi_ref, ar_ref, ai_ref, bbr_ref, bbi_ref, pr_ref, pi_ref):
        a_re, a_im, bb_re, bb_im = _ssm_disc(lr_ref[...], li_ref[...], ldt_ref[...], br_ref[...], bi_ref[...])
        ar_ref[...] = a_re
        ai_ref[...] = a_im
        bbr_ref[...] = bb_re
        bbi_ref[...] = bb_im
        lane = lax.broadcasted_iota(jnp.int32, (tg, 8), 1)
        pr, pi_ = a_re, a_im
        accr = jnp.zeros((tg, 8), F32)
        acci = jnp.zeros((tg, 8), F32)
        for k in range(8):
            accr = jnp.where(lane == k, pr, accr)
            acci = jnp.where(lane == k, pi_, acci)
            pr, pi_ = pr * a_re - pi_ * a_im, pr * a_im + pi_ * a_re
        pr_ref[...] = accr
        pi_ref[...] = acci

    tg = _tile(GP, 512, 8)
    blk = lambda w: pl.BlockSpec((tg, w), lambda i: (i, 0))
    col = _sds((GP, 1), F32)
    return pl.pallas_call(
        body, grid=(GP // tg,), in_specs=[blk(1), blk(1), blk(1), blk(C), blk(C)],
        out_specs=[blk(1), blk(1), blk(C), blk(C), blk(8), blk(8)],
        out_shape=[col, col, _sds((GP, C), F32), _sds((GP, C), F32), _sds((GP, 8), F32), _sds((GP, 8), F32)],
        compiler_params=_cparams(("parallel",), 48 * tg * LANES * 4), name=name,
    )(lr, li, ldt, br, bi)


def _ssm_prep_bwd(lr, li, ldt, br, bi, dar, dai, dbbr, dbbi, name):
    GP, C = br.shape

    def body(lr_ref, li_ref, ldt_ref, br_ref, bi_ref, dar_ref, dai_ref, dbbr_ref, dbbi_ref,
             glr_ref, gli_ref, gldt_ref, gbr_ref, gbi_ref):
        _, vjp = jax.vjp(_ssm_disc, lr_ref[...], li_ref[...], ldt_ref[...], br_ref[...], bi_ref[...])
        glr, gli, gldt, gbr, gbi = vjp((dar_ref[...], dai_ref[...], dbbr_ref[...], dbbi_ref[...]))
        glr_ref[...] = glr
        gli_ref[...] = gli
        gldt_ref[...] = gldt
        gbr_ref[...] = gbr
        gbi_ref[...] = gbi

    tg = _tile(GP, 512, 8)
    blk = lambda w: pl.BlockSpec((tg, w), lambda i: (i, 0))
    col = _sds((GP, 1), F32)
    return pl.pallas_call(
        body, grid=(GP // tg,), in_specs=[blk(1), blk(1), blk(1), blk(C), blk(C), blk(1), blk(1), blk(C), blk(C)],
        out_specs=[blk(1), blk(1), blk(1), blk(C), blk(C)],
        out_shape=[col, col, col, _sds((GP, C), F32), _sds((GP, C), F32)],
        compiler_params=_cparams(("parallel",), 96 * tg * LANES * 4), name=name,
    )(lr, li, ldt, br, bi, dar, dai, dbbr, dbbi)


def _cmul_add(xr, xi, mr, mi, sr, si):
    return xr + mr * sr - mi * si, xi + mr * si + mi * sr


def _ssm_fwd(z, u_col0, coef, bbr, bbi, ccr, cci, dskip, name):
    LP = z.shape[0]
    NB, CB, S = bbr.shape
    TS = _tile(LP, 640, 8)
    nt = LP // TS

    def body(u_ref, coef_ref, bbr_ref, bbi_ref, ccr_ref, cci_ref, ds_ref, y_ref, yg_ref, hr_ref, hi_ref, bur, bui, carry):
        i = pl.program_id(1)

        @pl.when(i == 0)
        def _():
            carry[...] = jnp.zeros_like(carry)

        u = u_ref[...]
        bur[...] = jnp.dot(u, bbr_ref[...], preferred_element_type=F32)
        bui[...] = jnp.dot(u, bbi_ref[...], preferred_element_type=F32)

        def step(g, c):
            cr, ci = c
            r0 = pl.multiple_of(g * 8, 8)
            xr = bur[pl.ds(r0, 8), :]
            xi = bui[pl.ds(r0, 8), :]
            for n, k in enumerate((1, 2, 4)):
                xr, xi = _cmul_add(xr, xi, coef_ref[2 * n], coef_ref[2 * n + 1], pltpu.roll(xr, k, 0), pltpu.roll(xi, k, 0))
            xr, xi = _cmul_add(xr, xi, coef_ref[6], coef_ref[7], cr, ci)
            hr_ref[pl.ds(r0, 8), :] = xr
            hi_ref[pl.ds(r0, 8), :] = xi
            return jnp.broadcast_to(xr[7:8, :], (8, S)), jnp.broadcast_to(xi[7:8, :], (8, S))

        cr, ci = lax.fori_loop(0, TS // 8, step, (carry[0], carry[1]))
        carry[0] = cr
        carry[1] = ci
        y = (jnp.dot(hr_ref[...].astype(BF16), ccr_ref[...], preferred_element_type=F32)
             - jnp.dot(hi_ref[...].astype(BF16), cci_ref[...], preferred_element_type=F32)
             + ds_ref[...] * u.astype(F32))
        y_ref[...] = y
        yg_ref[...] = _gelu(y).astype(yg_ref.dtype)

    ucb = u_col0 // CB
    in_specs = [
        pl.BlockSpec((TS, CB), lambda j, i: (i, ucb + j)),
        pl.BlockSpec((None, 8, 8, S), lambda j, i: (j, 0, 0, 0)),
        pl.BlockSpec((None, CB, S), lambda j, i: (j, 0, 0)),
        pl.BlockSpec((None, CB, S), lambda j, i: (j, 0, 0)),
        pl.BlockSpec((None, S, CB), lambda j, i: (j, 0, 0)),
        pl.BlockSpec((None, S, CB), lambda j, i: (j, 0, 0)),
        pl.BlockSpec((1, CB), lambda j, i: (0, j)),
    ]
    yb = pl.BlockSpec((TS, CB), lambda j, i: (i, j))
    hb = pl.BlockSpec((TS, S), lambda j, i: (i, j))
    est = 2 * (2 * TS * S * 4 + 3 * TS * CB * 4 + 8 * 8 * S * 4 + 4 * CB * S * 2) + 3 * TS * S * 4
    return pl.pallas_call(
        body, grid=(NB, nt), in_specs=in_specs, out_specs=[yb, yb, hb, hb],
        out_shape=[_sds((LP, NB * CB), F32), _sds((LP, NB * CB), BF16), _sds((LP, NB * S), F32), _sds((LP, NB * S), F32)],
        scratch_shapes=[pltpu.VMEM((TS, S), F32), pltpu.VMEM((TS, S), F32), pltpu.VMEM((2, 8, S), F32)],
        compiler_params=_cparams(("parallel", "arbitrary"), est), name=name,
    )(z, coef, bbr, bbi, ccr, cci, dskip)


def _ssm_bwd(z, u_col0, dyg, y, hr, hi, coef_rev, bbr_t, bbi_t, ccr_t, cci_t, dskip, name):
    LP = z.shape[0]
    NB, S, CB = bbr_t.shape
    TS = _tile(LP, 640, 8)
    nt = LP // TS
    ng = TS // 8
    TN = (((0,), (0,)), ((), ()))

    def body(u_ref, dyg_ref, y_ref, hr_ref, hi_ref, tr_ref, ti_ref, coef_ref, bbr_ref, bbi_ref, ccr_ref, cci_ref, ds_ref,
             du_ref, dbbr_ref, dbbi_ref, dccr_ref, dcci_ref, dar_ref, dai_ref, dd_ref,
             gr, gi, carry, acc_bbr, acc_bbi, acc_ccr, acc_cci, acc_a, acc_d):
        i = pl.program_id(1)

        @pl.when(i == 0)
        def _():
            for ref in (carry, acc_bbr, acc_bbi, acc_ccr, acc_cci, acc_a, acc_d):
                ref[...] = jnp.zeros_like(ref)

        u = u_ref[...]
        dy = dyg_ref[...] * _gelu_grad(y_ref[...])
        dyb = dy.astype(BF16)
        gr[...] = jnp.dot(dyb, ccr_ref[...], preferred_element_type=F32)
        gi[...] = -jnp.dot(dyb, cci_ref[...], preferred_element_type=F32)
        row8 = lax.broadcasted_iota(jnp.int32, (8, S), 0)
        first_chunk = i == nt - 1
        tail_r = jnp.where(first_chunk, 0.0, tr_ref[...])
        tail_i = jnp.where(first_chunk, 0.0, ti_ref[...])

        def step(t, c):
            cr, ci, sar, sai = c
            g = ng - 1 - t
            r0 = pl.multiple_of(g * 8, 8)
            xr = gr[pl.ds(r0, 8), :]
            xi = gi[pl.ds(r0, 8), :]
            for n, k in enumerate((1, 2, 4)):
                xr, xi = _cmul_add(xr, xi, coef_ref[2 * n], coef_ref[2 * n + 1], pltpu.roll(xr, 8 - k, 0), pltpu.roll(xi, 8 - k, 0))
            xr, xi = _cmul_add(xr, xi, coef_ref[6], coef_ref[7], cr, ci)
            gr[pl.ds(r0, 8), :] = xr
            gi[pl.ds(r0, 8), :] = xi
            p0 = pl.multiple_of(jnp.maximum(g - 1, 0) * 8, 8)
            pr = jnp.where(g > 0, hr_ref[pl.ds(p0, 8), :], tail_r)
            pi_ = jnp.where(g > 0, hi_ref[pl.ds(p0, 8), :], tail_i)
            hpr = pltpu.roll(jnp.where(row8 == 7, pr, hr_ref[pl.ds(r0, 8), :]), 1, 0)
            hpi = pltpu.roll(jnp.where(row8 == 7, pi_, hi_ref[pl.ds(r0, 8), :]), 1, 0)
            sar = sar + xr * hpr + xi * hpi
            sai = sai + xi * hpr - xr * hpi
            return jnp.broadcast_to(xr[0:1, :], (8, S)), jnp.broadcast_to(xi[0:1, :], (8, S)), sar, sai

        zero = jnp.zeros((8, S), F32)
        cr, ci, sar, sai = lax.fori_loop(0, ng, step, (carry[0], carry[1], zero, zero))
        carry[0] = cr
        carry[1] = ci
        acc_a[0] += sar
        acc_a[1] += sai
        grb = gr[...].astype(BF16)
        gib = gi[...].astype(BF16)
        du = (jnp.dot(grb, bbr_ref[...], preferred_element_type=F32) + jnp.dot(gib, bbi_ref[...], preferred_element_type=F32)
              + ds_ref[...] * dy)
        du_ref[...] = du.astype(du_ref.dtype)
        acc_bbr[...] += lax.dot_general(u, grb, TN, preferred_element_type=F32)
        acc_bbi[...] += lax.dot_general(u, gib, TN, preferred_element_type=F32)
        acc_ccr[...] += lax.dot_general(hr_ref[...].astype(BF16), dyb, TN, preferred_element_type=F32)
        acc_cci[...] -= lax.dot_general(hi_ref[...].astype(BF16), dyb, TN, preferred_element_type=F32)
        acc_d[...] += jnp.sum(dy * u.astype(F32), axis=0, keepdims=True)

        @pl.when(i == nt - 1)
        def _():
            dbbr_ref[...] = acc_bbr[...]
            dbbi_ref[...] = acc_bbi[...]
            dccr_ref[...] = acc_ccr[...]
            dcci_ref[...] = acc_cci[...]
            dar_ref[...] = jnp.sum(acc_a[0], axis=0, keepdims=True)
            dai_ref[...] = jnp.sum(acc_a[1], axis=0, keepdims=True)
            dd_ref[...] = acc_d[...]

    ucb = u_col0 // CB
    rev = lambda i: nt - 1 - i
    tail = lambda j, i: (jnp.maximum(rev(i) * ng - 1, 0), j)
    yb = pl.BlockSpec((TS, CB), lambda j, i: (rev(i), j))
    hb = pl.BlockSpec((TS, S), lambda j, i: (rev(i), j))
    in_specs = [
        pl.BlockSpec((TS, CB), lambda j, i: (rev(i), ucb + j)), yb, yb, hb, hb,
        pl.BlockSpec((8, S), tail), pl.BlockSpec((8, S), tail),
        pl.BlockSpec((None, 8, 8, S), lambda j, i: (j, 0, 0, 0)),
        pl.BlockSpec((None, S, CB), lambda j, i: (j, 0, 0)),
        pl.BlockSpec((None, S, CB), lambda j, i: (j, 0, 0)),
        pl.BlockSpec((None, CB, S), lambda j, i: (j, 0, 0)),
        pl.BlockSpec((None, CB, S), lambda j, i: (j, 0, 0)),
        pl.BlockSpec((1, CB), lambda j, i: (0, j)),
    ]
    mat_cs = pl.BlockSpec((None, CB, S), lambda j, i: (j, 0, 0))
    mat_sc = pl.BlockSpec((None, S, CB), lambda j, i: (j, 0, 0))
    vec_s = pl.BlockSpec((None, 1, S), lambda j, i: (j, 0, 0))
    out_specs = [yb, mat_cs, mat_cs, mat_sc, mat_sc, vec_s, vec_s, pl.BlockSpec((1, CB), lambda j, i: (0, j))]
    out_shape = [_sds((LP, NB * CB), BF16), _sds((NB, CB, S), F32), _sds((NB, CB, S), F32), _sds((NB, S, CB), F32),
                 _sds((NB, S, CB), F32), _sds((NB, 1, S), F32), _sds((NB, 1, S), F32), _sds((1, NB * CB), F32)]
    scratch = [pltpu.VMEM((TS, S), F32), pltpu.VMEM((TS, S), F32), pltpu.VMEM((2, 8, S), F32),
               pltpu.VMEM((CB, S), F32), pltpu.VMEM((CB, S), F32), pltpu.VMEM((S, CB), F32), pltpu.VMEM((S, CB), F32),
               pltpu.VMEM((2, 8, S), F32), pltpu.VMEM((1, CB), F32)]
    est = 2 * (2 * TS * S * 4 + 4 * TS * CB * 4 + 8 * 8 * S * 4 + 12 * CB * S * 4) + 4 * TS * S * 4
    return pl.pallas_call(
        body, grid=(NB, nt), in_specs=in_specs, out_specs=out_specs, out_shape=out_shape, scratch_shapes=scratch,
        compiler_params=_cparams(("parallel", "arbitrary"), est), name=name,
    )(z, dyg, y, hr, hi, hr, hi, coef_rev, bbr_t, bbi_t, ccr_t, cci_t, dskip)


def _merge_fwd(yab, z, ao, D, ga0, gb0, name):
    LP = z.shape[0]
    tm = _tile(LP, 640, 8)
    tn = _ctile(512, D, ga0, gb0)
    nj = D // tn

    def body(ya_ref, yb_ref, ga_ref, gb_ref, ao_ref, o_ref):
        f = lambda r: r[...].astype(F32)
        ssm = f(ya_ref) * _sig(f(yb_ref))
        o_ref[...] = (_sig(f(ga_ref)) * ssm + _sig(f(gb_ref)) * f(ao_ref)).astype(o_ref.dtype)

    blk = lambda c0: pl.BlockSpec((tm, tn), lambda i, j: (i, c0 // tn + j))
    return pl.pallas_call(
        body, grid=(LP // tm, nj), in_specs=[blk(0), blk(D), blk(ga0), blk(gb0), blk(0)], out_specs=blk(0),
        out_shape=_sds((LP, D), BF16), compiler_params=_cparams(("parallel", "parallel"), 2 * 6 * tm * tn * 4), name=name,
    )(yab, yab, z, z, ao)


def _merge_bwd(dm, yab, z, ao, D, ga0, gb0, name):
    LP = z.shape[0]
    tm = _tile(LP, 640, 8)
    tn = _ctile(512, D, ga0, gb0)
    nj = D // tn

    def body(dm_ref, ya_ref, yb_ref, ga_ref, gb_ref, ao_ref, dya_ref, dyb_ref, dga_ref, dgb_ref, dao_ref):
        f = lambda r: r[...].astype(F32)
        dmv, ya, ao_v = f(dm_ref), f(ya_ref), f(ao_ref)
        sa, sb, sy = _sig(f(ga_ref)), _sig(f(gb_ref)), _sig(f(yb_ref))
        t = dmv * sa
        dya_ref[...] = (t * sy).astype(BF16)
        dyb_ref[...] = (t * ya * sy * (1.0 - sy)).astype(BF16)
        dga_ref[...] = (dmv * (ya * sy) * sa * (1.0 - sa)).astype(BF16)
        dgb_ref[...] = (dmv * ao_v * sb * (1.0 - sb)).astype(BF16)
        dao_ref[...] = (dmv * sb).astype(BF16)

    blk = lambda c0: pl.BlockSpec((tm, tn), lambda i, j: (i, c0 // tn + j))
    return pl.pallas_call(
        body, grid=(LP // tm, nj), in_specs=[blk(0), blk(0), blk(D), blk(ga0), blk(gb0), blk(0)], out_specs=[blk(0)] * 5,
        out_shape=[_sds((LP, D), BF16)] * 5, compiler_params=_cparams(("parallel", "parallel"), 2 * 11 * tm * tn * 4), name=name,
    )(dm, yab, yab, z, z, ao)


def _shift_down(x, halo, k, row8):
    s = pltpu.roll(x, k, 0)
    top = jnp.where(row8 < k, pltpu.roll(halo, k, 0), s[0:8])
    return jnp.concatenate([top, s[8:]], axis=0) if x.shape[0] > 8 else top


def _shift_up(x, halo, k, row8):
    tm = x.shape[0]
    s = pltpu.roll(x, tm - k, 0)
    bot = jnp.where(row8 >= 8 - k, pltpu.roll(halo, 8 - k, 0), s[tm - 8:])
    return jnp.concatenate([s[:tm - 8], bot], axis=0) if tm > 8 else bot


def _conv_gate(g, halo, w_ref, cb, row8):
    return cb + w_ref[0:1, :] * _shift_down(g, halo, 2, row8) + w_ref[1:2, :] * _shift_down(g, halo, 1, row8) + w_ref[2:3, :] * g


def _convact_fwd(gu, cw, cb, DFF, name):
    LP = gu.shape[0]
    tm = _tile(LP, 640, 8)
    tn = _tile(DFF, 512)
    nj = DFF // tn
    t8 = tm // 8

    def body(g_ref, h_ref, u_ref, w_ref, b_ref, o_ref):
        i = pl.program_id(0)
        row8 = lax.broadcasted_iota(jnp.int32, (8, tn), 0)
        g = g_ref[...].astype(F32)
        halo = jnp.where(i > 0, h_ref[...].astype(F32), 0.0)
        gc = _conv_gate(g, halo, w_ref, b_ref[...], row8)
        o_ref[...] = (gc * _sig(gc) * u_ref[...].astype(F32)).astype(o_ref.dtype)

    in_specs = [
        pl.BlockSpec((tm, tn), lambda i, j: (i, j)),
        pl.BlockSpec((8, tn), lambda i, j: (jnp.maximum(i * t8 - 1, 0), j)),
        pl.BlockSpec((tm, tn), lambda i, j: (i, nj + j)),
        pl.BlockSpec((3, tn), lambda i, j: (0, j)),
        pl.BlockSpec((1, tn), lambda i, j: (0, j)),
    ]
    return pl.pallas_call(
        body, grid=(LP // tm, nj), in_specs=in_specs, out_specs=pl.BlockSpec((tm, tn), lambda i, j: (i, j)),
        out_shape=_sds((LP, DFF), BF16), compiler_params=_cparams(("parallel", "parallel"), 2 * 8 * tm * tn * 4), name=name,
    )(gu, gu, gu, cw, cb)


def _convact_bwd(dact, gu, cw, cb, DFF, name):
    LP = gu.shape[0]
    tm = _tile(LP, 640, 8)
    tn = _tile(DFF, 512)
    nj = DFF // tn
    t8 = tm // 8

    def body(da_ref, g_ref, h_ref, u_ref, w_ref, b_ref, dgc_ref, du_ref):
        i = pl.program_id(0)
        row8 = lax.broadcasted_iota(jnp.int32, (8, tn), 0)
        g = g_ref[...].astype(F32)
        halo = jnp.where(i > 0, h_ref[...].astype(F32), 0.0)
        gc = _conv_gate(g, halo, w_ref, b_ref[...], row8)
        sg = _sig(gc)
        da = da_ref[...].astype(F32)
        du_ref[...] = (da * gc * sg).astype(du_ref.dtype)
        dgc_ref[...] = (da * u_ref[...].astype(F32) * sg * (1.0 + gc * (1.0 - sg))).astype(dgc_ref.dtype)

    blk = pl.BlockSpec((tm, tn), lambda i, j: (i, j))
    in_specs = [
        blk, blk,
        pl.BlockSpec((8, tn), lambda i, j: (jnp.maximum(i * t8 - 1, 0), j)),
        pl.BlockSpec((tm, tn), lambda i, j: (i, nj + j)),
        pl.BlockSpec((3, tn), lambda i, j: (0, j)),
        pl.BlockSpec((1, tn), lambda i, j: (0, j)),
    ]
    return pl.pallas_call(
        body, grid=(LP // tm, nj), in_specs=in_specs, out_specs=[blk, blk],
        out_shape=[_sds((LP, DFF), BF16), _sds((LP, DFF), BF16)],
        compiler_params=_cparams(("parallel", "parallel"), 2 * 10 * tm * tn * 4), name=name,
    )(dact, gu, gu, gu, cw, cb)


def _conv_bwd(dgc, gu, cw, DFF, pad, name):
    LP = gu.shape[0]
    tm = _tile(LP, 640, 8)
    tn = _tile(DFF, 512)
    nj = DFF // tn
    t8 = tm // 8
    nt = LP // tm

    def body(d_ref, dn_ref, g_ref, h_ref, w_ref, dg_ref, dw_ref, db_ref):
        i = pl.program_id(1)
        row8 = lax.broadcasted_iota(jnp.int32, (8, tn), 0)
        d = d_ref[...].astype(F32)
        nxt = jnp.where(i < nt - 1, dn_ref[...].astype(F32), 0.0)
        dg = w_ref[2:3, :] * d + w_ref[1:2, :] * _shift_up(d, nxt, 1, row8) + w_ref[0:1, :] * _shift_up(d, nxt, 2, row8)
        rows = i * tm + lax.broadcasted_iota(jnp.int32, (tm, 1), 0)
        dg_ref[...] = jnp.where(rows >= pad, dg, 0.0).astype(dg_ref.dtype)
        g = g_ref[...].astype(F32)
        halo = jnp.where(i > 0, h_ref[...].astype(F32), 0.0)
        row3 = lax.broadcasted_iota(jnp.int32, (3, tn), 0)
        s0 = jnp.sum(d * _shift_down(g, halo, 2, row8), axis=0, keepdims=True)
        s1 = jnp.sum(d * _shift_down(g, halo, 1, row8), axis=0, keepdims=True)
        s2 = jnp.sum(d * g, axis=0, keepdims=True)
        dw = jnp.where(row3 == 0, s0, jnp.where(row3 == 1, s1, s2))
        dbp = jnp.sum(d, axis=0, keepdims=True)

        @pl.when(i == 0)
        def _():
            dw_ref[...] = dw
            db_ref[...] = dbp

        @pl.when(i > 0)
        def _():
            dw_ref[...] += dw
            db_ref[...] += dbp

    blk = pl.BlockSpec((tm, tn), lambda j, i: (i, j))
    in_specs = [
        blk,
        pl.BlockSpec((8, tn), lambda j, i: (jnp.minimum((i + 1) * t8, LP // 8 - 1), j)),
        blk,
        pl.BlockSpec((8, tn), lambda j, i: (jnp.maximum(i * t8 - 1, 0), j)),
        pl.BlockSpec((3, tn), lambda j, i: (0, j)),
    ]
    out_specs = [blk, pl.BlockSpec((3, tn), lambda j, i: (0, j)), pl.BlockSpec((1, tn), lambda j, i: (0, j))]
    return pl.pallas_call(
        body, grid=(nj, nt), in_specs=in_specs, out_specs=out_specs,
        out_shape=[_sds((LP, DFF), BF16), _sds((3, DFF), F32), _sds((1, DFF), F32)],
        compiler_params=_cparams(("parallel", "arbitrary"), 2 * 10 * tm * tn * 4), name=name,
    )(dgc, dgc, gu, gu, cw)


def _adamw_math(w, g, m, v):
    m = ADAM_B1 * m + (1.0 - ADAM_B1) * g
    v = ADAM_B2 * v + (1.0 - ADAM_B2) * (g * g)
    m_hat = m / (1.0 - ADAM_B1 ** ADAM_STEP)
    v_hat = v / (1.0 - ADAM_B2 ** ADAM_STEP)
    delta = -ADAM_LR * (m_hat / (jnp.sqrt(v_hat) + ADAM_EPS) + ADAM_WD * w)
    return delta, m, v


def _adamw(w, g, m, v, name):
    R, C = g.shape
    tm = R if R * C * 4 <= (1 << 20) else _tile(R, max(8, ((1 << 20) // (C * 4)) // 8 * 8), 8)

    def body(w_ref, g_ref, m_ref, v_ref, d_ref, mo_ref, vo_ref):
        d_ref[...], mo_ref[...], vo_ref[...] = _adamw_math(w_ref[...], g_ref[...], m_ref[...], v_ref[...])

    blk = pl.BlockSpec((tm, C), lambda i: (i, 0))
    wblk = blk if w.ndim == 2 else pl.BlockSpec((None, tm, C), lambda i: (0, i, 0))
    return pl.pallas_call(
        body, grid=(R // tm,), in_specs=[wblk, blk, wblk, wblk], out_specs=[wblk] * 3, out_shape=[_sds(w.shape, F32)] * 3,
        compiler_params=_cparams(("parallel",), 2 * 7 * tm * (C + LANES) * 4), name=name,
    )(w, g, m, v)


def _sum_adamw(parts, w, m, v, name):
    n, R, C = parts.shape
    tm = _tile(R, 256, 8)

    def body(p_ref, w_ref, m_ref, v_ref, g_ref, d_ref, mo_ref, vo_ref):
        g = p_ref[0]
        for k in range(1, n):
            g = g + p_ref[k]
        g_ref[...] = g
        d_ref[...], mo_ref[...], vo_ref[...] = _adamw_math(w_ref[...], g, m_ref[...], v_ref[...])

    blk = pl.BlockSpec((tm, C), lambda i: (i, 0))
    return pl.pallas_call(
        body, grid=(R // tm,), in_specs=[pl.BlockSpec((n, tm, C), lambda i: (0, i, 0))] + [blk] * 3, out_specs=[blk] * 4,
        out_shape=[_sds((R, C), F32)] * 4,
        compiler_params=_cparams(("parallel",), 2 * (n + 7) * tm * C * 4), name=name,
    )(parts, w, m, v)


def _add_half(g, got, c_idx, name):
    n, R, C = g.shape
    HR = R // 2
    tm = _tile(HR, max(8, ((1 << 20) // (C * 4)) // 8 * 8), 8)
    nb = HR // tm

    def body(c_ref, g_ref, t_ref, o_ref):
        o_ref[...] = (g_ref[...] + t_ref[...]).astype(o_ref.dtype)

    grid_spec = pltpu.PrefetchScalarGridSpec(
        num_scalar_prefetch=1, grid=(n, nb),
        in_specs=[pl.BlockSpec((None, tm, C), lambda k, i, c: (k, c[0] * nb + i, 0)),
                  pl.BlockSpec((None, tm, C), lambda k, i, c: (k, i, 0))],
        out_specs=pl.BlockSpec((None, tm, C), lambda k, i, c: (k, i, 0)))
    return pl.pallas_call(
        body, grid_spec=grid_spec, out_shape=_sds((n, HR, C), BF16),
        compiler_params=_cparams(("parallel", "parallel"), 2 * 3 * tm * (C + LANES) * 4), name=name,
    )(c_idx, g, got)


def _sum_half(g, got, land, chip_c, name):
    n, R, C = g.shape
    HR = R // 2
    tm = _tile(HR, max(8, ((1 << 20) // (C * 4)) // 8 * 8), 8)
    nb = HR // tm

    def body(s_ref, g_ref, t_ref, l_ref, o_ref):
        acc = g_ref[...] + t_ref[...]
        for k in range(3):
            acc = acc + l_ref[k].astype(F32)
        o_ref[...] = acc

    grid_spec = pltpu.PrefetchScalarGridSpec(
        num_scalar_prefetch=1, grid=(nb,),
        in_specs=[pl.BlockSpec((None, tm, C), lambda i, sc: (sc[0], sc[1] * nb + i, 0)),
                  pl.BlockSpec((None, tm, C), lambda i, sc: (sc[0], i, 0)),
                  pl.BlockSpec((3, tm, C), lambda i, sc: (0, i, 0))],
        out_specs=pl.BlockSpec((tm, C), lambda i, sc: (sc[1] * nb + i, 0)))
    return pl.pallas_call(
        body, grid_spec=grid_spec, out_shape=_sds((R, C), F32),
        compiler_params=_cparams(("parallel",), 2 * 6 * tm * (C + LANES) * 4), name=name,
    )(chip_c, g, got, land)


def _cast_slot(w, chip_idx, name):
    R, C = w.shape
    tm = _tile(R, max(16, ((1 << 20) // (C * 4)) // 16 * 16), 16)

    def body(s_ref, w_ref, o_ref):
        o_ref[...] = w_ref[...].astype(o_ref.dtype)

    grid_spec = pltpu.PrefetchScalarGridSpec(
        num_scalar_prefetch=1, grid=(R // tm,),
        in_specs=[pl.BlockSpec((tm, C), lambda i, sc: (i, 0))],
        out_specs=pl.BlockSpec((None, tm, C), lambda i, sc: (sc[0], i, 0)))
    return pl.pallas_call(
        body, grid_spec=grid_spec, out_shape=_sds((4, R, C), BF16),
        compiler_params=_cparams(("parallel",), 2 * 2 * tm * (C + LANES) * 4), name=name,
    )(chip_idx, w)


def _gather_chips(gs, name):
    nw = len(gs)

    def body(*refs):
        start, forward, finish = _gather_phases(refs[nw:2 * nw], *refs[2 * nw:])
        start()
        forward()
        finish()

    return pl.pallas_call(
        body, out_shape=[_sds(g.shape, g.dtype) for g in gs], in_specs=[HBM_SPEC] * nw, out_specs=[HBM_SPEC] * nw,
        scratch_shapes=[pltpu.SemaphoreType.DMA((6 * nw,)), pltpu.SemaphoreType.DMA((6 * nw,))],
        input_output_aliases={i: i for i in range(nw)}, name=name,
    )(*gs)


def _sibling_halves(gs, name):
    nw = len(gs)
    halves = [g.shape[1] // 2 for g in gs]
    assert all(h % 8 == 0 for h in halves)

    def body(*refs):
        g_refs, land_refs = refs[:nw], refs[nw:2 * nw]
        send_sems, recv_sems = refs[2 * nw:]
        x, y, c = _place()
        cps = []
        for i in range(nw):
            q0 = pl.multiple_of((1 - c) * halves[i], 8)
            cp = pltpu.make_async_remote_copy(
                src_ref=g_refs[i].at[pl.ds(0, 4), pl.ds(q0, halves[i]), :], dst_ref=land_refs[i],
                send_sem=send_sems.at[i], recv_sem=recv_sems.at[i], device_id=(x, y, 1 - c), device_id_type=MESH)
            cp.start()
            cps.append(cp)
        for cp in cps:
            cp.wait()

    return pl.pallas_call(
        body, out_shape=[_sds((4, h, g.shape[2]), g.dtype) for g, h in zip(gs, halves)],
        in_specs=[HBM_SPEC] * nw, out_specs=[HBM_SPEC] * nw,
        scratch_shapes=[pltpu.SemaphoreType.DMA((nw,)), pltpu.SemaphoreType.DMA((nw,))], name=name,
    )(*gs)


def _scatter_chips(ss, name):
    nw = len(ss)

    def body(*refs):
        start, finish = _scatter_phases(refs[:nw], refs[nw:2 * nw], *refs[2 * nw:])
        start()
        finish()

    return pl.pallas_call(
        body, out_shape=[_sds((3,) + s_.shape[1:], s_.dtype) for s_ in ss], in_specs=[HBM_SPEC] * nw, out_specs=[HBM_SPEC] * nw,
        scratch_shapes=[pltpu.SemaphoreType.DMA((3 * nw,)), pltpu.SemaphoreType.DMA((3 * nw,))], name=name,
    )(*ss)


def _sibling_join(fs, name):
    nw = len(fs)
    assert all(f.shape[0] % 16 == 0 for f in fs)

    def body(*refs):
        o_refs = refs[nw:2 * nw]
        send_sems, recv_sems = refs[2 * nw:]
        x, y, c = _place()

        def copy(i, half):
            HR = o_refs[i].shape[0] // 2
            rows = o_refs[i].at[pl.ds(pl.multiple_of(half * HR, 8), HR), :]
            return pltpu.make_async_remote_copy(
                src_ref=rows, dst_ref=rows, send_sem=send_sems.at[i], recv_sem=recv_sems.at[i],
                device_id=(x, y, 1 - c), device_id_type=MESH)

        sends = [copy(i, c) for i in range(nw)]
        for cp in sends:
            cp.start()
        for i in range(nw):
            copy(i, 1 - c).wait_recv()
        for cp in sends:
            cp.wait_send()

    return pl.pallas_call(
        body, out_shape=[_sds(f.shape, f.dtype) for f in fs], in_specs=[HBM_SPEC] * nw, out_specs=[HBM_SPEC] * nw,
        scratch_shapes=[pltpu.SemaphoreType.DMA((nw,)), pltpu.SemaphoreType.DMA((nw,))],
        input_output_aliases={i: i for i in range(nw)}, name=name,
    )(*fs)


def _gather_all(v, name):
    M, W = v.shape

    def body(v_ref, o_ref, send_sems, recv_sems, local_sem):
        x, y, c = _place()
        me, sibling = (x, y, c), (x, y, 1 - c)
        chips = _other_chips(x, y)

        def slot(px, py, pc):
            return o_ref.at[4 * px + 2 * py + pc]

        def copy(k, block, to, src=None):
            return pltpu.make_async_remote_copy(
                src_ref=slot(*block) if src is None else src, dst_ref=slot(*block),
                send_sem=send_sems.at[k], recv_sem=recv_sems.at[k], device_id=to, device_id_type=MESH)

        mine = pltpu.make_async_copy(v_ref, slot(*me), local_sem)
        mine.start()
        first = [copy(0, me, sibling, src=v_ref)]
        first += [copy(1 + j, me, (*chip, c), src=v_ref) for j, chip in enumerate(chips)]
        for cp in first:
            cp.start()
        passed = [copy(4 + j, (*chip, c), sibling) for j, chip in enumerate(chips)]
        for j, chip in enumerate(chips):
            copy(1 + j, (*chip, c), me).wait_recv()
            passed[j].start()
        copy(0, sibling, me).wait_recv()
        for j, chip in enumerate(chips):
            copy(4 + j, (*chip, 1 - c), me).wait_recv()
        for cp in first + passed:
            cp.wait_send()
        mine.wait()

    vm = pl.BlockSpec(memory_space=pltpu.VMEM)
    return pl.pallas_call(
        body, out_shape=_sds((8, M, W), v.dtype), in_specs=[vm], out_specs=vm,
        scratch_shapes=[pltpu.SemaphoreType.DMA((7,)), pltpu.SemaphoreType.DMA((7,)), pltpu.SemaphoreType.DMA(())],
        compiler_params=pltpu.CompilerParams(vmem_limit_bytes=int(min(10 * M * W * 4 + (8 << 20), V7X_VMEM_BYTES - (8 << 20)))),
        name=name,
    )(v)


def _rows_for(n_elems, width, mult=8):
    rows = -(-n_elems // width)
    return -(-rows // mult) * mult


def _pack_small(arrs, total_rows):
    parts = []
    used = 0
    for a in arrs:
        rows = _rows_for(a.size, LANES)
        parts.append(jnp.pad(a.reshape(-1), (0, rows * LANES - a.size)).reshape(rows, LANES))
        used += rows
    if total_rows > used:
        parts.append(jnp.zeros((total_rows - used, LANES), F32))
    return jnp.concatenate(parts, axis=0)


def _unpack_small(p, shapes):
    outs, r = [], 0
    lead = p.shape[:-2]
    for shp in shapes:
        n = int(np.prod(shp))
        rows = _rows_for(n, LANES)
        outs.append(p[..., r:r + rows, :].reshape(lead + (rows * LANES,))[..., :n].reshape(lead + tuple(shp)))
        r += rows
    return outs


def _cols_to_chips(w):
    K, N = w.shape
    return w.reshape(K, 4, N // 4).transpose(1, 0, 2)


def _chips_to_cols(w):
    n4, K, n = w.shape
    return w.transpose(1, 0, 2).reshape(K, n4 * n)


def _block_diag(m, gpb):
    G, A, B = m.shape
    nb = G // gpb
    eye = jnp.eye(gpb, dtype=m.dtype)
    t = m.reshape(nb, gpb, A, B)[:, :, :, None, :] * eye[None, :, None, :, None]
    return t.reshape(nb, gpb * A, gpb * B)


def _block_diag_extract(m, gpb, A, B):
    nb = m.shape[0]
    t = m.reshape(nb, gpb, A, gpb, B)
    eye = jnp.eye(gpb, dtype=m.dtype)
    d = jnp.sum(t * eye[None, :, None, :, None], axis=3)
    return d.reshape(nb * gpb, A, B)


def kernel(x, meta, g_mix, w_in, b_f, lam_re, lam_im, log_dt, b_re, b_im, c_re, c_im, d_skip, w_glu, w_attn_o, w_out, g_ffn, w_up, conv_w, conv_b, w_down, g_final, loss_target, m_meta, m_g_mix, m_w_in, m_b_f, m_lam_re, m_lam_im, m_log_dt, m_b_re, m_b_im, m_c_re, m_c_im, m_d_skip, m_w_glu, m_w_attn_o, m_w_out, m_g_ffn, m_w_up, m_conv_w, m_conv_b, m_w_down, m_g_final, v_meta, v_g_mix, v_w_in, v_b_f, v_lam_re, v_lam_im, v_log_dt, v_b_re, v_b_im, v_c_re, v_c_im, v_d_skip, v_w_glu, v_w_attn_o, v_w_out, v_g_ffn, v_w_up, v_conv_w, v_conv_b, v_w_down, v_g_final):
    args = dict(locals())
    L, D = x.shape[1], x.shape[2]
    NM = meta.shape[0]
    H = b_f.shape[1]
    DA = H * HEAD_DIM
    G, P, C = b_re.shape[1:]
    DS, GP = G * C, G * P
    DFF = conv_b.shape[1]
    PAD = (-NM) % LANES
    OFF = PAD + NM
    LP = OFF + L
    NZ = 3 * DA + DS + 2 * D
    U0, GA0, GB0 = 3 * DA, 3 * DA + DS, 3 * DA + DS + D
    NB = G // GROUPS_PER_BLOCK
    chip = 2 * lax.axis_index("x") + lax.axis_index("y")
    core = lax.axis_index("c")

    big = ["w_in", "w_glu", "w_attn_o", "w_out", "w_up", "w_down"]
    local = {n: args[n][0] for n in big}
    chip_idx = chip.reshape(1).astype(jnp.int32)
    slots = {n: _cast_slot(local[n], chip_idx, "cast_" + n) for n in big}
    gathered = {"w_in": _gather_chips([slots["w_in"]], "gather_w_in")[0]}
    tiny_shapes = [conv_w.shape[1:], meta.shape]
    tiny_rows = sum(_rows_for(int(np.prod(sh)), LANES) for sh in tiny_shapes)
    tiny = _gather_all(_pack_small([conv_w[0], meta], tiny_rows), "gather_small_weights")[0::2]
    conv_w_c, meta_c = _unpack_small(tiny, tiny_shapes)
    conv_w_f = _chips_to_cols(conv_w_c)
    meta_full = _chips_to_cols(meta_c)
    w_in_f = _chips_to_cols(gathered["w_in"])
    w_f = jnp.pad(w_in_f[:, 3 * DA:3 * DA + H], ((0, 0), (0, LANES - H)))
    w_zf = jnp.concatenate([w_in_f[:, :3 * DA], w_in_f[:, 3 * DA + H:], w_f], axis=1)
    N_GLU, N_AO, N_UP = (slots[n].shape[2] for n in ("w_glu", "w_attn_o", "w_up"))

    col = lambda a: a.reshape(GP, 1)
    lr_c, li_c = col(lam_re[0]), col(lam_im[0])
    ldt_c = jnp.repeat(log_dt[0], P).reshape(GP, 1)
    br2, bi2 = b_re[0].reshape(GP, C), b_im[0].reshape(GP, C)
    a_re, a_im, bb_re, bb_im, pw_re, pw_im = _ssm_prep(lr_c, li_c, ldt_c, br2, bi2, "ssm_prep")
    S = GROUPS_PER_BLOCK * P
    CB = GROUPS_PER_BLOCK * C
    pw_r = pw_re.T.reshape(8, NB, S).transpose(1, 0, 2)
    pw_i = pw_im.T.reshape(8, NB, S).transpose(1, 0, 2)
    row8 = jnp.arange(8)[None, :, None]

    def masked_power(pw, k, keep):
        return jnp.where(keep, pw[:, k - 1][:, None, :], 0.0)

    coef = jnp.stack(
        [masked_power(pw, k, row8 >= k) for k in (1, 2, 4) for pw in (pw_r, pw_i)] + [pw_r, pw_i], axis=1)
    coef_rev = jnp.stack(
        [masked_power(pw, k, row8 < 8 - k) for k in (1, 2, 4) for pw in (pw_r, -pw_i)]
        + [pw_r[:, ::-1], -pw_i[:, ::-1]], axis=1)
    bd = lambda m: _block_diag(m, GROUPS_PER_BLOCK)
    bbr3, bbi3 = bb_re.reshape(G, P, C), bb_im.reshape(G, P, C)
    bbr_cs = bd(bbr3.transpose(0, 2, 1)).astype(BF16)
    bbi_cs = bd(bbi3.transpose(0, 2, 1)).astype(BF16)
    bbr_sc = bd(bbr3).astype(BF16)
    bbi_sc = bd(bbi3).astype(BF16)
    ccr_sc = bd(c_re[0].transpose(0, 2, 1)).astype(BF16)
    cci_sc = bd(c_im[0].transpose(0, 2, 1)).astype(BF16)
    ccr_cs = bd(c_re[0]).astype(BF16)
    cci_cs = bd(c_im[0]).astype(BF16)

    h0 = jnp.concatenate([jnp.zeros((PAD, D), F32), meta_full, x[0]], axis=0)
    n1 = _rms_fwd(h0, g_mix, "rms_mix")
    z = _mm(n1, w_zf, "nn", LP, NZ, D, BF16, "in_proj")
    fpre = _mm(n1, w_zf, "nn", LP, LANES, D, F32, "in_proj_f", b_off=(0, NZ))
    bf_pad = jnp.pad(b_f, ((0, 0), (0, LANES - H)))
    fcum = _fgate_fwd(fpre, bf_pad, PAD, "fgate_fwd")
    key_bias = jnp.where(jnp.arange(LP)[:, None] >= PAD, -fcum, NEG)
    bias_t = key_bias.T[:H].reshape(H, 1, LP)
    attn, attn_f32, lse_t, *rest = _attn_fwd(z, bias_t, H, PAD, "attn_fwd", gather=[slots[n] for n in big[1:]])
    gathered.update(zip(big[1:], rest))
    w_glu_c, w_ao_c, w_up_c = gathered["w_glu"], gathered["w_attn_o"], gathered["w_up"]
    w_out_f = gathered["w_out"].reshape(D, D)
    w_down_f = gathered["w_down"].reshape(DFF, D)
    ao = _mm(attn, w_ao_c, "nn", LP, D, DA, BF16, "attn_out", b_chips=N_AO)
    y, yg, hs_re, hs_im = _ssm_fwd(z, U0, coef, bbr_cs, bbi_cs, ccr_sc, cci_sc, d_skip, "ssm_fwd")
    yab = _mm(yg, w_glu_c, "nn", LP, 2 * D, DS, BF16, "glu_proj", b_chips=N_GLU)
    merged = _merge_fwd(yab, z, ao, D, GA0, GB0, "merge_fwd")
    h1 = _mm(merged, w_out_f, "nn", LP, D, D, F32, "out_proj", res=h0)
    n2 = _rms_fwd(h1, g_ffn, "rms_ffn")
    gu = _mm(n2, w_up_c, "nn", LP, 2 * DFF, D, BF16, "up_proj", tn=1408, b_chips=N_UP)
    act = _convact_fwd(gu, conv_w_f, conv_b, DFF, "convact_fwd")
    h2 = _mm(act, w_down_f, "nn", LP, D, DFF, F32, "down_proj", res=h1, tn=512, tk=DFF)
    dh2, dg_final, loss_v = _final_loss(h2, g_final.reshape(1, D), loss_target[0], OFF, "final_loss")
    loss = lax.psum(loss_v[0, 0], ("x", "y", "c"))

    KW = dict(tm=512, tn=512, tk=LP)
    dact = _mm(dh2, w_down_f, "nt", LP, DFF, D, BF16, "down_bwd_x")
    dw_down = _mm(act, dh2, "tn", DFF, D, LP, F32, "down_bwd_w", **KW)
    dgc, du_ffn = _convact_bwd(dact, gu, conv_w_f, conv_b, DFF, "convact_bwd")
    dg_ffn_in, dconv_w, dconv_b = _conv_bwd(dgc, gu, conv_w_f, DFF, PAD, "conv_bwd")
    dn2 = _mm(dg_ffn_in, w_up_c, "nt", LP, D, DFF, F32, "up_bwd_x_g", tn=512, tk=N_UP, b_chips=N_UP)
    dn2 = _mm(du_ffn, w_up_c, "nt", LP, D, DFF, F32, "up_bwd_x_u", res=dn2, b_off=(0, DFF), tn=512, tk=N_UP, b_chips=N_UP)
    dw_up = _mm(n2, dg_ffn_in, "tn", D, DFF, LP, F32, "up_bwd_w_g", tm=512, tn=256, tk=LP, out_chips=N_UP,
                out_into=(jnp.zeros((4, D, N_UP), F32), 0))
    dw_up = _mm(n2, du_ffn, "tn", D, DFF, LP, F32, "up_bwd_w_u", tm=512, tn=256, tk=LP, out_chips=N_UP,
                out_into=(dw_up, DFF // N_UP))
    dh1, dg_ffn = _rms_bwd(h1, g_ffn, dn2, dh2, "rms_ffn_bwd")
    c_idx = core.reshape(1).astype(jnp.int32)
    chip_c = jnp.stack([chip, core]).astype(jnp.int32)

    dmerged = _mm(dh1, w_out_f, "nt", LP, D, D, F32, "out_bwd_x")
    dw_out = _mm(merged, dh1, "tn", D, D, LP, F32, "out_bwd_w", **KW)
    dya, dyb, dga, dgb, dao = _merge_bwd(dmerged, yab, z, ao, D, GA0, GB0, "merge_bwd")
    dattn = _mm(dao, w_ao_c, "nt", LP, DA, D, BF16, "attn_out_bwd_x", b_chips=N_AO)
    dw_ao = _mm(attn, dao, "tn", DA, D, LP, F32, "attn_out_bwd_w", out_chips=N_AO, **KW)
    dyg = _mm(dya, w_glu_c, "nt", LP, DS, D, F32, "glu_bwd_x_a", b_chips=N_GLU)
    dyg = _mm(dyb, w_glu_c, "nt", LP, DS, D, F32, "glu_bwd_x_b", res=dyg, b_off=(0, D), b_chips=N_GLU)
    dw_glu = _mm(yg, dya, "tn", DS, D, LP, F32, "glu_bwd_w_a", out_chips=N_GLU,
                 out_into=(jnp.zeros((4, DS, N_GLU), F32), 0), **KW)
    dw_glu = _mm(yg, dyb, "tn", DS, D, LP, F32, "glu_bwd_w_b", out_chips=N_GLU, out_into=(dw_glu, D // N_GLU), **KW)
    (du_ssm, dbbr_d, dbbi_d, dccr_d, dcci_d, dar_b, dai_b, dd_skip) = _ssm_bwd(
        z, U0, dyg, y, hs_re, hs_im, coef_rev, bbr_sc, bbi_sc, ccr_cs, cci_cs, d_skip, "ssm_bwd")
    delta_t = _attn_delta(dattn, attn_f32, H, "attn_delta")
    early = ["w_glu", "w_attn_o", "w_out", "w_up", "w_down"]
    early_grads = [dw_glu, dw_ao, dw_out.reshape(4, D // 4, D), dw_up, dw_down.reshape(4, DFF // 4, D)]
    early_got = _sibling_halves(early_grads, "rs_sibling_early")
    early_part = [_add_half(g, t, c_idx, "rs_add_" + n) for n, g, t in zip(early, early_grads, early_got)]
    dq, dk, dv, dbias_t, *early_land = _attn_bwd(z, dattn, lse_t, delta_t, bias_t, H, "attn_bwd", scatter=early_part)
    dF = jnp.pad(-dbias_t[:, 0, :].T, ((0, 0), (0, LANES - H)))
    dfpre, db_f = _fgate_bwd(dF, fpre, bf_pad, PAD, "fgate_bwd")
    dz = jnp.concatenate([dq, dk, dv, du_ssm, dga, dgb, dfpre.astype(BF16)], axis=1)
    dn1 = _mm(dz, w_zf, "nt", LP, D, NZ + LANES, F32, "in_bwd_x", tm=640, tn=256, tk=NZ + LANES)
    dw_zf = _mm(n1, dz, "tn", D, NZ + LANES, LP, F32, "in_bwd_w", tm=512, tn=640, tk=LP)
    dh0, dg_mix = _rms_bwd(h0, g_mix, dn1, dh1, "rms_mix_bwd")
    grad_x = dh0[OFF:][None]
    dmeta_full = dh0[PAD:OFF]

    ext = lambda m, A, B: _block_diag_extract(m, GROUPS_PER_BLOCK, A, B)
    dbb_re = ext(dbbr_d, C, P).transpose(0, 2, 1).reshape(GP, C)
    dbb_im = ext(dbbi_d, C, P).transpose(0, 2, 1).reshape(GP, C)
    dc_re = ext(dccr_d, P, C).transpose(0, 2, 1)[None]
    dc_im = ext(dcci_d, P, C).transpose(0, 2, 1)[None]
    glr, gli, gldt, gbr, gbi = _ssm_prep_bwd(lr_c, li_c, ldt_c, br2, bi2, dar_b.reshape(GP, 1), dai_b.reshape(GP, 1),
                                             dbb_re, dbb_im, "ssm_prep_bwd")
    small_grads = {
        "g_mix": dg_mix, "b_f": db_f[:, :H], "lam_re": glr.reshape(1, G, P), "lam_im": gli.reshape(1, G, P),
        "log_dt": gldt.reshape(G, P).sum(axis=1)[None], "b_re": gbr.reshape(1, G, P, C), "b_im": gbi.reshape(1, G, P, C),
        "c_re": dc_re, "c_im": dc_im, "d_skip": dd_skip, "g_ffn": dg_ffn, "conv_b": dconv_b, "g_final": dg_final.reshape(D),
    }

    small = list(small_grads)
    rider_grads = [dconv_w, dmeta_full]
    small_shapes = [args[n].shape for n in small] + [g.shape for g in rider_grads]
    srows = sum(_rows_for(int(np.prod(sh)), LANES) for sh in small_shapes)
    srows = -(-srows // 256) * 256
    zeros_like_riders = [jnp.zeros(g.shape, F32) for g in rider_grads]
    pack = lambda arrs: _pack_small(arrs, srows)
    g_parts = _gather_all(pack([small_grads[n] for n in small] + rider_grads), "gather_small_grads")
    sm = _sum_adamw(g_parts, pack([args[n] for n in small] + zeros_like_riders),
                    pack([args["m_" + n] for n in small] + zeros_like_riders),
                    pack([args["v_" + n] for n in small] + zeros_like_riders), "small_adamw")
    unpacked = [_unpack_small(p, small_shapes) for p in sm]
    sg, sd, smm, svv = (dict(zip(small, u[:len(small)])) for u in unpacked)
    dconv_w_sum, dmeta_sum = unpacked[0][len(small):]
    n_cw, n_me = conv_w.shape[2], meta.shape[1]
    rider = {"conv_w": lax.dynamic_slice_in_dim(dconv_w_sum, chip * n_cw, n_cw, axis=1)[None],
             "meta": lax.dynamic_slice_in_dim(dmeta_sum, chip * n_me, n_me, axis=1)}

    dw_in_f = jnp.concatenate([dw_zf[:, :3 * DA], dw_zf[:, NZ:NZ + H], dw_zf[:, 3 * DA:NZ]], axis=1)
    late_grads = [_cols_to_chips(dw_in_f)]
    late_got = _sibling_halves(late_grads, "rs_sibling_w_in")
    late_part = [_add_half(late_grads[0], late_got[0], c_idx, "rs_add_w_in")]
    late_land = _scatter_chips(late_part, "rs_scatter_w_in")
    big = ["w_in"] + early
    halves = [_sum_half(g, t, l_, chip_c, "rs_sum_" + n) for n, g, t, l_ in
              zip(big, late_grads + early_grads, list(late_got) + list(early_got), list(late_land) + list(early_land))]
    shard_grads = dict(zip(big, _sibling_join(halves, "rs_join")))
    shard_grads.update({n: g.reshape(g.shape[-2:]) for n, g in rider.items()})
    bg, bd_, bm, bv = {}, {}, {}, {}
    for n, g in shard_grads.items():
        bd_[n], bm[n], bv[n] = _adamw(args[n], g, args["m_" + n], args["v_" + n], "adamw_" + n)
        bg[n] = g.reshape(args[n].shape)

    order = ["meta", "g_mix", "w_in", "b_f", "lam_re", "lam_im", "log_dt", "b_re", "b_im", "c_re", "c_im", "d_skip",
             "w_glu", "w_attn_o", "w_out", "g_ffn", "w_up", "conv_w", "conv_b", "w_down", "g_final"]
    pick = lambda bigd, smalld, n: bigd[n] if n in bigd else smalld[n]
    outs = [loss, grad_x]
    for bigd, smalld in ((bg, sg), (bd_, sd), (bm, smm), (bv, svv)):
        outs += [pick(bigd, smalld, n) for n in order]
    return tuple(outs)
```

```python
import functools
import math

import jax
import jax.numpy as jnp
import numpy as np
from jax import lax
from jax.experimental import pallas as pl
from jax.experimental.pallas import tpu as pltpu

F32 = jnp.float32
BF16 = jnp.bfloat16
MESH = pl.DeviceIdType.MESH

EPS = 1e-6
HEAD_DIM = 128
LANES = 128
NEG = -1e30
GELU_C = math.sqrt(2.0 / math.pi)
GELU_A = 0.044715
ADAM_LR, ADAM_B1, ADAM_B2, ADAM_EPS, ADAM_WD, ADAM_STEP = 0.001, 0.9, 0.999, 1e-08, 0.01, 10
V7X_VMEM_BYTES = 64 << 20
GROUPS_PER_BLOCK = 8


def _tile(n, pref, mult=LANES):
    if n <= pref:
        return n
    t = (pref // mult) * mult
    while t >= mult:
        if n % t == 0:
            return t
        t -= mult
    raise ValueError(f"no tile for {n} <= {pref} (multiple of {mult})")


def _ctile(pref, *vals):
    g = 0
    for v in vals:
        g = math.gcd(g, v)
    return _tile(g, pref)


def _cparams(sem, est_bytes):
    limit = int(min(max(est_bytes * 1.25 + (4 << 20), 16 << 20), V7X_VMEM_BYTES - (8 << 20)))
    return pltpu.CompilerParams(dimension_semantics=sem, vmem_limit_bytes=limit)


def _sds(shape, dtype):
    return jax.ShapeDtypeStruct(tuple(shape), dtype)


def _sig(x):
    return 0.5 * jnp.tanh(0.5 * x) + 0.5


def _sig_tail(x):
    return 1.0 / (1.0 + jnp.exp(-x))


def _gelu(x):
    t = jnp.tanh(GELU_C * (x + GELU_A * x * x * x))
    return 0.5 * x * (1.0 + t)


def _gelu_grad(x):
    t = jnp.tanh(GELU_C * (x + GELU_A * x * x * x))
    return 0.5 * (1.0 + t) + 0.5 * x * (1.0 - t * t) * GELU_C * (1.0 + 3.0 * GELU_A * x * x)


def _mm(a, b, mode, M, N, K, out_dtype, name, *, res=None, a_off=(0, 0), b_off=(0, 0),
        tm=640, tn=1024, tk=2048, b_chips=None, out_chips=None, out_into=None, ride=None):
    tm, tn, tk = _tile(M, tm, 8 if mode != "tn" else LANES), _tile(N, tn), _tile(K, tk, LANES if mode != "tn" else 8)
    if b_chips is not None and mode == "nt":
        tk = _ctile(tk, tk, b_chips)
    if b_chips is not None and mode != "nt":
        tn = _ctile(tn, tn, b_chips)
    if out_chips is not None:
        tn = _ctile(tn, tn, out_chips)
    nk = K // tk
    ar, ac = a_off
    br, bc = b_off
    if mode == "tn":
        assert ar % tk == 0 and ac % tm == 0
        a_spec = pl.BlockSpec((tk, tm), lambda i, j, k: (k + ar // tk, i + ac // tm))
        a_dims = 0
    else:
        assert ar % tm == 0 and ac % tk == 0
        a_spec = pl.BlockSpec((tm, tk), lambda i, j, k: (i + ar // tm, k + ac // tk))
        a_dims = 1
    if mode == "nt":
        assert br % tn == 0 and bc % tk == 0
        if b_chips is None:
            b_spec = pl.BlockSpec((tn, tk), lambda i, j, k: (j + br // tn, k + bc // tk))
        else:
            per = b_chips // tk
            b_spec = pl.BlockSpec((None, tn, tk), lambda i, j, k: ((k + bc // tk) // per, j + br // tn, (k + bc // tk) % per))
        b_dims = 1
    else:
        assert br % tk == 0 and bc % tn == 0
        if b_chips is None:
            b_spec = pl.BlockSpec((tk, tn), lambda i, j, k: (k + br // tk, j + bc // tn))
        else:
            per = b_chips // tn
            b_spec = pl.BlockSpec((None, tk, tn), lambda i, j, k: ((j + bc // tn) // per, k + br // tk, (j + bc // tn) % per))
        b_dims = 0
    dims = (((a_dims,), (b_dims,)), ((), ()))
    if out_chips is None:
        o_spec = pl.BlockSpec((tm, tn), lambda i, j, k: (i, j))
        o_shape = _sds((M, N), out_dtype)
    else:
        per_o = out_chips // tn
        chip0 = 0 if out_into is None else out_into[1]
        o_spec = pl.BlockSpec((None, tm, tn), lambda i, j, k: (chip0 + j // per_o, i, j % per_o))
        o_shape = _sds((N // out_chips if out_into is None else 4, M, out_chips), out_dtype)
    has_res = res is not None
    has_into = out_into is not None

    r_ins, r_outs, r_sems = _ride_parts(ride)
    n_in = 2 + has_res + has_into
    steps = (M // tm, N // tn, nk)

    def body(*refs):
        a_ref, b_ref = refs[:2]
        r_ref = refs[2] if has_res else None
        o_ref = refs[n_in + len(r_ins)]
        if ride is not None:
            start, finish = ride["fn"](refs[n_in:n_in + len(r_ins)],
                                       refs[n_in + len(r_ins) + 1:n_in + len(r_ins) + 1 + len(r_outs)], *refs[-2:])
            pid = [pl.program_id(d) for d in range(3)]
            pl.when((pid[0] == 0) & (pid[1] == 0) & (pid[2] == 0))(start)
        part = lax.dot_general(a_ref[...].astype(BF16), b_ref[...].astype(BF16), dims, preferred_element_type=F32)

        def write_out(acc):
            if has_res:
                acc = r_ref[...] + acc
            o_ref[...] = acc.astype(o_ref.dtype)

        if nk == 1:
            write_out(part)
        else:
            acc_ref = refs[n_in + len(r_ins) + 1 + len(r_outs)]
            k = pl.program_id(2)

            @pl.when(k == 0)
            def _():
                acc_ref[...] = part

            @pl.when(k > 0)
            def _():
                acc_ref[...] += part

            @pl.when(k == nk - 1)
            def _():
                write_out(acc_ref[...])

        if ride is not None:
            pl.when((pid[0] == steps[0] - 1) & (pid[1] == steps[1] - 1) & (pid[2] == steps[2] - 1))(finish)

    in_specs = ([a_spec, b_spec] + ([o_spec] if has_res else []) + ([pl.BlockSpec(memory_space=pl.ANY)] if has_into else [])
                + [HBM_SPEC] * len(r_ins))
    args = (a, b) + ((res,) if has_res else ()) + ((out_into[0],) if has_into else ()) + tuple(r_ins)
    isz = lambda x: jnp.dtype(x.dtype).itemsize
    est = 2 * (tm * tk * isz(a) + tk * tn * isz(b) + tm * tn * jnp.dtype(out_dtype).itemsize) + tm * tn * 4 * (2 + 2 * has_res)
    sem = ("parallel", "parallel", "arbitrary") if ride is None else ("arbitrary",) * 3
    out = pl.pallas_call(
        body, grid=steps, in_specs=in_specs, out_specs=[o_spec] + [HBM_SPEC] * len(r_outs),
        out_shape=[o_shape] + r_outs,
        scratch_shapes=([pltpu.VMEM((tm, tn), F32)] if nk > 1 else []) + r_sems,
        input_output_aliases={2: 0} if has_into else {},
        compiler_params=_cparams(sem, est), name=name,
    )(*args)
    return out[0] if ride is None else out


def _rms_fwd(h, g, name):
    LP, D = h.shape
    tm = _tile(LP, 640, 8)

    def body(h_ref, g_ref, o_ref):
        x = h_ref[...]
        r = lax.rsqrt(jnp.mean(x * x, axis=-1, keepdims=True) + EPS)
        o_ref[...] = (x * r * g_ref[...]).astype(o_ref.dtype)

    row = pl.BlockSpec((tm, D), lambda i: (i, 0))
    return pl.pallas_call(
        body, grid=(LP // tm,), in_specs=[row, pl.BlockSpec((1, D), lambda i: (0, 0))], out_specs=row,
        out_shape=_sds((LP, D), BF16), compiler_params=_cparams(("parallel",), 2 * tm * D * 6), name=name,
    )(h, g)


def _rms_bwd(h, g, dn, dres, name):
    LP, D = h.shape
    tm = _tile(LP, 320, 8)
    nt = LP // tm

    def body(h_ref, g_ref, dn_ref, dres_ref, dh_ref, dg_ref):
        i = pl.program_id(0)
        x = h_ref[...]
        r = lax.rsqrt(jnp.mean(x * x, axis=-1, keepdims=True) + EPS)
        xh = x * r
        dn_v = dn_ref[...]
        dxh = dn_v * g_ref[...]
        dh_ref[...] = dres_ref[...] + r * (dxh - xh * jnp.mean(dxh * xh, axis=-1, keepdims=True))
        part = jnp.sum(dn_v * xh, axis=0, keepdims=True)

        @pl.when(i == 0)
        def _():
            dg_ref[...] = part

        @pl.when(i > 0)
        def _():
            dg_ref[...] += part

    row = pl.BlockSpec((tm, D), lambda i: (i, 0))
    vec = pl.BlockSpec((1, D), lambda i: (0, 0))
    return pl.pallas_call(
        body, grid=(nt,), in_specs=[row, vec, row, row], out_specs=[row, vec],
        out_shape=[_sds((LP, D), F32), _sds((1, D), F32)],
        compiler_params=_cparams(("arbitrary",), 2 * 4 * tm * D * 4), name=name,
    )(h, g, dn, dres)


def _final_loss(h, g, tgt, off, name):
    LP, D = h.shape
    tm = LANES
    assert off % tm == 0
    ob = off // tm
    nt = LP // tm

    def body(h_ref, g_ref, t_ref, dh_ref, dg_ref, loss_ref):
        i = pl.program_id(0)
        x = h_ref[...]
        r = lax.rsqrt(jnp.mean(x * x, axis=-1, keepdims=True) + EPS)
        xh = x * r
        gv = g_ref[...]
        e = xh * gv - t_ref[...]
        valid = i >= ob
        dy = jnp.where(valid, e * (1.0 / D), 0.0)
        lpart = jnp.where(valid, 0.5 * jnp.sum(jnp.mean(e * e, axis=-1, keepdims=True), axis=0, keepdims=True), 0.0)
        dxh = dy * gv
        dh_ref[...] = r * (dxh - xh * jnp.mean(dxh * xh, axis=-1, keepdims=True))
        gpart = jnp.sum(dy * xh, axis=0, keepdims=True)

        @pl.when(i == 0)
        def _():
            dg_ref[...] = gpart
            loss_ref[...] = jnp.broadcast_to(lpart, loss_ref.shape)

        @pl.when(i > 0)
        def _():
            dg_ref[...] += gpart
            loss_ref[...] += jnp.broadcast_to(lpart, loss_ref.shape)

    row = pl.BlockSpec((tm, D), lambda i: (i, 0))
    vec = pl.BlockSpec((1, D), lambda i: (0, 0))
    return pl.pallas_call(
        body, grid=(nt,),
        in_specs=[row, vec, pl.BlockSpec((tm, D), lambda i: (jnp.maximum(i - ob, 0), 0))],
        out_specs=[row, vec, pl.BlockSpec((1, LANES), lambda i: (0, 0))],
        out_shape=[_sds((LP, D), F32), _sds((1, D), F32), _sds((1, LANES), F32)],
        compiler_params=_cparams(("arbitrary",), 2 * 3 * tm * D * 4), name=name,
    )(h, g, tgt)


def _fgate_fwd(fpre, bias, pad, name):
    LP, W = fpre.shape

    def body(f_ref, b_ref, o_ref):
        row8 = lax.broadcasted_iota(jnp.int32, (8, W), 0)
        bv = b_ref[...]

        def step(g, carry):
            r0 = pl.multiple_of(g * 8, 8)
            x = f_ref[pl.ds(r0, 8), :] + bv
            lf = jnp.minimum(x, 0.0) - jnp.log(1.0 + jnp.exp(-jnp.abs(x)))
            lf = jnp.where(r0 + row8 >= pad, lf, 0.0)
            for k in (1, 2, 4):
                lf = lf + jnp.where(row8 >= k, pltpu.roll(lf, k, 0), 0.0)
            lf = lf + carry
            o_ref[pl.ds(r0, 8), :] = lf
            return jnp.broadcast_to(lf[7:8, :], (8, W))

        lax.fori_loop(0, LP // 8, step, jnp.zeros((8, W), F32))

    return pl.pallas_call(
        body, out_shape=_sds((LP, W), F32),
        compiler_params=_cparams(None, 3 * LP * W * 4), name=name,
    )(fpre, bias)


def _fgate_bwd(dF, fpre, bias, pad, name):
    LP, W = fpre.shape
    ng = LP // 8

    def body(d_ref, f_ref, b_ref, o_ref, db_ref):
        row8 = lax.broadcasted_iota(jnp.int32, (8, W), 0)
        bv = b_ref[...]

        def step(t, carry):
            run, acc = carry
            g = ng - 1 - t
            r0 = pl.multiple_of(g * 8, 8)
            x = d_ref[pl.ds(r0, 8), :]
            for k in (1, 2, 4):
                x = x + jnp.where(row8 < 8 - k, pltpu.roll(x, 8 - k, 0), 0.0)
            x = x + run
            df = x * _sig_tail(-(f_ref[pl.ds(r0, 8), :] + bv))
            df = jnp.where(r0 + row8 >= pad, df, 0.0)
            o_ref[pl.ds(r0, 8), :] = df
            return jnp.broadcast_to(x[0:1, :], (8, W)), acc + df

        _, acc = lax.fori_loop(0, ng, step, (jnp.zeros((8, W), F32), jnp.zeros((8, W), F32)))
        db_ref[...] = jnp.sum(acc, axis=0, keepdims=True)

    return pl.pallas_call(
        body, out_shape=[_sds((LP, W), F32), _sds((1, W), F32)],
        compiler_params=_cparams(None, 4 * LP * W * 4), name=name,
    )(dF, fpre, bias)


def _place():
    return lax.axis_index("x"), lax.axis_index("y"), lax.axis_index("c")


def _other_chips(x, y):
    return [(1 - x, y), (x, 1 - y), (1 - x, 1 - y)]


def _gather_phases(g_refs, send_sems, recv_sems):
    nw = len(g_refs)
    x, y, c = _place()
    chips = _other_chips(x, y)
    me = 2 * x + y

    def copy(i, k, chip, half, to):
        HR = g_refs[i].shape[1] // 2
        rows = g_refs[i].at[chip, pl.ds(pl.multiple_of(half * HR, 16), HR), :]
        return pltpu.make_async_remote_copy(
            src_ref=rows, dst_ref=rows, send_sem=send_sems.at[6 * i + k], recv_sem=recv_sems.at[6 * i + k],
            device_id=to, device_id_type=MESH)

    pairs = [(i, k, cx, cy) for i in range(nw) for k, (cx, cy) in enumerate(chips)]

    def start():
        for i, k, cx, cy in pairs:
            copy(i, k, me, c, (cx, cy, c)).start()

    def forward():
        for i, k, cx, cy in pairs:
            copy(i, k, 2 * cx + cy, c, (cx, cy, c)).wait_recv()
            copy(i, 3 + k, 2 * cx + cy, c, (x, y, 1 - c)).start()

    def finish():
        for i, k, cx, cy in pairs:
            copy(i, 3 + k, 2 * cx + cy, 1 - c, (x, y, 1 - c)).wait_recv()
        for i, k, cx, cy in pairs:
            copy(i, k, me, c, (cx, cy, c)).wait_send()
            copy(i, 3 + k, 2 * cx + cy, c, (x, y, 1 - c)).wait_send()

    return start, forward, finish


def _scatter_phases(s_refs, land_refs, send_sems, recv_sems):
    x, y, c = _place()
    chips = _other_chips(x, y)

    def copy(i, k, cx, cy):
        return pltpu.make_async_remote_copy(
            src_ref=s_refs[i].at[2 * cx + cy], dst_ref=land_refs[i].at[k],
            send_sem=send_sems.at[3 * i + k], recv_sem=recv_sems.at[3 * i + k],
            device_id=(cx, cy, c), device_id_type=MESH)

    pairs = [(i, k, cx, cy) for i in range(len(s_refs)) for k, (cx, cy) in enumerate(chips)]

    def start():
        for p in pairs:
            copy(*p).start()

    def finish():
        for p in pairs:
            copy(*p).wait_recv()
        for p in pairs:
            copy(*p).wait_send()

    return start, finish


def _sibling_phases(g_refs, land_refs, send_sems, recv_sems):
    x, y, c = _place()

    def copy(i):
        HR = land_refs[i].shape[1]
        q0 = pl.multiple_of((1 - c) * HR, 8)
        return pltpu.make_async_remote_copy(
            src_ref=g_refs[i].at[pl.ds(0, 4), pl.ds(q0, HR), :], dst_ref=land_refs[i],
            send_sem=send_sems.at[i], recv_sem=recv_sems.at[i], device_id=(x, y, 1 - c), device_id_type=MESH)

    def start():
        for i in range(len(g_refs)):
            copy(i).start()

    def finish():
        for i in range(len(g_refs)):
            copy(i).wait()

    return start, finish


def _sibling_ride(gs):
    return dict(fn=_sibling_phases, ins=list(gs), outs=[_sds((4, g.shape[1] // 2, g.shape[2]), g.dtype) for g in gs],
                sems=len(gs))


def _scatter_ride(ss):
    return dict(fn=_scatter_phases, ins=list(ss), outs=[_sds((3,) + p.shape[1:], p.dtype) for p in ss], sems=3 * len(ss))


def _ride_parts(ride):
    if ride is None:
        return [], [], []
    return ride["ins"], ride["outs"], [pltpu.SemaphoreType.DMA((ride["sems"],)), pltpu.SemaphoreType.DMA((ride["sems"],))]


HBM_SPEC = pl.BlockSpec(memory_space=pltpu.HBM)


def _col_to_row(col):
    n = col.shape[0]
    return jnp.transpose(jnp.broadcast_to(col, (n, LANES)))[0:1, :]


def _row_to_col(row):
    n = row.shape[1]
    return jnp.transpose(jnp.broadcast_to(row, (LANES, n)))[:, 0:1]


def _attn_fwd(z, bias_t, H, pad, name, gather=()):
    LP = z.shape[0]
    BQ = BK = _tile(LP, 640)
    scale = HEAD_DIM ** -0.5
    NT = (((1,), (1,)), ((), ()))

    nw = len(gather)
    nq = LP // BQ

    def body(*refs):
        q_ref, k_ref, v_ref, b_ref = refs[:4]
        o_ref, of_ref, lse_ref = refs[4 + nw:7 + nw]
        hd = pl.program_id(0)
        qi = pl.program_id(1)
        if nw:
            start, forward, finish = _gather_phases(refs[7 + nw:7 + 2 * nw], *refs[7 + 2 * nw:])
            pl.when((hd == 0) & (qi == 0))(start)
            pl.when((hd == H // 2) & (qi == 0))(forward)
        q = q_ref[...]

        def tile(kb, carry, masked):
            m, l, acc = carry
            k0 = pl.multiple_of(kb * BK, BK)
            s = lax.dot_general(q, k_ref[pl.ds(k0, BK), :], NT, preferred_element_type=F32) * scale
            s = s + b_ref[:, pl.ds(k0, BK)]
            if masked:
                ri = lax.broadcasted_iota(jnp.int32, (BQ, BK), 0)
                ci = lax.broadcasted_iota(jnp.int32, (BQ, BK), 1)
                s = jnp.where(ri >= ci, s, NEG)
            mn = jnp.maximum(m, jnp.max(s, axis=-1, keepdims=True))
            p = jnp.exp(s - mn)
            alpha = jnp.exp(m - mn)
            l = alpha * l + jnp.sum(p, axis=-1, keepdims=True)
            vk = v_ref[pl.ds(k0, BK), :]
            p_hi = p.astype(BF16)
            p_lo = (p - p_hi.astype(F32)).astype(BF16)
            pv = jnp.dot(p_hi, vk, preferred_element_type=F32) + jnp.dot(p_lo, vk, preferred_element_type=F32)
            return mn, l, alpha * acc + pv

        carry = (jnp.full((BQ, 1), NEG, F32), jnp.zeros((BQ, 1), F32), jnp.zeros((BQ, HEAD_DIM), F32))
        carry = lax.fori_loop(0, qi, lambda kb, c: tile(kb, c, False), carry)
        m, l, acc = tile(qi, carry, True)
        rows = qi * BQ + lax.broadcasted_iota(jnp.int32, (BQ, 1), 0)
        o = jnp.where(rows >= pad, acc / l, 0.0)
        o_ref[...] = o.astype(o_ref.dtype)
        of_ref[...] = o
        lse_ref[...] = _col_to_row(m + jnp.log(l))
        if nw:
            pl.when((hd == H - 1) & (qi == nq - 1))(finish)

    in_specs = [
        pl.BlockSpec((BQ, HEAD_DIM), lambda h, i: (i, h)),
        pl.BlockSpec((LP, HEAD_DIM), lambda h, i: (0, H + h)),
        pl.BlockSpec((LP, HEAD_DIM), lambda h, i: (0, 2 * H + h)),
        pl.BlockSpec((None, 1, LP), lambda h, i: (h, 0, 0)),
    ]
    out_specs = [
        pl.BlockSpec((BQ, HEAD_DIM), lambda h, i: (i, h)),
        pl.BlockSpec((BQ, HEAD_DIM), lambda h, i: (i, h)),
        pl.BlockSpec((None, 1, BQ), lambda h, i: (h, 0, i)),
    ]
    est = 2 * (2 * LP * HEAD_DIM * 2 + 8 * LP * 4) + 20 * BQ * LANES * 4 + 8 * BQ * BK * 4
    sems = [pltpu.SemaphoreType.DMA((6 * nw,)), pltpu.SemaphoreType.DMA((6 * nw,))] if nw else []
    return pl.pallas_call(
        body, grid=(H, nq), in_specs=in_specs + [HBM_SPEC] * nw, out_specs=out_specs + [HBM_SPEC] * nw,
        out_shape=[_sds((LP, H * HEAD_DIM), BF16), _sds((LP, H * HEAD_DIM), F32), _sds((H, 1, LP), F32)]
        + [_sds(g.shape, g.dtype) for g in gather],
        scratch_shapes=sems, input_output_aliases={4 + i: 3 + i for i in range(nw)},
        compiler_params=_cparams(("arbitrary", "arbitrary"), est), name=name,
    )(z, z, z, bias_t, *gather)


def _attn_delta(do, o, H, name):
    LP = do.shape[0]
    tm = _tile(LP, 640)

    def body(do_ref, o_ref, d_ref):
        d_ref[...] = _col_to_row(jnp.sum(do_ref[...].astype(F32) * o_ref[...].astype(F32), axis=-1, keepdims=True))

    blk = pl.BlockSpec((tm, HEAD_DIM), lambda h, i: (i, h))
    return pl.pallas_call(
        body, grid=(H, LP // tm), in_specs=[blk, blk],
        out_specs=pl.BlockSpec((None, 1, tm), lambda h, i: (h, 0, i)),
        out_shape=_sds((H, 1, LP), F32),
        compiler_params=_cparams(("parallel", "parallel"), 8 * tm * LANES * 4), name=name,
    )(do, o)


def _attn_bwd(z, do, lse_t, delta_t, bias_t, H, name, scatter=()):
    LP = z.shape[0]
    BK = BQ = _tile(LP, 640)
    nk = nq = LP // BK
    scale = HEAD_DIM ** -0.5
    NT = (((1,), (1,)), ((), ()))
    TN = (((0,), (0,)), ((), ()))

    nw = len(scatter)

    def body(*refs):
        q_ref, k_ref, v_ref, do_ref, lse_ref, dl_ref, b_ref = refs[:7]
        dq_ref, dk_ref, dv_ref, db_ref = refs[7 + nw:11 + nw]
        dq_acc = refs[11 + 2 * nw]
        hd = pl.program_id(0)
        kj = pl.program_id(1)
        if nw:
            start, finish = _scatter_phases(refs[7:7 + nw], refs[11 + nw:11 + 2 * nw], *refs[12 + 2 * nw:])
            pl.when((hd == 0) & (kj == 0))(start)

        @pl.when(kj == 0)
        def _():
            dq_acc[...] = jnp.zeros_like(dq_acc)

        k = k_ref[...]
        v = v_ref[...]
        bcol = _row_to_col(b_ref[:, pl.ds(pl.multiple_of(kj * BK, BK), BK)])

        def tile(qc, carry, masked):
            dk, dv, db = carry
            q0 = pl.multiple_of(qc * BQ, BQ)
            q = q_ref[pl.ds(q0, BQ), :]
            dout = do_ref[pl.ds(q0, BQ), :]
            st = lax.dot_general(k, q, NT, preferred_element_type=F32) * scale + bcol
            if masked:
                ri = lax.broadcasted_iota(jnp.int32, (BK, BQ), 0)
                ci = lax.broadcasted_iota(jnp.int32, (BK, BQ), 1)
                st = jnp.where(ci >= ri, st, NEG)
            pt = jnp.exp(st - lse_ref[:, pl.ds(q0, BQ)])
            dv = dv + jnp.dot(pt.astype(BF16), dout, preferred_element_type=F32)
            dpt = lax.dot_general(v, dout, NT, preferred_element_type=F32)
            dst = pt * (dpt - dl_ref[:, pl.ds(q0, BQ)])
            db = db + jnp.sum(dst, axis=-1, keepdims=True)
            dsb = (dst * scale).astype(BF16)
            dk = dk + jnp.dot(dsb, q, preferred_element_type=F32)
            dq_acc[pl.ds(q0, BQ), :] += lax.dot_general(dsb, k, TN, preferred_element_type=F32)
            return dk, dv, db

        carry = (jnp.zeros((BK, HEAD_DIM), F32), jnp.zeros((BK, HEAD_DIM), F32), jnp.zeros((BK, 1), F32))
        carry = tile(kj, carry, True)
        dk, dv, db = lax.fori_loop(kj + 1, nq, lambda qc, c: tile(qc, c, False), carry)
        dk_ref[...] = dk.astype(dk_ref.dtype)
        dv_ref[...] = dv.astype(dv_ref.dtype)
        db_ref[...] = _col_to_row(db)

        @pl.when(kj == nk - 1)
        def _():
            dq_ref[...] = dq_acc[...].astype(dq_ref.dtype)

        if nw:
            pl.when((hd == H - 1) & (kj == nk - 1))(finish)

    full = lambda c0: pl.BlockSpec((LP, HEAD_DIM), lambda h, j: (0, c0 + h))
    blk = lambda c0: pl.BlockSpec((BK, HEAD_DIM), lambda h, j: (j, c0 + h))
    vec = pl.BlockSpec((None, 1, LP), lambda h, j: (h, 0, 0))
    in_specs = [full(0), blk(H), blk(2 * H), full(0), vec, vec, vec]
    out_specs = [full(0), blk(0), blk(0), pl.BlockSpec((None, 1, BK), lambda h, j: (h, 0, j))]
    est = 2 * (3 * LP * HEAD_DIM * 2 + 16 * LP * 4) + LP * HEAD_DIM * 4 + 24 * BK * LANES * 4 + 10 * BK * BQ * 4
    sems = [pltpu.SemaphoreType.DMA((3 * nw,)), pltpu.SemaphoreType.DMA((3 * nw,))] if nw else []
    return pl.pallas_call(
        body, grid=(H, nk), in_specs=in_specs + [HBM_SPEC] * nw, out_specs=out_specs + [HBM_SPEC] * nw,
        out_shape=[_sds((LP, H * HEAD_DIM), BF16)] * 3 + [_sds((H, 1, LP), F32)]
        + [_sds((3,) + p.shape[1:], p.dtype) for p in scatter],
        scratch_shapes=[pltpu.VMEM((LP, HEAD_DIM), F32)] + sems,
        compiler_params=_cparams(("arbitrary", "arbitrary"), est), name=name,
    )(z, z, z, do, lse_t, delta_t, bias_t, *scatter)


def _ssm_disc(lr, li, ldt, br, bi):
    dt = jnp.exp(ldt)
    mag = jnp.exp(lr * dt)
    a_re = mag * jnp.cos(li * dt)
    a_im = mag * jnp.sin(li * dt)
    den = lr * lr + li * li
    nr = a_re - 1.0
    z_re = (nr * lr + a_im * li) / den
    z_im = (a_im * lr - nr * li) / den
    return a_re, a_im, z_re * br - z_im * bi, z_re * bi + z_im * br


def _ssm_prep(lr, li, ldt, br, bi, name):
    GP, C = br.shape

    def body(lr_ref, li_ref, ldt_ref, br_ref, bi_ref, ar_ref, ai_ref, bbr_ref, bbi_ref, pr_ref, pi_ref):
        a_re, a_im, bb_re, bb_im = _ssm_disc(lr_ref[...], li_ref[...], ldt_ref[...], br_ref[...], bi_ref[...])
        ar_ref[...] = a_re
        ai_ref[...] = a_im
        bbr_ref[...] = bb_re
        bbi_ref[...] = bb_im
        lane = lax.broadcasted_iota(jnp.int32, (tg, 8), 1)
        pr, pi_ = a_re, a_im
        accr = jnp.zeros((tg, 8), F32)
        acci = jnp.zeros((tg, 8), F32)
        for k in range(8):
            accr = jnp.where(lane == k, pr, accr)
            acci = jnp.where(lane == k, pi_, acci)
            pr, pi_ = pr * a_re - pi_ * a_im, pr * a_im + pi_ * a_re
        pr_ref[...] = accr
        pi_ref[...] = acci

    tg = _tile(GP, 512, 8)
    blk = lambda w: pl.BlockSpec((tg, w), lambda i: (i, 0))
    col = _sds((GP, 1), F32)
    return pl.pallas_call(
        body, grid=(GP // tg,), in_specs=[blk(1), blk(1), blk(1), blk(C), blk(C)],
        out_specs=[blk(1), blk(1), blk(C), blk(C), blk(8), blk(8)],
        out_shape=[col, col, _sds((GP, C), F32), _sds((GP, C), F32), _sds((GP, 8), F32), _sds((GP, 8), F32)],
        compiler_params=_cparams(("parallel",), 48 * tg * LANES * 4), name=name,
    )(lr, li, ldt, br, bi)


def _ssm_prep_bwd(lr, li, ldt, br, bi, dar, dai, dbbr, dbbi, name):
    GP, C = br.shape

    def body(lr_ref, li_ref, ldt_ref, br_ref, bi_ref, dar_ref, dai_ref, dbbr_ref, dbbi_ref,
             glr_ref, gli_ref, gldt_ref, gbr_ref, gbi_ref):
        _, vjp = jax.vjp(_ssm_disc, lr_ref[...], li_ref[...], ldt_ref[...], br_ref[...], bi_ref[...])
        glr, gli, gldt, gbr, gbi = vjp((dar_ref[...], dai_ref[...], dbbr_ref[...], dbbi_ref[...]))
        glr_ref[...] = glr
        gli_ref[...] = gli
        gldt_ref[...] = gldt
        gbr_ref[...] = gbr
        gbi_ref[...] = gbi

    tg = _tile(GP, 512, 8)
    blk = lambda w: pl.BlockSpec((tg, w), lambda i: (i, 0))
    col = _sds((GP, 1), F32)
    return pl.pallas_call(
        body, grid=(GP // tg,), in_specs=[blk(1), blk(1), blk(1), blk(C), blk(C), blk(1), blk(1), blk(C), blk(C)],
        out_specs=[blk(1), blk(1), blk(1), blk(C), blk(C)],
        out_shape=[col, col, col, _sds((GP, C), F32), _sds((GP, C), F32)],
        compiler_params=_cparams(("parallel",), 96 * tg * LANES * 4), name=name,
    )(lr, li, ldt, br, bi, dar, dai, dbbr, dbbi)


def _cmul_add(xr, xi, mr, mi, sr, si):
    return xr + mr * sr - mi * si, xi + mr * si + mi * sr


def _ssm_fwd(z, u_col0, coef, bbr, bbi, ccr, cci, dskip, name):
    LP = z.shape[0]
    NB, CB, S = bbr.shape
    TS = _tile(LP, 640, 8)
    nt = LP // TS

    def body(u_ref, coef_ref, bbr_ref, bbi_ref, ccr_ref, cci_ref, ds_ref, y_ref, yg_ref, hr_ref, hi_ref, bur, bui, carry):
        i = pl.program_id(1)

        @pl.when(i == 0)
        def _():
            carry[...] = jnp.zeros_like(carry)

        u = u_ref[...]
        bur[...] = jnp.dot(u, bbr_ref[...], preferred_element_type=F32)
        bui[...] = jnp.dot(u, bbi_ref[...], preferred_element_type=F32)

        def step(g, c):
            cr, ci = c
            r0 = pl.multiple_of(g * 8, 8)
            xr = bur[pl.ds(r0, 8), :]
            xi = bui[pl.ds(r0, 8), :]
            for n, k in enumerate((1, 2, 4)):
                xr, xi = _cmul_add(xr, xi, coef_ref[2 * n], coef_ref[2 * n + 1], pltpu.roll(xr, k, 0), pltpu.roll(xi, k, 0))
            xr, xi = _cmul_add(xr, xi, coef_ref[6], coef_ref[7], cr, ci)
            hr_ref[pl.ds(r0, 8), :] = xr
            hi_ref[pl.ds(r0, 8), :] = xi
            return jnp.broadcast_to(xr[7:8, :], (8, S)), jnp.broadcast_to(xi[7:8, :], (8, S))

        cr, ci = lax.fori_loop(0, TS // 8, step, (carry[0], carry[1]))
        carry[0] = cr
        carry[1] = ci
        y = (jnp.dot(hr_ref[...].astype(BF16), ccr_ref[...], preferred_element_type=F32)
             - jnp.dot(hi_ref[...].astype(BF16), cci_ref[...], preferred_element_type=F32)
             + ds_ref[...] * u.astype(F32))
        y_ref[...] = y
        yg_ref[...] = _gelu(y).astype(yg_ref.dtype)

    ucb = u_col0 // CB
    in_specs = [
        pl.BlockSpec((TS, CB), lambda j, i: (i, ucb + j)),
        pl.BlockSpec((None, 8, 8, S), lambda j, i: (j, 0, 0, 0)),
        pl.BlockSpec((None, CB, S), lambda j, i: (j, 0, 0)),
        pl.BlockSpec((None, CB, S), lambda j, i: (j, 0, 0)),
        pl.BlockSpec((None, S, CB), lambda j, i: (j, 0, 0)),
        pl.BlockSpec((None, S, CB), lambda j, i: (j, 0, 0)),
        pl.BlockSpec((1, CB), lambda j, i: (0, j)),
    ]
    yb = pl.BlockSpec((TS, CB), lambda j, i: (i, j))
    hb = pl.BlockSpec((TS, S), lambda j, i: (i, j))
    est = 2 * (2 * TS * S * 4 + 3 * TS * CB * 4 + 8 * 8 * S * 4 + 4 * CB * S * 2) + 3 * TS * S * 4
    return pl.pallas_call(
        body, grid=(NB, nt), in_specs=in_specs, out_specs=[yb, yb, hb, hb],
        out_shape=[_sds((LP, NB * CB), F32), _sds((LP, NB * CB), BF16), _sds((LP, NB * S), F32), _sds((LP, NB * S), F32)],
        scratch_shapes=[pltpu.VMEM((TS, S), F32), pltpu.VMEM((TS, S), F32), pltpu.VMEM((2, 8, S), F32)],
        compiler_params=_cparams(("parallel", "arbitrary"), est), name=name,
    )(z, coef, bbr, bbi, ccr, cci, dskip)


def _ssm_bwd(z, u_col0, dyg, y, hr, hi, coef_rev, bbr_t, bbi_t, ccr_t, cci_t, dskip, name, ride=None):
    LP = z.shape[0]
    NB, S, CB = bbr_t.shape
    TS = _tile(LP, 640, 8)
    nt = LP // TS
    ng = TS // 8
    TN = (((0,), (0,)), ((), ()))

    r_ins, r_outs, r_sems = _ride_parts(ride)

    def body(*refs):
        n_in, n_out = 13 + len(r_ins), 8 + len(r_outs)
        if ride is not None:
            start, finish = ride["fn"](refs[13:n_in], refs[n_in + 8:n_in + n_out], *refs[-2:])
            pl.when((pl.program_id(0) == 0) & (pl.program_id(1) == 0))(start)
        step(*refs[:13], *refs[n_in:n_in + 8], *refs[n_in + n_out:n_in + n_out + 9])
        if ride is not None:
            pl.when((pl.program_id(0) == NB - 1) & (pl.program_id(1) == nt - 1))(finish)

    def step(u_ref, dyg_ref, y_ref, hr_ref, hi_ref, tr_ref, ti_ref, coef_ref, bbr_ref, bbi_ref, ccr_ref, cci_ref, ds_ref,
             du_ref, dbbr_ref, dbbi_ref, dccr_ref, dcci_ref, dar_ref, dai_ref, dd_ref,
             gr, gi, carry, acc_bbr, acc_bbi, acc_ccr, acc_cci, acc_a, acc_d):
        i = pl.program_id(1)

        @pl.when(i == 0)
        def _():
            for ref in (carry, acc_bbr, acc_bbi, acc_ccr, acc_cci, acc_a, acc_d):
                ref[...] = jnp.zeros_like(ref)

        u = u_ref[...]
        dy = dyg_ref[...] * _gelu_grad(y_ref[...])
        dyb = dy.astype(BF16)
        gr[...] = jnp.dot(dyb, ccr_ref[...], preferred_element_type=F32)
        gi[...] = -jnp.dot(dyb, cci_ref[...], preferred_element_type=F32)
        row8 = lax.broadcasted_iota(jnp.int32, (8, S), 0)
        first_chunk = i == nt - 1
        tail_r = jnp.where(first_chunk, 0.0, tr_ref[...])
        tail_i = jnp.where(first_chunk, 0.0, ti_ref[...])

        def scan(t, c):
            cr, ci, sar, sai = c
            g = ng - 1 - t
            r0 = pl.multiple_of(g * 8, 8)
            xr = gr[pl.ds(r0, 8), :]
            xi = gi[pl.ds(r0, 8), :]
            for n, k in enumerate((1, 2, 4)):
                xr, xi = _cmul_add(xr, xi, coef_ref[2 * n], coef_ref[2 * n + 1], pltpu.roll(xr, 8 - k, 0), pltpu.roll(xi, 8 - k, 0))
            xr, xi = _cmul_add(xr, xi, coef_ref[6], coef_ref[7], cr, ci)
            gr[pl.ds(r0, 8), :] = xr
            gi[pl.ds(r0, 8), :] = xi
            p0 = pl.multiple_of(jnp.maximum(g - 1, 0) * 8, 8)
            pr = jnp.where(g > 0, hr_ref[pl.ds(p0, 8), :], tail_r)
            pi_ = jnp.where(g > 0, hi_ref[pl.ds(p0, 8), :], tail_i)
            hpr = pltpu.roll(jnp.where(row8 == 7, pr, hr_ref[pl.ds(r0, 8), :]), 1, 0)
            hpi = pltpu.roll(jnp.where(row8 == 7, pi_, hi_ref[pl.ds(r0, 8), :]), 1, 0)
            sar = sar + xr * hpr + xi * hpi
            sai = sai + xi * hpr - xr * hpi
            return jnp.broadcast_to(xr[0:1, :], (8, S)), jnp.broadcast_to(xi[0:1, :], (8, S)), sar, sai

        zero = jnp.zeros((8, S), F32)
        cr, ci, sar, sai = lax.fori_loop(0, ng, scan, (carry[0], carry[1], zero, zero))
        carry[0] = cr
        carry[1] = ci
        acc_a[0] += sar
        acc_a[1] += sai
        grb = gr[...].astype(BF16)
        gib = gi[...].astype(BF16)
        du = (jnp.dot(grb, bbr_ref[...], preferred_element_type=F32) + jnp.dot(gib, bbi_ref[...], preferred_element_type=F32)
              + ds_ref[...] * dy)
        du_ref[...] = du.astype(du_ref.dtype)
        acc_bbr[...] += lax.dot_general(u, grb, TN, preferred_element_type=F32)
        acc_bbi[...] += lax.dot_general(u, gib, TN, preferred_element_type=F32)
        acc_ccr[...] += lax.dot_general(hr_ref[...].astype(BF16), dyb, TN, preferred_element_type=F32)
        acc_cci[...] -= lax.dot_general(hi_ref[...].astype(BF16), dyb, TN, preferred_element_type=F32)
        acc_d[...] += jnp.sum(dy * u.astype(F32), axis=0, keepdims=True)

        @pl.when(i == nt - 1)
        def _():
            dbbr_ref[...] = acc_bbr[...]
            dbbi_ref[...] = acc_bbi[...]
            dccr_ref[...] = acc_ccr[...]
            dcci_ref[...] = acc_cci[...]
            dar_ref[...] = jnp.sum(acc_a[0], axis=0, keepdims=True)
            dai_ref[...] = jnp.sum(acc_a[1], axis=0, keepdims=True)
            dd_ref[...] = acc_d[...]

    ucb = u_col0 // CB
    rev = lambda i: nt - 1 - i
    tail = lambda j, i: (jnp.maximum(rev(i) * ng - 1, 0), j)
    yb = pl.BlockSpec((TS, CB), lambda j, i: (rev(i), j))
    hb = pl.BlockSpec((TS, S), lambda j, i: (rev(i), j))
    in_specs = [
        pl.BlockSpec((TS, CB), lambda j, i: (rev(i), ucb + j)), yb, yb, hb, hb,
        pl.BlockSpec((8, S), tail), pl.BlockSpec((8, S), tail),
        pl.BlockSpec((None, 8, 8, S), lambda j, i: (j, 0, 0, 0)),
        pl.BlockSpec((None, S, CB), lambda j, i: (j, 0, 0)),
        pl.BlockSpec((None, S, CB), lambda j, i: (j, 0, 0)),
        pl.BlockSpec((None, CB, S), lambda j, i: (j, 0, 0)),
        pl.BlockSpec((None, CB, S), lambda j, i: (j, 0, 0)),
        pl.BlockSpec((1, CB), lambda j, i: (0, j)),
    ]
    mat_cs = pl.BlockSpec((None, CB, S), lambda j, i: (j, 0, 0))
    mat_sc = pl.BlockSpec((None, S, CB), lambda j, i: (j, 0, 0))
    vec_s = pl.BlockSpec((None, 1, S), lambda j, i: (j, 0, 0))
    out_specs = [yb, mat_cs, mat_cs, mat_sc, mat_sc, vec_s, vec_s, pl.BlockSpec((1, CB), lambda j, i: (0, j))]
    out_shape = [_sds((LP, NB * CB), BF16), _sds((NB, CB, S), F32), _sds((NB, CB, S), F32), _sds((NB, S, CB), F32),
                 _sds((NB, S, CB), F32), _sds((NB, 1, S), F32), _sds((NB, 1, S), F32), _sds((1, NB * CB), F32)]
    scratch = [pltpu.VMEM((TS, S), F32), pltpu.VMEM((TS, S), F32), pltpu.VMEM((2, 8, S), F32),
               pltpu.VMEM((CB, S), F32), pltpu.VMEM((CB, S), F32), pltpu.VMEM((S, CB), F32), pltpu.VMEM((S, CB), F32),
               pltpu.VMEM((2, 8, S), F32), pltpu.VMEM((1, CB), F32)]
    est = 2 * (2 * TS * S * 4 + 4 * TS * CB * 4 + 8 * 8 * S * 4 + 12 * CB * S * 4) + 4 * TS * S * 4
    return pl.pallas_call(
        body, grid=(NB, nt), in_specs=in_specs + [HBM_SPEC] * len(r_ins), out_specs=out_specs + [HBM_SPEC] * len(r_outs),
        out_shape=out_shape + r_outs, scratch_shapes=scratch + r_sems,
        compiler_params=_cparams(("arbitrary", "arbitrary"), est), name=name,
    )(z, dyg, y, hr, hi, hr, hi, coef_rev, bbr_t, bbi_t, ccr_t, cci_t, dskip, *r_ins)


def _merge_fwd(yab, z, ao, D, ga0, gb0, name):
    LP = z.shape[0]
    tm = _tile(LP, 640, 8)
    tn = _ctile(512, D, ga0, gb0)
    nj = D // tn

    def body(ya_ref, yb_ref, ga_ref, gb_ref, ao_ref, o_ref):
        f = lambda r: r[...].astype(F32)
        ssm = f(ya_ref) * _sig(f(yb_ref))
        o_ref[...] = (_sig(f(ga_ref)) * ssm + _sig(f(gb_ref)) * f(ao_ref)).astype(o_ref.dtype)

    blk = lambda c0: pl.BlockSpec((tm, tn), lambda i, j: (i, c0 // tn + j))
    return pl.pallas_call(
        body, grid=(LP // tm, nj), in_specs=[blk(0), blk(D), blk(ga0), blk(gb0), blk(0)], out_specs=blk(0),
        out_shape=_sds((LP, D), BF16), compiler_params=_cparams(("parallel", "parallel"), 2 * 6 * tm * tn * 4), name=name,
    )(yab, yab, z, z, ao)


def _merge_bwd(dm, yab, z, ao, D, ga0, gb0, name):
    LP = z.shape[0]
    tm = _tile(LP, 640, 8)
    tn = _ctile(512, D, ga0, gb0)
    nj = D // tn

    def body(dm_ref, ya_ref, yb_ref, ga_ref, gb_ref, ao_ref, dya_ref, dyb_ref, dga_ref, dgb_ref, dao_ref):
        f = lambda r: r[...].astype(F32)
        dmv, ya, ao_v = f(dm_ref), f(ya_ref), f(ao_ref)
        sa, sb, sy = _sig(f(ga_ref)), _sig(f(gb_ref)), _sig(f(yb_ref))
        t = dmv * sa
        dya_ref[...] = (t * sy).astype(BF16)
        dyb_ref[...] = (t * ya * sy * (1.0 - sy)).astype(BF16)
        dga_ref[...] = (dmv * (ya * sy) * sa * (1.0 - sa)).astype(BF16)
        dgb_ref[...] = (dmv * ao_v * sb * (1.0 - sb)).astype(BF16)
        dao_ref[...] = (dmv * sb).astype(BF16)

    blk = lambda c0: pl.BlockSpec((tm, tn), lambda i, j: (i, c0 // tn + j))
    return pl.pallas_call(
        body, grid=(LP // tm, nj), in_specs=[blk(0), blk(0), blk(D), blk(ga0), blk(gb0), blk(0)], out_specs=[blk(0)] * 5,
        out_shape=[_sds((LP, D), BF16)] * 5, compiler_params=_cparams(("parallel", "parallel"), 2 * 11 * tm * tn * 4), name=name,
    )(dm, yab, yab, z, z, ao)


def _shift_down(x, halo, k, row8):
    s = pltpu.roll(x, k, 0)
    top = jnp.where(row8 < k, pltpu.roll(halo, k, 0), s[0:8])
    return jnp.concatenate([top, s[8:]], axis=0) if x.shape[0] > 8 else top


def _shift_up(x, halo, k, row8):
    tm = x.shape[0]
    s = pltpu.roll(x, tm - k, 0)
    bot = jnp.where(row8 >= 8 - k, pltpu.roll(halo, 8 - k, 0), s[tm - 8:])
    return jnp.concatenate([s[:tm - 8], bot], axis=0) if tm > 8 else bot


def _conv_gate(g, halo, w_ref, cb, row8):
    return cb + w_ref[0:1, :] * _shift_down(g, halo, 2, row8) + w_ref[1:2, :] * _shift_down(g, halo, 1, row8) + w_ref[2:3, :] * g


def _convact_fwd(gu, cw, cb, DFF, name):
    LP = gu.shape[0]
    tm = _tile(LP, 640, 8)
    tn = _tile(DFF, 512)
    nj = DFF // tn
    t8 = tm // 8

    def body(g_ref, h_ref, u_ref, w_ref, b_ref, o_ref):
        i = pl.program_id(0)
        row8 = lax.broadcasted_iota(jnp.int32, (8, tn), 0)
        g = g_ref[...].astype(F32)
        halo = jnp.where(i > 0, h_ref[...].astype(F32), 0.0)
        gc = _conv_gate(g, halo, w_ref, b_ref[...], row8)
        o_ref[...] = (gc * _sig(gc) * u_ref[...].astype(F32)).astype(o_ref.dtype)

    in_specs = [
        pl.BlockSpec((tm, tn), lambda i, j: (i, j)),
        pl.BlockSpec((8, tn), lambda i, j: (jnp.maximum(i * t8 - 1, 0), j)),
        pl.BlockSpec((tm, tn), lambda i, j: (i, nj + j)),
        pl.BlockSpec((3, tn), lambda i, j: (0, j)),
        pl.BlockSpec((1, tn), lambda i, j: (0, j)),
    ]
    return pl.pallas_call(
        body, grid=(LP // tm, nj), in_specs=in_specs, out_specs=pl.BlockSpec((tm, tn), lambda i, j: (i, j)),
        out_shape=_sds((LP, DFF), BF16), compiler_params=_cparams(("parallel", "parallel"), 2 * 8 * tm * tn * 4), name=name,
    )(gu, gu, gu, cw, cb)


def _convact_bwd(dact, gu, cw, cb, DFF, name):
    LP = gu.shape[0]
    tm = _tile(LP, 640, 8)
    tn = _tile(DFF, 512)
    nj = DFF // tn
    t8 = tm // 8

    def body(da_ref, g_ref, h_ref, u_ref, w_ref, b_ref, dgc_ref, du_ref):
        i = pl.program_id(0)
        row8 = lax.broadcasted_iota(jnp.int32, (8, tn), 0)
        g = g_ref[...].astype(F32)
        halo = jnp.where(i > 0, h_ref[...].astype(F32), 0.0)
        gc = _conv_gate(g, halo, w_ref, b_ref[...], row8)
        sg = _sig(gc)
        da = da_ref[...].astype(F32)
        du_ref[...] = (da * gc * sg).astype(du_ref.dtype)
        dgc_ref[...] = (da * u_ref[...].astype(F32) * sg * (1.0 + gc * (1.0 - sg))).astype(dgc_ref.dtype)

    blk = pl.BlockSpec((tm, tn), lambda i, j: (i, j))
    in_specs = [
        blk, blk,
        pl.BlockSpec((8, tn), lambda i, j: (jnp.maximum(i * t8 - 1, 0), j)),
        pl.BlockSpec((tm, tn), lambda i, j: (i, nj + j)),
        pl.BlockSpec((3, tn), lambda i, j: (0, j)),
        pl.BlockSpec((1, tn), lambda i, j: (0, j)),
    ]
    return pl.pallas_call(
        body, grid=(LP // tm, nj), in_specs=in_specs, out_specs=[blk, blk],
        out_shape=[_sds((LP, DFF), BF16), _sds((LP, DFF), BF16)],
        compiler_params=_cparams(("parallel", "parallel"), 2 * 10 * tm * tn * 4), name=name,
    )(dact, gu, gu, gu, cw, cb)


def _conv_bwd(dgc, gu, cw, DFF, pad, name):
    LP = gu.shape[0]
    tm = _tile(LP, 640, 8)
    tn = _tile(DFF, 512)
    nj = DFF // tn
    t8 = tm // 8
    nt = LP // tm

    def body(d_ref, dn_ref, g_ref, h_ref, w_ref, dg_ref, dw_ref, db_ref):
        i = pl.program_id(1)
        row8 = lax.broadcasted_iota(jnp.int32, (8, tn), 0)
        d = d_ref[...].astype(F32)
        nxt = jnp.where(i < nt - 1, dn_ref[...].astype(F32), 0.0)
        dg = w_ref[2:3, :] * d + w_ref[1:2, :] * _shift_up(d, nxt, 1, row8) + w_ref[0:1, :] * _shift_up(d, nxt, 2, row8)
        rows = i * tm + lax.broadcasted_iota(jnp.int32, (tm, 1), 0)
        dg_ref[...] = jnp.where(rows >= pad, dg, 0.0).astype(dg_ref.dtype)
        g = g_ref[...].astype(F32)
        halo = jnp.where(i > 0, h_ref[...].astype(F32), 0.0)
        row3 = lax.broadcasted_iota(jnp.int32, (3, tn), 0)
        s0 = jnp.sum(d * _shift_down(g, halo, 2, row8), axis=0, keepdims=True)
        s1 = jnp.sum(d * _shift_down(g, halo, 1, row8), axis=0, keepdims=True)
        s2 = jnp.sum(d * g, axis=0, keepdims=True)
        dw = jnp.where(row3 == 0, s0, jnp.where(row3 == 1, s1, s2))
        dbp = jnp.sum(d, axis=0, keepdims=True)

        @pl.when(i == 0)
        def _():
            dw_ref[...] = dw
            db_ref[...] = dbp

        @pl.when(i > 0)
        def _():
            dw_ref[...] += dw
            db_ref[...] += dbp

    blk = pl.BlockSpec((tm, tn), lambda j, i: (i, j))
    in_specs = [
        blk,
        pl.BlockSpec((8, tn), lambda j, i: (jnp.minimum((i + 1) * t8, LP // 8 - 1), j)),
        blk,
        pl.BlockSpec((8, tn), lambda j, i: (jnp.maximum(i * t8 - 1, 0), j)),
        pl.BlockSpec((3, tn), lambda j, i: (0, j)),
    ]
    out_specs = [blk, pl.BlockSpec((3, tn), lambda j, i: (0, j)), pl.BlockSpec((1, tn), lambda j, i: (0, j))]
    return pl.pallas_call(
        body, grid=(nj, nt), in_specs=in_specs, out_specs=out_specs,
        out_shape=[_sds((LP, DFF), BF16), _sds((3, DFF), F32), _sds((1, DFF), F32)],
        compiler_params=_cparams(("parallel", "arbitrary"), 2 * 10 * tm * tn * 4), name=name,
    )(dgc, dgc, gu, gu, cw)


def _adamw_math(w, g, m, v):
    m = ADAM_B1 * m + (1.0 - ADAM_B1) * g
    v = ADAM_B2 * v + (1.0 - ADAM_B2) * (g * g)
    m_hat = m / (1.0 - ADAM_B1 ** ADAM_STEP)
    v_hat = v / (1.0 - ADAM_B2 ** ADAM_STEP)
    delta = -ADAM_LR * (m_hat / (jnp.sqrt(v_hat) + ADAM_EPS) + ADAM_WD * w)
    return delta, m, v


def _adamw(w, g, m, v, name):
    R, C = g.shape
    tm = R if R * C * 4 <= (1 << 20) else _tile(R, max(8, ((1 << 20) // (C * 4)) // 8 * 8), 8)

    def body(w_ref, g_ref, m_ref, v_ref, d_ref, mo_ref, vo_ref):
        d_ref[...], mo_ref[...], vo_ref[...] = _adamw_math(w_ref[...], g_ref[...], m_ref[...], v_ref[...])

    blk = pl.BlockSpec((tm, C), lambda i: (i, 0))
    wblk = blk if w.ndim == 2 else pl.BlockSpec((None, tm, C), lambda i: (0, i, 0))
    return pl.pallas_call(
        body, grid=(R // tm,), in_specs=[wblk, blk, wblk, wblk], out_specs=[wblk] * 3, out_shape=[_sds(w.shape, F32)] * 3,
        compiler_params=_cparams(("parallel",), 2 * 7 * tm * (C + LANES) * 4), name=name,
    )(w, g, m, v)


def _sum_adamw(parts, w, m, v, name):
    n, R, C = parts.shape
    tm = _tile(R, 256, 8)

    def body(p_ref, w_ref, m_ref, v_ref, g_ref, d_ref, mo_ref, vo_ref):
        g = p_ref[0]
        for k in range(1, n):
            g = g + p_ref[k]
        g_ref[...] = g
        d_ref[...], mo_ref[...], vo_ref[...] = _adamw_math(w_ref[...], g, m_ref[...], v_ref[...])

    blk = pl.BlockSpec((tm, C), lambda i: (i, 0))
    return pl.pallas_call(
        body, grid=(R // tm,), in_specs=[pl.BlockSpec((n, tm, C), lambda i: (0, i, 0))] + [blk] * 3, out_specs=[blk] * 4,
        out_shape=[_sds((R, C), F32)] * 4,
        compiler_params=_cparams(("parallel",), 2 * (n + 7) * tm * C * 4), name=name,
    )(parts, w, m, v)


def _add_half(g, got, c_idx, name):
    n, R, C = g.shape
    HR = R // 2
    tm = _tile(HR, max(8, ((1 << 20) // (C * 4)) // 8 * 8), 8)
    nb = HR // tm

    def body(c_ref, g_ref, t_ref, o_ref):
        o_ref[...] = (g_ref[...] + t_ref[...]).astype(o_ref.dtype)

    grid_spec = pltpu.PrefetchScalarGridSpec(
        num_scalar_prefetch=1, grid=(n, nb),
        in_specs=[pl.BlockSpec((None, tm, C), lambda k, i, c: (k, c[0] * nb + i, 0)),
                  pl.BlockSpec((None, tm, C), lambda k, i, c: (k, i, 0))],
        out_specs=pl.BlockSpec((None, tm, C), lambda k, i, c: (k, i, 0)))
    return pl.pallas_call(
        body, grid_spec=grid_spec, out_shape=_sds((n, HR, C), BF16),
        compiler_params=_cparams(("parallel", "parallel"), 2 * 3 * tm * (C + LANES) * 4), name=name,
    )(c_idx, g, got)


def _sum_half(g, got, land, chip_c, name):
    n, R, C = g.shape
    HR = R // 2
    tm = _tile(HR, max(8, ((1 << 20) // (C * 4)) // 8 * 8), 8)
    nb = HR // tm

    def body(s_ref, g_ref, t_ref, l_ref, o_ref):
        acc = g_ref[...] + t_ref[...]
        for k in range(3):
            acc = acc + l_ref[k].astype(F32)
        o_ref[...] = acc

    grid_spec = pltpu.PrefetchScalarGridSpec(
        num_scalar_prefetch=1, grid=(nb,),
        in_specs=[pl.BlockSpec((None, tm, C), lambda i, sc: (sc[0], sc[1] * nb + i, 0)),
                  pl.BlockSpec((None, tm, C), lambda i, sc: (sc[0], i, 0)),
                  pl.BlockSpec((3, tm, C), lambda i, sc: (0, i, 0))],
        out_specs=pl.BlockSpec((tm, C), lambda i, sc: (sc[1] * nb + i, 0)))
    return pl.pallas_call(
        body, grid_spec=grid_spec, out_shape=_sds((R, C), F32),
        compiler_params=_cparams(("parallel",), 2 * 6 * tm * (C + LANES) * 4), name=name,
    )(chip_c, g, got, land)


def _cast_slot(w, chip_idx, name):
    R, C = w.shape
    tm = _tile(R, max(16, ((1 << 20) // (C * 4)) // 16 * 16), 16)

    def body(s_ref, w_ref, o_ref):
        o_ref[...] = w_ref[...].astype(o_ref.dtype)

    grid_spec = pltpu.PrefetchScalarGridSpec(
        num_scalar_prefetch=1, grid=(R // tm,),
        in_specs=[pl.BlockSpec((tm, C), lambda i, sc: (i, 0))],
        out_specs=pl.BlockSpec((None, tm, C), lambda i, sc: (sc[0], i, 0)))
    return pl.pallas_call(
        body, grid_spec=grid_spec, out_shape=_sds((4, R, C), BF16),
        compiler_params=_cparams(("parallel",), 2 * 2 * tm * (C + LANES) * 4), name=name,
    )(chip_idx, w)


def _gather_chips(gs, name):
    nw = len(gs)

    def body(*refs):
        start, forward, finish = _gather_phases(refs[nw:2 * nw], *refs[2 * nw:])
        start()
        forward()
        finish()

    return pl.pallas_call(
        body, out_shape=[_sds(g.shape, g.dtype) for g in gs], in_specs=[HBM_SPEC] * nw, out_specs=[HBM_SPEC] * nw,
        scratch_shapes=[pltpu.SemaphoreType.DMA((6 * nw,)), pltpu.SemaphoreType.DMA((6 * nw,))],
        input_output_aliases={i: i for i in range(nw)}, name=name,
    )(*gs)


def _sibling_halves(gs, name):
    nw = len(gs)
    assert all(g.shape[1] % 16 == 0 for g in gs)

    def body(*refs):
        start, finish = _sibling_phases(refs[:nw], refs[nw:2 * nw], *refs[2 * nw:])
        start()
        finish()

    return pl.pallas_call(
        body, out_shape=[_sds((4, g.shape[1] // 2, g.shape[2]), g.dtype) for g in gs],
        in_specs=[HBM_SPEC] * nw, out_specs=[HBM_SPEC] * nw,
        scratch_shapes=[pltpu.SemaphoreType.DMA((nw,)), pltpu.SemaphoreType.DMA((nw,))], name=name,
    )(*gs)


def _sibling_join(fs, name):
    nw = len(fs)
    assert all(f.shape[0] % 16 == 0 for f in fs)

    def body(*refs):
        o_refs = refs[nw:2 * nw]
        send_sems, recv_sems = refs[2 * nw:]
        x, y, c = _place()

        def copy(i, half):
            HR = o_refs[i].shape[0] // 2
            rows = o_refs[i].at[pl.ds(pl.multiple_of(half * HR, 8), HR), :]
            return pltpu.make_async_remote_copy(
                src_ref=rows, dst_ref=rows, send_sem=send_sems.at[i], recv_sem=recv_sems.at[i],
                device_id=(x, y, 1 - c), device_id_type=MESH)

        sends = [copy(i, c) for i in range(nw)]
        for cp in sends:
            cp.start()
        for i in range(nw):
            copy(i, 1 - c).wait_recv()
        for cp in sends:
            cp.wait_send()

    return pl.pallas_call(
        body, out_shape=[_sds(f.shape, f.dtype) for f in fs], in_specs=[HBM_SPEC] * nw, out_specs=[HBM_SPEC] * nw,
        scratch_shapes=[pltpu.SemaphoreType.DMA((nw,)), pltpu.SemaphoreType.DMA((nw,))],
        input_output_aliases={i: i for i in range(nw)}, name=name,
    )(*fs)


def _gather_all(v, name):
    M, W = v.shape

    def body(v_ref, o_ref, send_sems, recv_sems, local_sem):
        x, y, c = _place()
        me, sibling = (x, y, c), (x, y, 1 - c)
        chips = _other_chips(x, y)

        def slot(px, py, pc):
            return o_ref.at[4 * px + 2 * py + pc]

        def copy(k, block, to, src=None):
            return pltpu.make_async_remote_copy(
                src_ref=slot(*block) if src is None else src, dst_ref=slot(*block),
                send_sem=send_sems.at[k], recv_sem=recv_sems.at[k], device_id=to, device_id_type=MESH)

        mine = pltpu.make_async_copy(v_ref, slot(*me), local_sem)
        mine.start()
        first = [copy(0, me, sibling, src=v_ref)]
        first += [copy(1 + j, me, (*chip, c), src=v_ref) for j, chip in enumerate(chips)]
        for cp in first:
            cp.start()
        passed = [copy(4 + j, (*chip, c), sibling) for j, chip in enumerate(chips)]
        for j, chip in enumerate(chips):
            copy(1 + j, (*chip, c), me).wait_recv()
            passed[j].start()
        copy(0, sibling, me).wait_recv()
        for j, chip in enumerate(chips):
            copy(4 + j, (*chip, 1 - c), me).wait_recv()
        for cp in first + passed:
            cp.wait_send()
        mine.wait()

    vm = pl.BlockSpec(memory_space=pltpu.VMEM)
    return pl.pallas_call(
        body, out_shape=_sds((8, M, W), v.dtype), in_specs=[vm], out_specs=vm,
        scratch_shapes=[pltpu.SemaphoreType.DMA((7,)), pltpu.SemaphoreType.DMA((7,)), pltpu.SemaphoreType.DMA(())],
        compiler_params=pltpu.CompilerParams(vmem_limit_bytes=int(min(10 * M * W * 4 + (8 << 20), V7X_VMEM_BYTES - (8 << 20)))),
        name=name,
    )(v)


def _rows_for(n_elems, width, mult=8):
    rows = -(-n_elems // width)
    return -(-rows // mult) * mult


def _pack_small(arrs, total_rows):
    parts = []
    used = 0
    for a in arrs:
        rows = _rows_for(a.size, LANES)
        parts.append(jnp.pad(a.reshape(-1), (0, rows * LANES - a.size)).reshape(rows, LANES))
        used += rows
    if total_rows > used:
        parts.append(jnp.zeros((total_rows - used, LANES), F32))
    return jnp.concatenate(parts, axis=0)


def _unpack_small(p, shapes):
    outs, r = [], 0
    lead = p.shape[:-2]
    for shp in shapes:
        n = int(np.prod(shp))
        rows = _rows_for(n, LANES)
        outs.append(p[..., r:r + rows, :].reshape(lead + (rows * LANES,))[..., :n].reshape(lead + tuple(shp)))
        r += rows
    return outs


def _cols_to_chips(w):
    K, N = w.shape
    return w.reshape(K, 4, N // 4).transpose(1, 0, 2)


def _chips_to_cols(w):
    n4, K, n = w.shape
    return w.transpose(1, 0, 2).reshape(K, n4 * n)


def _block_diag(m, gpb):
    G, A, B = m.shape
    nb = G // gpb
    eye = jnp.eye(gpb, dtype=m.dtype)
    t = m.reshape(nb, gpb, A, B)[:, :, :, None, :] * eye[None, :, None, :, None]
    return t.reshape(nb, gpb * A, gpb * B)


def _block_diag_extract(m, gpb, A, B):
    nb = m.shape[0]
    t = m.reshape(nb, gpb, A, gpb, B)
    eye = jnp.eye(gpb, dtype=m.dtype)
    d = jnp.sum(t * eye[None, :, None, :, None], axis=3)
    return d.reshape(nb * gpb, A, B)


def kernel(x, meta, g_mix, w_in, b_f, lam_re, lam_im, log_dt, b_re, b_im, c_re, c_im, d_skip, w_glu, w_attn_o, w_out, g_ffn, w_up, conv_w, conv_b, w_down, g_final, loss_target, m_meta, m_g_mix, m_w_in, m_b_f, m_lam_re, m_lam_im, m_log_dt, m_b_re, m_b_im, m_c_re, m_c_im, m_d_skip, m_w_glu, m_w_attn_o, m_w_out, m_g_ffn, m_w_up, m_conv_w, m_conv_b, m_w_down, m_g_final, v_meta, v_g_mix, v_w_in, v_b_f, v_lam_re, v_lam_im, v_log_dt, v_b_re, v_b_im, v_c_re, v_c_im, v_d_skip, v_w_glu, v_w_attn_o, v_w_out, v_g_ffn, v_w_up, v_conv_w, v_conv_b, v_w_down, v_g_final):
    args = dict(locals())
    L, D = x.shape[1], x.shape[2]
    NM = meta.shape[0]
    H = b_f.shape[1]
    DA = H * HEAD_DIM
    G, P, C = b_re.shape[1:]
    DS, GP = G * C, G * P
    DFF = conv_b.shape[1]
    PAD = (-NM) % LANES
    OFF = PAD + NM
    LP = OFF + L
    NZ = 3 * DA + DS + 2 * D
    U0, GA0, GB0 = 3 * DA, 3 * DA + DS, 3 * DA + DS + D
    NB = G // GROUPS_PER_BLOCK
    chip = 2 * lax.axis_index("x") + lax.axis_index("y")
    core = lax.axis_index("c")

    big = ["w_in", "w_glu", "w_attn_o", "w_out", "w_up", "w_down"]
    local = {n: args[n][0] for n in big}
    chip_idx = chip.reshape(1).astype(jnp.int32)
    slots = {n: _cast_slot(local[n], chip_idx, "cast_" + n) for n in big}
    gathered = {"w_in": _gather_chips([slots["w_in"]], "gather_w_in")[0]}
    tiny_shapes = [conv_w.shape[1:], meta.shape]
    tiny_rows = sum(_rows_for(int(np.prod(sh)), LANES) for sh in tiny_shapes)
    tiny = _gather_all(_pack_small([conv_w[0], meta], tiny_rows), "gather_small_weights")[0::2]
    conv_w_c, meta_c = _unpack_small(tiny, tiny_shapes)
    conv_w_f = _chips_to_cols(conv_w_c)
    meta_full = _chips_to_cols(meta_c)
    w_in_f = _chips_to_cols(gathered["w_in"])
    w_f = jnp.pad(w_in_f[:, 3 * DA:3 * DA + H], ((0, 0), (0, LANES - H)))
    w_zf = jnp.concatenate([w_in_f[:, :3 * DA], w_in_f[:, 3 * DA + H:], w_f], axis=1)
    N_GLU, N_AO, N_UP = (slots[n].shape[2] for n in ("w_glu", "w_attn_o", "w_up"))

    col = lambda a: a.reshape(GP, 1)
    lr_c, li_c = col(lam_re[0]), col(lam_im[0])
    ldt_c = jnp.repeat(log_dt[0], P).reshape(GP, 1)
    br2, bi2 = b_re[0].reshape(GP, C), b_im[0].reshape(GP, C)
    a_re, a_im, bb_re, bb_im, pw_re, pw_im = _ssm_prep(lr_c, li_c, ldt_c, br2, bi2, "ssm_prep")
    S = GROUPS_PER_BLOCK * P
    CB = GROUPS_PER_BLOCK * C
    pw_r = pw_re.T.reshape(8, NB, S).transpose(1, 0, 2)
    pw_i = pw_im.T.reshape(8, NB, S).transpose(1, 0, 2)
    row8 = jnp.arange(8)[None, :, None]

    def masked_power(pw, k, keep):
        return jnp.where(keep, pw[:, k - 1][:, None, :], 0.0)

    coef = jnp.stack(
        [masked_power(pw, k, row8 >= k) for k in (1, 2, 4) for pw in (pw_r, pw_i)] + [pw_r, pw_i], axis=1)
    coef_rev = jnp.stack(
        [masked_power(pw, k, row8 < 8 - k) for k in (1, 2, 4) for pw in (pw_r, -pw_i)]
        + [pw_r[:, ::-1], -pw_i[:, ::-1]], axis=1)
    bd = lambda m: _block_diag(m, GROUPS_PER_BLOCK)
    bbr3, bbi3 = bb_re.reshape(G, P, C), bb_im.reshape(G, P, C)
    bbr_cs = bd(bbr3.transpose(0, 2, 1)).astype(BF16)
    bbi_cs = bd(bbi3.transpose(0, 2, 1)).astype(BF16)
    bbr_sc = bd(bbr3).astype(BF16)
    bbi_sc = bd(bbi3).astype(BF16)
    ccr_sc = bd(c_re[0].transpose(0, 2, 1)).astype(BF16)
    cci_sc = bd(c_im[0].transpose(0, 2, 1)).astype(BF16)
    ccr_cs = bd(c_re[0]).astype(BF16)
    cci_cs = bd(c_im[0]).astype(BF16)

    h0 = jnp.concatenate([jnp.zeros((PAD, D), F32), meta_full, x[0]], axis=0)
    n1 = _rms_fwd(h0, g_mix, "rms_mix")
    z = _mm(n1, w_zf, "nn", LP, NZ, D, BF16, "in_proj")
    fpre = _mm(n1, w_zf, "nn", LP, LANES, D, F32, "in_proj_f", b_off=(0, NZ))
    bf_pad = jnp.pad(b_f, ((0, 0), (0, LANES - H)))
    fcum = _fgate_fwd(fpre, bf_pad, PAD, "fgate_fwd")
    key_bias = jnp.where(jnp.arange(LP)[:, None] >= PAD, -fcum, NEG)
    bias_t = key_bias.T[:H].reshape(H, 1, LP)
    attn, attn_f32, lse_t, *rest = _attn_fwd(z, bias_t, H, PAD, "attn_fwd", gather=[slots[n] for n in big[1:]])
    gathered.update(zip(big[1:], rest))
    w_glu_c, w_ao_c, w_up_c = gathered["w_glu"], gathered["w_attn_o"], gathered["w_up"]
    w_out_f = gathered["w_out"].reshape(D, D)
    w_down_f = gathered["w_down"].reshape(DFF, D)
    ao = _mm(attn, w_ao_c, "nn", LP, D, DA, BF16, "attn_out", b_chips=N_AO)
    y, yg, hs_re, hs_im = _ssm_fwd(z, U0, coef, bbr_cs, bbi_cs, ccr_sc, cci_sc, d_skip, "ssm_fwd")
    yab = _mm(yg, w_glu_c, "nn", LP, 2 * D, DS, BF16, "glu_proj", b_chips=N_GLU)
    merged = _merge_fwd(yab, z, ao, D, GA0, GB0, "merge_fwd")
    h1 = _mm(merged, w_out_f, "nn", LP, D, D, F32, "out_proj", res=h0)
    n2 = _rms_fwd(h1, g_ffn, "rms_ffn")
    gu = _mm(n2, w_up_c, "nn", LP, 2 * DFF, D, BF16, "up_proj", tn=1408, b_chips=N_UP)
    act = _convact_fwd(gu, conv_w_f, conv_b, DFF, "convact_fwd")
    h2 = _mm(act, w_down_f, "nn", LP, D, DFF, F32, "down_proj", res=h1, tn=512, tk=DFF)
    dh2, dg_final, loss_v = _final_loss(h2, g_final.reshape(1, D), loss_target[0], OFF, "final_loss")
    loss = lax.psum(loss_v[0, 0], ("x", "y", "c"))

    KW = dict(tm=512, tn=512, tk=LP)
    dact = _mm(dh2, w_down_f, "nt", LP, DFF, D, BF16, "down_bwd_x")
    dw_down = _mm(act, dh2, "tn", DFF, D, LP, F32, "down_bwd_w", **KW)
    dgc, du_ffn = _convact_bwd(dact, gu, conv_w_f, conv_b, DFF, "convact_bwd")
    dg_ffn_in, dconv_w, dconv_b = _conv_bwd(dgc, gu, conv_w_f, DFF, PAD, "conv_bwd")
    dn2 = _mm(dg_ffn_in, w_up_c, "nt", LP, D, DFF, F32, "up_bwd_x_g", tn=512, tk=N_UP, b_chips=N_UP)
    dn2 = _mm(du_ffn, w_up_c, "nt", LP, D, DFF, F32, "up_bwd_x_u", res=dn2, b_off=(0, DFF), tn=512, tk=N_UP, b_chips=N_UP)
    dw_up = _mm(n2, dg_ffn_in, "tn", D, DFF, LP, F32, "up_bwd_w_g", tm=512, tn=256, tk=LP, out_chips=N_UP,
                out_into=(jnp.zeros((4, D, N_UP), F32), 0))
    dw_up = _mm(n2, du_ffn, "tn", D, DFF, LP, F32, "up_bwd_w_u", tm=512, tn=256, tk=LP, out_chips=N_UP,
                out_into=(dw_up, DFF // N_UP))
    dh1, dg_ffn = _rms_bwd(h1, g_ffn, dn2, dh2, "rms_ffn_bwd")
    c_idx = core.reshape(1).astype(jnp.int32)
    chip_c = jnp.stack([chip, core]).astype(jnp.int32)

    dmerged = _mm(dh1, w_out_f, "nt", LP, D, D, F32, "out_bwd_x")
    dw_out = _mm(merged, dh1, "tn", D, D, LP, F32, "out_bwd_w", **KW)
    dya, dyb, dga, dgb, dao = _merge_bwd(dmerged, yab, z, ao, D, GA0, GB0, "merge_bwd")
    dattn = _mm(dao, w_ao_c, "nt", LP, DA, D, BF16, "attn_out_bwd_x", b_chips=N_AO)
    dw_ao = _mm(attn, dao, "tn", DA, D, LP, F32, "attn_out_bwd_w", out_chips=N_AO, **KW)
    dyg = _mm(dya, w_glu_c, "nt", LP, DS, D, F32, "glu_bwd_x_a", b_chips=N_GLU)
    dyg = _mm(dyb, w_glu_c, "nt", LP, DS, D, F32, "glu_bwd_x_b", res=dyg, b_off=(0, D), b_chips=N_GLU)
    dw_glu = _mm(yg, dya, "tn", DS, D, LP, F32, "glu_bwd_w_a", out_chips=N_GLU,
                 out_into=(jnp.zeros((4, DS, N_GLU), F32), 0), **KW)
    dw_glu = _mm(yg, dyb, "tn", DS, D, LP, F32, "glu_bwd_w_b", out_chips=N_GLU, out_into=(dw_glu, D // N_GLU), **KW)
    early = ["w_glu", "w_attn_o", "w_out", "w_up", "w_down"]
    early_grads = [dw_glu, dw_ao, dw_out.reshape(4, D // 4, D), dw_up, dw_down.reshape(4, DFF // 4, D)]
    (du_ssm, dbbr_d, dbbi_d, dccr_d, dcci_d, dar_b, dai_b, dd_skip, *early_got) = _ssm_bwd(
        z, U0, dyg, y, hs_re, hs_im, coef_rev, bbr_sc, bbi_sc, ccr_cs, cci_cs, d_skip, "ssm_bwd",
        ride=_sibling_ride(early_grads))
    delta_t = _attn_delta(dattn, attn_f32, H, "attn_delta")
    early_part = [_add_half(g, t, c_idx, "rs_add_" + n) for n, g, t in zip(early, early_grads, early_got)]
    dq, dk, dv, dbias_t, *early_land = _attn_bwd(z, dattn, lse_t, delta_t, bias_t, H, "attn_bwd", scatter=early_part)
    dF = jnp.pad(-dbias_t[:, 0, :].T, ((0, 0), (0, LANES - H)))
    dfpre, db_f = _fgate_bwd(dF, fpre, bf_pad, PAD, "fgate_bwd")
    dz = jnp.concatenate([dq, dk, dv, du_ssm, dga, dgb, dfpre.astype(BF16)], axis=1)
    dw_zf = _mm(n1, dz, "tn", D, NZ + LANES, LP, F32, "in_bwd_w", tm=512, tn=640, tk=LP)
    dw_in_f = jnp.concatenate([dw_zf[:, :3 * DA], dw_zf[:, NZ:NZ + H], dw_zf[:, 3 * DA:NZ]], axis=1)
    late_grads = [_cols_to_chips(dw_in_f)]
    late_got = _sibling_halves(late_grads, "rs_sibling_w_in")
    late_part = [_add_half(late_grads[0], late_got[0], c_idx, "rs_add_w_in")]
    dn1, *late_land = _mm(dz, w_zf, "nt", LP, D, NZ + LANES, F32, "in_bwd_x", tm=640, tn=256, tk=NZ + LANES,
                          ride=_scatter_ride(late_part))
    dh0, dg_mix = _rms_bwd(h0, g_mix, dn1, dh1, "rms_mix_bwd")
    grad_x = dh0[OFF:][None]
    dmeta_full = dh0[PAD:OFF]

    ext = lambda m, A, B: _block_diag_extract(m, GROUPS_PER_BLOCK, A, B)
    dbb_re = ext(dbbr_d, C, P).transpose(0, 2, 1).reshape(GP, C)
    dbb_im = ext(dbbi_d, C, P).transpose(0, 2, 1).reshape(GP, C)
    dc_re = ext(dccr_d, P, C).transpose(0, 2, 1)[None]
    dc_im = ext(dcci_d, P, C).transpose(0, 2, 1)[None]
    glr, gli, gldt, gbr, gbi = _ssm_prep_bwd(lr_c, li_c, ldt_c, br2, bi2, dar_b.reshape(GP, 1), dai_b.reshape(GP, 1),
                                             dbb_re, dbb_im, "ssm_prep_bwd")
    small_grads = {
        "g_mix": dg_mix, "b_f": db_f[:, :H], "lam_re": glr.reshape(1, G, P), "lam_im": gli.reshape(1, G, P),
        "log_dt": gldt.reshape(G, P).sum(axis=1)[None], "b_re": gbr.reshape(1, G, P, C), "b_im": gbi.reshape(1, G, P, C),
        "c_re": dc_re, "c_im": dc_im, "d_skip": dd_skip, "g_ffn": dg_ffn, "conv_b": dconv_b, "g_final": dg_final.reshape(D),
    }

    small = list(small_grads)
    rider_grads = [dconv_w, dmeta_full]
    small_shapes = [args[n].shape for n in small] + [g.shape for g in rider_grads]
    srows = sum(_rows_for(int(np.prod(sh)), LANES) for sh in small_shapes)
    srows = -(-srows // 256) * 256
    zeros_like_riders = [jnp.zeros(g.shape, F32) for g in rider_grads]
    pack = lambda arrs: _pack_small(arrs, srows)
    g_parts = _gather_all(pack([small_grads[n] for n in small] + rider_grads), "gather_small_grads")
    sm = _sum_adamw(g_parts, pack([args[n] for n in small] + zeros_like_riders),
                    pack([args["m_" + n] for n in small] + zeros_like_riders),
                    pack([args["v_" + n] for n in small] + zeros_like_riders), "small_adamw")
    unpacked = [_unpack_small(p, small_shapes) for p in sm]
    sg, sd, smm, svv = (dict(zip(small, u[:len(small)])) for u in unpacked)
    dconv_w_sum, dmeta_sum = unpacked[0][len(small):]
    n_cw, n_me = conv_w.shape[2], meta.shape[1]
    rider = {"conv_w": lax.dynamic_slice_in_dim(dconv_w_sum, chip * n_cw, n_cw, axis=1)[None],
             "meta": lax.dynamic_slice_in_dim(dmeta_sum, chip * n_me, n_me, axis=1)}

    big = ["w_in"] + early
    halves = [_sum_half(g, t, l_, chip_c, "rs_sum_" + n) for n, g, t, l_ in
              zip(big, late_grads + early_grads, list(late_got) + list(early_got), list(late_land) + list(early_land))]
    shard_grads = dict(zip(big, _sibling_join(halves, "rs_join")))
    shard_grads.update({n: g.reshape(g.shape[-2:]) for n, g in rider.items()})
    bg, bd_, bm, bv = {}, {}, {}, {}
    for n, g in shard_grads.items():
        bd_[n], bm[n], bv[n] = _adamw(args[n], g, args["m_" + n], args["v_" + n], "adamw_" + n)
        bg[n] = g.reshape(args[n].shape)

    order = ["meta", "g_mix", "w_in", "b_f", "lam_re", "lam_im", "log_dt", "b_re", "b_im", "c_re", "c_im", "d_skip",
             "w_glu", "w_attn_o", "w_out", "g_ffn", "w_up", "conv_w", "conv_b", "w_down", "g_final"]
    pick = lambda bigd, smalld, n: bigd[n] if n in bigd else smalld[n]
    outs = [loss, grad_x]
    for bigd, smalld in ((bg, sg), (bd_, sd), (bm, smm), (bv, svv)):
        outs += [pick(bigd, smalld, n) for n in order]
    return tuple(outs)
```

```python
import functools
import math

import jax
import jax.numpy as jnp
import numpy as np
from jax import lax
from jax.experimental import pallas as pl
from jax.experimental.pallas import tpu as pltpu

F32 = jnp.float32
BF16 = jnp.bfloat16
MESH = pl.DeviceIdType.MESH

EPS = 1e-6
HEAD_DIM = 128
LANES = 128
NEG = -1e30
GELU_C = math.sqrt(2.0 / math.pi)
GELU_A = 0.044715
ADAM_LR, ADAM_B1, ADAM_B2, ADAM_EPS, ADAM_WD, ADAM_STEP = 0.001, 0.9, 0.999, 1e-08, 0.01, 10
V7X_VMEM_BYTES = 64 << 20
GROUPS_PER_BLOCK = 8


def _tile(n, pref, mult=LANES):
    if n <= pref:
        return n
    t = (pref // mult) * mult
    while t >= mult:
        if n % t == 0:
            return t
        t -= mult
    raise ValueError(f"no tile for {n} <= {pref} (multiple of {mult})")


def _ctile(pref, *vals):
    g = 0
    for v in vals:
        g = math.gcd(g, v)
    return _tile(g, pref)


def _cparams(sem, est_bytes):
    limit = int(min(max(est_bytes * 1.25 + (4 << 20), 16 << 20), V7X_VMEM_BYTES - (8 << 20)))
    return pltpu.CompilerParams(dimension_semantics=sem, vmem_limit_bytes=limit)


def _sds(shape, dtype):
    return jax.ShapeDtypeStruct(tuple(shape), dtype)


def _sig(x):
    return 0.5 * jnp.tanh(0.5 * x) + 0.5


def _sig_tail(x):
    return 1.0 / (1.0 + jnp.exp(-x))


def _gelu(x):
    t = jnp.tanh(GELU_C * (x + GELU_A * x * x * x))
    return 0.5 * x * (1.0 + t)


def _gelu_grad(x):
    t = jnp.tanh(GELU_C * (x + GELU_A * x * x * x))
    return 0.5 * (1.0 + t) + 0.5 * x * (1.0 - t * t) * GELU_C * (1.0 + 3.0 * GELU_A * x * x)


def _mm(a, b, mode, M, N, K, out_dtype, name, *, res=None, a_off=(0, 0), b_off=(0, 0),
        tm=640, tn=1024, tk=2048, b_chips=None, out_chips=None, out_into=None, ride=None):
    tm, tn, tk = _tile(M, tm, 8 if mode != "tn" else LANES), _tile(N, tn), _tile(K, tk, LANES if mode != "tn" else 8)
    if b_chips is not None and mode == "nt":
        tk = _ctile(tk, tk, b_chips)
    if b_chips is not None and mode != "nt":
        tn = _ctile(tn, tn, b_chips)
    if out_chips is not None:
        tn = _ctile(tn, tn, out_chips)
    nk = K // tk
    ar, ac = a_off
    br, bc = b_off
    if mode == "tn":
        assert ar % tk == 0 and ac % tm == 0
        a_spec = pl.BlockSpec((tk, tm), lambda i, j, k: (k + ar // tk, i + ac // tm))
        a_dims = 0
    else:
        assert ar % tm == 0 and ac % tk == 0
        a_spec = pl.BlockSpec((tm, tk), lambda i, j, k: (i + ar // tm, k + ac // tk))
        a_dims = 1
    if mode == "nt":
        assert br % tn == 0 and bc % tk == 0
        if b_chips is None:
            b_spec = pl.BlockSpec((tn, tk), lambda i, j, k: (j + br // tn, k + bc // tk))
        else:
            per = b_chips // tk
            b_spec = pl.BlockSpec((None, tn, tk), lambda i, j, k: ((k + bc // tk) // per, j + br // tn, (k + bc // tk) % per))
        b_dims = 1
    else:
        assert br % tk == 0 and bc % tn == 0
        if b_chips is None:
            b_spec = pl.BlockSpec((tk, tn), lambda i, j, k: (k + br // tk, j + bc // tn))
        else:
            per = b_chips // tn
            b_spec = pl.BlockSpec((None, tk, tn), lambda i, j, k: ((j + bc // tn) // per, k + br // tk, (j + bc // tn) % per))
        b_dims = 0
    dims = (((a_dims,), (b_dims,)), ((), ()))
    if out_chips is None:
        o_spec = pl.BlockSpec((tm, tn), lambda i, j, k: (i, j))
        o_shape = _sds((M, N), out_dtype)
    else:
        per_o = out_chips // tn
        chip0 = 0 if out_into is None else out_into[1]
        o_spec = pl.BlockSpec((None, tm, tn), lambda i, j, k: (chip0 + j // per_o, i, j % per_o))
        o_shape = _sds((N // out_chips if out_into is None else 4, M, out_chips), out_dtype)
    has_res = res is not None
    has_into = out_into is not None

    r_ins, r_outs, r_sems = _ride_parts(ride)
    n_in = 2 + has_res + has_into
    steps = (M // tm, N // tn, nk)

    def body(*refs):
        a_ref, b_ref = refs[:2]
        r_ref = refs[2] if has_res else None
        o_ref = refs[n_in + len(r_ins)]
        if ride is not None:
            start, finish = ride["fn"](refs[n_in:n_in + len(r_ins)],
                                       refs[n_in + len(r_ins) + 1:n_in + len(r_ins) + 1 + len(r_outs)], *refs[-2:])
            pid = [pl.program_id(d) for d in range(3)]
            pl.when((pid[0] == 0) & (pid[1] == 0) & (pid[2] == 0))(start)
        part = lax.dot_general(a_ref[...].astype(BF16), b_ref[...].astype(BF16), dims, preferred_element_type=F32)

        def write_out(acc):
            if has_res:
                acc = r_ref[...] + acc
            o_ref[...] = acc.astype(o_ref.dtype)

        if nk == 1:
            write_out(part)
        else:
            acc_ref = refs[n_in + len(r_ins) + 1 + len(r_outs)]
            k = pl.program_id(2)

            @pl.when(k == 0)
            def _():
                acc_ref[...] = part

            @pl.when(k > 0)
            def _():
                acc_ref[...] += part

            @pl.when(k == nk - 1)
            def _():
                write_out(acc_ref[...])

        if ride is not None:
            pl.when((pid[0] == steps[0] - 1) & (pid[1] == steps[1] - 1) & (pid[2] == steps[2] - 1))(finish)

    in_specs = ([a_spec, b_spec] + ([o_spec] if has_res else []) + ([pl.BlockSpec(memory_space=pl.ANY)] if has_into else [])
                + [HBM_SPEC] * len(r_ins))
    args = (a, b) + ((res,) if has_res else ()) + ((out_into[0],) if has_into else ()) + tuple(r_ins)
    isz = lambda x: jnp.dtype(x.dtype).itemsize
    est = 2 * (tm * tk * isz(a) + tk * tn * isz(b) + tm * tn * jnp.dtype(out_dtype).itemsize) + tm * tn * 4 * (2 + 2 * has_res)
    sem = ("parallel", "parallel", "arbitrary") if ride is None else ("arbitrary",) * 3
    out = pl.pallas_call(
        body, grid=steps, in_specs=in_specs, out_specs=[o_spec] + [HBM_SPEC] * len(r_outs),
        out_shape=[o_shape] + r_outs,
        scratch_shapes=([pltpu.VMEM((tm, tn), F32)] if nk > 1 else []) + r_sems,
        input_output_aliases={2: 0} if has_into else {},
        compiler_params=_cparams(sem, est), name=name,
    )(*args)
    return out[0] if ride is None else out


def _rms_fwd(h, g, name):
    LP, D = h.shape
    tm = _tile(LP, 640, 8)

    def body(h_ref, g_ref, o_ref):
        x = h_ref[...]
        r = lax.rsqrt(jnp.mean(x * x, axis=-1, keepdims=True) + EPS)
        o_ref[...] = (x * r * g_ref[...]).astype(o_ref.dtype)

    row = pl.BlockSpec((tm, D), lambda i: (i, 0))
    return pl.pallas_call(
        body, grid=(LP // tm,), in_specs=[row, pl.BlockSpec((1, D), lambda i: (0, 0))], out_specs=row,
        out_shape=_sds((LP, D), BF16), compiler_params=_cparams(("parallel",), 2 * tm * D * 6), name=name,
    )(h, g)


def _rms_bwd(h, g, dn, dres, name):
    LP, D = h.shape
    tm = _tile(LP, 320, 8)
    nt = LP // tm

    def body(h_ref, g_ref, dn_ref, dres_ref, dh_ref, dg_ref):
        i = pl.program_id(0)
        x = h_ref[...]
        r = lax.rsqrt(jnp.mean(x * x, axis=-1, keepdims=True) + EPS)
        xh = x * r
        dn_v = dn_ref[...]
        dxh = dn_v * g_ref[...]
        dh_ref[...] = dres_ref[...] + r * (dxh - xh * jnp.mean(dxh * xh, axis=-1, keepdims=True))
        part = jnp.sum(dn_v * xh, axis=0, keepdims=True)

        @pl.when(i == 0)
        def _():
            dg_ref[...] = part

        @pl.when(i > 0)
        def _():
            dg_ref[...] += part

    row = pl.BlockSpec((tm, D), lambda i: (i, 0))
    vec = pl.BlockSpec((1, D), lambda i: (0, 0))
    return pl.pallas_call(
        body, grid=(nt,), in_specs=[row, vec, row, row], out_specs=[row, vec],
        out_shape=[_sds((LP, D), F32), _sds((1, D), F32)],
        compiler_params=_cparams(("arbitrary",), 2 * 4 * tm * D * 4), name=name,
    )(h, g, dn, dres)


def _final_loss(h, g, tgt, off, name):
    LP, D = h.shape
    tm = LANES
    assert off % tm == 0
    ob = off // tm
    nt = LP // tm

    def body(h_ref, g_ref, t_ref, dh_ref, dg_ref, loss_ref):
        i = pl.program_id(0)
        x = h_ref[...]
        r = lax.rsqrt(jnp.mean(x * x, axis=-1, keepdims=True) + EPS)
        xh = x * r
        gv = g_ref[...]
        e = xh * gv - t_ref[...]
        valid = i >= ob
        dy = jnp.where(valid, e * (1.0 / D), 0.0)
        lpart = jnp.where(valid, 0.5 * jnp.sum(jnp.mean(e * e, axis=-1, keepdims=True), axis=0, keepdims=True), 0.0)
        dxh = dy * gv
        dh_ref[...] = r * (dxh - xh * jnp.mean(dxh * xh, axis=-1, keepdims=True))
        gpart = jnp.sum(dy * xh, axis=0, keepdims=True)

        @pl.when(i == 0)
        def _():
            dg_ref[...] = gpart
            loss_ref[...] = jnp.broadcast_to(lpart, loss_ref.shape)

        @pl.when(i > 0)
        def _():
            dg_ref[...] += gpart
            loss_ref[...] += jnp.broadcast_to(lpart, loss_ref.shape)

    row = pl.BlockSpec((tm, D), lambda i: (i, 0))
    vec = pl.BlockSpec((1, D), lambda i: (0, 0))
    return pl.pallas_call(
        body, grid=(nt,),
        in_specs=[row, vec, pl.BlockSpec((tm, D), lambda i: (jnp.maximum(i - ob, 0), 0))],
        out_specs=[row, vec, pl.BlockSpec((1, LANES), lambda i: (0, 0))],
        out_shape=[_sds((LP, D), F32), _sds((1, D), F32), _sds((1, LANES), F32)],
        compiler_params=_cparams(("arbitrary",), 2 * 3 * tm * D * 4), name=name,
    )(h, g, tgt)


def _fgate_fwd(fpre, bias, pad, name):
    LP, W = fpre.shape

    def body(f_ref, b_ref, o_ref):
        row8 = lax.broadcasted_iota(jnp.int32, (8, W), 0)
        bv = b_ref[...]

        def step(g, carry):
            r0 = pl.multiple_of(g * 8, 8)
            x = f_ref[pl.ds(r0, 8), :] + bv
            lf = jnp.minimum(x, 0.0) - jnp.log(1.0 + jnp.exp(-jnp.abs(x)))
            lf = jnp.where(r0 + row8 >= pad, lf, 0.0)
            for k in (1, 2, 4):
                lf = lf + jnp.where(row8 >= k, pltpu.roll(lf, k, 0), 0.0)
            lf = lf + carry
            o_ref[pl.ds(r0, 8), :] = lf
            return jnp.broadcast_to(lf[7:8, :], (8, W))

        lax.fori_loop(0, LP // 8, step, jnp.zeros((8, W), F32))

    return pl.pallas_call(
        body, out_shape=_sds((LP, W), F32),
        compiler_params=_cparams(None, 3 * LP * W * 4), name=name,
    )(fpre, bias)


def _fgate_bwd(dF, fpre, bias, pad, name):
    LP, W = fpre.shape
    ng = LP // 8

    def body(d_ref, f_ref, b_ref, o_ref, db_ref):
        row8 = lax.broadcasted_iota(jnp.int32, (8, W), 0)
        bv = b_ref[...]

        def step(t, carry):
            run, acc = carry
            g = ng - 1 - t
            r0 = pl.multiple_of(g * 8, 8)
            x = d_ref[pl.ds(r0, 8), :]
            for k in (1, 2, 4):
                x = x + jnp.where(row8 < 8 - k, pltpu.roll(x, 8 - k, 0), 0.0)
            x = x + run
            df = x * _sig_tail(-(f_ref[pl.ds(r0, 8), :] + bv))
            df = jnp.where(r0 + row8 >= pad, df, 0.0)
            o_ref[pl.ds(r0, 8), :] = df
            return jnp.broadcast_to(x[0:1, :], (8, W)), acc + df

        _, acc = lax.fori_loop(0, ng, step, (jnp.zeros((8, W), F32), jnp.zeros((8, W), F32)))
        db_ref[...] = jnp.sum(acc, axis=0, keepdims=True)

    return pl.pallas_call(
        body, out_shape=[_sds((LP, W), F32), _sds((1, W), F32)],
        compiler_params=_cparams(None, 4 * LP * W * 4), name=name,
    )(dF, fpre, bias)


def _place():
    return lax.axis_index("x"), lax.axis_index("y"), lax.axis_index("c")


def _other_chips(x, y):
    return [(1 - x, y), (x, 1 - y), (1 - x, 1 - y)]


def _gather_phases(g_refs, send_sems, recv_sems):
    nw = len(g_refs)
    x, y, c = _place()
    chips = _other_chips(x, y)
    me = 2 * x + y

    def copy(i, k, chip, half, to):
        HR = g_refs[i].shape[1] // 2
        rows = g_refs[i].at[chip, pl.ds(pl.multiple_of(half * HR, 16), HR), :]
        return pltpu.make_async_remote_copy(
            src_ref=rows, dst_ref=rows, send_sem=send_sems.at[6 * i + k], recv_sem=recv_sems.at[6 * i + k],
            device_id=to, device_id_type=MESH)

    pairs = [(i, k, cx, cy) for i in range(nw) for k, (cx, cy) in enumerate(chips)]

    def start():
        for i, k, cx, cy in pairs:
            copy(i, k, me, c, (cx, cy, c)).start()

    def forward():
        for i, k, cx, cy in pairs:
            copy(i, k, 2 * cx + cy, c, (cx, cy, c)).wait_recv()
            copy(i, 3 + k, 2 * cx + cy, c, (x, y, 1 - c)).start()

    def finish():
        for i, k, cx, cy in pairs:
            copy(i, 3 + k, 2 * cx + cy, 1 - c, (x, y, 1 - c)).wait_recv()
        for i, k, cx, cy in pairs:
            copy(i, k, me, c, (cx, cy, c)).wait_send()
            copy(i, 3 + k, 2 * cx + cy, c, (x, y, 1 - c)).wait_send()

    return start, forward, finish


def _scatter_phases(s_refs, land_refs, send_sems, recv_sems):
    x, y, c = _place()
    chips = _other_chips(x, y)

    def copy(i, k, cx, cy):
        return pltpu.make_async_remote_copy(
            src_ref=s_refs[i].at[2 * cx + cy], dst_ref=land_refs[i].at[k],
            send_sem=send_sems.at[3 * i + k], recv_sem=recv_sems.at[3 * i + k],
            device_id=(cx, cy, c), device_id_type=MESH)

    pairs = [(i, k, cx, cy) for i in range(len(s_refs)) for k, (cx, cy) in enumerate(chips)]

    def start():
        for p in pairs:
            copy(*p).start()

    def finish():
        for p in pairs:
            copy(*p).wait_recv()
        for p in pairs:
            copy(*p).wait_send()

    return start, finish


def _sibling_phases(g_refs, land_refs, send_sems, recv_sems):
    x, y, c = _place()

    def copy(i):
        HR = land_refs[i].shape[1]
        q0 = pl.multiple_of((1 - c) * HR, 8)
        return pltpu.make_async_remote_copy(
            src_ref=g_refs[i].at[pl.ds(0, 4), pl.ds(q0, HR), :], dst_ref=land_refs[i],
            send_sem=send_sems.at[i], recv_sem=recv_sems.at[i], device_id=(x, y, 1 - c), device_id_type=MESH)

    def start():
        for i in range(len(g_refs)):
            copy(i).start()

    def finish():
        for i in range(len(g_refs)):
            copy(i).wait()

    return start, finish


def _sibling_ride(gs):
    return dict(fn=_sibling_phases, ins=list(gs), outs=[_sds((4, g.shape[1] // 2, g.shape[2]), g.dtype) for g in gs],
                sems=len(gs))


def _scatter_ride(ss):
    return dict(fn=_scatter_phases, ins=list(ss), outs=[_sds((3,) + p.shape[1:], p.dtype) for p in ss], sems=3 * len(ss))


def _ride_parts(ride):
    if ride is None:
        return [], [], []
    return ride["ins"], ride["outs"], [pltpu.SemaphoreType.DMA((ride["sems"],)), pltpu.SemaphoreType.DMA((ride["sems"],))]


HBM_SPEC = pl.BlockSpec(memory_space=pltpu.HBM)


def _col_to_row(col):
    n = col.shape[0]
    return jnp.transpose(jnp.broadcast_to(col, (n, LANES)))[0:1, :]


def _row_to_col(row):
    n = row.shape[1]
    return jnp.transpose(jnp.broadcast_to(row, (LANES, n)))[:, 0:1]


def _attn_fwd(z, bias_t, H, pad, name, gather=()):
    LP = z.shape[0]
    BQ = BK = _tile(LP, 640)
    scale = HEAD_DIM ** -0.5
    NT = (((1,), (1,)), ((), ()))

    nw = len(gather)
    nq = LP // BQ

    def body(*refs):
        q_ref, k_ref, v_ref, b_ref = refs[:4]
        o_ref, of_ref, lse_ref = refs[4 + nw:7 + nw]
        hd = pl.program_id(0)
        qi = pl.program_id(1)
        if nw:
            start, forward, finish = _gather_phases(refs[7 + nw:7 + 2 * nw], *refs[7 + 2 * nw:])
            pl.when((hd == 0) & (qi == 0))(start)
            pl.when((hd == H // 2) & (qi == 0))(forward)
        q = q_ref[...]

        def tile(kb, carry, masked):
            m, l, acc = carry
            k0 = pl.multiple_of(kb * BK, BK)
            s = lax.dot_general(q, k_ref[pl.ds(k0, BK), :], NT, preferred_element_type=F32) * scale
            s = s + b_ref[:, pl.ds(k0, BK)]
            if masked:
                ri = lax.broadcasted_iota(jnp.int32, (BQ, BK), 0)
                ci = lax.broadcasted_iota(jnp.int32, (BQ, BK), 1)
                s = jnp.where(ri >= ci, s, NEG)
            mn = jnp.maximum(m, jnp.max(s, axis=-1, keepdims=True))
            p = jnp.exp(s - mn)
            alpha = jnp.exp(m - mn)
            l = alpha * l + jnp.sum(p, axis=-1, keepdims=True)
            vk = v_ref[pl.ds(k0, BK), :]
            p_hi = p.astype(BF16)
            p_lo = (p - p_hi.astype(F32)).astype(BF16)
            pv = jnp.dot(p_hi, vk, preferred_element_type=F32) + jnp.dot(p_lo, vk, preferred_element_type=F32)
            return mn, l, alpha * acc + pv

        carry = (jnp.full((BQ, 1), NEG, F32), jnp.zeros((BQ, 1), F32), jnp.zeros((BQ, HEAD_DIM), F32))
        carry = lax.fori_loop(0, qi, lambda kb, c: tile(kb, c, False), carry)
        m, l, acc = tile(qi, carry, True)
        rows = qi * BQ + lax.broadcasted_iota(jnp.int32, (BQ, 1), 0)
        o = jnp.where(rows >= pad, acc / l, 0.0)
        o_ref[...] = o.astype(o_ref.dtype)
        of_ref[...] = o
        lse_ref[...] = _col_to_row(m + jnp.log(l))
        if nw:
            pl.when((hd == H - 1) & (qi == nq - 1))(finish)

    in_specs = [
        pl.BlockSpec((BQ, HEAD_DIM), lambda h, i: (i, h)),
        pl.BlockSpec((LP, HEAD_DIM), lambda h, i: (0, H + h)),
        pl.BlockSpec((LP, HEAD_DIM), lambda h, i: (0, 2 * H + h)),
        pl.BlockSpec((None, 1, LP), lambda h, i: (h, 0, 0)),
    ]
    out_specs = [
        pl.BlockSpec((BQ, HEAD_DIM), lambda h, i: (i, h)),
        pl.BlockSpec((BQ, HEAD_DIM), lambda h, i: (i, h)),
        pl.BlockSpec((None, 1, BQ), lambda h, i: (h, 0, i)),
    ]
    est = 2 * (2 * LP * HEAD_DIM * 2 + 8 * LP * 4) + 20 * BQ * LANES * 4 + 8 * BQ * BK * 4
    sems = [pltpu.SemaphoreType.DMA((6 * nw,)), pltpu.SemaphoreType.DMA((6 * nw,))] if nw else []
    return pl.pallas_call(
        body, grid=(H, nq), in_specs=in_specs + [HBM_SPEC] * nw, out_specs=out_specs + [HBM_SPEC] * nw,
        out_shape=[_sds((LP, H * HEAD_DIM), BF16), _sds((LP, H * HEAD_DIM), F32), _sds((H, 1, LP), F32)]
        + [_sds(g.shape, g.dtype) for g in gather],
        scratch_shapes=sems, input_output_aliases={4 + i: 3 + i for i in range(nw)},
        compiler_params=_cparams(("arbitrary", "arbitrary"), est), name=name,
    )(z, z, z, bias_t, *gather)


def _attn_delta(do, o, H, name):
    LP = do.shape[0]
    tm = _tile(LP, 640)

    def body(do_ref, o_ref, d_ref):
        d_ref[...] = _col_to_row(jnp.sum(do_ref[...].astype(F32) * o_ref[...].astype(F32), axis=-1, keepdims=True))

    blk = pl.BlockSpec((tm, HEAD_DIM), lambda h, i: (i, h))
    return pl.pallas_call(
        body, grid=(H, LP // tm), in_specs=[blk, blk],
        out_specs=pl.BlockSpec((None, 1, tm), lambda h, i: (h, 0, i)),
        out_shape=_sds((H, 1, LP), F32),
        compiler_params=_cparams(("parallel", "parallel"), 8 * tm * LANES * 4), name=name,
    )(do, o)


def _attn_bwd(z, do, lse_t, delta_t, bias_t, H, name, scatter=()):
    LP = z.shape[0]
    BK = BQ = _tile(LP, 640)
    nk = nq = LP // BK
    scale = HEAD_DIM ** -0.5
    NT = (((1,), (1,)), ((), ()))
    TN = (((0,), (0,)), ((), ()))

    nw = len(scatter)

    def body(*refs):
        q_ref, k_ref, v_ref, do_ref, lse_ref, dl_ref, b_ref = refs[:7]
        dq_ref, dk_ref, dv_ref, db_ref = refs[7 + nw:11 + nw]
        dq_acc = refs[11 + 2 * nw]
        hd = pl.program_id(0)
        kj = pl.program_id(1)
        if nw:
            start, finish = _scatter_phases(refs[7:7 + nw], refs[11 + nw:11 + 2 * nw], *refs[12 + 2 * nw:])
            pl.when((hd == 0) & (kj == 0))(start)

        @pl.when(kj == 0)
        def _():
            dq_acc[...] = jnp.zeros_like(dq_acc)

        k = k_ref[...]
        v = v_ref[...]
        bcol = _row_to_col(b_ref[:, pl.ds(pl.multiple_of(kj * BK, BK), BK)])

        def tile(qc, carry, masked):
            dk, dv, db = carry
            q0 = pl.multiple_of(qc * BQ, BQ)
            q = q_ref[pl.ds(q0, BQ), :]
            dout = do_ref[pl.ds(q0, BQ), :]
            st = lax.dot_general(k, q, NT, preferred_element_type=F32) * scale + bcol
            if masked:
                ri = lax.broadcasted_iota(jnp.int32, (BK, BQ), 0)
                ci = lax.broadcasted_iota(jnp.int32, (BK, BQ), 1)
                st = jnp.where(ci >= ri, st, NEG)
            pt = jnp.exp(st - lse_ref[:, pl.ds(q0, BQ)])
            dv = dv + jnp.dot(pt.astype(BF16), dout, preferred_element_type=F32)
            dpt = lax.dot_general(v, dout, NT, preferred_element_type=F32)
            dst = pt * (dpt - dl_ref[:, pl.ds(q0, BQ)])
            db = db + jnp.sum(dst, axis=-1, keepdims=True)
            dsb = (dst * scale).astype(BF16)
            dk = dk + jnp.dot(dsb, q, preferred_element_type=F32)
            dq_acc[pl.ds(q0, BQ), :] += lax.dot_general(dsb, k, TN, preferred_element_type=F32)
            return dk, dv, db

        carry = (jnp.zeros((BK, HEAD_DIM), F32), jnp.zeros((BK, HEAD_DIM), F32), jnp.zeros((BK, 1), F32))
        carry = tile(kj, carry, True)
        dk, dv, db = lax.fori_loop(kj + 1, nq, lambda qc, c: tile(qc, c, False), carry)
        dk_ref[...] = dk.astype(dk_ref.dtype)
        dv_ref[...] = dv.astype(dv_ref.dtype)
        db_ref[...] = _col_to_row(db)

        @pl.when(kj == nk - 1)
        def _():
            dq_ref[...] = dq_acc[...].astype(dq_ref.dtype)

        if nw:
            pl.when((hd == H - 1) & (kj == nk - 1))(finish)

    full = lambda c0: pl.BlockSpec((LP, HEAD_DIM), lambda h, j: (0, c0 + h))
    blk = lambda c0: pl.BlockSpec((BK, HEAD_DIM), lambda h, j: (j, c0 + h))
    vec = pl.BlockSpec((None, 1, LP), lambda h, j: (h, 0, 0))
    in_specs = [full(0), blk(H), blk(2 * H), full(0), vec, vec, vec]
    out_specs = [full(0), blk(0), blk(0), pl.BlockSpec((None, 1, BK), lambda h, j: (h, 0, j))]
    est = 2 * (3 * LP * HEAD_DIM * 2 + 16 * LP * 4) + LP * HEAD_DIM * 4 + 24 * BK * LANES * 4 + 10 * BK * BQ * 4
    sems = [pltpu.SemaphoreType.DMA((3 * nw,)), pltpu.SemaphoreType.DMA((3 * nw,))] if nw else []
    return pl.pallas_call(
        body, grid=(H, nk), in_specs=in_specs + [HBM_SPEC] * nw, out_specs=out_specs + [HBM_SPEC] * nw,
        out_shape=[_sds((LP, H * HEAD_DIM), BF16)] * 3 + [_sds((H, 1, LP), F32)]
        + [_sds((3,) + p.shape[1:], p.dtype) for p in scatter],
        scratch_shapes=[pltpu.VMEM((LP, HEAD_DIM), F32)] + sems,
        compiler_params=_cparams(("arbitrary", "arbitrary"), est), name=name,
    )(z, z, z, do, lse_t, delta_t, bias_t, *scatter)


def _ssm_disc(lr, li, ldt, br, bi):
    dt = jnp.exp(ldt)
    mag = jnp.exp(lr * dt)
    a_re = mag * jnp.cos(li * dt)
    a_im = mag * jnp.sin(li * dt)
    den = lr * lr + li * li
    nr = a_re - 1.0
    z_re = (nr * lr + a_im * li) / den
    z_im = (a_im * lr - nr * li) / den
    return a_re, a_im, z_re * br - z_im * bi, z_re * bi + z_im * br


def _ssm_prep(lr, li, ldt, br, bi, name):
    GP, C = br.shape

    def body(lr_ref, li_ref, ldt_ref, br_ref, bi_ref, ar_ref, ai_ref, bbr_ref, bbi_ref, pr_ref, pi_ref):
        a_re, a_im, bb_re, bb_im = _ssm_disc(lr_ref[...], li_ref[...], ldt_ref[...], br_ref[...], bi_ref[...])
        ar_ref[...] = a_re
        ai_ref[...] = a_im
        bbr_ref[...] = bb_re
        bbi_ref[...] = bb_im
        lane = lax.broadcasted_iota(jnp.int32, (tg, 8), 1)
        pr, pi_ = a_re, a_im
        accr = jnp.zeros((tg, 8), F32)
        acci = jnp.zeros((tg, 8), F32)
        for k in range(8):
            accr = jnp.where(lane == k, pr, accr)
            acci = jnp.where(lane == k, pi_, acci)
            pr, pi_ = pr * a_re - pi_ * a_im, pr * a_im + pi_ * a_re
        pr_ref[...] = accr
        pi_ref[...] = acci

    tg = _tile(GP, 512, 8)
    blk = lambda w: pl.BlockSpec((tg, w), lambda i: (i, 0))
    col = _sds((GP, 1), F32)
    return pl.pallas_call(
        body, grid=(GP // tg,), in_specs=[blk(1), blk(1), blk(1), blk(C), blk(C)],
        out_specs=[blk(1), blk(1), blk(C), blk(C), blk(8), blk(8)],
        out_shape=[col, col, _sds((GP, C), F32), _sds((GP, C), F32), _sds((GP, 8), F32), _sds((GP, 8), F32)],
        compiler_params=_cparams(("parallel",), 48 * tg * LANES * 4), name=name,
    )(lr, li, ldt, br, bi)


def _ssm_prep_bwd(lr, li, ldt, br, bi, dar, dai, dbbr, dbbi, name):
    GP, C = br.shape

    def body(lr_ref, li_ref, ldt_ref, br_ref, bi_ref, dar_ref, dai_ref, dbbr_ref, dbbi_ref,
             glr_ref, gli_ref, gldt_ref, gbr_ref, gbi_ref):
        _, vjp = jax.vjp(_ssm_disc, lr_ref[...], li_ref[...], ldt_ref[...], br_ref[...], bi_ref[...])
        glr, gli, gldt, gbr, gbi = vjp((dar_ref[...], dai_ref[...], dbbr_ref[...], dbbi_ref[...]))
        glr_ref[...] = glr
        gli_ref[...] = gli
        gldt_ref[...] = gldt
        gbr_ref[...] = gbr
        gbi_ref[...] = gbi

    tg = _tile(GP, 512, 8)
    blk = lambda w: pl.BlockSpec((tg, w), lambda i: (i, 0))
    col = _sds((GP, 1), F32)
    return pl.pallas_call(
        body, grid=(GP // tg,), in_specs=[blk(1), blk(1), blk(1), blk(C), blk(C), blk(1), blk(1), blk(C), blk(C)],
        out_specs=[blk(1), blk(1), blk(1), blk(C), blk(C)],
        out_shape=[col, col, col, _sds((GP, C), F32), _sds((GP, C), F32)],
        compiler_params=_cparams(("parallel",), 96 * tg * LANES * 4), name=name,
    )(lr, li, ldt, br, bi, dar, dai, dbbr, dbbi)


def _cmul_add(xr, xi, mr, mi, sr, si):
    return xr + mr * sr - mi * si, xi + mr * si + mi * sr


def _ssm_fwd(z, u_col0, coef, bbr, bbi, ccr, cci, dskip, name):
    LP = z.shape[0]
    NB, CB, S = bbr.shape
    TS = _tile(LP, 640, 8)
    nt = LP // TS

    def body(u_ref, coef_ref, bbr_ref, bbi_ref, ccr_ref, cci_ref, ds_ref, y_ref, yg_ref, hr_ref, hi_ref, bur, bui, carry):
        i = pl.program_id(1)

        @pl.when(i == 0)
        def _():
            carry[...] = jnp.zeros_like(carry)

        u = u_ref[...]
        bur[...] = jnp.dot(u, bbr_ref[...], preferred_element_type=F32)
        bui[...] = jnp.dot(u, bbi_ref[...], preferred_element_type=F32)

        def step(g, c):
            cr, ci = c
            r0 = pl.multiple_of(g * 8, 8)
            xr = bur[pl.ds(r0, 8), :]
            xi = bui[pl.ds(r0, 8), :]
            for n, k in enumerate((1, 2, 4)):
                xr, xi = _cmul_add(xr, xi, coef_ref[2 * n], coef_ref[2 * n + 1], pltpu.roll(xr, k, 0), pltpu.roll(xi, k, 0))
            xr, xi = _cmul_add(xr, xi, coef_ref[6], coef_ref[7], cr, ci)
            hr_ref[pl.ds(r0, 8), :] = xr
            hi_ref[pl.ds(r0, 8), :] = xi
            return jnp.broadcast_to(xr[7:8, :], (8, S)), jnp.broadcast_to(xi[7:8, :], (8, S))

        cr, ci = lax.fori_loop(0, TS // 8, step, (carry[0], carry[1]))
        carry[0] = cr
        carry[1] = ci
        y = (jnp.dot(hr_ref[...].astype(BF16), ccr_ref[...], preferred_element_type=F32)
             - jnp.dot(hi_ref[...].astype(BF16), cci_ref[...], preferred_element_type=F32)
             + ds_ref[...] * u.astype(F32))
        y_ref[...] = y
        yg_ref[...] = _gelu(y).astype(yg_ref.dtype)

    ucb = u_col0 // CB
    in_specs = [
        pl.BlockSpec((TS, CB), lambda j, i: (i, ucb + j)),
        pl.BlockSpec((None, 8, 8, S), lambda j, i: (j, 0, 0, 0)),
        pl.BlockSpec((None, CB, S), lambda j, i: (j, 0, 0)),
        pl.BlockSpec((None, CB, S), lambda j, i: (j, 0, 0)),
        pl.BlockSpec((None, S, CB), lambda j, i: (j, 0, 0)),
        pl.BlockSpec((None, S, CB), lambda j, i: (j, 0, 0)),
        pl.BlockSpec((1, CB), lambda j, i: (0, j)),
    ]
    yb = pl.BlockSpec((TS, CB), lambda j, i: (i, j))
    hb = pl.BlockSpec((TS, S), lambda j, i: (i, j))
    est = 2 * (2 * TS * S * 4 + 3 * TS * CB * 4 + 8 * 8 * S * 4 + 4 * CB * S * 2) + 3 * TS * S * 4
    return pl.pallas_call(
        body, grid=(NB, nt), in_specs=in_specs, out_specs=[yb, yb, hb, hb],
        out_shape=[_sds((LP, NB * CB), F32), _sds((LP, NB * CB), BF16), _sds((LP, NB * S), F32), _sds((LP, NB * S), F32)],
        scratch_shapes=[pltpu.VMEM((TS, S), F32), pltpu.VMEM((TS, S), F32), pltpu.VMEM((2, 8, S), F32)],
        compiler_params=_cparams(("parallel", "arbitrary"), est), name=name,
    )(z, coef, bbr, bbi, ccr, cci, dskip)


def _ssm_bwd(z, u_col0, dyg, y, hr, hi, coef_rev, bbr_t, bbi_t, ccr_t, cci_t, dskip, name, ride=None):
    LP = z.shape[0]
    NB, S, CB = bbr_t.shape
    TS = _tile(LP, 640, 8)
    nt = LP // TS
    ng = TS // 8
    TN = (((0,), (0,)), ((), ()))

    r_ins, r_outs, r_sems = _ride_parts(ride)

    def body(*refs):
        n_in, n_out = 13 + len(r_ins), 8 + len(r_outs)
        if ride is not None:
            start, finish = ride["fn"](refs[13:n_in], refs[n_in + 8:n_in + n_out], *refs[-2:])
            pl.when((pl.program_id(0) == 0) & (pl.program_id(1) == 0))(start)
        step(*refs[:13], *refs[n_in:n_in + 8], *refs[n_in + n_out:n_in + n_out + 9])
        if ride is not None:
            pl.when((pl.program_id(0) == NB - 1) & (pl.program_id(1) == nt - 1))(finish)

    def step(u_ref, dyg_ref, y_ref, hr_ref, hi_ref, tr_ref, ti_ref, coef_ref, bbr_ref, bbi_ref, ccr_ref, cci_ref, ds_ref,
             du_ref, dbbr_ref, dbbi_ref, dccr_ref, dcci_ref, dar_ref, dai_ref, dd_ref,
             gr, gi, carry, acc_bbr, acc_bbi, acc_ccr, acc_cci, acc_a, acc_d):
        i = pl.program_id(1)

        @pl.when(i == 0)
        def _():
            for ref in (carry, acc_bbr, acc_bbi, acc_ccr, acc_cci, acc_a, acc_d):
                ref[...] = jnp.zeros_like(ref)

        u = u_ref[...]
        dy = dyg_ref[...] * _gelu_grad(y_ref[...])
        dyb = dy.astype(BF16)
        gr[...] = jnp.dot(dyb, ccr_ref[...], preferred_element_type=F32)
        gi[...] = -jnp.dot(dyb, cci_ref[...], preferred_element_type=F32)
        row8 = lax.broadcasted_iota(jnp.int32, (8, S), 0)
        first_chunk = i == nt - 1
        tail_r = jnp.where(first_chunk, 0.0, tr_ref[...])
        tail_i = jnp.where(first_chunk, 0.0, ti_ref[...])

        def scan(t, c):
            cr, ci, sar, sai = c
            g = ng - 1 - t
            r0 = pl.multiple_of(g * 8, 8)
            xr = gr[pl.ds(r0, 8), :]
            xi = gi[pl.ds(r0, 8), :]
            for n, k in enumerate((1, 2, 4)):
                xr, xi = _cmul_add(xr, xi, coef_ref[2 * n], coef_ref[2 * n + 1], pltpu.roll(xr, 8 - k, 0), pltpu.roll(xi, 8 - k, 0))
            xr, xi = _cmul_add(xr, xi, coef_ref[6], coef_ref[7], cr, ci)
            gr[pl.ds(r0, 8), :] = xr
            gi[pl.ds(r0, 8), :] = xi
            p0 = pl.multiple_of(jnp.maximum(g - 1, 0) * 8, 8)
            pr = jnp.where(g > 0, hr_ref[pl.ds(p0, 8), :], tail_r)
            pi_ = jnp.where(g > 0, hi_ref[pl.ds(p0, 8), :], tail_i)
            hpr = pltpu.roll(jnp.where(row8 == 7, pr, hr_ref[pl.ds(r0, 8), :]), 1, 0)
            hpi = pltpu.roll(jnp.where(row8 == 7, pi_, hi_ref[pl.ds(r0, 8), :]), 1, 0)
            sar = sar + xr * hpr + xi * hpi
            sai = sai + xi * hpr - xr * hpi
            return jnp.broadcast_to(xr[0:1, :], (8, S)), jnp.broadcast_to(xi[0:1, :], (8, S)), sar, sai

        zero = jnp.zeros((8, S), F32)
        cr, ci, sar, sai = lax.fori_loop(0, ng, scan, (carry[0], carry[1], zero, zero))
        carry[0] = cr
        carry[1] = ci
        acc_a[0] += sar
        acc_a[1] += sai
        grb = gr[...].astype(BF16)
        gib = gi[...].astype(BF16)
        du = (jnp.dot(grb, bbr_ref[...], preferred_element_type=F32) + jnp.dot(gib, bbi_ref[...], preferred_element_type=F32)
              + ds_ref[...] * dy)
        du_ref[...] = du.astype(du_ref.dtype)
        acc_bbr[...] += lax.dot_general(u, grb, TN, preferred_element_type=F32)
        acc_bbi[...] += lax.dot_general(u, gib, TN, preferred_element_type=F32)
        acc_ccr[...] += lax.dot_general(hr_ref[...].astype(BF16), dyb, TN, preferred_element_type=F32)
        acc_cci[...] -= lax.dot_general(hi_ref[...].astype(BF16), dyb, TN, preferred_element_type=F32)
        acc_d[...] += jnp.sum(dy * u.astype(F32), axis=0, keepdims=True)

        @pl.when(i == nt - 1)
        def _():
            dbbr_ref[...] = acc_bbr[...]
            dbbi_ref[...] = acc_bbi[...]
            dccr_ref[...] = acc_ccr[...]
            dcci_ref[...] = acc_cci[...]
            dar_ref[...] = jnp.sum(acc_a[0], axis=0, keepdims=True)
            dai_ref[...] = jnp.sum(acc_a[1], axis=0, keepdims=True)
            dd_ref[...] = acc_d[...]

    ucb = u_col0 // CB
    rev = lambda i: nt - 1 - i
    tail = lambda j, i: (jnp.maximum(rev(i) * ng - 1, 0), j)
    yb = pl.BlockSpec((TS, CB), lambda j, i: (rev(i), j))
    hb = pl.BlockSpec((TS, S), lambda j, i: (rev(i), j))
    in_specs = [
        pl.BlockSpec((TS, CB), lambda j, i: (rev(i), ucb + j)), yb, yb, hb, hb,
        pl.BlockSpec((8, S), tail), pl.BlockSpec((8, S), tail),
        pl.BlockSpec((None, 8, 8, S), lambda j, i: (j, 0, 0, 0)),
        pl.BlockSpec((None, S, CB), lambda j, i: (j, 0, 0)),
        pl.BlockSpec((None, S, CB), lambda j, i: (j, 0, 0)),
        pl.BlockSpec((None, CB, S), lambda j, i: (j, 0, 0)),
        pl.BlockSpec((None, CB, S), lambda j, i: (j, 0, 0)),
        pl.BlockSpec((1, CB), lambda j, i: (0, j)),
    ]
    mat_cs = pl.BlockSpec((None, CB, S), lambda j, i: (j, 0, 0))
    mat_sc = pl.BlockSpec((None, S, CB), lambda j, i: (j, 0, 0))
    vec_s = pl.BlockSpec((None, 1, S), lambda j, i: (j, 0, 0))
    out_specs = [yb, mat_cs, mat_cs, mat_sc, mat_sc, vec_s, vec_s, pl.BlockSpec((1, CB), lambda j, i: (0, j))]
    out_shape = [_sds((LP, NB * CB), BF16), _sds((NB, CB, S), F32), _sds((NB, CB, S), F32), _sds((NB, S, CB), F32),
                 _sds((NB, S, CB), F32), _sds((NB, 1, S), F32), _sds((NB, 1, S), F32), _sds((1, NB * CB), F32)]
    scratch = [pltpu.VMEM((TS, S), F32), pltpu.VMEM((TS, S), F32), pltpu.VMEM((2, 8, S), F32),
               pltpu.VMEM((CB, S), F32), pltpu.VMEM((CB, S), F32), pltpu.VMEM((S, CB), F32), pltpu.VMEM((S, CB), F32),
               pltpu.VMEM((2, 8, S), F32), pltpu.VMEM((1, CB), F32)]
    est = 2 * (2 * TS * S * 4 + 4 * TS * CB * 4 + 8 * 8 * S * 4 + 12 * CB * S * 4) + 4 * TS * S * 4
    return pl.pallas_call(
        body, grid=(NB, nt), in_specs=in_specs + [HBM_SPEC] * len(r_ins), out_specs=out_specs + [HBM_SPEC] * len(r_outs),
        out_shape=out_shape + r_outs, scratch_shapes=scratch + r_sems,
        compiler_params=_cparams(("arbitrary", "arbitrary"), est), name=name,
    )(z, dyg, y, hr, hi, hr, hi, coef_rev, bbr_t, bbi_t, ccr_t, cci_t, dskip, *r_ins)


def _merge_fwd(yab, z, ao, D, ga0, gb0, name):
    LP = z.shape[0]
    tm = _tile(LP, 640, 8)
    tn = _ctile(512, D, ga0, gb0)
    nj = D // tn

    def body(ya_ref, yb_ref, ga_ref, gb_ref, ao_ref, o_ref):
        f = lambda r: r[...].astype(F32)
        ssm = f(ya_ref) * _sig(f(yb_ref))
        o_ref[...] = (_sig(f(ga_ref)) * ssm + _sig(f(gb_ref)) * f(ao_ref)).astype(o_ref.dtype)

    blk = lambda c0: pl.BlockSpec((tm, tn), lambda i, j: (i, c0 // tn + j))
    return pl.pallas_call(
        body, grid=(LP // tm, nj), in_specs=[blk(0), blk(D), blk(ga0), blk(gb0), blk(0)], out_specs=blk(0),
        out_shape=_sds((LP, D), BF16), compiler_params=_cparams(("parallel", "parallel"), 2 * 6 * tm * tn * 4), name=name,
    )(yab, yab, z, z, ao)


def _merge_bwd(dm, yab, z, ao, D, ga0, gb0, name):
    LP = z.shape[0]
    tm = _tile(LP, 640, 8)
    tn = _ctile(512, D, ga0, gb0)
    nj = D // tn

    def body(dm_ref, ya_ref, yb_ref, ga_ref, gb_ref, ao_ref, dya_ref, dyb_ref, dga_ref, dgb_ref, dao_ref):
        f = lambda r: r[...].astype(F32)
        dmv, ya, ao_v = f(dm_ref), f(ya_ref), f(ao_ref)
        sa, sb, sy = _sig(f(ga_ref)), _sig(f(gb_ref)), _sig(f(yb_ref))
        t = dmv * sa
        dya_ref[...] = (t * sy).astype(BF16)
        dyb_ref[...] = (t * ya * sy * (1.0 - sy)).astype(BF16)
        dga_ref[...] = (dmv * (ya * sy) * sa * (1.0 - sa)).astype(BF16)
        dgb_ref[...] = (dmv * ao_v * sb * (1.0 - sb)).astype(BF16)
        dao_ref[...] = (dmv * sb).astype(BF16)

    blk = lambda c0: pl.BlockSpec((tm, tn), lambda i, j: (i, c0 // tn + j))
    return pl.pallas_call(
        body, grid=(LP // tm, nj), in_specs=[blk(0), blk(0), blk(D), blk(ga0), blk(gb0), blk(0)], out_specs=[blk(0)] * 5,
        out_shape=[_sds((LP, D), BF16)] * 5, compiler_params=_cparams(("parallel", "parallel"), 2 * 11 * tm * tn * 4), name=name,
    )(dm, yab, yab, z, z, ao)


def _shift_down(x, halo, k, row8):
    s = pltpu.roll(x, k, 0)
    top = jnp.where(row8 < k, pltpu.roll(halo, k, 0), s[0:8])
    return jnp.concatenate([top, s[8:]], axis=0) if x.shape[0] > 8 else top


def _shift_up(x, halo, k, row8):
    tm = x.shape[0]
    s = pltpu.roll(x, tm - k, 0)
    bot = jnp.where(row8 >= 8 - k, pltpu.roll(halo, 8 - k, 0), s[tm - 8:])
    return jnp.concatenate([s[:tm - 8], bot], axis=0) if tm > 8 else bot


def _conv_gate(g, halo, w_ref, cb, row8):
    return cb + w_ref[0:1, :] * _shift_down(g, halo, 2, row8) + w_ref[1:2, :] * _shift_down(g, halo, 1, row8) + w_ref[2:3, :] * g


def _convact_fwd(gu, cw, cb, DFF, name):
    LP = gu.shape[0]
    tm = _tile(LP, 640, 8)
    tn = _tile(DFF, 512)
    nj = DFF // tn
    t8 = tm // 8

    def body(g_ref, h_ref, u_ref, w_ref, b_ref, o_ref):
        i = pl.program_id(0)
        row8 = lax.broadcasted_iota(jnp.int32, (8, tn), 0)
        g = g_ref[...].astype(F32)
        halo = jnp.where(i > 0, h_ref[...].astype(F32), 0.0)
        gc = _conv_gate(g, halo, w_ref, b_ref[...], row8)
        o_ref[...] = (gc * _sig(gc) * u_ref[...].astype(F32)).astype(o_ref.dtype)

    in_specs = [
        pl.BlockSpec((tm, tn), lambda i, j: (i, j)),
        pl.BlockSpec((8, tn), lambda i, j: (jnp.maximum(i * t8 - 1, 0), j)),
        pl.BlockSpec((tm, tn), lambda i, j: (i, nj + j)),
        pl.BlockSpec((3, tn), lambda i, j: (0, j)),
        pl.BlockSpec((1, tn), lambda i, j: (0, j)),
    ]
    return pl.pallas_call(
        body, grid=(LP // tm, nj), in_specs=in_specs, out_specs=pl.BlockSpec((tm, tn), lambda i, j: (i, j)),
        out_shape=_sds((LP, DFF), BF16), compiler_params=_cparams(("parallel", "parallel"), 2 * 8 * tm * tn * 4), name=name,
    )(gu, gu, gu, cw, cb)


def _convact_bwd(dact, gu, cw, cb, DFF, name):
    LP = gu.shape[0]
    tm = _tile(LP, 640, 8)
    tn = _tile(DFF, 512)
    nj = DFF // tn
    t8 = tm // 8

    def body(da_ref, g_ref, h_ref, u_ref, w_ref, b_ref, dgc_ref, du_ref):
        i = pl.program_id(0)
        row8 = lax.broadcasted_iota(jnp.int32, (8, tn), 0)
        g = g_ref[...].astype(F32)
        halo = jnp.where(i > 0, h_ref[...].astype(F32), 0.0)
        gc = _conv_gate(g, halo, w_ref, b_ref[...], row8)
        sg = _sig(gc)
        da = da_ref[...].astype(F32)
        du_ref[...] = (da * gc * sg).astype(du_ref.dtype)
        dgc_ref[...] = (da * u_ref[...].astype(F32) * sg * (1.0 + gc * (1.0 - sg))).astype(dgc_ref.dtype)

    blk = pl.BlockSpec((tm, tn), lambda i, j: (i, j))
    in_specs = [
        blk, blk,
        pl.BlockSpec((8, tn), lambda i, j: (jnp.maximum(i * t8 - 1, 0), j)),
        pl.BlockSpec((tm, tn), lambda i, j: (i, nj + j)),
        pl.BlockSpec((3, tn), lambda i, j: (0, j)),
        pl.BlockSpec((1, tn), lambda i, j: (0, j)),
    ]
    return pl.pallas_call(
        body, grid=(LP // tm, nj), in_specs=in_specs, out_specs=[blk, blk],
        out_shape=[_sds((LP, DFF), BF16), _sds((LP, DFF), BF16)],
        compiler_params=_cparams(("parallel", "parallel"), 2 * 10 * tm * tn * 4), name=name,
    )(dact, gu, gu, gu, cw, cb)


def _conv_bwd(dgc, gu, cw, DFF, pad, name):
    LP = gu.shape[0]
    tm = _tile(LP, 640, 8)
    tn = _tile(DFF, 512)
    nj = DFF // tn
    t8 = tm // 8
    nt = LP // tm

    def body(d_ref, dn_ref, g_ref, h_ref, w_ref, dg_ref, dw_ref, db_ref):
        i = pl.program_id(1)
        row8 = lax.broadcasted_iota(jnp.int32, (8, tn), 0)
        d = d_ref[...].astype(F32)
        nxt = jnp.where(i < nt - 1, dn_ref[...].astype(F32), 0.0)
        dg = w_ref[2:3, :] * d + w_ref[1:2, :] * _shift_up(d, nxt, 1, row8) + w_ref[0:1, :] * _shift_up(d, nxt, 2, row8)
        rows = i * tm + lax.broadcasted_iota(jnp.int32, (tm, 1), 0)
        dg_ref[...] = jnp.where(rows >= pad, dg, 0.0).astype(dg_ref.dtype)
        g = g_ref[...].astype(F32)
        halo = jnp.where(i > 0, h_ref[...].astype(F32), 0.0)
        row3 = lax.broadcasted_iota(jnp.int32, (3, tn), 0)
        s0 = jnp.sum(d * _shift_down(g, halo, 2, row8), axis=0, keepdims=True)
        s1 = jnp.sum(d * _shift_down(g, halo, 1, row8), axis=0, keepdims=True)
        s2 = jnp.sum(d * g, axis=0, keepdims=True)
        dw = jnp.where(row3 == 0, s0, jnp.where(row3 == 1, s1, s2))
        dbp = jnp.sum(d, axis=0, keepdims=True)

        @pl.when(i == 0)
        def _():
            dw_ref[...] = dw
            db_ref[...] = dbp

        @pl.when(i > 0)
        def _():
            dw_ref[...] += dw
            db_ref[...] += dbp

    blk = pl.BlockSpec((tm, tn), lambda j, i: (i, j))
    in_specs = [
        blk,
        pl.BlockSpec((8, tn), lambda j, i: (jnp.minimum((i + 1) * t8, LP // 8 - 1), j)),
        blk,
        pl.BlockSpec((8, tn), lambda j, i: (jnp.maximum(i * t8 - 1, 0), j)),
        pl.BlockSpec((3, tn), lambda j, i: (0, j)),
    ]
    out_specs = [blk, pl.BlockSpec((3, tn), lambda j, i: (0, j)), pl.BlockSpec((1, tn), lambda j, i: (0, j))]
    return pl.pallas_call(
        body, grid=(nj, nt), in_specs=in_specs, out_specs=out_specs,
        out_shape=[_sds((LP, DFF), BF16), _sds((3, DFF), F32), _sds((1, DFF), F32)],
        compiler_params=_cparams(("parallel", "arbitrary"), 2 * 10 * tm * tn * 4), name=name,
    )(dgc, dgc, gu, gu, cw)


def _adamw_math(w, g, m, v):
    m = ADAM_B1 * m + (1.0 - ADAM_B1) * g
    v = ADAM_B2 * v + (1.0 - ADAM_B2) * (g * g)
    m_hat = m / (1.0 - ADAM_B1 ** ADAM_STEP)
    v_hat = v / (1.0 - ADAM_B2 ** ADAM_STEP)
    delta = -ADAM_LR * (m_hat / (jnp.sqrt(v_hat) + ADAM_EPS) + ADAM_WD * w)
    return delta, m, v


def _adamw(w, g, m, v, name):
    R, C = g.shape
    tm = R if R * C * 4 <= (1 << 20) else _tile(R, max(8, ((1 << 20) // (C * 4)) // 8 * 8), 8)

    def body(w_ref, g_ref, m_ref, v_ref, d_ref, mo_ref, vo_ref):
        d_ref[...], mo_ref[...], vo_ref[...] = _adamw_math(w_ref[...], g_ref[...], m_ref[...], v_ref[...])

    blk = pl.BlockSpec((tm, C), lambda i: (i, 0))
    wblk = blk if w.ndim == 2 else pl.BlockSpec((None, tm, C), lambda i: (0, i, 0))
    return pl.pallas_call(
        body, grid=(R // tm,), in_specs=[wblk, blk, wblk, wblk], out_specs=[wblk] * 3, out_shape=[_sds(w.shape, F32)] * 3,
        compiler_params=_cparams(("parallel",), 2 * 7 * tm * (C + LANES) * 4), name=name,
    )(w, g, m, v)


def _sum_adamw(parts, w, m, v, name):
    n, R, C = parts.shape
    tm = _tile(R, 256, 8)

    def body(p_ref, w_ref, m_ref, v_ref, g_ref, d_ref, mo_ref, vo_ref):
        g = p_ref[0]
        for k in range(1, n):
            g = g + p_ref[k]
        g_ref[...] = g
        d_ref[...], mo_ref[...], vo_ref[...] = _adamw_math(w_ref[...], g, m_ref[...], v_ref[...])

    blk = pl.BlockSpec((tm, C), lambda i: (i, 0))
    return pl.pallas_call(
        body, grid=(R // tm,), in_specs=[pl.BlockSpec((n, tm, C), lambda i: (0, i, 0))] + [blk] * 3, out_specs=[blk] * 4,
        out_shape=[_sds((R, C), F32)] * 4,
        compiler_params=_cparams(("parallel",), 2 * (n + 7) * tm * C * 4), name=name,
    )(parts, w, m, v)


def _add_half(g, got, c_idx, name):
    n, R, C = g.shape
    HR = R // 2
    tm = _tile(HR, max(8, ((1 << 20) // (C * 4)) // 8 * 8), 8)
    nb = HR // tm

    def body(c_ref, g_ref, t_ref, o_ref):
        o_ref[...] = (g_ref[...] + t_ref[...]).astype(o_ref.dtype)

    grid_spec = pltpu.PrefetchScalarGridSpec(
        num_scalar_prefetch=1, grid=(n, nb),
        in_specs=[pl.BlockSpec((None, tm, C), lambda k, i, c: (k, c[0] * nb + i, 0)),
                  pl.BlockSpec((None, tm, C), lambda k, i, c: (k, i, 0))],
        out_specs=pl.BlockSpec((None, tm, C), lambda k, i, c: (k, i, 0)))
    return pl.pallas_call(
        body, grid_spec=grid_spec, out_shape=_sds((n, HR, C), BF16),
        compiler_params=_cparams(("parallel", "parallel"), 2 * 3 * tm * (C + LANES) * 4), name=name,
    )(c_idx, g, got)


def _sum_half(g, got, land, chip_c, name):
    n, R, C = g.shape
    HR = R // 2
    tm = _tile(HR, max(8, ((1 << 20) // (C * 4)) // 8 * 8), 8)
    nb = HR // tm

    def body(s_ref, g_ref, t_ref, l_ref, o_ref):
        acc = g_ref[...] + t_ref[...]
        for k in range(3):
            acc = acc + l_ref[k].astype(F32)
        o_ref[...] = acc

    grid_spec = pltpu.PrefetchScalarGridSpec(
        num_scalar_prefetch=1, grid=(nb,),
        in_specs=[pl.BlockSpec((None, tm, C), lambda i, sc: (sc[0], sc[1] * nb + i, 0)),
                  pl.BlockSpec((None, tm, C), lambda i, sc: (sc[0], i, 0)),
                  pl.BlockSpec((3, tm, C), lambda i, sc: (0, i, 0))],
        out_specs=pl.BlockSpec((tm, C), lambda i, sc: (sc[1] * nb + i, 0)))
    return pl.pallas_call(
        body, grid_spec=grid_spec, out_shape=_sds((R, C), F32),
        compiler_params=_cparams(("parallel",), 2 * 6 * tm * (C + LANES) * 4), name=name,
    )(chip_c, g, got, land)


def _cast_slot(w, chip_idx, name):
    R, C = w.shape
    tm = _tile(R, max(16, ((1 << 20) // (C * 4)) // 16 * 16), 16)

    def body(s_ref, w_ref, o_ref):
        o_ref[...] = w_ref[...].astype(o_ref.dtype)

    grid_spec = pltpu.PrefetchScalarGridSpec(
        num_scalar_prefetch=1, grid=(R // tm,),
        in_specs=[pl.BlockSpec((tm, C), lambda i, sc: (i, 0))],
        out_specs=pl.BlockSpec((None, tm, C), lambda i, sc: (sc[0], i, 0)))
    return pl.pallas_call(
        body, grid_spec=grid_spec, out_shape=_sds((4, R, C), BF16),
        compiler_params=_cparams(("parallel",), 2 * 2 * tm * (C + LANES) * 4), name=name,
    )(chip_idx, w)


def _gather_chips(gs, name):
    nw = len(gs)

    def body(*refs):
        start, forward, finish = _gather_phases(refs[nw:2 * nw], *refs[2 * nw:])
        start()
        forward()
        finish()

    return pl.pallas_call(
        body, out_shape=[_sds(g.shape, g.dtype) for g in gs], in_specs=[HBM_SPEC] * nw, out_specs=[HBM_SPEC] * nw,
        scratch_shapes=[pltpu.SemaphoreType.DMA((6 * nw,)), pltpu.SemaphoreType.DMA((6 * nw,))],
        input_output_aliases={i: i for i in range(nw)}, name=name,
    )(*gs)


def _sibling_halves(gs, name):
    nw = len(gs)
    assert all(g.shape[1] % 16 == 0 for g in gs)

    def body(*refs):
        start, finish = _sibling_phases(refs[:nw], refs[nw:2 * nw], *refs[2 * nw:])
        start()
        finish()

    return pl.pallas_call(
        body, out_shape=[_sds((4, g.shape[1] // 2, g.shape[2]), g.dtype) for g in gs],
        in_specs=[HBM_SPEC] * nw, out_specs=[HBM_SPEC] * nw,
        scratch_shapes=[pltpu.SemaphoreType.DMA((nw,)), pltpu.SemaphoreType.DMA((nw,))], name=name,
    )(*gs)


def _sibling_join(fs, name):
    nw = len(fs)
    assert all(f.shape[0] % 16 == 0 for f in fs)

    def body(*refs):
        o_refs = refs[nw:2 * nw]
        send_sems, recv_sems = refs[2 * nw:]
        x, y, c = _place()

        def copy(i, half):
            HR = o_refs[i].shape[0] // 2
            rows = o_refs[i].at[pl.ds(pl.multiple_of(half * HR, 8), HR), :]
            return pltpu.make_async_remote_copy(
                src_ref=rows, dst_ref=rows, send_sem=send_sems.at[i], recv_sem=recv_sems.at[i],
                device_id=(x, y, 1 - c), device_id_type=MESH)

        sends = [copy(i, c) for i in range(nw)]
        for cp in sends:
            cp.start()
        for i in range(nw):
            copy(i, 1 - c).wait_recv()
        for cp in sends:
            cp.wait_send()

    return pl.pallas_call(
        body, out_shape=[_sds(f.shape, f.dtype) for f in fs], in_specs=[HBM_SPEC] * nw, out_specs=[HBM_SPEC] * nw,
        scratch_shapes=[pltpu.SemaphoreType.DMA((nw,)), pltpu.SemaphoreType.DMA((nw,))],
        input_output_aliases={i: i for i in range(nw)}, name=name,
    )(*fs)


def _gather_all(v, name):
    M, W = v.shape

    def body(v_ref, o_ref, send_sems, recv_sems, local_sem):
        x, y, c = _place()
        me, sibling = (x, y, c), (x, y, 1 - c)
        chips = _other_chips(x, y)

        def slot(px, py, pc):
            return o_ref.at[4 * px + 2 * py + pc]

        def copy(k, block, to, src=None):
            return pltpu.make_async_remote_copy(
                src_ref=slot(*block) if src is None else src, dst_ref=slot(*block),
                send_sem=send_sems.at[k], recv_sem=recv_sems.at[k], device_id=to, device_id_type=MESH)

        mine = pltpu.make_async_copy(v_ref, slot(*me), local_sem)
        mine.start()
        first = [copy(0, me, sibling, src=v_ref)]
        first += [copy(1 + j, me, (*chip, c), src=v_ref) for j, chip in enumerate(chips)]
        for cp in first:
            cp.start()
        passed = [copy(4 + j, (*chip, c), sibling) for j, chip in enumerate(chips)]
        for j, chip in enumerate(chips):
            copy(1 + j, (*chip, c), me).wait_recv()
            passed[j].start()
        copy(0, sibling, me).wait_recv()
        for j, chip in enumerate(chips):
            copy(4 + j, (*chip, 1 - c), me).wait_recv()
        for cp in first + passed:
            cp.wait_send()
        mine.wait()

    vm = pl.BlockSpec(memory_space=pltpu.VMEM)
    return pl.pallas_call(
        body, out_shape=_sds((8, M, W), v.dtype), in_specs=[vm], out_specs=vm,
        scratch_shapes=[pltpu.SemaphoreType.DMA((7,)), pltpu.SemaphoreType.DMA((7,)), pltpu.SemaphoreType.DMA(())],
        compiler_params=pltpu.CompilerParams(vmem_limit_bytes=int(min(10 * M * W * 4 + (8 << 20), V7X_VMEM_BYTES - (8 << 20)))),
        name=name,
    )(v)


def _rows_for(n_elems, width, mult=8):
    rows = -(-n_elems // width)
    return -(-rows // mult) * mult


def _pack_small(arrs, total_rows):
    parts = []
    used = 0
    for a in arrs:
        rows = _rows_for(a.size, LANES)
        parts.append(jnp.pad(a.reshape(-1), (0, rows * LANES - a.size)).reshape(rows, LANES))
        used += rows
    if total_rows > used:
        parts.append(jnp.zeros((total_rows - used, LANES), F32))
    return jnp.concatenate(parts, axis=0)


def _unpack_small(p, shapes):
    outs, r = [], 0
    lead = p.shape[:-2]
    for shp in shapes:
        n = int(np.prod(shp))
        rows = _rows_for(n, LANES)
        outs.append(p[..., r:r + rows, :].reshape(lead + (rows * LANES,))[..., :n].reshape(lead + tuple(shp)))
        r += rows
    return outs


def _chip_pieces(lo, hi, n):
    out = []
    while lo < hi:
        q = lo // n
        b = min(hi, (q + 1) * n)
        out.append((q, lo - q * n, b - q * n))
        lo = b
    return out


def _chips_to_cols(w):
    n4, K, n = w.shape
    return w.transpose(1, 0, 2).reshape(K, n4 * n)


def _block_diag(m, gpb):
    G, A, B = m.shape
    nb = G // gpb
    eye = jnp.eye(gpb, dtype=m.dtype)
    t = m.reshape(nb, gpb, A, B)[:, :, :, None, :] * eye[None, :, None, :, None]
    return t.reshape(nb, gpb * A, gpb * B)


def _block_diag_extract(m, gpb, A, B):
    nb = m.shape[0]
    t = m.reshape(nb, gpb, A, gpb, B)
    eye = jnp.eye(gpb, dtype=m.dtype)
    d = jnp.sum(t * eye[None, :, None, :, None], axis=3)
    return d.reshape(nb * gpb, A, B)


def kernel(x, meta, g_mix, w_in, b_f, lam_re, lam_im, log_dt, b_re, b_im, c_re, c_im, d_skip, w_glu, w_attn_o, w_out, g_ffn, w_up, conv_w, conv_b, w_down, g_final, loss_target, m_meta, m_g_mix, m_w_in, m_b_f, m_lam_re, m_lam_im, m_log_dt, m_b_re, m_b_im, m_c_re, m_c_im, m_d_skip, m_w_glu, m_w_attn_o, m_w_out, m_g_ffn, m_w_up, m_conv_w, m_conv_b, m_w_down, m_g_final, v_meta, v_g_mix, v_w_in, v_b_f, v_lam_re, v_lam_im, v_log_dt, v_b_re, v_b_im, v_c_re, v_c_im, v_d_skip, v_w_glu, v_w_attn_o, v_w_out, v_g_ffn, v_w_up, v_conv_w, v_conv_b, v_w_down, v_g_final):
    args = dict(locals())
    L, D = x.shape[1], x.shape[2]
    NM = meta.shape[0]
    H = b_f.shape[1]
    DA = H * HEAD_DIM
    G, P, C = b_re.shape[1:]
    DS, GP = G * C, G * P
    DFF = conv_b.shape[1]
    PAD = (-NM) % LANES
    OFF = PAD + NM
    LP = OFF + L
    NZ = 3 * DA + DS + 2 * D
    U0, GA0, GB0 = 3 * DA, 3 * DA + DS, 3 * DA + DS + D
    NB = G // GROUPS_PER_BLOCK
    chip = 2 * lax.axis_index("x") + lax.axis_index("y")
    core = lax.axis_index("c")

    big = ["w_in", "w_glu", "w_attn_o", "w_out", "w_up", "w_down"]
    local = {n: args[n][0] for n in big}
    chip_idx = chip.reshape(1).astype(jnp.int32)
    slots = {n: _cast_slot(local[n], chip_idx, "cast_" + n) for n in big}
    gathered = {"w_in": _gather_chips([slots["w_in"]], "gather_w_in")[0]}
    tiny_shapes = [conv_w.shape[1:], meta.shape]
    tiny_rows = sum(_rows_for(int(np.prod(sh)), LANES) for sh in tiny_shapes)
    tiny = _gather_all(_pack_small([conv_w[0], meta], tiny_rows), "gather_small_weights")[0::2]
    conv_w_c, meta_c = _unpack_small(tiny, tiny_shapes)
    conv_w_f = _chips_to_cols(conv_w_c)
    meta_full = _chips_to_cols(meta_c)
    g_in = gathered["w_in"]
    n_in = g_in.shape[2]
    cols = lambda lo, hi: [g_in[q, :, a:b] for q, a, b in _chip_pieces(lo, hi, n_in)]
    w_zf = jnp.concatenate(cols(0, 3 * DA) + cols(3 * DA + H, 4 * n_in) + cols(3 * DA, 3 * DA + H)
                           + [jnp.zeros((D, LANES - H), BF16)], axis=1)
    N_GLU, N_AO, N_UP = (slots[n].shape[2] for n in ("w_glu", "w_attn_o", "w_up"))

    col = lambda a: a.reshape(GP, 1)
    lr_c, li_c = col(lam_re[0]), col(lam_im[0])
    ldt_c = jnp.repeat(log_dt[0], P).reshape(GP, 1)
    br2, bi2 = b_re[0].reshape(GP, C), b_im[0].reshape(GP, C)
    a_re, a_im, bb_re, bb_im, pw_re, pw_im = _ssm_prep(lr_c, li_c, ldt_c, br2, bi2, "ssm_prep")
    S = GROUPS_PER_BLOCK * P
    CB = GROUPS_PER_BLOCK * C
    pw_r = pw_re.T.reshape(8, NB, S).transpose(1, 0, 2)
    pw_i = pw_im.T.reshape(8, NB, S).transpose(1, 0, 2)
    row8 = jnp.arange(8)[None, :, None]

    def masked_power(pw, k, keep):
        return jnp.where(keep, pw[:, k - 1][:, None, :], 0.0)

    coef = jnp.stack(
        [masked_power(pw, k, row8 >= k) for k in (1, 2, 4) for pw in (pw_r, pw_i)] + [pw_r, pw_i], axis=1)
    coef_rev = jnp.stack(
        [masked_power(pw, k, row8 < 8 - k) for k in (1, 2, 4) for pw in (pw_r, -pw_i)]
        + [pw_r[:, ::-1], -pw_i[:, ::-1]], axis=1)
    bd = lambda m: _block_diag(m, GROUPS_PER_BLOCK)
    bbr3, bbi3 = bb_re.reshape(G, P, C), bb_im.reshape(G, P, C)
    bbr_cs = bd(bbr3.transpose(0, 2, 1)).astype(BF16)
    bbi_cs = bd(bbi3.transpose(0, 2, 1)).astype(BF16)
    bbr_sc = bd(bbr3).astype(BF16)
    bbi_sc = bd(bbi3).astype(BF16)
    ccr_sc = bd(c_re[0].transpose(0, 2, 1)).astype(BF16)
    cci_sc = bd(c_im[0].transpose(0, 2, 1)).astype(BF16)
    ccr_cs = bd(c_re[0]).astype(BF16)
    cci_cs = bd(c_im[0]).astype(BF16)

    h0 = jnp.concatenate([jnp.zeros((PAD, D), F32), meta_full, x[0]], axis=0)
    n1 = _rms_fwd(h0, g_mix, "rms_mix")
    z = _mm(n1, w_zf, "nn", LP, NZ, D, BF16, "in_proj")
    fpre = _mm(n1, w_zf, "nn", LP, LANES, D, F32, "in_proj_f", b_off=(0, NZ))
    bf_pad = jnp.pad(b_f, ((0, 0), (0, LANES - H)))
    fcum = _fgate_fwd(fpre, bf_pad, PAD, "fgate_fwd")
    key_bias = jnp.where(jnp.arange(LP)[:, None] >= PAD, -fcum, NEG)
    bias_t = key_bias.T[:H].reshape(H, 1, LP)
    attn, attn_f32, lse_t, *rest = _attn_fwd(z, bias_t, H, PAD, "attn_fwd", gather=[slots[n] for n in big[1:]])
    gathered.update(zip(big[1:], rest))
    w_glu_c, w_ao_c, w_up_c = gathered["w_glu"], gathered["w_attn_o"], gathered["w_up"]
    w_out_f = gathered["w_out"].reshape(D, D)
    w_down_f = gathered["w_down"].reshape(DFF, D)
    ao = _mm(attn, w_ao_c, "nn", LP, D, DA, BF16, "attn_out", b_chips=N_AO)
    y, yg, hs_re, hs_im = _ssm_fwd(z, U0, coef, bbr_cs, bbi_cs, ccr_sc, cci_sc, d_skip, "ssm_fwd")
    yab = _mm(yg, w_glu_c, "nn", LP, 2 * D, DS, BF16, "glu_proj", b_chips=N_GLU)
    merged = _merge_fwd(yab, z, ao, D, GA0, GB0, "merge_fwd")
    h1 = _mm(merged, w_out_f, "nn", LP, D, D, F32, "out_proj", res=h0)
    n2 = _rms_fwd(h1, g_ffn, "rms_ffn")
    gu = _mm(n2, w_up_c, "nn", LP, 2 * DFF, D, BF16, "up_proj", tn=1408, b_chips=N_UP)
    act = _convact_fwd(gu, conv_w_f, conv_b, DFF, "convact_fwd")
    h2 = _mm(act, w_down_f, "nn", LP, D, DFF, F32, "down_proj", res=h1, tn=512, tk=DFF)
    dh2, dg_final, loss_v = _final_loss(h2, g_final.reshape(1, D), loss_target[0], OFF, "final_loss")
    loss = lax.psum(loss_v[0, 0], ("x", "y", "c"))

    KW = dict(tm=512, tn=512, tk=LP)
    dact = _mm(dh2, w_down_f, "nt", LP, DFF, D, BF16, "down_bwd_x")
    dw_down = _mm(act, dh2, "tn", DFF, D, LP, F32, "down_bwd_w", **KW)
    dgc, du_ffn = _convact_bwd(dact, gu, conv_w_f, conv_b, DFF, "convact_bwd")
    dg_ffn_in, dconv_w, dconv_b = _conv_bwd(dgc, gu, conv_w_f, DFF, PAD, "conv_bwd")
    dn2 = _mm(dg_ffn_in, w_up_c, "nt", LP, D, DFF, F32, "up_bwd_x_g", tn=512, tk=N_UP, b_chips=N_UP)
    dn2 = _mm(du_ffn, w_up_c, "nt", LP, D, DFF, F32, "up_bwd_x_u", res=dn2, b_off=(0, DFF), tn=512, tk=N_UP, b_chips=N_UP)
    dw_up = _mm(n2, dg_ffn_in, "tn", D, DFF, LP, F32, "up_bwd_w_g", tm=512, tn=256, tk=LP, out_chips=N_UP,
                out_into=(lax.empty((4, D, N_UP), F32), 0))
    dw_up = _mm(n2, du_ffn, "tn", D, DFF, LP, F32, "up_bwd_w_u", tm=512, tn=256, tk=LP, out_chips=N_UP,
                out_into=(dw_up, DFF // N_UP))
    dh1, dg_ffn = _rms_bwd(h1, g_ffn, dn2, dh2, "rms_ffn_bwd")
    c_idx = core.reshape(1).astype(jnp.int32)
    chip_c = jnp.stack([chip, core]).astype(jnp.int32)

    dmerged = _mm(dh1, w_out_f, "nt", LP, D, D, F32, "out_bwd_x")
    dw_out = _mm(merged, dh1, "tn", D, D, LP, F32, "out_bwd_w", **KW)
    dya, dyb, dga, dgb, dao = _merge_bwd(dmerged, yab, z, ao, D, GA0, GB0, "merge_bwd")
    dattn = _mm(dao, w_ao_c, "nt", LP, DA, D, BF16, "attn_out_bwd_x", b_chips=N_AO)
    dw_ao = _mm(attn, dao, "tn", DA, D, LP, F32, "attn_out_bwd_w", out_chips=N_AO, **KW)
    dyg = _mm(dya, w_glu_c, "nt", LP, DS, D, F32, "glu_bwd_x_a", b_chips=N_GLU)
    dyg = _mm(dyb, w_glu_c, "nt", LP, DS, D, F32, "glu_bwd_x_b", res=dyg, b_off=(0, D), b_chips=N_GLU)
    dw_glu = _mm(yg, dya, "tn", DS, D, LP, F32, "glu_bwd_w_a", out_chips=N_GLU,
                 out_into=(lax.empty((4, DS, N_GLU), F32), 0), **KW)
    dw_glu = _mm(yg, dyb, "tn", DS, D, LP, F32, "glu_bwd_w_b", out_chips=N_GLU, out_into=(dw_glu, D // N_GLU), **KW)
    early = ["w_glu", "w_attn_o", "w_out", "w_up", "w_down"]
    early_grads = [dw_glu, dw_ao, dw_out.reshape(4, D // 4, D), dw_up, dw_down.reshape(4, DFF // 4, D)]
    (du_ssm, dbbr_d, dbbi_d, dccr_d, dcci_d, dar_b, dai_b, dd_skip, *early_got) = _ssm_bwd(
        z, U0, dyg, y, hs_re, hs_im, coef_rev, bbr_sc, bbi_sc, ccr_cs, cci_cs, d_skip, "ssm_bwd",
        ride=_sibling_ride(early_grads))
    delta_t = _attn_delta(dattn, attn_f32, H, "attn_delta")
    early_part = [_add_half(g, t, c_idx, "rs_add_" + n) for n, g, t in zip(early, early_grads, early_got)]
    dq, dk, dv, dbias_t, *early_land = _attn_bwd(z, dattn, lse_t, delta_t, bias_t, H, "attn_bwd", scatter=early_part)
    dF = jnp.pad(-dbias_t[:, 0, :].T, ((0, 0), (0, LANES - H)))
    dfpre, db_f = _fgate_bwd(dF, fpre, bf_pad, PAD, "fgate_bwd")
    dz = jnp.concatenate([dq, dk, dv, du_ssm, dga, dgb, dfpre.astype(BF16)], axis=1)
    dw_zf = _mm(n1, dz, "tn", D, NZ + LANES, LP, F32, "in_bwd_w", tm=512, tn=640, tk=LP)
    def orig_cols(lo, hi):
        parts_ = []
        for a, b, shift in ((0, 3 * DA, 0), (3 * DA, 3 * DA + H, NZ - 3 * DA), (3 * DA + H, 4 * n_in, -H)):
            a, b = max(a, lo), min(b, hi)
            if a < b:
                parts_.append(dw_zf[:, a + shift:b + shift])
        return jnp.concatenate(parts_, axis=1)

    late_grads = [jnp.stack([orig_cols(q * n_in, (q + 1) * n_in) for q in range(4)], axis=0)]
    late_got = _sibling_halves(late_grads, "rs_sibling_w_in")
    late_part = [_add_half(late_grads[0], late_got[0], c_idx, "rs_add_w_in")]
    dn1, *late_land = _mm(dz, w_zf, "nt", LP, D, NZ + LANES, F32, "in_bwd_x", tm=640, tn=256, tk=NZ + LANES,
                          ride=_scatter_ride(late_part))
    dh0, dg_mix = _rms_bwd(h0, g_mix, dn1, dh1, "rms_mix_bwd")
    grad_x = dh0[OFF:][None]
    dmeta_full = dh0[PAD:OFF]

    ext = lambda m, A, B: _block_diag_extract(m, GROUPS_PER_BLOCK, A, B)
    dbb_re = ext(dbbr_d, C, P).transpose(0, 2, 1).reshape(GP, C)
    dbb_im = ext(dbbi_d, C, P).transpose(0, 2, 1).reshape(GP, C)
    dc_re = ext(dccr_d, P, C).transpose(0, 2, 1)[None]
    dc_im = ext(dcci_d, P, C).transpose(0, 2, 1)[None]
    glr, gli, gldt, gbr, gbi = _ssm_prep_bwd(lr_c, li_c, ldt_c, br2, bi2, dar_b.reshape(GP, 1), dai_b.reshape(GP, 1),
                                             dbb_re, dbb_im, "ssm_prep_bwd")
    small_grads = {
        "g_mix": dg_mix, "b_f": db_f[:, :H], "lam_re": glr.reshape(1, G, P), "lam_im": gli.reshape(1, G, P),
        "log_dt": gldt.reshape(G, P).sum(axis=1)[None], "b_re": gbr.reshape(1, G, P, C), "b_im": gbi.reshape(1, G, P, C),
        "c_re": dc_re, "c_im": dc_im, "d_skip": dd_skip, "g_ffn": dg_ffn, "conv_b": dconv_b, "g_final": dg_final.reshape(D),
    }

    small = list(small_grads)
    rider_grads = [dconv_w, dmeta_full]
    small_shapes = [args[n].shape for n in small] + [g.shape for g in rider_grads]
    srows = sum(_rows_for(int(np.prod(sh)), LANES) for sh in small_shapes)
    srows = -(-srows // 256) * 256
    zeros_like_riders = [jnp.zeros(g.shape, F32) for g in rider_grads]
    pack = lambda arrs: _pack_small(arrs, srows)
    g_parts = _gather_all(pack([small_grads[n] for n in small] + rider_grads), "gather_small_grads")
    sm = _sum_adamw(g_parts, pack([args[n] for n in small] + zeros_like_riders),
                    pack([args["m_" + n] for n in small] + zeros_like_riders),
                    pack([args["v_" + n] for n in small] + zeros_like_riders), "small_adamw")
    unpacked = [_unpack_small(p, small_shapes) for p in sm]
    sg, sd, smm, svv = (dict(zip(small, u[:len(small)])) for u in unpacked)
    dconv_w_sum, dmeta_sum = unpacked[0][len(small):]
    n_cw, n_me = conv_w.shape[2], meta.shape[1]
    rider = {"conv_w": lax.dynamic_slice_in_dim(dconv_w_sum, chip * n_cw, n_cw, axis=1)[None],
             "meta": lax.dynamic_slice_in_dim(dmeta_sum, chip * n_me, n_me, axis=1)}

    big = ["w_in"] + early
    halves = [_sum_half(g, t, l_, chip_c, "rs_sum_" + n) for n, g, t, l_ in
              zip(big, late_grads + early_grads, list(late_got) + list(early_got), list(late_land) + list(early_land))]
    shard_grads = dict(zip(big, _sibling_join(halves, "rs_join")))
    shard_grads.update({n: g.reshape(g.shape[-2:]) for n, g in rider.items()})
    bg, bd_, bm, bv = {}, {}, {}, {}
    for n, g in shard_grads.items():
        bd_[n], bm[n], bv[n] = _adamw(args[n], g, args["m_" + n], args["v_" + n], "adamw_" + n)
        bg[n] = g.reshape(args[n].shape)

    order = ["meta", "g_mix", "w_in", "b_f", "lam_re", "lam_im", "log_dt", "b_re", "b_im", "c_re", "c_im", "d_skip",
             "w_glu", "w_attn_o", "w_out", "g_ffn", "w_up", "conv_w", "conv_b", "w_down", "g_final"]
    pick = lambda bigd, smalld, n: bigd[n] if n in bigd else smalld[n]
    outs = [loss, grad_x]
    for bigd, smalld in ((bg, sg), (bd_, sd), (bm, smm), (bv, svv)):
        outs += [pick(bigd, smalld, n) for n in order]
    return tuple(outs)
```

```python
import functools
import math

import jax
import jax.numpy as jnp
import numpy as np
from jax import lax
from jax.experimental import pallas as pl
from jax.experimental.pallas import tpu as pltpu

F32 = jnp.float32
BF16 = jnp.bfloat16
MESH = pl.DeviceIdType.MESH

EPS = 1e-6
HEAD_DIM = 128
LANES = 128
NEG = -1e30
GELU_C = math.sqrt(2.0 / math.pi)
GELU_A = 0.044715
ADAM_LR, ADAM_B1, ADAM_B2, ADAM_EPS, ADAM_WD, ADAM_STEP = 0.001, 0.9, 0.999, 1e-08, 0.01, 10
V7X_VMEM_BYTES = 64 << 20
GROUPS_PER_BLOCK = 8


def _tile(n, pref, mult=LANES):
    if n <= pref:
        return n
    t = (pref // mult) * mult
    while t >= mult:
        if n % t == 0:
            return t
        t -= mult
    raise ValueError(f"no tile for {n} <= {pref} (multiple of {mult})")


def _ctile(pref, *vals):
    g = 0
    for v in vals:
        g = math.gcd(g, v)
    return _tile(g, pref)


def _cparams(sem, est_bytes):
    limit = int(min(max(est_bytes * 1.25 + (4 << 20), 16 << 20), V7X_VMEM_BYTES - (8 << 20)))
    return pltpu.CompilerParams(dimension_semantics=sem, vmem_limit_bytes=limit)


def _sds(shape, dtype):
    return jax.ShapeDtypeStruct(tuple(shape), dtype)


def _sig(x):
    return 0.5 * jnp.tanh(0.5 * x) + 0.5


def _sig_tail(x):
    return 1.0 / (1.0 + jnp.exp(-x))


def _gelu(x):
    t = jnp.tanh(GELU_C * (x + GELU_A * x * x * x))
    return 0.5 * x * (1.0 + t)


def _gelu_grad(x):
    t = jnp.tanh(GELU_C * (x + GELU_A * x * x * x))
    return 0.5 * (1.0 + t) + 0.5 * x * (1.0 - t * t) * GELU_C * (1.0 + 3.0 * GELU_A * x * x)


def _mm(a, b, mode, M, N, K, out_dtype, name, *, res=None, a_off=(0, 0), b_off=(0, 0),
        tm=640, tn=1024, tk=2048, b_chips=None, out_chips=None, out_into=None, ride=None):
    tm, tn, tk = _tile(M, tm, 8 if mode != "tn" else LANES), _tile(N, tn), _tile(K, tk, LANES if mode != "tn" else 8)
    if b_chips is not None and mode == "nt":
        tk = _ctile(tk, tk, b_chips)
    if b_chips is not None and mode != "nt":
        tn = _ctile(tn, tn, b_chips)
    if out_chips is not None:
        tn = _ctile(tn, tn, out_chips)
    nk = K // tk
    ar, ac = a_off
    br, bc = b_off
    if mode == "tn":
        assert ar % tk == 0 and ac % tm == 0
        a_spec = pl.BlockSpec((tk, tm), lambda i, j, k: (k + ar // tk, i + ac // tm))
        a_dims = 0
    else:
        assert ar % tm == 0 and ac % tk == 0
        a_spec = pl.BlockSpec((tm, tk), lambda i, j, k: (i + ar // tm, k + ac // tk))
        a_dims = 1
    if mode == "nt":
        assert br % tn == 0 and bc % tk == 0
        if b_chips is None:
            b_spec = pl.BlockSpec((tn, tk), lambda i, j, k: (j + br // tn, k + bc // tk))
        else:
            per = b_chips // tk
            b_spec = pl.BlockSpec((None, tn, tk), lambda i, j, k: ((k + bc // tk) // per, j + br // tn, (k + bc // tk) % per))
        b_dims = 1
    else:
        assert br % tk == 0 and bc % tn == 0
        if b_chips is None:
            b_spec = pl.BlockSpec((tk, tn), lambda i, j, k: (k + br // tk, j + bc // tn))
        else:
            per = b_chips // tn
            b_spec = pl.BlockSpec((None, tk, tn), lambda i, j, k: ((j + bc // tn) // per, k + br // tk, (j + bc // tn) % per))
        b_dims = 0
    dims = (((a_dims,), (b_dims,)), ((), ()))
    if out_chips is None:
        o_spec = pl.BlockSpec((tm, tn), lambda i, j, k: (i, j))
        o_shape = _sds((M, N), out_dtype)
    else:
        per_o = out_chips // tn
        chip0 = 0 if out_into is None else out_into[1]
        o_spec = pl.BlockSpec((None, tm, tn), lambda i, j, k: (chip0 + j // per_o, i, j % per_o))
        o_shape = _sds((N // out_chips if out_into is None else 4, M, out_chips), out_dtype)
    has_res = res is not None
    has_into = out_into is not None

    r_ins, r_outs, r_sems = _ride_parts(ride)
    n_in = 2 + has_res + has_into
    steps = (M // tm, N // tn, nk)

    def body(*refs):
        a_ref, b_ref = refs[:2]
        r_ref = refs[2] if has_res else None
        o_ref = refs[n_in + len(r_ins)]
        if ride is not None:
            start, finish = ride["fn"](refs[n_in:n_in + len(r_ins)],
                                       refs[n_in + len(r_ins) + 1:n_in + len(r_ins) + 1 + len(r_outs)], *refs[-2:])
            pid = [pl.program_id(d) for d in range(3)]
            pl.when((pid[0] == 0) & (pid[1] == 0) & (pid[2] == 0))(start)
        part = lax.dot_general(a_ref[...].astype(BF16), b_ref[...].astype(BF16), dims, preferred_element_type=F32)

        def write_out(acc):
            if has_res:
                acc = r_ref[...] + acc
            o_ref[...] = acc.astype(o_ref.dtype)

        if nk == 1:
            write_out(part)
        else:
            acc_ref = refs[n_in + len(r_ins) + 1 + len(r_outs)]
            k = pl.program_id(2)

            @pl.when(k == 0)
            def _():
                acc_ref[...] = part

            @pl.when(k > 0)
            def _():
                acc_ref[...] += part

            @pl.when(k == nk - 1)
            def _():
                write_out(acc_ref[...])

        if ride is not None:
            pl.when((pid[0] == steps[0] - 1) & (pid[1] == steps[1] - 1) & (pid[2] == steps[2] - 1))(finish)

    in_specs = ([a_spec, b_spec] + ([o_spec] if has_res else []) + ([pl.BlockSpec(memory_space=pl.ANY)] if has_into else [])
                + [HBM_SPEC] * len(r_ins))
    args = (a, b) + ((res,) if has_res else ()) + ((out_into[0],) if has_into else ()) + tuple(r_ins)
    isz = lambda x: jnp.dtype(x.dtype).itemsize
    est = 2 * (tm * tk * isz(a) + tk * tn * isz(b) + tm * tn * jnp.dtype(out_dtype).itemsize) + tm * tn * 4 * (2 + 2 * has_res)
    sem = ("parallel", "parallel", "arbitrary") if ride is None else ("arbitrary",) * 3
    out = pl.pallas_call(
        body, grid=steps, in_specs=in_specs, out_specs=[o_spec] + [HBM_SPEC] * len(r_outs),
        out_shape=[o_shape] + r_outs,
        scratch_shapes=([pltpu.VMEM((tm, tn), F32)] if nk > 1 else []) + r_sems,
        input_output_aliases={2: 0} if has_into else {},
        compiler_params=_cparams(sem, est), name=name,
    )(*args)
    return out[0] if ride is None else out


def _rms_fwd(h, g, name):
    LP, D = h.shape
    tm = _tile(LP, 640, 8)

    def body(h_ref, g_ref, o_ref):
        x = h_ref[...]
        r = lax.rsqrt(jnp.mean(x * x, axis=-1, keepdims=True) + EPS)
        o_ref[...] = (x * r * g_ref[...]).astype(o_ref.dtype)

    row = pl.BlockSpec((tm, D), lambda i: (i, 0))
    return pl.pallas_call(
        body, grid=(LP // tm,), in_specs=[row, pl.BlockSpec((1, D), lambda i: (0, 0))], out_specs=row,
        out_shape=_sds((LP, D), BF16), compiler_params=_cparams(("parallel",), 2 * tm * D * 6), name=name,
    )(h, g)


def _rms_bwd(h, g, dn, dres, name):
    LP, D = h.shape
    tm = _tile(LP, 320, 8)
    nt = LP // tm

    def body(h_ref, g_ref, dn_ref, dres_ref, dh_ref, dg_ref):
        i = pl.program_id(0)
        x = h_ref[...]
        r = lax.rsqrt(jnp.mean(x * x, axis=-1, keepdims=True) + EPS)
        xh = x * r
        dn_v = dn_ref[...]
        dxh = dn_v * g_ref[...]
        dh_ref[...] = dres_ref[...] + r * (dxh - xh * jnp.mean(dxh * xh, axis=-1, keepdims=True))
        part = jnp.sum(dn_v * xh, axis=0, keepdims=True)

        @pl.when(i == 0)
        def _():
            dg_ref[...] = part

        @pl.when(i > 0)
        def _():
            dg_ref[...] += part

    row = pl.BlockSpec((tm, D), lambda i: (i, 0))
    vec = pl.BlockSpec((1, D), lambda i: (0, 0))
    return pl.pallas_call(
        body, grid=(nt,), in_specs=[row, vec, row, row], out_specs=[row, vec],
        out_shape=[_sds((LP, D), F32), _sds((1, D), F32)],
        compiler_params=_cparams(("arbitrary",), 2 * 4 * tm * D * 4), name=name,
    )(h, g, dn, dres)


def _final_loss(h, g, tgt, off, name):
    LP, D = h.shape
    tm = LANES
    assert off % tm == 0
    ob = off // tm
    nt = LP // tm

    def body(h_ref, g_ref, t_ref, dh_ref, dg_ref, loss_ref):
        i = pl.program_id(0)
        x = h_ref[...]
        r = lax.rsqrt(jnp.mean(x * x, axis=-1, keepdims=True) + EPS)
        xh = x * r
        gv = g_ref[...]
        e = xh * gv - t_ref[...]
        valid = i >= ob
        dy = jnp.where(valid, e * (1.0 / D), 0.0)
        lpart = jnp.where(valid, 0.5 * jnp.sum(jnp.mean(e * e, axis=-1, keepdims=True), axis=0, keepdims=True), 0.0)
        dxh = dy * gv
        dh_ref[...] = r * (dxh - xh * jnp.mean(dxh * xh, axis=-1, keepdims=True))
        gpart = jnp.sum(dy * xh, axis=0, keepdims=True)

        @pl.when(i == 0)
        def _():
            dg_ref[...] = gpart
            loss_ref[...] = jnp.broadcast_to(lpart, loss_ref.shape)

        @pl.when(i > 0)
        def _():
            dg_ref[...] += gpart
            loss_ref[...] += jnp.broadcast_to(lpart, loss_ref.shape)

    row = pl.BlockSpec((tm, D), lambda i: (i, 0))
    vec = pl.BlockSpec((1, D), lambda i: (0, 0))
    return pl.pallas_call(
        body, grid=(nt,),
        in_specs=[row, vec, pl.BlockSpec((tm, D), lambda i: (jnp.maximum(i - ob, 0), 0))],
        out_specs=[row, vec, pl.BlockSpec((1, LANES), lambda i: (0, 0))],
        out_shape=[_sds((LP, D), F32), _sds((1, D), F32), _sds((1, LANES), F32)],
        compiler_params=_cparams(("arbitrary",), 2 * 3 * tm * D * 4), name=name,
    )(h, g, tgt)


def _fgate_fwd(fpre, bias, pad, name):
    LP, W = fpre.shape

    def body(f_ref, b_ref, o_ref):
        row8 = lax.broadcasted_iota(jnp.int32, (8, W), 0)
        bv = b_ref[...]

        def step(g, carry):
            r0 = pl.multiple_of(g * 8, 8)
            x = f_ref[pl.ds(r0, 8), :] + bv
            lf = jnp.minimum(x, 0.0) - jnp.log(1.0 + jnp.exp(-jnp.abs(x)))
            lf = jnp.where(r0 + row8 >= pad, lf, 0.0)
            for k in (1, 2, 4):
                lf = lf + jnp.where(row8 >= k, pltpu.roll(lf, k, 0), 0.0)
            lf = lf + carry
            o_ref[pl.ds(r0, 8), :] = lf
            return jnp.broadcast_to(lf[7:8, :], (8, W))

        lax.fori_loop(0, LP // 8, step, jnp.zeros((8, W), F32))

    return pl.pallas_call(
        body, out_shape=_sds((LP, W), F32),
        compiler_params=_cparams(None, 3 * LP * W * 4), name=name,
    )(fpre, bias)


def _fgate_bwd(dF, fpre, bias, pad, name):
    LP, W = fpre.shape
    ng = LP // 8

    def body(d_ref, f_ref, b_ref, o_ref, db_ref):
        row8 = lax.broadcasted_iota(jnp.int32, (8, W), 0)
        bv = b_ref[...]

        def step(t, carry):
            run, acc = carry
            g = ng - 1 - t
            r0 = pl.multiple_of(g * 8, 8)
            x = d_ref[pl.ds(r0, 8), :]
            for k in (1, 2, 4):
                x = x + jnp.where(row8 < 8 - k, pltpu.roll(x, 8 - k, 0), 0.0)
            x = x + run
            df = x * _sig_tail(-(f_ref[pl.ds(r0, 8), :] + bv))
            df = jnp.where(r0 + row8 >= pad, df, 0.0)
            o_ref[pl.ds(r0, 8), :] = df
            return jnp.broadcast_to(x[0:1, :], (8, W)), acc + df

        _, acc = lax.fori_loop(0, ng, step, (jnp.zeros((8, W), F32), jnp.zeros((8, W), F32)))
        db_ref[...] = jnp.sum(acc, axis=0, keepdims=True)

    return pl.pallas_call(
        body, out_shape=[_sds((LP, W), F32), _sds((1, W), F32)],
        compiler_params=_cparams(None, 4 * LP * W * 4), name=name,
    )(dF, fpre, bias)


def _place():
    return lax.axis_index("x"), lax.axis_index("y"), lax.axis_index("c")


def _other_chips(x, y):
    return [(1 - x, y), (x, 1 - y), (1 - x, 1 - y)]


def _gather_phases(g_refs, send_sems, recv_sems):
    nw = len(g_refs)
    x, y, c = _place()
    chips = _other_chips(x, y)
    me = 2 * x + y

    def copy(i, k, chip, half, to):
        HR = g_refs[i].shape[1] // 2
        rows = g_refs[i].at[chip, pl.ds(pl.multiple_of(half * HR, 16), HR), :]
        return pltpu.make_async_remote_copy(
            src_ref=rows, dst_ref=rows, send_sem=send_sems.at[6 * i + k], recv_sem=recv_sems.at[6 * i + k],
            device_id=to, device_id_type=MESH)

    pairs = [(i, k, cx, cy) for i in range(nw) for k, (cx, cy) in enumerate(chips)]

    def start():
        for i, k, cx, cy in pairs:
            copy(i, k, me, c, (cx, cy, c)).start()

    def forward():
        for i, k, cx, cy in pairs:
            copy(i, k, 2 * cx + cy, c, (cx, cy, c)).wait_recv()
            copy(i, 3 + k, 2 * cx + cy, c, (x, y, 1 - c)).start()

    def finish():
        for i, k, cx, cy in pairs:
            copy(i, 3 + k, 2 * cx + cy, 1 - c, (x, y, 1 - c)).wait_recv()
        for i, k, cx, cy in pairs:
            copy(i, k, me, c, (cx, cy, c)).wait_send()
            copy(i, 3 + k, 2 * cx + cy, c, (x, y, 1 - c)).wait_send()

    return start, forward, finish


def _scatter_phases(s_refs, land_refs, send_sems, recv_sems):
    x, y, c = _place()
    chips = _other_chips(x, y)

    def copy(i, k, cx, cy):
        return pltpu.make_async_remote_copy(
            src_ref=s_refs[i].at[2 * cx + cy], dst_ref=land_refs[i].at[k],
            send_sem=send_sems.at[3 * i + k], recv_sem=recv_sems.at[3 * i + k],
            device_id=(cx, cy, c), device_id_type=MESH)

    pairs = [(i, k, cx, cy) for i in range(len(s_refs)) for k, (cx, cy) in enumerate(chips)]

    def start():
        for p in pairs:
            copy(*p).start()

    def finish():
        for p in pairs:
            copy(*p).wait_recv()
        for p in pairs:
            copy(*p).wait_send()

    return start, finish


def _sibling_phases(g_refs, land_refs, send_sems, recv_sems):
    x, y, c = _place()

    def copy(i):
        HR = land_refs[i].shape[1]
        q0 = pl.multiple_of((1 - c) * HR, 8)
        return pltpu.make_async_remote_copy(
            src_ref=g_refs[i].at[pl.ds(0, 4), pl.ds(q0, HR), :], dst_ref=land_refs[i],
            send_sem=send_sems.at[i], recv_sem=recv_sems.at[i], device_id=(x, y, 1 - c), device_id_type=MESH)

    def start():
        for i in range(len(g_refs)):
            copy(i).start()

    def finish():
        for i in range(len(g_refs)):
            copy(i).wait()

    return start, finish


def _sibling_ride(gs):
    return dict(fn=_sibling_phases, ins=list(gs), outs=[_sds((4, g.shape[1] // 2, g.shape[2]), g.dtype) for g in gs],
                sems=len(gs))


def _scatter_ride(ss):
    return dict(fn=_scatter_phases, ins=list(ss), outs=[_sds((3,) + p.shape[1:], p.dtype) for p in ss], sems=3 * len(ss))


def _ride_parts(ride):
    if ride is None:
        return [], [], []
    return ride["ins"], ride["outs"], [pltpu.SemaphoreType.DMA((ride["sems"],)), pltpu.SemaphoreType.DMA((ride["sems"],))]


HBM_SPEC = pl.BlockSpec(memory_space=pltpu.HBM)


def _col_to_row(col):
    n = col.shape[0]
    return jnp.transpose(jnp.broadcast_to(col, (n, LANES)))[0:1, :]


def _row_to_col(row):
    n = row.shape[1]
    return jnp.transpose(jnp.broadcast_to(row, (LANES, n)))[:, 0:1]


def _attn_fwd(z, bias_t, H, pad, name, gather=()):
    LP = z.shape[0]
    BQ = BK = _tile(LP, 640)
    scale = HEAD_DIM ** -0.5
    NT = (((1,), (1,)), ((), ()))

    nw = len(gather)
    nq = LP // BQ

    def body(*refs):
        q_ref, k_ref, v_ref, b_ref = refs[:4]
        o_ref, of_ref, lse_ref = refs[4 + nw:7 + nw]
        hd = pl.program_id(0)
        qi = pl.program_id(1)
        if nw:
            start, forward, finish = _gather_phases(refs[7 + nw:7 + 2 * nw], *refs[7 + 2 * nw:])
            pl.when((hd == 0) & (qi == 0))(start)
            pl.when((hd == H // 2) & (qi == 0))(forward)
        q = q_ref[...]

        def tile(kb, carry, masked):
            m, l, acc = carry
            k0 = pl.multiple_of(kb * BK, BK)
            s = lax.dot_general(q, k_ref[pl.ds(k0, BK), :], NT, preferred_element_type=F32) * scale
            s = s + b_ref[:, pl.ds(k0, BK)]
            if masked:
                ri = lax.broadcasted_iota(jnp.int32, (BQ, BK), 0)
                ci = lax.broadcasted_iota(jnp.int32, (BQ, BK), 1)
                s = jnp.where(ri >= ci, s, NEG)
            mn = jnp.maximum(m, jnp.max(s, axis=-1, keepdims=True))
            p = jnp.exp(s - mn)
            alpha = jnp.exp(m - mn)
            l = alpha * l + jnp.sum(p, axis=-1, keepdims=True)
            vk = v_ref[pl.ds(k0, BK), :]
            p_hi = p.astype(BF16)
            p_lo = (p - p_hi.astype(F32)).astype(BF16)
            pv = jnp.dot(p_hi, vk, preferred_element_type=F32) + jnp.dot(p_lo, vk, preferred_element_type=F32)
            return mn, l, alpha * acc + pv

        carry = (jnp.full((BQ, 1), NEG, F32), jnp.zeros((BQ, 1), F32), jnp.zeros((BQ, HEAD_DIM), F32))
        carry = lax.fori_loop(0, qi, lambda kb, c: tile(kb, c, False), carry)
        m, l, acc = tile(qi, carry, True)
        rows = qi * BQ + lax.broadcasted_iota(jnp.int32, (BQ, 1), 0)
        o = jnp.where(rows >= pad, acc / l, 0.0)
        o_ref[...] = o.astype(o_ref.dtype)
        of_ref[...] = o
        lse_ref[...] = _col_to_row(m + jnp.log(l))
        if nw:
            pl.when((hd == H - 1) & (qi == nq - 1))(finish)

    in_specs = [
        pl.BlockSpec((BQ, HEAD_DIM), lambda h, i: (i, h)),
        pl.BlockSpec((LP, HEAD_DIM), lambda h, i: (0, H + h)),
        pl.BlockSpec((LP, HEAD_DIM), lambda h, i: (0, 2 * H + h)),
        pl.BlockSpec((None, 1, LP), lambda h, i: (h, 0, 0)),
    ]
    out_specs = [
        pl.BlockSpec((BQ, HEAD_DIM), lambda h, i: (i, h)),
        pl.BlockSpec((BQ, HEAD_DIM), lambda h, i: (i, h)),
        pl.BlockSpec((None, 1, BQ), lambda h, i: (h, 0, i)),
    ]
    est = 2 * (2 * LP * HEAD_DIM * 2 + 8 * LP * 4) + 20 * BQ * LANES * 4 + 8 * BQ * BK * 4
    sems = [pltpu.SemaphoreType.DMA((6 * nw,)), pltpu.SemaphoreType.DMA((6 * nw,))] if nw else []
    return pl.pallas_call(
        body, grid=(H, nq), in_specs=in_specs + [HBM_SPEC] * nw, out_specs=out_specs + [HBM_SPEC] * nw,
        out_shape=[_sds((LP, H * HEAD_DIM), BF16), _sds((LP, H * HEAD_DIM), F32), _sds((H, 1, LP), F32)]
        + [_sds(g.shape, g.dtype) for g in gather],
        scratch_shapes=sems, input_output_aliases={4 + i: 3 + i for i in range(nw)},
        compiler_params=_cparams(("arbitrary", "arbitrary"), est), name=name,
    )(z, z, z, bias_t, *gather)


def _attn_delta(do, o, H, name):
    LP = do.shape[0]
    tm = _tile(LP, 640)

    def body(do_ref, o_ref, d_ref):
        d_ref[...] = _col_to_row(jnp.sum(do_ref[...].astype(F32) * o_ref[...].astype(F32), axis=-1, keepdims=True))

    blk = pl.BlockSpec((tm, HEAD_DIM), lambda h, i: (i, h))
    return pl.pallas_call(
        body, grid=(H, LP // tm), in_specs=[blk, blk],
        out_specs=pl.BlockSpec((None, 1, tm), lambda h, i: (h, 0, i)),
        out_shape=_sds((H, 1, LP), F32),
        compiler_params=_cparams(("parallel", "parallel"), 8 * tm * LANES * 4), name=name,
    )(do, o)


def _attn_bwd(z, do, lse_t, delta_t, bias_t, H, name, scatter=()):
    LP = z.shape[0]
    BK = BQ = _tile(LP, 640)
    nk = nq = LP // BK
    scale = HEAD_DIM ** -0.5
    NT = (((1,), (1,)), ((), ()))
    TN = (((0,), (0,)), ((), ()))

    nw = len(scatter)

    def body(*refs):
        q_ref, k_ref, v_ref, do_ref, lse_ref, dl_ref, b_ref = refs[:7]
        dq_ref, dk_ref, dv_ref, db_ref = refs[7 + nw:11 + nw]
        dq_acc = refs[11 + 2 * nw]
        hd = pl.program_id(0)
        kj = pl.program_id(1)
        if nw:
            start, finish = _scatter_phases(refs[7:7 + nw], refs[11 + nw:11 + 2 * nw], *refs[12 + 2 * nw:])
            pl.when((hd == 0) & (kj == 0))(start)

        @pl.when(kj == 0)
        def _():
            dq_acc[...] = jnp.zeros_like(dq_acc)

        k = k_ref[...]
        v = v_ref[...]
        bcol = _row_to_col(b_ref[:, pl.ds(pl.multiple_of(kj * BK, BK), BK)])

        def tile(qc, carry, masked):
            dk, dv, db = carry
            q0 = pl.multiple_of(qc * BQ, BQ)
            q = q_ref[pl.ds(q0, BQ), :]
            dout = do_ref[pl.ds(q0, BQ), :]
            st = lax.dot_general(k, q, NT, preferred_element_type=F32) * scale + bcol
            if masked:
                ri = lax.broadcasted_iota(jnp.int32, (BK, BQ), 0)
                ci = lax.broadcasted_iota(jnp.int32, (BK, BQ), 1)
                st = jnp.where(ci >= ri, st, NEG)
            pt = jnp.exp(st - lse_ref[:, pl.ds(q0, BQ)])
            dv = dv + jnp.dot(pt.astype(BF16), dout, preferred_element_type=F32)
            dpt = lax.dot_general(v, dout, NT, preferred_element_type=F32)
            dst = pt * (dpt - dl_ref[:, pl.ds(q0, BQ)])
            db = db + jnp.sum(dst, axis=-1, keepdims=True)
            dsb = (dst * scale).astype(BF16)
            dk = dk + jnp.dot(dsb, q, preferred_element_type=F32)
            dq_acc[pl.ds(q0, BQ), :] += lax.dot_general(dsb, k, TN, preferred_element_type=F32)
            return dk, dv, db

        carry = (jnp.zeros((BK, HEAD_DIM), F32), jnp.zeros((BK, HEAD_DIM), F32), jnp.zeros((BK, 1), F32))
        carry = tile(kj, carry, True)
        dk, dv, db = lax.fori_loop(kj + 1, nq, lambda qc, c: tile(qc, c, False), carry)
        dk_ref[...] = dk.astype(dk_ref.dtype)
        dv_ref[...] = dv.astype(dv_ref.dtype)
        db_ref[...] = _col_to_row(db)

        @pl.when(kj == nk - 1)
        def _():
            dq_ref[...] = dq_acc[...].astype(dq_ref.dtype)

        if nw:
            pl.when((hd == H - 1) & (kj == nk - 1))(finish)

    full = lambda c0: pl.BlockSpec((LP, HEAD_DIM), lambda h, j: (0, c0 + h))
    blk = lambda c0: pl.BlockSpec((BK, HEAD_DIM), lambda h, j: (j, c0 + h))
    vec = pl.BlockSpec((None, 1, LP), lambda h, j: (h, 0, 0))
    in_specs = [full(0), blk(H), blk(2 * H), full(0), vec, vec, vec]
    out_specs = [full(0), blk(0), blk(0), pl.BlockSpec((None, 1, BK), lambda h, j: (h, 0, j))]
    est = 2 * (3 * LP * HEAD_DIM * 2 + 16 * LP * 4) + LP * HEAD_DIM * 4 + 24 * BK * LANES * 4 + 10 * BK * BQ * 4
    sems = [pltpu.SemaphoreType.DMA((3 * nw,)), pltpu.SemaphoreType.DMA((3 * nw,))] if nw else []
    return pl.pallas_call(
        body, grid=(H, nk), in_specs=in_specs + [HBM_SPEC] * nw, out_specs=out_specs + [HBM_SPEC] * nw,
        out_shape=[_sds((LP, H * HEAD_DIM), BF16)] * 3 + [_sds((H, 1, LP), F32)]
        + [_sds((3,) + p.shape[1:], p.dtype) for p in scatter],
        scratch_shapes=[pltpu.VMEM((LP, HEAD_DIM), F32)] + sems,
        compiler_params=_cparams(("arbitrary", "arbitrary"), est), name=name,
    )(z, z, z, do, lse_t, delta_t, bias_t, *scatter)


def _ssm_disc(lr, li, ldt, br, bi):
    dt = jnp.exp(ldt)
    mag = jnp.exp(lr * dt)
    a_re = mag * jnp.cos(li * dt)
    a_im = mag * jnp.sin(li * dt)
    den = lr * lr + li * li
    nr = a_re - 1.0
    z_re = (nr * lr + a_im * li) / den
    z_im = (a_im * lr - nr * li) / den
    return a_re, a_im, z_re * br - z_im * bi, z_re * bi + z_im * br


def _ssm_prep(lr, li, ldt, br, bi, name):
    GP, C = br.shape

    def body(lr_ref, li_ref, ldt_ref, br_ref, bi_ref, ar_ref, ai_ref, bbr_ref, bbi_ref, pr_ref, pi_ref):
        a_re, a_im, bb_re, bb_im = _ssm_disc(lr_ref[...], li_ref[...], ldt_ref[...], br_ref[...], bi_ref[...])
        ar_ref[...] = a_re
        ai_ref[...] = a_im
        bbr_ref[...] = bb_re
        bbi_ref[...] = bb_im
        lane = lax.broadcasted_iota(jnp.int32, (tg, 8), 1)
        pr, pi_ = a_re, a_im
        accr = jnp.zeros((tg, 8), F32)
        acci = jnp.zeros((tg, 8), F32)
        for k in range(8):
            accr = jnp.where(lane == k, pr, accr)
            acci = jnp.where(lane == k, pi_, acci)
            pr, pi_ = pr * a_re - pi_ * a_im, pr * a_im + pi_ * a_re
        pr_ref[...] = accr
        pi_ref[...] = acci

    tg = _tile(GP, 512, 8)
    blk = lambda w: pl.BlockSpec((tg, w), lambda i: (i, 0))
    col = _sds((GP, 1), F32)
    return pl.pallas_call(
        body, grid=(GP // tg,), in_specs=[blk(1), blk(1), blk(1), blk(C), blk(C)],
        out_specs=[blk(1), blk(1), blk(C), blk(C), blk(8), blk(8)],
        out_shape=[col, col, _sds((GP, C), F32), _sds((GP, C), F32), _sds((GP, 8), F32), _sds((GP, 8), F32)],
        compiler_params=_cparams(("parallel",), 48 * tg * LANES * 4), name=name,
    )(lr, li, ldt, br, bi)


def _ssm_prep_bwd(lr, li, ldt, br, bi, dar, dai, dbbr, dbbi, name):
    GP, C = br.shape

    def body(lr_ref, li_ref, ldt_ref, br_ref, bi_ref, dar_ref, dai_ref, dbbr_ref, dbbi_ref,
             glr_ref, gli_ref, gldt_ref, gbr_ref, gbi_ref):
        _, vjp = jax.vjp(_ssm_disc, lr_ref[...], li_ref[...], ldt_ref[...], br_ref[...], bi_ref[...])
        glr, gli, gldt, gbr, gbi = vjp((dar_ref[...], dai_ref[...], dbbr_ref[...], dbbi_ref[...]))
        glr_ref[...] = glr
        gli_ref[...] = gli
        gldt_ref[...] = gldt
        gbr_ref[...] = gbr
        gbi_ref[...] = gbi

    tg = _tile(GP, 512, 8)
    blk = lambda w: pl.BlockSpec((tg, w), lambda i: (i, 0))
    col = _sds((GP, 1), F32)
    return pl.pallas_call(
        body, grid=(GP // tg,), in_specs=[blk(1), blk(1), blk(1), blk(C), blk(C), blk(1), blk(1), blk(C), blk(C)],
        out_specs=[blk(1), blk(1), blk(1), blk(C), blk(C)],
        out_shape=[col, col, col, _sds((GP, C), F32), _sds((GP, C), F32)],
        compiler_params=_cparams(("parallel",), 96 * tg * LANES * 4), name=name,
    )(lr, li, ldt, br, bi, dar, dai, dbbr, dbbi)


def _cmul_add(xr, xi, mr, mi, sr, si):
    return xr + mr * sr - mi * si, xi + mr * si + mi * sr


def _ssm_fwd(z, u_col0, coef, bbr, bbi, ccr, cci, dskip, name):
    LP = z.shape[0]
    NB, CB, S = bbr.shape
    TS = _tile(LP, 640, 8)
    nt = LP // TS

    def body(u_ref, coef_ref, bbr_ref, bbi_ref, ccr_ref, cci_ref, ds_ref, y_ref, yg_ref, hr_ref, hi_ref, bur, bui, carry):
        i = pl.program_id(1)

        @pl.when(i == 0)
        def _():
            carry[...] = jnp.zeros_like(carry)

        u = u_ref[...]
        bur[...] = jnp.dot(u, bbr_ref[...], preferred_element_type=F32)
        bui[...] = jnp.dot(u, bbi_ref[...], preferred_element_type=F32)

        def step(g, c):
            cr, ci = c
            r0 = pl.multiple_of(g * 8, 8)
            xr = bur[pl.ds(r0, 8), :]
            xi = bui[pl.ds(r0, 8), :]
            for n, k in enumerate((1, 2, 4)):
                xr, xi = _cmul_add(xr, xi, coef_ref[2 * n], coef_ref[2 * n + 1], pltpu.roll(xr, k, 0), pltpu.roll(xi, k, 0))
            xr, xi = _cmul_add(xr, xi, coef_ref[6], coef_ref[7], cr, ci)
            hr_ref[pl.ds(r0, 8), :] = xr
            hi_ref[pl.ds(r0, 8), :] = xi
            return jnp.broadcast_to(xr[7:8, :], (8, S)), jnp.broadcast_to(xi[7:8, :], (8, S))

        cr, ci = lax.fori_loop(0, TS // 8, step, (carry[0], carry[1]))
        carry[0] = cr
        carry[1] = ci
        y = (jnp.dot(hr_ref[...].astype(BF16), ccr_ref[...], preferred_element_type=F32)
             - jnp.dot(hi_ref[...].astype(BF16), cci_ref[...], preferred_element_type=F32)
             + ds_ref[...] * u.astype(F32))
        y_ref[...] = y
        yg_ref[...] = _gelu(y).astype(yg_ref.dtype)

    ucb = u_col0 // CB
    in_specs = [
        pl.BlockSpec((TS, CB), lambda j, i: (i, ucb + j)),
        pl.BlockSpec((None, 8, 8, S), lambda j, i: (j, 0, 0, 0)),
        pl.BlockSpec((None, CB, S), lambda j, i: (j, 0, 0)),
        pl.BlockSpec((None, CB, S), lambda j, i: (j, 0, 0)),
        pl.BlockSpec((None, S, CB), lambda j, i: (j, 0, 0)),
        pl.BlockSpec((None, S, CB), lambda j, i: (j, 0, 0)),
        pl.BlockSpec((1, CB), lambda j, i: (0, j)),
    ]
    yb = pl.BlockSpec((TS, CB), lambda j, i: (i, j))
    hb = pl.BlockSpec((TS, S), lambda j, i: (i, j))
    est = 2 * (2 * TS * S * 4 + 3 * TS * CB * 4 + 8 * 8 * S * 4 + 4 * CB * S * 2) + 3 * TS * S * 4
    return pl.pallas_call(
        body, grid=(NB, nt), in_specs=in_specs, out_specs=[yb, yb, hb, hb],
        out_shape=[_sds((LP, NB * CB), F32), _sds((LP, NB * CB), BF16), _sds((LP, NB * S), F32), _sds((LP, NB * S), F32)],
        scratch_shapes=[pltpu.VMEM((TS, S), F32), pltpu.VMEM((TS, S), F32), pltpu.VMEM((2, 8, S), F32)],
        compiler_params=_cparams(("parallel", "arbitrary"), est), name=name,
    )(z, coef, bbr, bbi, ccr, cci, dskip)


def _ssm_bwd(z, u_col0, dyg, y, hr, hi, coef_rev, bbr_t, bbi_t, ccr_t, cci_t, dskip, name, ride=None):
    LP = z.shape[0]
    NB, S, CB = bbr_t.shape
    TS = _tile(LP, 640, 8)
    nt = LP // TS
    ng = TS // 8
    TN = (((0,), (0,)), ((), ()))

    r_ins, r_outs, r_sems = _ride_parts(ride)

    def body(*refs):
        n_in, n_out = 13 + len(r_ins), 8 + len(r_outs)
        if ride is not None:
            start, finish = ride["fn"](refs[13:n_in], refs[n_in + 8:n_in + n_out], *refs[-2:])
            pl.when((pl.program_id(0) == 0) & (pl.program_id(1) == 0))(start)
        step(*refs[:13], *refs[n_in:n_in + 8], *refs[n_in + n_out:n_in + n_out + 9])
        if ride is not None:
            pl.when((pl.program_id(0) == NB - 1) & (pl.program_id(1) == nt - 1))(finish)

    def step(u_ref, dyg_ref, y_ref, hr_ref, hi_ref, tr_ref, ti_ref, coef_ref, bbr_ref, bbi_ref, ccr_ref, cci_ref, ds_ref,
             du_ref, dbbr_ref, dbbi_ref, dccr_ref, dcci_ref, dar_ref, dai_ref, dd_ref,
             gr, gi, carry, acc_bbr, acc_bbi, acc_ccr, acc_cci, acc_a, acc_d):
        i = pl.program_id(1)

        @pl.when(i == 0)
        def _():
            for ref in (carry, acc_bbr, acc_bbi, acc_ccr, acc_cci, acc_a, acc_d):
                ref[...] = jnp.zeros_like(ref)

        u = u_ref[...]
        dy = dyg_ref[...] * _gelu_grad(y_ref[...])
        dyb = dy.astype(BF16)
        gr[...] = jnp.dot(dyb, ccr_ref[...], preferred_element_type=F32)
        gi[...] = -jnp.dot(dyb, cci_ref[...], preferred_element_type=F32)
        row8 = lax.broadcasted_iota(jnp.int32, (8, S), 0)
        first_chunk = i == nt - 1
        tail_r = jnp.where(first_chunk, 0.0, tr_ref[...])
        tail_i = jnp.where(first_chunk, 0.0, ti_ref[...])

        def scan(t, c):
            cr, ci, sar, sai = c
            g = ng - 1 - t
            r0 = pl.multiple_of(g * 8, 8)
            xr = gr[pl.ds(r0, 8), :]
            xi = gi[pl.ds(r0, 8), :]
            for n, k in enumerate((1, 2, 4)):
                xr, xi = _cmul_add(xr, xi, coef_ref[2 * n], coef_ref[2 * n + 1], pltpu.roll(xr, 8 - k, 0), pltpu.roll(xi, 8 - k, 0))
            xr, xi = _cmul_add(xr, xi, coef_ref[6], coef_ref[7], cr, ci)
            gr[pl.ds(r0, 8), :] = xr
            gi[pl.ds(r0, 8), :] = xi
            p0 = pl.multiple_of(jnp.maximum(g - 1, 0) * 8, 8)
            pr = jnp.where(g > 0, hr_ref[pl.ds(p0, 8), :], tail_r)
            pi_ = jnp.where(g > 0, hi_ref[pl.ds(p0, 8), :], tail_i)
            hpr = pltpu.roll(jnp.where(row8 == 7, pr, hr_ref[pl.ds(r0, 8), :]), 1, 0)
            hpi = pltpu.roll(jnp.where(row8 == 7, pi_, hi_ref[pl.ds(r0, 8), :]), 1, 0)
            sar = sar + xr * hpr + xi * hpi
            sai = sai + xi * hpr - xr * hpi
            return jnp.broadcast_to(xr[0:1, :], (8, S)), jnp.broadcast_to(xi[0:1, :], (8, S)), sar, sai

        zero = jnp.zeros((8, S), F32)
        cr, ci, sar, sai = lax.fori_loop(0, ng, scan, (carry[0], carry[1], zero, zero))
        carry[0] = cr
        carry[1] = ci
        acc_a[0] += sar
        acc_a[1] += sai
        grb = gr[...].astype(BF16)
        gib = gi[...].astype(BF16)
        du = (jnp.dot(grb, bbr_ref[...], preferred_element_type=F32) + jnp.dot(gib, bbi_ref[...], preferred_element_type=F32)
              + ds_ref[...] * dy)
        du_ref[...] = du.astype(du_ref.dtype)
        acc_bbr[...] += lax.dot_general(u, grb, TN, preferred_element_type=F32)
        acc_bbi[...] += lax.dot_general(u, gib, TN, preferred_element_type=F32)
        acc_ccr[...] += lax.dot_general(hr_ref[...].astype(BF16), dyb, TN, preferred_element_type=F32)
        acc_cci[...] -= lax.dot_general(hi_ref[...].astype(BF16), dyb, TN, preferred_element_type=F32)
        acc_d[...] += jnp.sum(dy * u.astype(F32), axis=0, keepdims=True)

        @pl.when(i == nt - 1)
        def _():
            dbbr_ref[...] = acc_bbr[...]
            dbbi_ref[...] = acc_bbi[...]
            dccr_ref[...] = acc_ccr[...]
            dcci_ref[...] = acc_cci[...]
            dar_ref[...] = jnp.sum(acc_a[0], axis=0, keepdims=True)
            dai_ref[...] = jnp.sum(acc_a[1], axis=0, keepdims=True)
            dd_ref[...] = acc_d[...]

    ucb = u_col0 // CB
    rev = lambda i: nt - 1 - i
    tail = lambda j, i: (jnp.maximum(rev(i) * ng - 1, 0), j)
    yb = pl.BlockSpec((TS, CB), lambda j, i: (rev(i), j))
    hb = pl.BlockSpec((TS, S), lambda j, i: (rev(i), j))
    in_specs = [
        pl.BlockSpec((TS, CB), lambda j, i: (rev(i), ucb + j)), yb, yb, hb, hb,
        pl.BlockSpec((8, S), tail), pl.BlockSpec((8, S), tail),
        pl.BlockSpec((None, 8, 8, S), lambda j, i: (j, 0, 0, 0)),
        pl.BlockSpec((None, S, CB), lambda j, i: (j, 0, 0)),
        pl.BlockSpec((None, S, CB), lambda j, i: (j, 0, 0)),
        pl.BlockSpec((None, CB, S), lambda j, i: (j, 0, 0)),
        pl.BlockSpec((None, CB, S), lambda j, i: (j, 0, 0)),
        pl.BlockSpec((1, CB), lambda j, i: (0, j)),
    ]
    mat_cs = pl.BlockSpec((None, CB, S), lambda j, i: (j, 0, 0))
    mat_sc = pl.BlockSpec((None, S, CB), lambda j, i: (j, 0, 0))
    vec_s = pl.BlockSpec((None, 1, S), lambda j, i: (j, 0, 0))
    out_specs = [yb, mat_cs, mat_cs, mat_sc, mat_sc, vec_s, vec_s, pl.BlockSpec((1, CB), lambda j, i: (0, j))]
    out_shape = [_sds((LP, NB * CB), BF16), _sds((NB, CB, S), F32), _sds((NB, CB, S), F32), _sds((NB, S, CB), F32),
                 _sds((NB, S, CB), F32), _sds((NB, 1, S), F32), _sds((NB, 1, S), F32), _sds((1, NB * CB), F32)]
    scratch = [pltpu.VMEM((TS, S), F32), pltpu.VMEM((TS, S), F32), pltpu.VMEM((2, 8, S), F32),
               pltpu.VMEM((CB, S), F32), pltpu.VMEM((CB, S), F32), pltpu.VMEM((S, CB), F32), pltpu.VMEM((S, CB), F32),
               pltpu.VMEM((2, 8, S), F32), pltpu.VMEM((1, CB), F32)]
    est = 2 * (2 * TS * S * 4 + 4 * TS * CB * 4 + 8 * 8 * S * 4 + 12 * CB * S * 4) + 4 * TS * S * 4
    return pl.pallas_call(
        body, grid=(NB, nt), in_specs=in_specs + [HBM_SPEC] * len(r_ins), out_specs=out_specs + [HBM_SPEC] * len(r_outs),
        out_shape=out_shape + r_outs, scratch_shapes=scratch + r_sems,
        compiler_params=_cparams(("arbitrary", "arbitrary"), est), name=name,
    )(z, dyg, y, hr, hi, hr, hi, coef_rev, bbr_t, bbi_t, ccr_t, cci_t, dskip, *r_ins)


def _merge_fwd(yab, z, ao, D, ga0, gb0, name):
    LP = z.shape[0]
    tm = _tile(LP, 640, 8)
    tn = _ctile(512, D, ga0, gb0)
    nj = D // tn

    def body(ya_ref, yb_ref, ga_ref, gb_ref, ao_ref, o_ref):
        f = lambda r: r[...].astype(F32)
        ssm = f(ya_ref) * _sig(f(yb_ref))
        o_ref[...] = (_sig(f(ga_ref)) * ssm + _sig(f(gb_ref)) * f(ao_ref)).astype(o_ref.dtype)

    blk = lambda c0: pl.BlockSpec((tm, tn), lambda i, j: (i, c0 // tn + j))
    return pl.pallas_call(
        body, grid=(LP // tm, nj), in_specs=[blk(0), blk(D), blk(ga0), blk(gb0), blk(0)], out_specs=blk(0),
        out_shape=_sds((LP, D), BF16), compiler_params=_cparams(("parallel", "parallel"), 2 * 6 * tm * tn * 4), name=name,
    )(yab, yab, z, z, ao)


def _merge_bwd(dm, yab, z, ao, D, ga0, gb0, name):
    LP = z.shape[0]
    tm = _tile(LP, 640, 8)
    tn = _ctile(512, D, ga0, gb0)
    nj = D // tn

    def body(dm_ref, ya_ref, yb_ref, ga_ref, gb_ref, ao_ref, dya_ref, dyb_ref, dga_ref, dgb_ref, dao_ref):
        f = lambda r: r[...].astype(F32)
        dmv, ya, ao_v = f(dm_ref), f(ya_ref), f(ao_ref)
        sa, sb, sy = _sig(f(ga_ref)), _sig(f(gb_ref)), _sig(f(yb_ref))
        t = dmv * sa
        dya_ref[...] = (t * sy).astype(BF16)
        dyb_ref[...] = (t * ya * sy * (1.0 - sy)).astype(BF16)
        dga_ref[...] = (dmv * (ya * sy) * sa * (1.0 - sa)).astype(BF16)
        dgb_ref[...] = (dmv * ao_v * sb * (1.0 - sb)).astype(BF16)
        dao_ref[...] = (dmv * sb).astype(BF16)

    blk = lambda c0: pl.BlockSpec((tm, tn), lambda i, j: (i, c0 // tn + j))
    return pl.pallas_call(
        body, grid=(LP // tm, nj), in_specs=[blk(0), blk(0), blk(D), blk(ga0), blk(gb0), blk(0)], out_specs=[blk(0)] * 5,
        out_shape=[_sds((LP, D), BF16)] * 5, compiler_params=_cparams(("parallel", "parallel"), 2 * 11 * tm * tn * 4), name=name,
    )(dm, yab, yab, z, z, ao)


def _shift_down(x, halo, k, row8):
    s = pltpu.roll(x, k, 0)
    top = jnp.where(row8 < k, pltpu.roll(halo, k, 0), s[0:8])
    return jnp.concatenate([top, s[8:]], axis=0) if x.shape[0] > 8 else top


def _shift_up(x, halo, k, row8):
    tm = x.shape[0]
    s = pltpu.roll(x, tm - k, 0)
    bot = jnp.where(row8 >= 8 - k, pltpu.roll(halo, 8 - k, 0), s[tm - 8:])
    return jnp.concatenate([s[:tm - 8], bot], axis=0) if tm > 8 else bot


def _conv_gate(g, halo, w_ref, cb, row8):
    return cb + w_ref[0:1, :] * _shift_down(g, halo, 2, row8) + w_ref[1:2, :] * _shift_down(g, halo, 1, row8) + w_ref[2:3, :] * g


def _convact_fwd(gu, cw, cb, DFF, name):
    LP = gu.shape[0]
    tm = _tile(LP, 640, 8)
    tn = _tile(DFF, 512)
    nj = DFF // tn
    t8 = tm // 8

    def body(g_ref, h_ref, u_ref, w_ref, b_ref, o_ref):
        i = pl.program_id(0)
        row8 = lax.broadcasted_iota(jnp.int32, (8, tn), 0)
        g = g_ref[...].astype(F32)
        halo = jnp.where(i > 0, h_ref[...].astype(F32), 0.0)
        gc = _conv_gate(g, halo, w_ref, b_ref[...], row8)
        o_ref[...] = (gc * _sig(gc) * u_ref[...].astype(F32)).astype(o_ref.dtype)

    in_specs = [
        pl.BlockSpec((tm, tn), lambda i, j: (i, j)),
        pl.BlockSpec((8, tn), lambda i, j: (jnp.maximum(i * t8 - 1, 0), j)),
        pl.BlockSpec((tm, tn), lambda i, j: (i, nj + j)),
        pl.BlockSpec((3, tn), lambda i, j: (0, j)),
        pl.BlockSpec((1, tn), lambda i, j: (0, j)),
    ]
    return pl.pallas_call(
        body, grid=(LP // tm, nj), in_specs=in_specs, out_specs=pl.BlockSpec((tm, tn), lambda i, j: (i, j)),
        out_shape=_sds((LP, DFF), BF16), compiler_params=_cparams(("parallel", "parallel"), 2 * 8 * tm * tn * 4), name=name,
    )(gu, gu, gu, cw, cb)


def _convact_bwd(dact, gu, cw, cb, DFF, name):
    LP = gu.shape[0]
    tm = _tile(LP, 640, 8)
    tn = _tile(DFF, 512)
    nj = DFF // tn
    t8 = tm // 8

    def body(da_ref, g_ref, h_ref, u_ref, w_ref, b_ref, dgc_ref, du_ref):
        i = pl.program_id(0)
        row8 = lax.broadcasted_iota(jnp.int32, (8, tn), 0)
        g = g_ref[...].astype(F32)
        halo = jnp.where(i > 0, h_ref[...].astype(F32), 0.0)
        gc = _conv_gate(g, halo, w_ref, b_ref[...], row8)
        sg = _sig(gc)
        da = da_ref[...].astype(F32)
        du_ref[...] = (da * gc * sg).astype(du_ref.dtype)
        dgc_ref[...] = (da * u_ref[...].astype(F32) * sg * (1.0 + gc * (1.0 - sg))).astype(dgc_ref.dtype)

    blk = pl.BlockSpec((tm, tn), lambda i, j: (i, j))
    in_specs = [
        blk, blk,
        pl.BlockSpec((8, tn), lambda i, j: (jnp.maximum(i * t8 - 1, 0), j)),
        pl.BlockSpec((tm, tn), lambda i, j: (i, nj + j)),
        pl.BlockSpec((3, tn), lambda i, j: (0, j)),
        pl.BlockSpec((1, tn), lambda i, j: (0, j)),
    ]
    return pl.pallas_call(
        body, grid=(LP // tm, nj), in_specs=in_specs, out_specs=[blk, blk],
        out_shape=[_sds((LP, DFF), BF16), _sds((LP, DFF), BF16)],
        compiler_params=_cparams(("parallel", "parallel"), 2 * 10 * tm * tn * 4), name=name,
    )(dact, gu, gu, gu, cw, cb)


def _conv_bwd(dgc, gu, cw, DFF, pad, name):
    LP = gu.shape[0]
    tm = _tile(LP, 640, 8)
    tn = _tile(DFF, 512)
    nj = DFF // tn
    t8 = tm // 8
    nt = LP // tm

    def body(d_ref, dn_ref, g_ref, h_ref, w_ref, dg_ref, dw_ref, db_ref):
        i = pl.program_id(1)
        row8 = lax.broadcasted_iota(jnp.int32, (8, tn), 0)
        d = d_ref[...].astype(F32)
        nxt = jnp.where(i < nt - 1, dn_ref[...].astype(F32), 0.0)
        dg = w_ref[2:3, :] * d + w_ref[1:2, :] * _shift_up(d, nxt, 1, row8) + w_ref[0:1, :] * _shift_up(d, nxt, 2, row8)
        rows = i * tm + lax.broadcasted_iota(jnp.int32, (tm, 1), 0)
        dg_ref[...] = jnp.where(rows >= pad, dg, 0.0).astype(dg_ref.dtype)
        g = g_ref[...].astype(F32)
        halo = jnp.where(i > 0, h_ref[...].astype(F32), 0.0)
        row3 = lax.broadcasted_iota(jnp.int32, (3, tn), 0)
        s0 = jnp.sum(d * _shift_down(g, halo, 2, row8), axis=0, keepdims=True)
        s1 = jnp.sum(d * _shift_down(g, halo, 1, row8), axis=0, keepdims=True)
        s2 = jnp.sum(d * g, axis=0, keepdims=True)
        dw = jnp.where(row3 == 0, s0, jnp.where(row3 == 1, s1, s2))
        dbp = jnp.sum(d, axis=0, keepdims=True)

        @pl.when(i == 0)
        def _():
            dw_ref[...] = dw
            db_ref[...] = dbp

        @pl.when(i > 0)
        def _():
            dw_ref[...] += dw
            db_ref[...] += dbp

    blk = pl.BlockSpec((tm, tn), lambda j, i: (i, j))
    in_specs = [
        blk,
        pl.BlockSpec((8, tn), lambda j, i: (jnp.minimum((i + 1) * t8, LP // 8 - 1), j)),
        blk,
        pl.BlockSpec((8, tn), lambda j, i: (jnp.maximum(i * t8 - 1, 0), j)),
        pl.BlockSpec((3, tn), lambda j, i: (0, j)),
    ]
    out_specs = [blk, pl.BlockSpec((3, tn), lambda j, i: (0, j)), pl.BlockSpec((1, tn), lambda j, i: (0, j))]
    return pl.pallas_call(
        body, grid=(nj, nt), in_specs=in_specs, out_specs=out_specs,
        out_shape=[_sds((LP, DFF), BF16), _sds((3, DFF), F32), _sds((1, DFF), F32)],
        compiler_params=_cparams(("parallel", "arbitrary"), 2 * 10 * tm * tn * 4), name=name,
    )(dgc, dgc, gu, gu, cw)


def _adamw_math(w, g, m, v):
    m = ADAM_B1 * m + (1.0 - ADAM_B1) * g
    v = ADAM_B2 * v + (1.0 - ADAM_B2) * (g * g)
    m_hat = m / (1.0 - ADAM_B1 ** ADAM_STEP)
    v_hat = v / (1.0 - ADAM_B2 ** ADAM_STEP)
    delta = -ADAM_LR * (m_hat / (jnp.sqrt(v_hat) + ADAM_EPS) + ADAM_WD * w)
    return delta, m, v


def _adamw(w, g, m, v, name):
    R, C = g.shape
    tm = R if R * C * 4 <= (1 << 20) else _tile(R, max(8, ((1 << 20) // (C * 4)) // 8 * 8), 8)

    def body(w_ref, g_ref, m_ref, v_ref, d_ref, mo_ref, vo_ref):
        d_ref[...], mo_ref[...], vo_ref[...] = _adamw_math(w_ref[...], g_ref[...], m_ref[...], v_ref[...])

    blk = pl.BlockSpec((tm, C), lambda i: (i, 0))
    wblk = blk if w.ndim == 2 else pl.BlockSpec((None, tm, C), lambda i: (0, i, 0))
    return pl.pallas_call(
        body, grid=(R // tm,), in_specs=[wblk, blk, wblk, wblk], out_specs=[wblk] * 3, out_shape=[_sds(w.shape, F32)] * 3,
        compiler_params=_cparams(("parallel",), 2 * 7 * tm * (C + LANES) * 4), name=name,
    )(w, g, m, v)


def _sum_adamw(parts, w, m, v, name):
    n, R, C = parts.shape
    tm = _tile(R, 256, 8)

    def body(p_ref, w_ref, m_ref, v_ref, g_ref, d_ref, mo_ref, vo_ref):
        g = p_ref[0]
        for k in range(1, n):
            g = g + p_ref[k]
        g_ref[...] = g
        d_ref[...], mo_ref[...], vo_ref[...] = _adamw_math(w_ref[...], g, m_ref[...], v_ref[...])

    blk = pl.BlockSpec((tm, C), lambda i: (i, 0))
    return pl.pallas_call(
        body, grid=(R // tm,), in_specs=[pl.BlockSpec((n, tm, C), lambda i: (0, i, 0))] + [blk] * 3, out_specs=[blk] * 4,
        out_shape=[_sds((R, C), F32)] * 4,
        compiler_params=_cparams(("parallel",), 2 * (n + 7) * tm * C * 4), name=name,
    )(parts, w, m, v)


def _add_half(g, got, c_idx, name):
    n, R, C = g.shape
    HR = R // 2
    tm = _tile(HR, max(8, ((1 << 20) // (C * 4)) // 8 * 8), 8)
    nb = HR // tm

    def body(c_ref, g_ref, t_ref, o_ref):
        o_ref[...] = (g_ref[...] + t_ref[...]).astype(o_ref.dtype)

    grid_spec = pltpu.PrefetchScalarGridSpec(
        num_scalar_prefetch=1, grid=(n, nb),
        in_specs=[pl.BlockSpec((None, tm, C), lambda k, i, c: (k, c[0] * nb + i, 0)),
                  pl.BlockSpec((None, tm, C), lambda k, i, c: (k, i, 0))],
        out_specs=pl.BlockSpec((None, tm, C), lambda k, i, c: (k, i, 0)))
    return pl.pallas_call(
        body, grid_spec=grid_spec, out_shape=_sds((n, HR, C), BF16),
        compiler_params=_cparams(("parallel", "parallel"), 2 * 3 * tm * (C + LANES) * 4), name=name,
    )(c_idx, g, got)


def _sum_half(g, got, land, chip_c, name):
    n, R, C = g.shape
    HR = R // 2
    tm = _tile(HR, max(8, ((1 << 20) // (C * 4)) // 8 * 8), 8)
    nb = HR // tm

    def body(s_ref, g_ref, t_ref, l_ref, o_ref):
        acc = g_ref[...] + t_ref[...]
        for k in range(3):
            acc = acc + l_ref[k].astype(F32)
        o_ref[...] = acc

    grid_spec = pltpu.PrefetchScalarGridSpec(
        num_scalar_prefetch=1, grid=(nb,),
        in_specs=[pl.BlockSpec((None, tm, C), lambda i, sc: (sc[0], sc[1] * nb + i, 0)),
                  pl.BlockSpec((None, tm, C), lambda i, sc: (sc[0], i, 0)),
                  pl.BlockSpec((3, tm, C), lambda i, sc: (0, i, 0))],
        out_specs=pl.BlockSpec((tm, C), lambda i, sc: (sc[1] * nb + i, 0)))
    return pl.pallas_call(
        body, grid_spec=grid_spec, out_shape=_sds((R, C), F32),
        compiler_params=_cparams(("parallel",), 2 * 6 * tm * (C + LANES) * 4), name=name,
    )(chip_c, g, got, land)


def _cast_slot(w, chip_idx, name):
    R, C = w.shape
    tm = _tile(R, max(16, ((1 << 20) // (C * 4)) // 16 * 16), 16)

    def body(s_ref, w_ref, o_ref):
        o_ref[...] = w_ref[...].astype(o_ref.dtype)

    grid_spec = pltpu.PrefetchScalarGridSpec(
        num_scalar_prefetch=1, grid=(R // tm,),
        in_specs=[pl.BlockSpec((tm, C), lambda i, sc: (i, 0))],
        out_specs=pl.BlockSpec((None, tm, C), lambda i, sc: (sc[0], i, 0)))
    return pl.pallas_call(
        body, grid_spec=grid_spec, out_shape=_sds((4, R, C), BF16),
        compiler_params=_cparams(("parallel",), 2 * 2 * tm * (C + LANES) * 4), name=name,
    )(chip_idx, w)


def _gather_chips(gs, name):
    nw = len(gs)

    def body(*refs):
        start, forward, finish = _gather_phases(refs[nw:2 * nw], *refs[2 * nw:])
        start()
        forward()
        finish()

    return pl.pallas_call(
        body, out_shape=[_sds(g.shape, g.dtype) for g in gs], in_specs=[HBM_SPEC] * nw, out_specs=[HBM_SPEC] * nw,
        scratch_shapes=[pltpu.SemaphoreType.DMA((6 * nw,)), pltpu.SemaphoreType.DMA((6 * nw,))],
        input_output_aliases={i: i for i in range(nw)}, name=name,
    )(*gs)


def _sibling_halves(gs, name):
    nw = len(gs)
    assert all(g.shape[1] % 16 == 0 for g in gs)

    def body(*refs):
        start, finish = _sibling_phases(refs[:nw], refs[nw:2 * nw], *refs[2 * nw:])
        start()
        finish()

    return pl.pallas_call(
        body, out_shape=[_sds((4, g.shape[1] // 2, g.shape[2]), g.dtype) for g in gs],
        in_specs=[HBM_SPEC] * nw, out_specs=[HBM_SPEC] * nw,
        scratch_shapes=[pltpu.SemaphoreType.DMA((nw,)), pltpu.SemaphoreType.DMA((nw,))], name=name,
    )(*gs)


def _sibling_join(fs, name):
    nw = len(fs)
    assert all(f.shape[0] % 16 == 0 for f in fs)

    def body(*refs):
        o_refs = refs[nw:2 * nw]
        send_sems, recv_sems = refs[2 * nw:]
        x, y, c = _place()

        def copy(i, half):
            HR = o_refs[i].shape[0] // 2
            rows = o_refs[i].at[pl.ds(pl.multiple_of(half * HR, 8), HR), :]
            return pltpu.make_async_remote_copy(
                src_ref=rows, dst_ref=rows, send_sem=send_sems.at[i], recv_sem=recv_sems.at[i],
                device_id=(x, y, 1 - c), device_id_type=MESH)

        sends = [copy(i, c) for i in range(nw)]
        for cp in sends:
            cp.start()
        for i in range(nw):
            copy(i, 1 - c).wait_recv()
        for cp in sends:
            cp.wait_send()

    return pl.pallas_call(
        body, out_shape=[_sds(f.shape, f.dtype) for f in fs], in_specs=[HBM_SPEC] * nw, out_specs=[HBM_SPEC] * nw,
        scratch_shapes=[pltpu.SemaphoreType.DMA((nw,)), pltpu.SemaphoreType.DMA((nw,))],
        input_output_aliases={i: i for i in range(nw)}, name=name,
    )(*fs)


def _gather_all(v, name):
    M, W = v.shape

    def body(v_ref, o_ref, send_sems, recv_sems, local_sem):
        x, y, c = _place()
        me, sibling = (x, y, c), (x, y, 1 - c)
        chips = _other_chips(x, y)

        def slot(px, py, pc):
            return o_ref.at[4 * px + 2 * py + pc]

        def copy(k, block, to, src=None):
            return pltpu.make_async_remote_copy(
                src_ref=slot(*block) if src is None else src, dst_ref=slot(*block),
                send_sem=send_sems.at[k], recv_sem=recv_sems.at[k], device_id=to, device_id_type=MESH)

        mine = pltpu.make_async_copy(v_ref, slot(*me), local_sem)
        mine.start()
        first = [copy(0, me, sibling, src=v_ref)]
        first += [copy(1 + j, me, (*chip, c), src=v_ref) for j, chip in enumerate(chips)]
        for cp in first:
            cp.start()
        passed = [copy(4 + j, (*chip, c), sibling) for j, chip in enumerate(chips)]
        for j, chip in enumerate(chips):
            copy(1 + j, (*chip, c), me).wait_recv()
            passed[j].start()
        copy(0, sibling, me).wait_recv()
        for j, chip in enumerate(chips):
            copy(4 + j, (*chip, 1 - c), me).wait_recv()
        for cp in first + passed:
            cp.wait_send()
        mine.wait()

    vm = pl.BlockSpec(memory_space=pltpu.VMEM)
    return pl.pallas_call(
        body, out_shape=_sds((8, M, W), v.dtype), in_specs=[vm], out_specs=vm,
        scratch_shapes=[pltpu.SemaphoreType.DMA((7,)), pltpu.SemaphoreType.DMA((7,)), pltpu.SemaphoreType.DMA(())],
        compiler_params=pltpu.CompilerParams(vmem_limit_bytes=int(min(10 * M * W * 4 + (8 << 20), V7X_VMEM_BYTES - (8 << 20)))),
        name=name,
    )(v)


def _rows_for(n_elems, width, mult=8):
    rows = -(-n_elems // width)
    return -(-rows // mult) * mult


def _pack_small(arrs, total_rows):
    parts = []
    used = 0
    for a in arrs:
        rows = _rows_for(a.size, LANES)
        parts.append(jnp.pad(a.reshape(-1), (0, rows * LANES - a.size)).reshape(rows, LANES))
        used += rows
    if total_rows > used:
        parts.append(jnp.zeros((total_rows - used, LANES), F32))
    return jnp.concatenate(parts, axis=0)


def _unpack_small(p, shapes):
    outs, r = [], 0
    lead = p.shape[:-2]
    for shp in shapes:
        n = int(np.prod(shp))
        rows = _rows_for(n, LANES)
        outs.append(p[..., r:r + rows, :].reshape(lead + (rows * LANES,))[..., :n].reshape(lead + tuple(shp)))
        r += rows
    return outs


def _chip_pieces(lo, hi, n):
    out = []
    while lo < hi:
        q = lo // n
        b = min(hi, (q + 1) * n)
        out.append((q, lo - q * n, b - q * n))
        lo = b
    return out


def _chips_to_cols(w):
    n4, K, n = w.shape
    return w.transpose(1, 0, 2).reshape(K, n4 * n)


def _block_diag(m, gpb):
    G, A, B = m.shape
    nb = G // gpb
    eye = jnp.eye(gpb, dtype=m.dtype)
    t = m.reshape(nb, gpb, A, B)[:, :, :, None, :] * eye[None, :, None, :, None]
    return t.reshape(nb, gpb * A, gpb * B)


def _block_diag_extract(m, gpb, A, B):
    nb = m.shape[0]
    t = m.reshape(nb, gpb, A, gpb, B)
    eye = jnp.eye(gpb, dtype=m.dtype)
    d = jnp.sum(t * eye[None, :, None, :, None], axis=3)
    return d.reshape(nb * gpb, A, B)


def kernel(x, meta, g_mix, w_in, b_f, lam_re, lam_im, log_dt, b_re, b_im, c_re, c_im, d_skip, w_glu, w_attn_o, w_out, g_ffn, w_up, conv_w, conv_b, w_down, g_final, loss_target, m_meta, m_g_mix, m_w_in, m_b_f, m_lam_re, m_lam_im, m_log_dt, m_b_re, m_b_im, m_c_re, m_c_im, m_d_skip, m_w_glu, m_w_attn_o, m_w_out, m_g_ffn, m_w_up, m_conv_w, m_conv_b, m_w_down, m_g_final, v_meta, v_g_mix, v_w_in, v_b_f, v_lam_re, v_lam_im, v_log_dt, v_b_re, v_b_im, v_c_re, v_c_im, v_d_skip, v_w_glu, v_w_attn_o, v_w_out, v_g_ffn, v_w_up, v_conv_w, v_conv_b, v_w_down, v_g_final):
    args = dict(locals())
    L, D = x.shape[1], x.shape[2]
    NM = meta.shape[0]
    H = b_f.shape[1]
    DA = H * HEAD_DIM
    G, P, C = b_re.shape[1:]
    DS, GP = G * C, G * P
    DFF = conv_b.shape[1]
    PAD = (-NM) % LANES
    OFF = PAD + NM
    LP = OFF + L
    NZ = 3 * DA + DS + 2 * D
    U0, GA0, GB0 = 3 * DA, 3 * DA + DS, 3 * DA + DS + D
    NB = G // GROUPS_PER_BLOCK
    chip = 2 * lax.axis_index("x") + lax.axis_index("y")
    core = lax.axis_index("c")

    big = ["w_in", "w_glu", "w_attn_o", "w_out", "w_up", "w_down"]
    local = {n: args[n][0] for n in big}
    chip_idx = chip.reshape(1).astype(jnp.int32)
    slots = {n: _cast_slot(local[n], chip_idx, "cast_" + n) for n in big}
    gathered = {"w_in": _gather_chips([slots["w_in"]], "gather_w_in")[0]}
    tiny_shapes = [conv_w.shape[1:], meta.shape]
    tiny_rows = sum(_rows_for(int(np.prod(sh)), LANES) for sh in tiny_shapes)
    tiny = _gather_all(_pack_small([conv_w[0], meta], tiny_rows), "gather_small_weights")[0::2]
    conv_w_c, meta_c = _unpack_small(tiny, tiny_shapes)
    conv_w_f = _chips_to_cols(conv_w_c)
    meta_full = _chips_to_cols(meta_c)
    g_in = gathered["w_in"]
    n_in = g_in.shape[2]
    cols = lambda lo, hi: [g_in[q, :, a:b] for q, a, b in _chip_pieces(lo, hi, n_in)]
    w_zf = jnp.concatenate(cols(0, 3 * DA) + cols(3 * DA + H, 4 * n_in) + cols(3 * DA, 3 * DA + H)
                           + [jnp.zeros((D, LANES - H), BF16)], axis=1)
    N_GLU, N_AO, N_UP = (slots[n].shape[2] for n in ("w_glu", "w_attn_o", "w_up"))

    col = lambda a: a.reshape(GP, 1)
    lr_c, li_c = col(lam_re[0]), col(lam_im[0])
    ldt_c = jnp.repeat(log_dt[0], P).reshape(GP, 1)
    br2, bi2 = b_re[0].reshape(GP, C), b_im[0].reshape(GP, C)
    a_re, a_im, bb_re, bb_im, pw_re, pw_im = _ssm_prep(lr_c, li_c, ldt_c, br2, bi2, "ssm_prep")
    S = GROUPS_PER_BLOCK * P
    CB = GROUPS_PER_BLOCK * C
    pw_r = pw_re.T.reshape(8, NB, S).transpose(1, 0, 2)
    pw_i = pw_im.T.reshape(8, NB, S).transpose(1, 0, 2)
    row8 = jnp.arange(8)[None, :, None]

    def masked_power(pw, k, keep):
        return jnp.where(keep, pw[:, k - 1][:, None, :], 0.0)

    coef = jnp.stack(
        [masked_power(pw, k, row8 >= k) for k in (1, 2, 4) for pw in (pw_r, pw_i)] + [pw_r, pw_i], axis=1)
    coef_rev = jnp.stack(
        [masked_power(pw, k, row8 < 8 - k) for k in (1, 2, 4) for pw in (pw_r, -pw_i)]
        + [pw_r[:, ::-1], -pw_i[:, ::-1]], axis=1)
    bd = lambda m: _block_diag(m, GROUPS_PER_BLOCK)
    bbr3, bbi3 = bb_re.reshape(G, P, C), bb_im.reshape(G, P, C)
    bbr_cs = bd(bbr3.transpose(0, 2, 1)).astype(BF16)
    bbi_cs = bd(bbi3.transpose(0, 2, 1)).astype(BF16)
    bbr_sc = bd(bbr3).astype(BF16)
    bbi_sc = bd(bbi3).astype(BF16)
    ccr_sc = bd(c_re[0].transpose(0, 2, 1)).astype(BF16)
    cci_sc = bd(c_im[0].transpose(0, 2, 1)).astype(BF16)
    ccr_cs = bd(c_re[0]).astype(BF16)
    cci_cs = bd(c_im[0]).astype(BF16)

    h0 = jnp.concatenate([jnp.zeros((PAD, D), F32), meta_full, x[0]], axis=0)
    n1 = _rms_fwd(h0, g_mix, "rms_mix")
    z = _mm(n1, w_zf, "nn", LP, NZ, D, BF16, "in_proj")
    fpre = _mm(n1, w_zf, "nn", LP, LANES, D, F32, "in_proj_f", b_off=(0, NZ))
    bf_pad = jnp.pad(b_f, ((0, 0), (0, LANES - H)))
    fcum = _fgate_fwd(fpre, bf_pad, PAD, "fgate_fwd")
    key_bias = jnp.where(jnp.arange(LP)[:, None] >= PAD, -fcum, NEG)
    bias_t = key_bias.T[:H].reshape(H, 1, LP)
    attn, attn_f32, lse_t, *rest = _attn_fwd(z, bias_t, H, PAD, "attn_fwd", gather=[slots[n] for n in big[1:]])
    gathered.update(zip(big[1:], rest))
    w_glu_c, w_ao_c, w_up_c = gathered["w_glu"], gathered["w_attn_o"], gathered["w_up"]
    w_out_f = gathered["w_out"].reshape(D, D)
    w_down_f = gathered["w_down"].reshape(DFF, D)
    ao = _mm(attn, w_ao_c, "nn", LP, D, DA, BF16, "attn_out", b_chips=N_AO)
    y, yg, hs_re, hs_im = _ssm_fwd(z, U0, coef, bbr_cs, bbi_cs, ccr_sc, cci_sc, d_skip, "ssm_fwd")
    yab = _mm(yg, w_glu_c, "nn", LP, 2 * D, DS, BF16, "glu_proj", b_chips=N_GLU)
    merged = _merge_fwd(yab, z, ao, D, GA0, GB0, "merge_fwd")
    h1 = _mm(merged, w_out_f, "nn", LP, D, D, F32, "out_proj", res=h0)
    n2 = _rms_fwd(h1, g_ffn, "rms_ffn")
    gu = _mm(n2, w_up_c, "nn", LP, 2 * DFF, D, BF16, "up_proj", tn=1408, b_chips=N_UP)
    act = _convact_fwd(gu, conv_w_f, conv_b, DFF, "convact_fwd")
    h2 = _mm(act, w_down_f, "nn", LP, D, DFF, F32, "down_proj", res=h1, tn=512, tk=DFF)
    dh2, dg_final, loss_v = _final_loss(h2, g_final.reshape(1, D), loss_target[0], OFF, "final_loss")
    loss = lax.psum(loss_v[0, 0], ("x", "y", "c"))

    KW = dict(tm=512, tn=256, tk=LP)
    dact = _mm(dh2, w_down_f, "nt", LP, DFF, D, BF16, "down_bwd_x")
    dw_down = _mm(act, dh2, "tn", DFF, D, LP, F32, "down_bwd_w", **KW)
    dgc, du_ffn = _convact_bwd(dact, gu, conv_w_f, conv_b, DFF, "convact_bwd")
    dg_ffn_in, dconv_w, dconv_b = _conv_bwd(dgc, gu, conv_w_f, DFF, PAD, "conv_bwd")
    dn2 = _mm(dg_ffn_in, w_up_c, "nt", LP, D, DFF, F32, "up_bwd_x_g", tn=512, tk=N_UP, b_chips=N_UP)
    dn2 = _mm(du_ffn, w_up_c, "nt", LP, D, DFF, F32, "up_bwd_x_u", res=dn2, b_off=(0, DFF), tn=512, tk=N_UP, b_chips=N_UP)
    dw_up = _mm(n2, dg_ffn_in, "tn", D, DFF, LP, F32, "up_bwd_w_g", tm=512, tn=256, tk=LP, out_chips=N_UP,
                out_into=(lax.empty((4, D, N_UP), F32), 0))
    dw_up = _mm(n2, du_ffn, "tn", D, DFF, LP, F32, "up_bwd_w_u", tm=512, tn=256, tk=LP, out_chips=N_UP,
                out_into=(dw_up, DFF // N_UP))
    dh1, dg_ffn = _rms_bwd(h1, g_ffn, dn2, dh2, "rms_ffn_bwd")
    c_idx = core.reshape(1).astype(jnp.int32)
    chip_c = jnp.stack([chip, core]).astype(jnp.int32)

    dmerged = _mm(dh1, w_out_f, "nt", LP, D, D, F32, "out_bwd_x")
    dw_out = _mm(merged, dh1, "tn", D, D, LP, F32, "out_bwd_w", **KW)
    dya, dyb, dga, dgb, dao = _merge_bwd(dmerged, yab, z, ao, D, GA0, GB0, "merge_bwd")
    dattn = _mm(dao, w_ao_c, "nt", LP, DA, D, BF16, "attn_out_bwd_x", b_chips=N_AO)
    dw_ao = _mm(attn, dao, "tn", DA, D, LP, F32, "attn_out_bwd_w", out_chips=N_AO, **KW)
    dyg = _mm(dya, w_glu_c, "nt", LP, DS, D, F32, "glu_bwd_x_a", b_chips=N_GLU)
    dyg = _mm(dyb, w_glu_c, "nt", LP, DS, D, F32, "glu_bwd_x_b", res=dyg, b_off=(0, D), b_chips=N_GLU)
    dw_glu = _mm(yg, dya, "tn", DS, D, LP, F32, "glu_bwd_w_a", out_chips=N_GLU,
                 out_into=(lax.empty((4, DS, N_GLU), F32), 0), **KW)
    dw_glu = _mm(yg, dyb, "tn", DS, D, LP, F32, "glu_bwd_w_b", out_chips=N_GLU, out_into=(dw_glu, D // N_GLU), **KW)
    early = ["w_glu", "w_attn_o", "w_out", "w_up", "w_down"]
    early_grads = [dw_glu, dw_ao, dw_out.reshape(4, D // 4, D), dw_up, dw_down.reshape(4, DFF // 4, D)]
    (du_ssm, dbbr_d, dbbi_d, dccr_d, dcci_d, dar_b, dai_b, dd_skip, *early_got) = _ssm_bwd(
        z, U0, dyg, y, hs_re, hs_im, coef_rev, bbr_sc, bbi_sc, ccr_cs, cci_cs, d_skip, "ssm_bwd",
        ride=_sibling_ride(early_grads))
    delta_t = _attn_delta(dattn, attn_f32, H, "attn_delta")
    early_part = [_add_half(g, t, c_idx, "rs_add_" + n) for n, g, t in zip(early, early_grads, early_got)]
    dq, dk, dv, dbias_t, *early_land = _attn_bwd(z, dattn, lse_t, delta_t, bias_t, H, "attn_bwd", scatter=early_part)
    dF = jnp.pad(-dbias_t[:, 0, :].T, ((0, 0), (0, LANES - H)))
    dfpre, db_f = _fgate_bwd(dF, fpre, bf_pad, PAD, "fgate_bwd")
    dz = jnp.concatenate([dq, dk, dv, du_ssm, dga, dgb, dfpre.astype(BF16)], axis=1)
    dw_zf = _mm(n1, dz, "tn", D, NZ + LANES, LP, F32, "in_bwd_w", tm=512, tn=640, tk=LP)
    def orig_cols(lo, hi):
        parts_ = []
        for a, b, shift in ((0, 3 * DA, 0), (3 * DA, 3 * DA + H, NZ - 3 * DA), (3 * DA + H, 4 * n_in, -H)):
            a, b = max(a, lo), min(b, hi)
            if a < b:
                parts_.append(dw_zf[:, a + shift:b + shift])
        return jnp.concatenate(parts_, axis=1)

    late_grads = [jnp.stack([orig_cols(q * n_in, (q + 1) * n_in) for q in range(4)], axis=0)]
    late_got = _sibling_halves(late_grads, "rs_sibling_w_in")
    late_part = [_add_half(late_grads[0], late_got[0], c_idx, "rs_add_w_in")]
    dn1, *late_land = _mm(dz, w_zf, "nt", LP, D, NZ + LANES, F32, "in_bwd_x", tm=640, tn=256, tk=NZ + LANES,
                          ride=_scatter_ride(late_part))
    dh0, dg_mix = _rms_bwd(h0, g_mix, dn1, dh1, "rms_mix_bwd")
    grad_x = dh0[OFF:][None]
    dmeta_full = dh0[PAD:OFF]

    ext = lambda m, A, B: _block_diag_extract(m, GROUPS_PER_BLOCK, A, B)
    dbb_re = ext(dbbr_d, C, P).transpose(0, 2, 1).reshape(GP, C)
    dbb_im = ext(dbbi_d, C, P).transpose(0, 2, 1).reshape(GP, C)
    dc_re = ext(dccr_d, P, C).transpose(0, 2, 1)[None]
    dc_im = ext(dcci_d, P, C).transpose(0, 2, 1)[None]
    glr, gli, gldt, gbr, gbi = _ssm_prep_bwd(lr_c, li_c, ldt_c, br2, bi2, dar_b.reshape(GP, 1), dai_b.reshape(GP, 1),
                                             dbb_re, dbb_im, "ssm_prep_bwd")
    small_grads = {
        "g_mix": dg_mix, "b_f": db_f[:, :H], "lam_re": glr.reshape(1, G, P), "lam_im": gli.reshape(1, G, P),
        "log_dt": gldt.reshape(G, P).sum(axis=1)[None], "b_re": gbr.reshape(1, G, P, C), "b_im": gbi.reshape(1, G, P, C),
        "c_re": dc_re, "c_im": dc_im, "d_skip": dd_skip, "g_ffn": dg_ffn, "conv_b": dconv_b, "g_final": dg_final.reshape(D),
    }

    small = list(small_grads)
    rider_grads = [dconv_w, dmeta_full]
    small_shapes = [args[n].shape for n in small] + [g.shape for g in rider_grads]
    srows = sum(_rows_for(int(np.prod(sh)), LANES) for sh in small_shapes)
    srows = -(-srows // 256) * 256
    zeros_like_riders = [jnp.zeros(g.shape, F32) for g in rider_grads]
    pack = lambda arrs: _pack_small(arrs, srows)
    g_parts = _gather_all(pack([small_grads[n] for n in small] + rider_grads), "gather_small_grads")
    sm = _sum_adamw(g_parts, pack([args[n] for n in small] + zeros_like_riders),
                    pack([args["m_" + n] for n in small] + zeros_like_riders),
                    pack([args["v_" + n] for n in small] + zeros_like_riders), "small_adamw")
    unpacked = [_unpack_small(p, small_shapes) for p in sm]
    sg, sd, smm, svv = (dict(zip(small, u[:len(small)])) for u in unpacked)
    dconv_w_sum, dmeta_sum = unpacked[0][len(small):]
    n_cw, n_me = conv_w.shape[2], meta.shape[1]
    rider = {"conv_w": lax.dynamic_slice_in_dim(dconv_w_sum, chip * n_cw, n_cw, axis=1)[None],
             "meta": lax.dynamic_slice_in_dim(dmeta_sum, chip * n_me, n_me, axis=1)}

    big = ["w_in"] + early
    halves = [_sum_half(g, t, l_, chip_c, "rs_sum_" + n) for n, g, t, l_ in
              zip(big, late_grads + early_grads, list(late_got) + list(early_got), list(late_land) + list(early_land))]
    shard_grads = dict(zip(big, _sibling_join(halves, "rs_join")))
    shard_grads.update({n: g.reshape(g.shape[-2:]) for n, g in rider.items()})
    bg, bd_, bm, bv = {}, {}, {}, {}
    for n, g in shard_grads.items():
        bd_[n], bm[n], bv[n] = _adamw(args[n], g, args["m_" + n], args["v_" + n], "adamw_" + n)
        bg[n] = g.reshape(args[n].shape)

    order = ["meta", "g_mix", "w_in", "b_f", "lam_re", "lam_im", "log_dt", "b_re", "b_im", "c_re", "c_im", "d_skip",
             "w_glu", "w_attn_o", "w_out", "g_ffn", "w_up", "conv_w", "conv_b", "w_down", "g_final"]
    pick = lambda bigd, smalld, n: bigd[n] if n in bigd else smalld[n]
    outs = [loss, grad_x]
    for bigd, smalld in ((bg, sg), (bd_, sd), (bm, smm), (bv, svv)):
        outs += [pick(bigd, smalld, n) for n in order]
    return tuple(outs)
```

```python
import functools
import math

import jax
import jax.numpy as jnp
import numpy as np
from jax import lax
from jax.experimental import pallas as pl
from jax.experimental.pallas import tpu as pltpu

F32 = jnp.float32
BF16 = jnp.bfloat16
MESH = pl.DeviceIdType.MESH

EPS = 1e-6
HEAD_DIM = 128
LANES = 128
NEG = -1e30
GELU_C = math.sqrt(2.0 / math.pi)
GELU_A = 0.044715
ADAM_LR, ADAM_B1, ADAM_B2, ADAM_EPS, ADAM_WD, ADAM_STEP = 0.001, 0.9, 0.999, 1e-08, 0.01, 10
V7X_VMEM_BYTES = 64 << 20
GROUPS_PER_BLOCK = 8


def _tile(n, pref, mult=LANES):
    if n <= pref:
        return n
    t = (pref // mult) * mult
    while t >= mult:
        if n % t == 0:
            return t
        t -= mult
    raise ValueError(f"no tile for {n} <= {pref} (multiple of {mult})")


def _ctile(pref, *vals):
    g = 0
    for v in vals:
        g = math.gcd(g, v)
    return _tile(g, pref)


def _cparams(sem, est_bytes):
    limit = int(min(max(est_bytes * 1.25 + (4 << 20), 16 << 20), V7X_VMEM_BYTES - (8 << 20)))
    return pltpu.CompilerParams(dimension_semantics=sem, vmem_limit_bytes=limit)


def _sds(shape, dtype):
    return jax.ShapeDtypeStruct(tuple(shape), dtype)


def _sig(x):
    return 0.5 * jnp.tanh(0.5 * x) + 0.5


def _sig_tail(x):
    return 1.0 / (1.0 + jnp.exp(-x))


def _gelu(x):
    t = jnp.tanh(GELU_C * (x + GELU_A * x * x * x))
    return 0.5 * x * (1.0 + t)


def _gelu_grad(x):
    t = jnp.tanh(GELU_C * (x + GELU_A * x * x * x))
    return 0.5 * (1.0 + t) + 0.5 * x * (1.0 - t * t) * GELU_C * (1.0 + 3.0 * GELU_A * x * x)


def _mm(a, b, mode, M, N, K, out_dtype, name, *, res=None, a_off=(0, 0), b_off=(0, 0),
        tm=640, tn=1024, tk=2048, b_chips=None, out_chips=None, out_into=None, ride=None):
    tm, tn, tk = _tile(M, tm, 8 if mode != "tn" else LANES), _tile(N, tn), _tile(K, tk, LANES if mode != "tn" else 8)
    if b_chips is not None and mode == "nt":
        tk = _ctile(tk, tk, b_chips)
    if b_chips is not None and mode != "nt":
        tn = _ctile(tn, tn, b_chips)
    if out_chips is not None:
        tn = _ctile(tn, tn, out_chips)
    nk = K // tk
    ar, ac = a_off
    br, bc = b_off
    if mode == "tn":
        assert ar % tk == 0 and ac % tm == 0
        a_spec = pl.BlockSpec((tk, tm), lambda i, j, k: (k + ar // tk, i + ac // tm))
        a_dims = 0
    else:
        assert ar % tm == 0 and ac % tk == 0
        a_spec = pl.BlockSpec((tm, tk), lambda i, j, k: (i + ar // tm, k + ac // tk))
        a_dims = 1
    if mode == "nt":
        assert br % tn == 0 and bc % tk == 0
        if b_chips is None:
            b_spec = pl.BlockSpec((tn, tk), lambda i, j, k: (j + br // tn, k + bc // tk))
        else:
            per = b_chips // tk
            b_spec = pl.BlockSpec((None, tn, tk), lambda i, j, k: ((k + bc // tk) // per, j + br // tn, (k + bc // tk) % per))
        b_dims = 1
    else:
        assert br % tk == 0 and bc % tn == 0
        if b_chips is None:
            b_spec = pl.BlockSpec((tk, tn), lambda i, j, k: (k + br // tk, j + bc // tn))
        else:
            per = b_chips // tn
            b_spec = pl.BlockSpec((None, tk, tn), lambda i, j, k: ((j + bc // tn) // per, k + br // tk, (j + bc // tn) % per))
        b_dims = 0
    dims = (((a_dims,), (b_dims,)), ((), ()))
    if out_chips is None:
        o_spec = pl.BlockSpec((tm, tn), lambda i, j, k: (i, j))
        o_shape = _sds((M, N), out_dtype)
    else:
        per_o = out_chips // tn
        chip0 = 0 if out_into is None else out_into[1]
        o_spec = pl.BlockSpec((None, tm, tn), lambda i, j, k: (chip0 + j // per_o, i, j % per_o))
        o_shape = _sds((N // out_chips if out_into is None else 4, M, out_chips), out_dtype)
    has_res = res is not None
    has_into = out_into is not None

    r_ins, r_outs, r_sems = _ride_parts(ride)
    n_in = 2 + has_res + has_into
    steps = (M // tm, N // tn, nk)

    def body(*refs):
        a_ref, b_ref = refs[:2]
        r_ref = refs[2] if has_res else None
        o_ref = refs[n_in + len(r_ins)]
        if ride is not None:
            start, finish = ride["fn"](refs[n_in:n_in + len(r_ins)],
                                       refs[n_in + len(r_ins) + 1:n_in + len(r_ins) + 1 + len(r_outs)], *refs[-2:])
            pid = [pl.program_id(d) for d in range(3)]
            pl.when((pid[0] == 0) & (pid[1] == 0) & (pid[2] == 0))(start)
        part = lax.dot_general(a_ref[...].astype(BF16), b_ref[...].astype(BF16), dims, preferred_element_type=F32)

        def write_out(acc):
            if has_res:
                acc = r_ref[...] + acc
            o_ref[...] = acc.astype(o_ref.dtype)

        if nk == 1:
            write_out(part)
        else:
            acc_ref = refs[n_in + len(r_ins) + 1 + len(r_outs)]
            k = pl.program_id(2)

            @pl.when(k == 0)
            def _():
                acc_ref[...] = part

            @pl.when(k > 0)
            def _():
                acc_ref[...] += part

            @pl.when(k == nk - 1)
            def _():
                write_out(acc_ref[...])

        if ride is not None:
            pl.when((pid[0] == steps[0] - 1) & (pid[1] == steps[1] - 1) & (pid[2] == steps[2] - 1))(finish)

    in_specs = ([a_spec, b_spec] + ([o_spec] if has_res else []) + ([pl.BlockSpec(memory_space=pl.ANY)] if has_into else [])
                + [HBM_SPEC] * len(r_ins))
    args = (a, b) + ((res,) if has_res else ()) + ((out_into[0],) if has_into else ()) + tuple(r_ins)
    isz = lambda x: jnp.dtype(x.dtype).itemsize
    est = 2 * (tm * tk * isz(a) + tk * tn * isz(b) + tm * tn * jnp.dtype(out_dtype).itemsize) + tm * tn * 4 * (2 + 2 * has_res)
    sem = ("parallel", "parallel", "arbitrary") if ride is None else ("arbitrary",) * 3
    out = pl.pallas_call(
        body, grid=steps, in_specs=in_specs, out_specs=[o_spec] + [HBM_SPEC] * len(r_outs),
        out_shape=[o_shape] + r_outs,
        scratch_shapes=([pltpu.VMEM((tm, tn), F32)] if nk > 1 else []) + r_sems,
        input_output_aliases={2: 0} if has_into else {},
        compiler_params=_cparams(sem, est), name=name,
    )(*args)
    return out[0] if ride is None else out


def _rms_fwd(h, g, name):
    LP, D = h.shape
    tm = _tile(LP, 640, 8)

    def body(h_ref, g_ref, o_ref):
        x = h_ref[...]
        r = lax.rsqrt(jnp.mean(x * x, axis=-1, keepdims=True) + EPS)
        o_ref[...] = (x * r * g_ref[...]).astype(o_ref.dtype)

    row = pl.BlockSpec((tm, D), lambda i: (i, 0))
    return pl.pallas_call(
        body, grid=(LP // tm,), in_specs=[row, pl.BlockSpec((1, D), lambda i: (0, 0))], out_specs=row,
        out_shape=_sds((LP, D), BF16), compiler_params=_cparams(("parallel",), 2 * tm * D * 6), name=name,
    )(h, g)


def _rms_bwd(h, g, dn, dres, name):
    LP, D = h.shape
    tm = _tile(LP, 320, 8)
    nt = LP // tm

    def body(h_ref, g_ref, dn_ref, dres_ref, dh_ref, dg_ref):
        i = pl.program_id(0)
        x = h_ref[...]
        r = lax.rsqrt(jnp.mean(x * x, axis=-1, keepdims=True) + EPS)
        xh = x * r
        dn_v = dn_ref[...]
        dxh = dn_v * g_ref[...]
        dh_ref[...] = dres_ref[...] + r * (dxh - xh * jnp.mean(dxh * xh, axis=-1, keepdims=True))
        part = jnp.sum(dn_v * xh, axis=0, keepdims=True)

        @pl.when(i == 0)
        def _():
            dg_ref[...] = part

        @pl.when(i > 0)
        def _():
            dg_ref[...] += part

    row = pl.BlockSpec((tm, D), lambda i: (i, 0))
    vec = pl.BlockSpec((1, D), lambda i: (0, 0))
    return pl.pallas_call(
        body, grid=(nt,), in_specs=[row, vec, row, row], out_specs=[row, vec],
        out_shape=[_sds((LP, D), F32), _sds((1, D), F32)],
        compiler_params=_cparams(("arbitrary",), 2 * 4 * tm * D * 4), name=name,
    )(h, g, dn, dres)


def _final_loss(h, g, tgt, off, name):
    LP, D = h.shape
    tm = LANES
    assert off % tm == 0
    ob = off // tm
    nt = LP // tm

    def body(h_ref, g_ref, t_ref, dh_ref, dg_ref, loss_ref):
        i = pl.program_id(0)
        x = h_ref[...]
        r = lax.rsqrt(jnp.mean(x * x, axis=-1, keepdims=True) + EPS)
        xh = x * r
        gv = g_ref[...]
        e = xh * gv - t_ref[...]
        valid = i >= ob
        dy = jnp.where(valid, e * (1.0 / D), 0.0)
        lpart = jnp.where(valid, 0.5 * jnp.sum(jnp.mean(e * e, axis=-1, keepdims=True), axis=0, keepdims=True), 0.0)
        dxh = dy * gv
        dh_ref[...] = r * (dxh - xh * jnp.mean(dxh * xh, axis=-1, keepdims=True))
        gpart = jnp.sum(dy * xh, axis=0, keepdims=True)

        @pl.when(i == 0)
        def _():
            dg_ref[...] = gpart
            loss_ref[...] = jnp.broadcast_to(lpart, loss_ref.shape)

        @pl.when(i > 0)
        def _():
            dg_ref[...] += gpart
            loss_ref[...] += jnp.broadcast_to(lpart, loss_ref.shape)

    row = pl.BlockSpec((tm, D), lambda i: (i, 0))
    vec = pl.BlockSpec((1, D), lambda i: (0, 0))
    return pl.pallas_call(
        body, grid=(nt,),
        in_specs=[row, vec, pl.BlockSpec((tm, D), lambda i: (jnp.maximum(i - ob, 0), 0))],
        out_specs=[row, vec, pl.BlockSpec((1, LANES), lambda i: (0, 0))],
        out_shape=[_sds((LP, D), F32), _sds((1, D), F32), _sds((1, LANES), F32)],
        compiler_params=_cparams(("arbitrary",), 2 * 3 * tm * D * 4), name=name,
    )(h, g, tgt)


def _fgate_fwd(fpre, bias, pad, name):
    LP, W = fpre.shape

    def body(f_ref, b_ref, o_ref):
        row8 = lax.broadcasted_iota(jnp.int32, (8, W), 0)
        bv = b_ref[...]

        def step(g, carry):
            r0 = pl.multiple_of(g * 8, 8)
            x = f_ref[pl.ds(r0, 8), :] + bv
            lf = jnp.minimum(x, 0.0) - jnp.log(1.0 + jnp.exp(-jnp.abs(x)))
            lf = jnp.where(r0 + row8 >= pad, lf, 0.0)
            for k in (1, 2, 4):
                lf = lf + jnp.where(row8 >= k, pltpu.roll(lf, k, 0), 0.0)
            lf = lf + carry
            o_ref[pl.ds(r0, 8), :] = lf
            return jnp.broadcast_to(lf[7:8, :], (8, W))

        lax.fori_loop(0, LP // 8, step, jnp.zeros((8, W), F32))

    return pl.pallas_call(
        body, out_shape=_sds((LP, W), F32),
        compiler_params=_cparams(None, 3 * LP * W * 4), name=name,
    )(fpre, bias)


def _fgate_bwd(dF, fpre, bias, pad, name):
    LP, W = fpre.shape
    ng = LP // 8

    def body(d_ref, f_ref, b_ref, o_ref, db_ref):
        row8 = lax.broadcasted_iota(jnp.int32, (8, W), 0)
        bv = b_ref[...]

        def step(t, carry):
            run, acc = carry
            g = ng - 1 - t
            r0 = pl.multiple_of(g * 8, 8)
            x = d_ref[pl.ds(r0, 8), :]
            for k in (1, 2, 4):
                x = x + jnp.where(row8 < 8 - k, pltpu.roll(x, 8 - k, 0), 0.0)
            x = x + run
            df = x * _sig_tail(-(f_ref[pl.ds(r0, 8), :] + bv))
            df = jnp.where(r0 + row8 >= pad, df, 0.0)
            o_ref[pl.ds(r0, 8), :] = df
            return jnp.broadcast_to(x[0:1, :], (8, W)), acc + df

        _, acc = lax.fori_loop(0, ng, step, (jnp.zeros((8, W), F32), jnp.zeros((8, W), F32)))
        db_ref[...] = jnp.sum(acc, axis=0, keepdims=True)

    return pl.pallas_call(
        body, out_shape=[_sds((LP, W), F32), _sds((1, W), F32)],
        compiler_params=_cparams(None, 4 * LP * W * 4), name=name,
    )(dF, fpre, bias)


def _place():
    return lax.axis_index("x"), lax.axis_index("y"), lax.axis_index("c")


def _other_chips(x, y):
    return [(1 - x, y), (x, 1 - y), (1 - x, 1 - y)]


def _gather_phases(g_refs, send_sems, recv_sems):
    nw = len(g_refs)
    x, y, c = _place()
    chips = _other_chips(x, y)
    me = 2 * x + y

    def copy(i, k, chip, half, to):
        HR = g_refs[i].shape[1] // 2
        rows = g_refs[i].at[chip, pl.ds(pl.multiple_of(half * HR, 16), HR), :]
        return pltpu.make_async_remote_copy(
            src_ref=rows, dst_ref=rows, send_sem=send_sems.at[6 * i + k], recv_sem=recv_sems.at[6 * i + k],
            device_id=to, device_id_type=MESH)

    pairs = [(i, k, cx, cy) for i in range(nw) for k, (cx, cy) in enumerate(chips)]

    def start():
        for i, k, cx, cy in pairs:
            copy(i, k, me, c, (cx, cy, c)).start()

    def forward():
        for i, k, cx, cy in pairs:
            copy(i, k, 2 * cx + cy, c, (cx, cy, c)).wait_recv()
            copy(i, 3 + k, 2 * cx + cy, c, (x, y, 1 - c)).start()

    def finish():
        for i, k, cx, cy in pairs:
            copy(i, 3 + k, 2 * cx + cy, 1 - c, (x, y, 1 - c)).wait_recv()
        for i, k, cx, cy in pairs:
            copy(i, k, me, c, (cx, cy, c)).wait_send()
            copy(i, 3 + k, 2 * cx + cy, c, (x, y, 1 - c)).wait_send()

    return start, forward, finish


def _scatter_phases(s_refs, land_refs, send_sems, recv_sems):
    x, y, c = _place()
    chips = _other_chips(x, y)

    def copy(i, k, cx, cy):
        return pltpu.make_async_remote_copy(
            src_ref=s_refs[i].at[2 * cx + cy], dst_ref=land_refs[i].at[k],
            send_sem=send_sems.at[3 * i + k], recv_sem=recv_sems.at[3 * i + k],
            device_id=(cx, cy, c), device_id_type=MESH)

    pairs = [(i, k, cx, cy) for i in range(len(s_refs)) for k, (cx, cy) in enumerate(chips)]

    def start():
        for p in pairs:
            copy(*p).start()

    def finish():
        for p in pairs:
            copy(*p).wait_recv()
        for p in pairs:
            copy(*p).wait_send()

    return start, finish


def _sibling_phases(g_refs, land_refs, send_sems, recv_sems):
    x, y, c = _place()

    def copy(i):
        HR = land_refs[i].shape[1]
        q0 = pl.multiple_of((1 - c) * HR, 8)
        return pltpu.make_async_remote_copy(
            src_ref=g_refs[i].at[pl.ds(0, 4), pl.ds(q0, HR), :], dst_ref=land_refs[i],
            send_sem=send_sems.at[i], recv_sem=recv_sems.at[i], device_id=(x, y, 1 - c), device_id_type=MESH)

    def start():
        for i in range(len(g_refs)):
            copy(i).start()

    def finish():
        for i in range(len(g_refs)):
            copy(i).wait()

    return start, finish


def _sibling_ride(gs):
    return dict(fn=_sibling_phases, ins=list(gs), outs=[_sds((4, g.shape[1] // 2, g.shape[2]), g.dtype) for g in gs],
                sems=len(gs))


def _scatter_ride(ss):
    return dict(fn=_scatter_phases, ins=list(ss), outs=[_sds((3,) + p.shape[1:], p.dtype) for p in ss], sems=3 * len(ss))


def _ride_parts(ride):
    if ride is None:
        return [], [], []
    return ride["ins"], ride["outs"], [pltpu.SemaphoreType.DMA((ride["sems"],)), pltpu.SemaphoreType.DMA((ride["sems"],))]


HBM_SPEC = pl.BlockSpec(memory_space=pltpu.HBM)


def _col_to_row(col):
    n = col.shape[0]
    return jnp.transpose(jnp.broadcast_to(col, (n, LANES)))[0:1, :]


def _row_to_col(row):
    n = row.shape[1]
    return jnp.transpose(jnp.broadcast_to(row, (LANES, n)))[:, 0:1]


def _attn_fwd(z, bias_t, H, pad, name, gather=()):
    LP = z.shape[0]
    BQ = BK = _tile(LP, 640)
    scale = HEAD_DIM ** -0.5
    NT = (((1,), (1,)), ((), ()))

    nw = len(gather)
    nq = LP // BQ

    def body(*refs):
        q_ref, k_ref, v_ref, b_ref = refs[:4]
        o_ref, of_ref, lse_ref = refs[4 + nw:7 + nw]
        hd = pl.program_id(0)
        qi = pl.program_id(1)
        if nw:
            start, forward, finish = _gather_phases(refs[7 + nw:7 + 2 * nw], *refs[7 + 2 * nw:])
            pl.when((hd == 0) & (qi == 0))(start)
            pl.when((hd == H // 2) & (qi == 0))(forward)
        q = q_ref[...]

        def tile(kb, carry, masked):
            m, l, acc = carry
            k0 = pl.multiple_of(kb * BK, BK)
            s = lax.dot_general(q, k_ref[pl.ds(k0, BK), :], NT, preferred_element_type=F32) * scale
            s = s + b_ref[:, pl.ds(k0, BK)]
            if masked:
                ri = lax.broadcasted_iota(jnp.int32, (BQ, BK), 0)
                ci = lax.broadcasted_iota(jnp.int32, (BQ, BK), 1)
                s = jnp.where(ri >= ci, s, NEG)
            mn = jnp.maximum(m, jnp.max(s, axis=-1, keepdims=True))
            p = jnp.exp(s - mn)
            alpha = jnp.exp(m - mn)
            l = alpha * l + jnp.sum(p, axis=-1, keepdims=True)
            vk = v_ref[pl.ds(k0, BK), :]
            p_hi = p.astype(BF16)
            p_lo = (p - p_hi.astype(F32)).astype(BF16)
            pv = jnp.dot(p_hi, vk, preferred_element_type=F32) + jnp.dot(p_lo, vk, preferred_element_type=F32)
            return mn, l, alpha * acc + pv

        carry = (jnp.full((BQ, 1), NEG, F32), jnp.zeros((BQ, 1), F32), jnp.zeros((BQ, HEAD_DIM), F32))
        carry = lax.fori_loop(0, qi, lambda kb, c: tile(kb, c, False), carry)
        m, l, acc = tile(qi, carry, True)
        rows = qi * BQ + lax.broadcasted_iota(jnp.int32, (BQ, 1), 0)
        o = jnp.where(rows >= pad, acc / l, 0.0)
        o_ref[...] = o.astype(o_ref.dtype)
        of_ref[...] = o
        lse_ref[...] = _col_to_row(m + jnp.log(l))
        if nw:
            pl.when((hd == H - 1) & (qi == nq - 1))(finish)

    in_specs = [
        pl.BlockSpec((BQ, HEAD_DIM), lambda h, i: (i, h)),
        pl.BlockSpec((LP, HEAD_DIM), lambda h, i: (0, H + h)),
        pl.BlockSpec((LP, HEAD_DIM), lambda h, i: (0, 2 * H + h)),
        pl.BlockSpec((None, 1, LP), lambda h, i: (h, 0, 0)),
    ]
    out_specs = [
        pl.BlockSpec((BQ, HEAD_DIM), lambda h, i: (i, h)),
        pl.BlockSpec((BQ, HEAD_DIM), lambda h, i: (i, h)),
        pl.BlockSpec((None, 1, BQ), lambda h, i: (h, 0, i)),
    ]
    est = 2 * (2 * LP * HEAD_DIM * 2 + 8 * LP * 4) + 20 * BQ * LANES * 4 + 8 * BQ * BK * 4
    sems = [pltpu.SemaphoreType.DMA((6 * nw,)), pltpu.SemaphoreType.DMA((6 * nw,))] if nw else []
    return pl.pallas_call(
        body, grid=(H, nq), in_specs=in_specs + [HBM_SPEC] * nw, out_specs=out_specs + [HBM_SPEC] * nw,
        out_shape=[_sds((LP, H * HEAD_DIM), BF16), _sds((LP, H * HEAD_DIM), F32), _sds((H, 1, LP), F32)]
        + [_sds(g.shape, g.dtype) for g in gather],
        scratch_shapes=sems, input_output_aliases={4 + i: 3 + i for i in range(nw)},
        compiler_params=_cparams(("arbitrary", "arbitrary"), est), name=name,
    )(z, z, z, bias_t, *gather)


def _attn_delta(do, o, H, name):
    LP = do.shape[0]
    tm = _tile(LP, 640)

    def body(do_ref, o_ref, d_ref):
        d_ref[...] = _col_to_row(jnp.sum(do_ref[...].astype(F32) * o_ref[...].astype(F32), axis=-1, keepdims=True))

    blk = pl.BlockSpec((tm, HEAD_DIM), lambda h, i: (i, h))
    return pl.pallas_call(
        body, grid=(H, LP // tm), in_specs=[blk, blk],
        out_specs=pl.BlockSpec((None, 1, tm), lambda h, i: (h, 0, i)),
        out_shape=_sds((H, 1, LP), F32),
        compiler_params=_cparams(("parallel", "parallel"), 8 * tm * LANES * 4), name=name,
    )(do, o)


def _attn_bwd(z, do, lse_t, delta_t, bias_t, H, name, scatter=()):
    LP = z.shape[0]
    BK = BQ = _tile(LP, 640)
    nk = nq = LP // BK
    scale = HEAD_DIM ** -0.5
    NT = (((1,), (1,)), ((), ()))
    TN = (((0,), (0,)), ((), ()))

    nw = len(scatter)

    def body(*refs):
        q_ref, k_ref, v_ref, do_ref, lse_ref, dl_ref, b_ref = refs[:7]
        dq_ref, dk_ref, dv_ref, db_ref = refs[7 + nw:11 + nw]
        dq_acc = refs[11 + 2 * nw]
        hd = pl.program_id(0)
        kj = pl.program_id(1)
        if nw:
            start, finish = _scatter_phases(refs[7:7 + nw], refs[11 + nw:11 + 2 * nw], *refs[12 + 2 * nw:])
            pl.when((hd == 0) & (kj == 0))(start)

        @pl.when(kj == 0)
        def _():
            dq_acc[...] = jnp.zeros_like(dq_acc)

        k = k_ref[...]
        v = v_ref[...]
        bcol = _row_to_col(b_ref[:, pl.ds(pl.multiple_of(kj * BK, BK), BK)])

        def tile(qc, carry, masked):
            dk, dv, db = carry
            q0 = pl.multiple_of(qc * BQ, BQ)
            q = q_ref[pl.ds(q0, BQ), :]
            dout = do_ref[pl.ds(q0, BQ), :]
            st = lax.dot_general(k, q, NT, preferred_element_type=F32) * scale + bcol
            if masked:
                ri = lax.broadcasted_iota(jnp.int32, (BK, BQ), 0)
                ci = lax.broadcasted_iota(jnp.int32, (BK, BQ), 1)
                st = jnp.where(ci >= ri, st, NEG)
            pt = jnp.exp(st - lse_ref[:, pl.ds(q0, BQ)])
            dv = dv + jnp.dot(pt.astype(BF16), dout, preferred_element_type=F32)
            dpt = lax.dot_general(v, dout, NT, preferred_element_type=F32)
            dst = pt * (dpt - dl_ref[:, pl.ds(q0, BQ)])
            db = db + jnp.sum(dst, axis=-1, keepdims=True)
            dsb = (dst * scale).astype(BF16)
            dk = dk + jnp.dot(dsb, q, preferred_element_type=F32)
            dq_acc[pl.ds(q0, BQ), :] += lax.dot_general(dsb, k, TN, preferred_element_type=F32)
            return dk, dv, db

        carry = (jnp.zeros((BK, HEAD_DIM), F32), jnp.zeros((BK, HEAD_DIM), F32), jnp.zeros((BK, 1), F32))
        carry = tile(kj, carry, True)
        dk, dv, db = lax.fori_loop(kj + 1, nq, lambda qc, c: tile(qc, c, False), carry)
        dk_ref[...] = dk.astype(dk_ref.dtype)
        dv_ref[...] = dv.astype(dv_ref.dtype)
        db_ref[...] = _col_to_row(db)

        @pl.when(kj == nk - 1)
        def _():
            dq_ref[...] = dq_acc[...].astype(dq_ref.dtype)

        if nw:
            pl.when((hd == H - 1) & (kj == nk - 1))(finish)

    full = lambda c0: pl.BlockSpec((LP, HEAD_DIM), lambda h, j: (0, c0 + h))
    blk = lambda c0: pl.BlockSpec((BK, HEAD_DIM), lambda h, j: (j, c0 + h))
    vec = pl.BlockSpec((None, 1, LP), lambda h, j: (h, 0, 0))
    in_specs = [full(0), blk(H), blk(2 * H), full(0), vec, vec, vec]
    out_specs = [full(0), blk(0), blk(0), pl.BlockSpec((None, 1, BK), lambda h, j: (h, 0, j))]
    est = 2 * (3 * LP * HEAD_DIM * 2 + 16 * LP * 4) + LP * HEAD_DIM * 4 + 24 * BK * LANES * 4 + 10 * BK * BQ * 4
    sems = [pltpu.SemaphoreType.DMA((3 * nw,)), pltpu.SemaphoreType.DMA((3 * nw,))] if nw else []
    return pl.pallas_call(
        body, grid=(H, nk), in_specs=in_specs + [HBM_SPEC] * nw, out_specs=out_specs + [HBM_SPEC] * nw,
        out_shape=[_sds((LP, H * HEAD_DIM), BF16)] * 3 + [_sds((H, 1, LP), F32)]
        + [_sds((3,) + p.shape[1:], p.dtype) for p in scatter],
        scratch_shapes=[pltpu.VMEM((LP, HEAD_DIM), F32)] + sems,
        compiler_params=_cparams(("arbitrary", "arbitrary"), est), name=name,
    )(z, z, z, do, lse_t, delta_t, bias_t, *scatter)


def _ssm_disc(lr, li, ldt, br, bi):
    dt = jnp.exp(ldt)
    mag = jnp.exp(lr * dt)
    a_re = mag * jnp.cos(li * dt)
    a_im = mag * jnp.sin(li * dt)
    den = lr * lr + li * li
    nr = a_re - 1.0
    z_re = (nr * lr + a_im * li) / den
    z_im = (a_im * lr - nr * li) / den
    return a_re, a_im, z_re * br - z_im * bi, z_re * bi + z_im * br


def _ssm_prep(lr, li, ldt, br, bi, seg, name):
    GP, C = br.shape
    assert seg <= LANES

    def body(lr_ref, li_ref, ldt_ref, br_ref, bi_ref, ar_ref, ai_ref, bbr_ref, bbi_ref, pr_ref, pi_ref,
             pw_r_ref, pw_i_ref, pb_r_ref, pb_i_ref):
        a_re, a_im, bb_re, bb_im = _ssm_disc(lr_ref[...], li_ref[...], ldt_ref[...], br_ref[...], bi_ref[...])
        ar_ref[...] = a_re
        ai_ref[...] = a_im
        bbr_ref[...] = bb_re
        bbi_ref[...] = bb_im

        def powers(base_r, base_i, n, width):
            lane = lax.broadcasted_iota(jnp.int32, (tg, width), 1)
            pr, pi_ = base_r, base_i
            accr = jnp.zeros((tg, width), F32)
            acci = jnp.zeros((tg, width), F32)
            for k in range(n):
                accr = jnp.where(lane == k, pr, accr)
                acci = jnp.where(lane == k, pi_, acci)
                last = (pr, pi_)
                pr, pi_ = pr * base_r - pi_ * base_i, pr * base_i + pi_ * base_r
            return accr, acci, last

        pr_ref[...], pi_ref[...], _ = powers(a_re, a_im, 8, 8)
        pwr, pwi, (b_re, b_im) = powers(a_re, a_im, seg, LANES)
        pw_r_ref[...] = pwr
        pw_i_ref[...] = pwi
        pb_r_ref[...], pb_i_ref[...], _ = powers(b_re, b_im, 8, 8)

    tg = _tile(GP, 128, 8)
    blk = lambda w: pl.BlockSpec((tg, w), lambda i: (i, 0))
    col = _sds((GP, 1), F32)
    return pl.pallas_call(
        body, grid=(GP // tg,), in_specs=[blk(1), blk(1), blk(1), blk(C), blk(C)],
        out_specs=[blk(1), blk(1), blk(C), blk(C), blk(8), blk(8), blk(LANES), blk(LANES), blk(8), blk(8)],
        out_shape=[col, col, _sds((GP, C), F32), _sds((GP, C), F32), _sds((GP, 8), F32), _sds((GP, 8), F32),
                   _sds((GP, LANES), F32), _sds((GP, LANES), F32), _sds((GP, 8), F32), _sds((GP, 8), F32)],
        compiler_params=_cparams(("parallel",), 64 * tg * LANES * 4), name=name,
    )(lr, li, ldt, br, bi)


def _ssm_prep_bwd(lr, li, ldt, br, bi, dar, dai, dbbr, dbbi, name):
    GP, C = br.shape

    def body(lr_ref, li_ref, ldt_ref, br_ref, bi_ref, dar_ref, dai_ref, dbbr_ref, dbbi_ref,
             glr_ref, gli_ref, gldt_ref, gbr_ref, gbi_ref):
        _, vjp = jax.vjp(_ssm_disc, lr_ref[...], li_ref[...], ldt_ref[...], br_ref[...], bi_ref[...])
        glr, gli, gldt, gbr, gbi = vjp((dar_ref[...], dai_ref[...], dbbr_ref[...], dbbi_ref[...]))
        glr_ref[...] = glr
        gli_ref[...] = gli
        gldt_ref[...] = gldt
        gbr_ref[...] = gbr
        gbi_ref[...] = gbi

    tg = _tile(GP, 512, 8)
    blk = lambda w: pl.BlockSpec((tg, w), lambda i: (i, 0))
    col = _sds((GP, 1), F32)
    return pl.pallas_call(
        body, grid=(GP // tg,), in_specs=[blk(1), blk(1), blk(1), blk(C), blk(C), blk(1), blk(1), blk(C), blk(C)],
        out_specs=[blk(1), blk(1), blk(1), blk(C), blk(C)],
        out_shape=[col, col, col, _sds((GP, C), F32), _sds((GP, C), F32)],
        compiler_params=_cparams(("parallel",), 96 * tg * LANES * 4), name=name,
    )(lr, li, ldt, br, bi, dar, dai, dbbr, dbbi)


def _cmul_add(xr, xi, mr, mi, sr, si):
    return xr + mr * sr - mi * si, xi + mr * si + mi * sr


def _seg_len(LP):
    return _tile(LP, 640, 64) // 8


def _seg_carries(er, ei, coef_ref, lanes, cin_r, cin_i):
    row8 = lax.broadcasted_iota(jnp.int32, (8, LANES), 0)
    xr, xi = er, ei
    for n, k in enumerate((1, 2, 4)):
        xr, xi = _cmul_add(xr, xi, coef_ref[2 * n, :, lanes], coef_ref[2 * n + 1, :, lanes],
                           pltpu.roll(xr, k, 0), pltpu.roll(xi, k, 0))
    xr, xi = _cmul_add(xr, xi, coef_ref[6, :, lanes], coef_ref[7, :, lanes], cin_r, cin_i)
    from_r = jnp.where(row8 == 0, cin_r, pltpu.roll(xr, 1, 0))
    from_i = jnp.where(row8 == 0, cin_i, pltpu.roll(xi, 1, 0))
    bc = lambda x: jnp.broadcast_to(x[7:8, :], (8, LANES))
    return from_r, from_i, bc(xr), bc(xi)


def _ssm_fwd(z, u_col0, coef, pw, bbr, bbi, ccr, cci, dskip, name):
    LP = z.shape[0]
    NB, CB, S = bbr.shape
    TS = _tile(LP, 640, 64)
    SL = TS // 8
    NL = S // LANES
    nt = LP // TS

    def body(u_ref, coef_ref, pw_ref, bbr_ref, bbi_ref, ccr_ref, cci_ref, ds_ref, y_ref, yg_ref, hr_ref, hi_ref, bur, bui, carry):
        i = pl.program_id(1)

        @pl.when(i == 0)
        def _():
            carry[...] = jnp.zeros_like(carry)

        u = u_ref[...]
        bu_r = jnp.dot(u, bbr_ref[...], preferred_element_type=F32)
        bu_i = jnp.dot(u, bbi_ref[...], preferred_element_type=F32)
        rows = lambda j: pl.ds(j, 8, stride=SL)
        lanes = [pl.ds(lb * LANES, LANES) for lb in range(NL)]
        for lb in range(NL):
            bur[lb] = bu_r[:, lb * LANES:(lb + 1) * LANES]
            bui[lb] = bu_i[:, lb * LANES:(lb + 1) * LANES]
        a_r = [pw_ref[0, 0:8, lanes[lb]] for lb in range(NL)]
        a_i = [pw_ref[1, 0:8, lanes[lb]] for lb in range(NL)]

        def local(j, c):
            out = []
            for lb in range(NL):
                lr, li = _cmul_add(bur.at[lb][rows(j), :], bui.at[lb][rows(j), :], a_r[lb], a_i[lb], c[2 * lb], c[2 * lb + 1])
                bur.at[lb][rows(j), :] = lr
                bui.at[lb][rows(j), :] = li
                out += [lr, li]
            return tuple(out)

        ends = lax.fori_loop(0, SL, local, (jnp.zeros((8, LANES), F32),) * (2 * NL))
        starts = []
        for lb in range(NL):
            fr, fi, cr, ci = _seg_carries(ends[2 * lb], ends[2 * lb + 1], coef_ref, lanes[lb],
                                          carry[0, :, lanes[lb]], carry[1, :, lanes[lb]])
            carry[0, :, lanes[lb]] = cr
            carry[1, :, lanes[lb]] = ci
            starts += [fr, fi]

        def fix(j, c):
            j8 = pl.ds(pl.multiple_of(j * 8, 8), 8)
            for lb in range(NL):
                xr, xi = _cmul_add(bur.at[lb][rows(j), :], bui.at[lb][rows(j), :], pw_ref[0, j8, lanes[lb]], pw_ref[1, j8, lanes[lb]],
                                   starts[2 * lb], starts[2 * lb + 1])
                bur.at[lb][rows(j), :] = xr
                bui.at[lb][rows(j), :] = xi
            return c

        lax.fori_loop(0, SL, fix, 0)
        for lb in range(NL):
            hr_ref[:, lb * LANES:(lb + 1) * LANES] = bur[lb]
            hi_ref[:, lb * LANES:(lb + 1) * LANES] = bui[lb]
        y = (jnp.dot(hr_ref[...].astype(BF16), ccr_ref[...], preferred_element_type=F32)
             - jnp.dot(hi_ref[...].astype(BF16), cci_ref[...], preferred_element_type=F32)
             + ds_ref[...] * u.astype(F32))
        y_ref[...] = y
        yg_ref[...] = _gelu(y).astype(yg_ref.dtype)

    ucb = u_col0 // CB
    in_specs = [
        pl.BlockSpec((TS, CB), lambda j, i: (i, ucb + j)),
        pl.BlockSpec((None, 8, 8, S), lambda j, i: (j, 0, 0, 0)),
        pl.BlockSpec((None, 2, 8 * SL, S), lambda j, i: (j, 0, 0, 0)),
        pl.BlockSpec((None, CB, S), lambda j, i: (j, 0, 0)),
        pl.BlockSpec((None, CB, S), lambda j, i: (j, 0, 0)),
        pl.BlockSpec((None, S, CB), lambda j, i: (j, 0, 0)),
        pl.BlockSpec((None, S, CB), lambda j, i: (j, 0, 0)),
        pl.BlockSpec((1, CB), lambda j, i: (0, j)),
    ]
    yb = pl.BlockSpec((TS, CB), lambda j, i: (i, j))
    hb = pl.BlockSpec((TS, S), lambda j, i: (i, j))
    est = 2 * (2 * TS * S * 4 + 3 * TS * CB * 4 + (8 * 8 + 16 * SL) * S * 4 + 4 * CB * S * 2) + 5 * TS * S * 4
    return pl.pallas_call(
        body, grid=(NB, nt), in_specs=in_specs, out_specs=[yb, yb, hb, hb],
        out_shape=[_sds((LP, NB * CB), F32), _sds((LP, NB * CB), BF16), _sds((LP, NB * S), F32), _sds((LP, NB * S), F32)],
        scratch_shapes=[pltpu.VMEM((NL, TS, LANES), F32), pltpu.VMEM((NL, TS, LANES), F32), pltpu.VMEM((2, 8, S), F32)],
        compiler_params=_cparams(("parallel", "arbitrary"), est), name=name,
    )(z, coef, pw, bbr, bbi, ccr, cci, dskip)


def _ssm_bwd(z, u_col0, dyg, y, hr, hi, coef_rev, bbr_t, bbi_t, ccr_t, cci_t, dskip, name, ride=None):
    LP = z.shape[0]
    NB, S, CB = bbr_t.shape
    TS = _tile(LP, 640, 8)
    nt = LP // TS
    ng = TS // 8
    TN = (((0,), (0,)), ((), ()))

    r_ins, r_outs, r_sems = _ride_parts(ride)

    def body(*refs):
        n_in, n_out = 13 + len(r_ins), 8 + len(r_outs)
        if ride is not None:
            start, finish = ride["fn"](refs[13:n_in], refs[n_in + 8:n_in + n_out], *refs[-2:])
            pl.when((pl.program_id(0) == 0) & (pl.program_id(1) == 0))(start)
        step(*refs[:13], *refs[n_in:n_in + 8], *refs[n_in + n_out:n_in + n_out + 9])
        if ride is not None:
            pl.when((pl.program_id(0) == NB - 1) & (pl.program_id(1) == nt - 1))(finish)

    def step(u_ref, dyg_ref, y_ref, hr_ref, hi_ref, tr_ref, ti_ref, coef_ref, bbr_ref, bbi_ref, ccr_ref, cci_ref, ds_ref,
             du_ref, dbbr_ref, dbbi_ref, dccr_ref, dcci_ref, dar_ref, dai_ref, dd_ref,
             gr, gi, carry, acc_bbr, acc_bbi, acc_ccr, acc_cci, acc_a, acc_d):
        i = pl.program_id(1)

        @pl.when(i == 0)
        def _():
            for ref in (carry, acc_bbr, acc_bbi, acc_ccr, acc_cci, acc_a, acc_d):
                ref[...] = jnp.zeros_like(ref)

        u = u_ref[...]
        dy = dyg_ref[...] * _gelu_grad(y_ref[...])
        dyb = dy.astype(BF16)
        gr[...] = jnp.dot(dyb, ccr_ref[...], preferred_element_type=F32)
        gi[...] = -jnp.dot(dyb, cci_ref[...], preferred_element_type=F32)
        row8 = lax.broadcasted_iota(jnp.int32, (8, S), 0)
        first_chunk = i == nt - 1
        tail_r = jnp.where(first_chunk, 0.0, tr_ref[...])
        tail_i = jnp.where(first_chunk, 0.0, ti_ref[...])

        def scan(t, c):
            cr, ci, sar, sai = c
            g = ng - 1 - t
            r0 = pl.multiple_of(g * 8, 8)
            xr = gr[pl.ds(r0, 8), :]
            xi = gi[pl.ds(r0, 8), :]
            for n, k in enumerate((1, 2, 4)):
                xr, xi = _cmul_add(xr, xi, coef_ref[2 * n], coef_ref[2 * n + 1], pltpu.roll(xr, 8 - k, 0), pltpu.roll(xi, 8 - k, 0))
            xr, xi = _cmul_add(xr, xi, coef_ref[6], coef_ref[7], cr, ci)
            gr[pl.ds(r0, 8), :] = xr
            gi[pl.ds(r0, 8), :] = xi
            p0 = pl.multiple_of(jnp.maximum(g - 1, 0) * 8, 8)
            pr = jnp.where(g > 0, hr_ref[pl.ds(p0, 8), :], tail_r)
            pi_ = jnp.where(g > 0, hi_ref[pl.ds(p0, 8), :], tail_i)
            hpr = pltpu.roll(jnp.where(row8 == 7, pr, hr_ref[pl.ds(r0, 8), :]), 1, 0)
            hpi = pltpu.roll(jnp.where(row8 == 7, pi_, hi_ref[pl.ds(r0, 8), :]), 1, 0)
            sar = sar + xr * hpr + xi * hpi
            sai = sai + xi * hpr - xr * hpi
            return jnp.broadcast_to(xr[0:1, :], (8, S)), jnp.broadcast_to(xi[0:1, :], (8, S)), sar, sai

        zero = jnp.zeros((8, S), F32)
        cr, ci, sar, sai = lax.fori_loop(0, ng, scan, (carry[0], carry[1], zero, zero))
        carry[0] = cr
        carry[1] = ci
        acc_a[0] += sar
        acc_a[1] += sai
        grb = gr[...].astype(BF16)
        gib = gi[...].astype(BF16)
        du = (jnp.dot(grb, bbr_ref[...], preferred_element_type=F32) + jnp.dot(gib, bbi_ref[...], preferred_element_type=F32)
              + ds_ref[...] * dy)
        du_ref[...] = du.astype(du_ref.dtype)
        acc_bbr[...] += lax.dot_general(u, grb, TN, preferred_element_type=F32)
        acc_bbi[...] += lax.dot_general(u, gib, TN, preferred_element_type=F32)
        acc_ccr[...] += lax.dot_general(hr_ref[...].astype(BF16), dyb, TN, preferred_element_type=F32)
        acc_cci[...] -= lax.dot_general(hi_ref[...].astype(BF16), dyb, TN, preferred_element_type=F32)
        acc_d[...] += jnp.sum(dy * u.astype(F32), axis=0, keepdims=True)

        @pl.when(i == nt - 1)
        def _():
            dbbr_ref[...] = acc_bbr[...]
            dbbi_ref[...] = acc_bbi[...]
            dccr_ref[...] = acc_ccr[...]
            dcci_ref[...] = acc_cci[...]
            dar_ref[...] = jnp.sum(acc_a[0], axis=0, keepdims=True)
            dai_ref[...] = jnp.sum(acc_a[1], axis=0, keepdims=True)
            dd_ref[...] = acc_d[...]

    ucb = u_col0 // CB
    rev = lambda i: nt - 1 - i
    tail = lambda j, i: (jnp.maximum(rev(i) * ng - 1, 0), j)
    yb = pl.BlockSpec((TS, CB), lambda j, i: (rev(i), j))
    hb = pl.BlockSpec((TS, S), lambda j, i: (rev(i), j))
    in_specs = [
        pl.BlockSpec((TS, CB), lambda j, i: (rev(i), ucb + j)), yb, yb, hb, hb,
        pl.BlockSpec((8, S), tail), pl.BlockSpec((8, S), tail),
        pl.BlockSpec((None, 8, 8, S), lambda j, i: (j, 0, 0, 0)),
        pl.BlockSpec((None, S, CB), lambda j, i: (j, 0, 0)),
        pl.BlockSpec((None, S, CB), lambda j, i: (j, 0, 0)),
        pl.BlockSpec((None, CB, S), lambda j, i: (j, 0, 0)),
        pl.BlockSpec((None, CB, S), lambda j, i: (j, 0, 0)),
        pl.BlockSpec((1, CB), lambda j, i: (0, j)),
    ]
    mat_cs = pl.BlockSpec((None, CB, S), lambda j, i: (j, 0, 0))
    mat_sc = pl.BlockSpec((None, S, CB), lambda j, i: (j, 0, 0))
    vec_s = pl.BlockSpec((None, 1, S), lambda j, i: (j, 0, 0))
    out_specs = [yb, mat_cs, mat_cs, mat_sc, mat_sc, vec_s, vec_s, pl.BlockSpec((1, CB), lambda j, i: (0, j))]
    out_shape = [_sds((LP, NB * CB), BF16), _sds((NB, CB, S), F32), _sds((NB, CB, S), F32), _sds((NB, S, CB), F32),
                 _sds((NB, S, CB), F32), _sds((NB, 1, S), F32), _sds((NB, 1, S), F32), _sds((1, NB * CB), F32)]
    scratch = [pltpu.VMEM((TS, S), F32), pltpu.VMEM((TS, S), F32), pltpu.VMEM((2, 8, S), F32),
               pltpu.VMEM((CB, S), F32), pltpu.VMEM((CB, S), F32), pltpu.VMEM((S, CB), F32), pltpu.VMEM((S, CB), F32),
               pltpu.VMEM((2, 8, S), F32), pltpu.VMEM((1, CB), F32)]
    est = 2 * (2 * TS * S * 4 + 4 * TS * CB * 4 + 8 * 8 * S * 4 + 12 * CB * S * 4) + 4 * TS * S * 4
    return pl.pallas_call(
        body, grid=(NB, nt), in_specs=in_specs + [HBM_SPEC] * len(r_ins), out_specs=out_specs + [HBM_SPEC] * len(r_outs),
        out_shape=out_shape + r_outs, scratch_shapes=scratch + r_sems,
        compiler_params=_cparams(("arbitrary", "arbitrary"), est), name=name,
    )(z, dyg, y, hr, hi, hr, hi, coef_rev, bbr_t, bbi_t, ccr_t, cci_t, dskip, *r_ins)


def _merge_fwd(yab, z, ao, D, ga0, gb0, name):
    LP = z.shape[0]
    tm = _tile(LP, 640, 8)
    tn = _ctile(512, D, ga0, gb0)
    nj = D // tn

    def body(ya_ref, yb_ref, ga_ref, gb_ref, ao_ref, o_ref):
        f = lambda r: r[...].astype(F32)
        ssm = f(ya_ref) * _sig(f(yb_ref))
        o_ref[...] = (_sig(f(ga_ref)) * ssm + _sig(f(gb_ref)) * f(ao_ref)).astype(o_ref.dtype)

    blk = lambda c0: pl.BlockSpec((tm, tn), lambda i, j: (i, c0 // tn + j))
    return pl.pallas_call(
        body, grid=(LP // tm, nj), in_specs=[blk(0), blk(D), blk(ga0), blk(gb0), blk(0)], out_specs=blk(0),
        out_shape=_sds((LP, D), BF16), compiler_params=_cparams(("parallel", "parallel"), 2 * 6 * tm * tn * 4), name=name,
    )(yab, yab, z, z, ao)


def _merge_bwd(dm, yab, z, ao, D, ga0, gb0, name):
    LP = z.shape[0]
    tm = _tile(LP, 640, 8)
    tn = _ctile(512, D, ga0, gb0)
    nj = D // tn

    def body(dm_ref, ya_ref, yb_ref, ga_ref, gb_ref, ao_ref, dya_ref, dyb_ref, dga_ref, dgb_ref, dao_ref):
        f = lambda r: r[...].astype(F32)
        dmv, ya, ao_v = f(dm_ref), f(ya_ref), f(ao_ref)
        sa, sb, sy = _sig(f(ga_ref)), _sig(f(gb_ref)), _sig(f(yb_ref))
        t = dmv * sa
        dya_ref[...] = (t * sy).astype(BF16)
        dyb_ref[...] = (t * ya * sy * (1.0 - sy)).astype(BF16)
        dga_ref[...] = (dmv * (ya * sy) * sa * (1.0 - sa)).astype(BF16)
        dgb_ref[...] = (dmv * ao_v * sb * (1.0 - sb)).astype(BF16)
        dao_ref[...] = (dmv * sb).astype(BF16)

    blk = lambda c0: pl.BlockSpec((tm, tn), lambda i, j: (i, c0 // tn + j))
    return pl.pallas_call(
        body, grid=(LP // tm, nj), in_specs=[blk(0), blk(0), blk(D), blk(ga0), blk(gb0), blk(0)], out_specs=[blk(0)] * 5,
        out_shape=[_sds((LP, D), BF16)] * 5, compiler_params=_cparams(("parallel", "parallel"), 2 * 11 * tm * tn * 4), name=name,
    )(dm, yab, yab, z, z, ao)


def _shift_down(x, halo, k, row8):
    s = pltpu.roll(x, k, 0)
    top = jnp.where(row8 < k, pltpu.roll(halo, k, 0), s[0:8])
    return jnp.concatenate([top, s[8:]], axis=0) if x.shape[0] > 8 else top


def _shift_up(x, halo, k, row8):
    tm = x.shape[0]
    s = pltpu.roll(x, tm - k, 0)
    bot = jnp.where(row8 >= 8 - k, pltpu.roll(halo, 8 - k, 0), s[tm - 8:])
    return jnp.concatenate([s[:tm - 8], bot], axis=0) if tm > 8 else bot


def _conv_gate(g, halo, w_ref, cb, row8):
    return cb + w_ref[0:1, :] * _shift_down(g, halo, 2, row8) + w_ref[1:2, :] * _shift_down(g, halo, 1, row8) + w_ref[2:3, :] * g


def _convact_fwd(gu, cw, cb, DFF, name):
    LP = gu.shape[0]
    tm = _tile(LP, 640, 8)
    tn = _tile(DFF, 512)
    nj = DFF // tn
    t8 = tm // 8

    def body(g_ref, h_ref, u_ref, w_ref, b_ref, o_ref):
        i = pl.program_id(0)
        row8 = lax.broadcasted_iota(jnp.int32, (8, tn), 0)
        g = g_ref[...].astype(F32)
        halo = jnp.where(i > 0, h_ref[...].astype(F32), 0.0)
        gc = _conv_gate(g, halo, w_ref, b_ref[...], row8)
        o_ref[...] = (gc * _sig(gc) * u_ref[...].astype(F32)).astype(o_ref.dtype)

    in_specs = [
        pl.BlockSpec((tm, tn), lambda i, j: (i, j)),
        pl.BlockSpec((8, tn), lambda i, j: (jnp.maximum(i * t8 - 1, 0), j)),
        pl.BlockSpec((tm, tn), lambda i, j: (i, nj + j)),
        pl.BlockSpec((3, tn), lambda i, j: (0, j)),
        pl.BlockSpec((1, tn), lambda i, j: (0, j)),
    ]
    return pl.pallas_call(
        body, grid=(LP // tm, nj), in_specs=in_specs, out_specs=pl.BlockSpec((tm, tn), lambda i, j: (i, j)),
        out_shape=_sds((LP, DFF), BF16), compiler_params=_cparams(("parallel", "parallel"), 2 * 8 * tm * tn * 4), name=name,
    )(gu, gu, gu, cw, cb)


def _convact_bwd(dact, gu, cw, cb, DFF, name):
    LP = gu.shape[0]
    tm = _tile(LP, 640, 8)
    tn = _tile(DFF, 512)
    nj = DFF // tn
    t8 = tm // 8

    def body(da_ref, g_ref, h_ref, u_ref, w_ref, b_ref, dgc_ref, du_ref):
        i = pl.program_id(0)
        row8 = lax.broadcasted_iota(jnp.int32, (8, tn), 0)
        g = g_ref[...].astype(F32)
        halo = jnp.where(i > 0, h_ref[...].astype(F32), 0.0)
        gc = _conv_gate(g, halo, w_ref, b_ref[...], row8)
        sg = _sig(gc)
        da = da_ref[...].astype(F32)
        du_ref[...] = (da * gc * sg).astype(du_ref.dtype)
        dgc_ref[...] = (da * u_ref[...].astype(F32) * sg * (1.0 + gc * (1.0 - sg))).astype(dgc_ref.dtype)

    blk = pl.BlockSpec((tm, tn), lambda i, j: (i, j))
    in_specs = [
        blk, blk,
        pl.BlockSpec((8, tn), lambda i, j: (jnp.maximum(i * t8 - 1, 0), j)),
        pl.BlockSpec((tm, tn), lambda i, j: (i, nj + j)),
        pl.BlockSpec((3, tn), lambda i, j: (0, j)),
        pl.BlockSpec((1, tn), lambda i, j: (0, j)),
    ]
    return pl.pallas_call(
        body, grid=(LP // tm, nj), in_specs=in_specs, out_specs=[blk, blk],
        out_shape=[_sds((LP, DFF), BF16), _sds((LP, DFF), BF16)],
        compiler_params=_cparams(("parallel", "parallel"), 2 * 10 * tm * tn * 4), name=name,
    )(dact, gu, gu, gu, cw, cb)


def _conv_bwd(dgc, gu, cw, DFF, pad, name):
    LP = gu.shape[0]
    tm = _tile(LP, 640, 8)
    tn = _tile(DFF, 512)
    nj = DFF // tn
    t8 = tm // 8
    nt = LP // tm

    def body(d_ref, dn_ref, g_ref, h_ref, w_ref, dg_ref, dw_ref, db_ref):
        i = pl.program_id(1)
        row8 = lax.broadcasted_iota(jnp.int32, (8, tn), 0)
        d = d_ref[...].astype(F32)
        nxt = jnp.where(i < nt - 1, dn_ref[...].astype(F32), 0.0)
        dg = w_ref[2:3, :] * d + w_ref[1:2, :] * _shift_up(d, nxt, 1, row8) + w_ref[0:1, :] * _shift_up(d, nxt, 2, row8)
        rows = i * tm + lax.broadcasted_iota(jnp.int32, (tm, 1), 0)
        dg_ref[...] = jnp.where(rows >= pad, dg, 0.0).astype(dg_ref.dtype)
        g = g_ref[...].astype(F32)
        halo = jnp.where(i > 0, h_ref[...].astype(F32), 0.0)
        row3 = lax.broadcasted_iota(jnp.int32, (3, tn), 0)
        s0 = jnp.sum(d * _shift_down(g, halo, 2, row8), axis=0, keepdims=True)
        s1 = jnp.sum(d * _shift_down(g, halo, 1, row8), axis=0, keepdims=True)
        s2 = jnp.sum(d * g, axis=0, keepdims=True)
        dw = jnp.where(row3 == 0, s0, jnp.where(row3 == 1, s1, s2))
        dbp = jnp.sum(d, axis=0, keepdims=True)

        @pl.when(i == 0)
        def _():
            dw_ref[...] = dw
            db_ref[...] = dbp

        @pl.when(i > 0)
        def _():
            dw_ref[...] += dw
            db_ref[...] += dbp

    blk = pl.BlockSpec((tm, tn), lambda j, i: (i, j))
    in_specs = [
        blk,
        pl.BlockSpec((8, tn), lambda j, i: (jnp.minimum((i + 1) * t8, LP // 8 - 1), j)),
        blk,
        pl.BlockSpec((8, tn), lambda j, i: (jnp.maximum(i * t8 - 1, 0), j)),
        pl.BlockSpec((3, tn), lambda j, i: (0, j)),
    ]
    out_specs = [blk, pl.BlockSpec((3, tn), lambda j, i: (0, j)), pl.BlockSpec((1, tn), lambda j, i: (0, j))]
    return pl.pallas_call(
        body, grid=(nj, nt), in_specs=in_specs, out_specs=out_specs,
        out_shape=[_sds((LP, DFF), BF16), _sds((3, DFF), F32), _sds((1, DFF), F32)],
        compiler_params=_cparams(("parallel", "arbitrary"), 2 * 10 * tm * tn * 4), name=name,
    )(dgc, dgc, gu, gu, cw)


def _adamw_math(w, g, m, v):
    m = ADAM_B1 * m + (1.0 - ADAM_B1) * g
    v = ADAM_B2 * v + (1.0 - ADAM_B2) * (g * g)
    m_hat = m / (1.0 - ADAM_B1 ** ADAM_STEP)
    v_hat = v / (1.0 - ADAM_B2 ** ADAM_STEP)
    delta = -ADAM_LR * (m_hat / (jnp.sqrt(v_hat) + ADAM_EPS) + ADAM_WD * w)
    return delta, m, v


def _adamw(w, g, m, v, name):
    R, C = g.shape
    tm = R if R * C * 4 <= (1 << 20) else _tile(R, max(8, ((1 << 20) // (C * 4)) // 8 * 8), 8)

    def body(w_ref, g_ref, m_ref, v_ref, d_ref, mo_ref, vo_ref):
        d_ref[...], mo_ref[...], vo_ref[...] = _adamw_math(w_ref[...], g_ref[...], m_ref[...], v_ref[...])

    blk = pl.BlockSpec((tm, C), lambda i: (i, 0))
    wblk = blk if w.ndim == 2 else pl.BlockSpec((None, tm, C), lambda i: (0, i, 0))
    return pl.pallas_call(
        body, grid=(R // tm,), in_specs=[wblk, blk, wblk, wblk], out_specs=[wblk] * 3, out_shape=[_sds(w.shape, F32)] * 3,
        compiler_params=_cparams(("parallel",), 2 * 7 * tm * (C + LANES) * 4), name=name,
    )(w, g, m, v)


def _sum_adamw(parts, w, m, v, name):
    n, R, C = parts.shape
    tm = _tile(R, 256, 8)

    def body(p_ref, w_ref, m_ref, v_ref, g_ref, d_ref, mo_ref, vo_ref):
        g = p_ref[0]
        for k in range(1, n):
            g = g + p_ref[k]
        g_ref[...] = g
        d_ref[...], mo_ref[...], vo_ref[...] = _adamw_math(w_ref[...], g, m_ref[...], v_ref[...])

    blk = pl.BlockSpec((tm, C), lambda i: (i, 0))
    return pl.pallas_call(
        body, grid=(R // tm,), in_specs=[pl.BlockSpec((n, tm, C), lambda i: (0, i, 0))] + [blk] * 3, out_specs=[blk] * 4,
        out_shape=[_sds((R, C), F32)] * 4,
        compiler_params=_cparams(("parallel",), 2 * (n + 7) * tm * C * 4), name=name,
    )(parts, w, m, v)


def _add_half(g, got, c_idx, name):
    n, R, C = g.shape
    HR = R // 2
    tm = _tile(HR, max(8, ((1 << 20) // (C * 4)) // 8 * 8), 8)
    nb = HR // tm

    def body(c_ref, g_ref, t_ref, o_ref):
        o_ref[...] = (g_ref[...] + t_ref[...]).astype(o_ref.dtype)

    grid_spec = pltpu.PrefetchScalarGridSpec(
        num_scalar_prefetch=1, grid=(n, nb),
        in_specs=[pl.BlockSpec((None, tm, C), lambda k, i, c: (k, c[0] * nb + i, 0)),
                  pl.BlockSpec((None, tm, C), lambda k, i, c: (k, i, 0))],
        out_specs=pl.BlockSpec((None, tm, C), lambda k, i, c: (k, i, 0)))
    return pl.pallas_call(
        body, grid_spec=grid_spec, out_shape=_sds((n, HR, C), BF16),
        compiler_params=_cparams(("parallel", "parallel"), 2 * 3 * tm * (C + LANES) * 4), name=name,
    )(c_idx, g, got)


def _sum_half(g, got, land, chip_c, name):
    n, R, C = g.shape
    HR = R // 2
    tm = _tile(HR, max(8, ((1 << 20) // (C * 4)) // 8 * 8), 8)
    nb = HR // tm

    def body(s_ref, g_ref, t_ref, l_ref, o_ref):
        acc = g_ref[...] + t_ref[...]
        for k in range(3):
            acc = acc + l_ref[k].astype(F32)
        o_ref[...] = acc

    grid_spec = pltpu.PrefetchScalarGridSpec(
        num_scalar_prefetch=1, grid=(nb,),
        in_specs=[pl.BlockSpec((None, tm, C), lambda i, sc: (sc[0], sc[1] * nb + i, 0)),
                  pl.BlockSpec((None, tm, C), lambda i, sc: (sc[0], i, 0)),
                  pl.BlockSpec((3, tm, C), lambda i, sc: (0, i, 0))],
        out_specs=pl.BlockSpec((tm, C), lambda i, sc: (sc[1] * nb + i, 0)))
    return pl.pallas_call(
        body, grid_spec=grid_spec, out_shape=_sds((R, C), F32),
        compiler_params=_cparams(("parallel",), 2 * 6 * tm * (C + LANES) * 4), name=name,
    )(chip_c, g, got, land)


def _cast_slot(w, chip_idx, name):
    R, C = w.shape
    tm = _tile(R, max(16, ((1 << 20) // (C * 4)) // 16 * 16), 16)

    def body(s_ref, w_ref, o_ref):
        o_ref[...] = w_ref[...].astype(o_ref.dtype)

    grid_spec = pltpu.PrefetchScalarGridSpec(
        num_scalar_prefetch=1, grid=(R // tm,),
        in_specs=[pl.BlockSpec((tm, C), lambda i, sc: (i, 0))],
        out_specs=pl.BlockSpec((None, tm, C), lambda i, sc: (sc[0], i, 0)))
    return pl.pallas_call(
        body, grid_spec=grid_spec, out_shape=_sds((4, R, C), BF16),
        compiler_params=_cparams(("parallel",), 2 * 2 * tm * (C + LANES) * 4), name=name,
    )(chip_idx, w)


def _gather_chips(gs, name):
    nw = len(gs)

    def body(*refs):
        start, forward, finish = _gather_phases(refs[nw:2 * nw], *refs[2 * nw:])
        start()
        forward()
        finish()

    return pl.pallas_call(
        body, out_shape=[_sds(g.shape, g.dtype) for g in gs], in_specs=[HBM_SPEC] * nw, out_specs=[HBM_SPEC] * nw,
        scratch_shapes=[pltpu.SemaphoreType.DMA((6 * nw,)), pltpu.SemaphoreType.DMA((6 * nw,))],
        input_output_aliases={i: i for i in range(nw)}, name=name,
    )(*gs)


def _sibling_halves(gs, name):
    nw = len(gs)
    assert all(g.shape[1] % 16 == 0 for g in gs)

    def body(*refs):
        start, finish = _sibling_phases(refs[:nw], refs[nw:2 * nw], *refs[2 * nw:])
        start()
        finish()

    return pl.pallas_call(
        body, out_shape=[_sds((4, g.shape[1] // 2, g.shape[2]), g.dtype) for g in gs],
        in_specs=[HBM_SPEC] * nw, out_specs=[HBM_SPEC] * nw,
        scratch_shapes=[pltpu.SemaphoreType.DMA((nw,)), pltpu.SemaphoreType.DMA((nw,))], name=name,
    )(*gs)


def _sibling_join(fs, name):
    nw = len(fs)
    assert all(f.shape[0] % 16 == 0 for f in fs)

    def body(*refs):
        o_refs = refs[nw:2 * nw]
        send_sems, recv_sems = refs[2 * nw:]
        x, y, c = _place()

        def copy(i, half):
            HR = o_refs[i].shape[0] // 2
            rows = o_refs[i].at[pl.ds(pl.multiple_of(half * HR, 8), HR), :]
            return pltpu.make_async_remote_copy(
                src_ref=rows, dst_ref=rows, send_sem=send_sems.at[i], recv_sem=recv_sems.at[i],
                device_id=(x, y, 1 - c), device_id_type=MESH)

        sends = [copy(i, c) for i in range(nw)]
        for cp in sends:
            cp.start()
        for i in range(nw):
            copy(i, 1 - c).wait_recv()
        for cp in sends:
            cp.wait_send()

    return pl.pallas_call(
        body, out_shape=[_sds(f.shape, f.dtype) for f in fs], in_specs=[HBM_SPEC] * nw, out_specs=[HBM_SPEC] * nw,
        scratch_shapes=[pltpu.SemaphoreType.DMA((nw,)), pltpu.SemaphoreType.DMA((nw,))],
        input_output_aliases={i: i for i in range(nw)}, name=name,
    )(*fs)


def _gather_all(v, name):
    M, W = v.shape

    def body(v_ref, o_ref, send_sems, recv_sems, local_sem):
        x, y, c = _place()
        me, sibling = (x, y, c), (x, y, 1 - c)
        chips = _other_chips(x, y)

        def slot(px, py, pc):
            return o_ref.at[4 * px + 2 * py + pc]

        def copy(k, block, to, src=None):
            return pltpu.make_async_remote_copy(
                src_ref=slot(*block) if src is None else src, dst_ref=slot(*block),
                send_sem=send_sems.at[k], recv_sem=recv_sems.at[k], device_id=to, device_id_type=MESH)

        mine = pltpu.make_async_copy(v_ref, slot(*me), local_sem)
        mine.start()
        first = [copy(0, me, sibling, src=v_ref)]
        first += [copy(1 + j, me, (*chip, c), src=v_ref) for j, chip in enumerate(chips)]
        for cp in first:
            cp.start()
        passed = [copy(4 + j, (*chip, c), sibling) for j, chip in enumerate(chips)]
        for j, chip in enumerate(chips):
            copy(1 + j, (*chip, c), me).wait_recv()
            passed[j].start()
        copy(0, sibling, me).wait_recv()
        for j, chip in enumerate(chips):
            copy(4 + j, (*chip, 1 - c), me).wait_recv()
        for cp in first + passed:
            cp.wait_send()
        mine.wait()

    vm = pl.BlockSpec(memory_space=pltpu.VMEM)
    return pl.pallas_call(
        body, out_shape=_sds((8, M, W), v.dtype), in_specs=[vm], out_specs=vm,
        scratch_shapes=[pltpu.SemaphoreType.DMA((7,)), pltpu.SemaphoreType.DMA((7,)), pltpu.SemaphoreType.DMA(())],
        compiler_params=pltpu.CompilerParams(vmem_limit_bytes=int(min(10 * M * W * 4 + (8 << 20), V7X_VMEM_BYTES - (8 << 20)))),
        name=name,
    )(v)


def _rows_for(n_elems, width, mult=8):
    rows = -(-n_elems // width)
    return -(-rows // mult) * mult


def _pack_small(arrs, total_rows):
    parts = []
    used = 0
    for a in arrs:
        rows = _rows_for(a.size, LANES)
        parts.append(jnp.pad(a.reshape(-1), (0, rows * LANES - a.size)).reshape(rows, LANES))
        used += rows
    if total_rows > used:
        parts.append(jnp.zeros((total_rows - used, LANES), F32))
    return jnp.concatenate(parts, axis=0)


def _unpack_small(p, shapes):
    outs, r = [], 0
    lead = p.shape[:-2]
    for shp in shapes:
        n = int(np.prod(shp))
        rows = _rows_for(n, LANES)
        outs.append(p[..., r:r + rows, :].reshape(lead + (rows * LANES,))[..., :n].reshape(lead + tuple(shp)))
        r += rows
    return outs


def _chip_pieces(lo, hi, n):
    out = []
    while lo < hi:
        q = lo // n
        b = min(hi, (q + 1) * n)
        out.append((q, lo - q * n, b - q * n))
        lo = b
    return out


def _chips_to_cols(w):
    n4, K, n = w.shape
    return w.transpose(1, 0, 2).reshape(K, n4 * n)


def _block_diag(m, gpb):
    G, A, B = m.shape
    nb = G // gpb
    eye = jnp.eye(gpb, dtype=m.dtype)
    t = m.reshape(nb, gpb, A, B)[:, :, :, None, :] * eye[None, :, None, :, None]
    return t.reshape(nb, gpb * A, gpb * B)


def _block_diag_extract(m, gpb, A, B):
    nb = m.shape[0]
    t = m.reshape(nb, gpb, A, gpb, B)
    eye = jnp.eye(gpb, dtype=m.dtype)
    d = jnp.sum(t * eye[None, :, None, :, None], axis=3)
    return d.reshape(nb * gpb, A, B)


def kernel(x, meta, g_mix, w_in, b_f, lam_re, lam_im, log_dt, b_re, b_im, c_re, c_im, d_skip, w_glu, w_attn_o, w_out, g_ffn, w_up, conv_w, conv_b, w_down, g_final, loss_target, m_meta, m_g_mix, m_w_in, m_b_f, m_lam_re, m_lam_im, m_log_dt, m_b_re, m_b_im, m_c_re, m_c_im, m_d_skip, m_w_glu, m_w_attn_o, m_w_out, m_g_ffn, m_w_up, m_conv_w, m_conv_b, m_w_down, m_g_final, v_meta, v_g_mix, v_w_in, v_b_f, v_lam_re, v_lam_im, v_log_dt, v_b_re, v_b_im, v_c_re, v_c_im, v_d_skip, v_w_glu, v_w_attn_o, v_w_out, v_g_ffn, v_w_up, v_conv_w, v_conv_b, v_w_down, v_g_final):
    args = dict(locals())
    L, D = x.shape[1], x.shape[2]
    NM = meta.shape[0]
    H = b_f.shape[1]
    DA = H * HEAD_DIM
    G, P, C = b_re.shape[1:]
    DS, GP = G * C, G * P
    DFF = conv_b.shape[1]
    PAD = (-NM) % LANES
    OFF = PAD + NM
    LP = OFF + L
    NZ = 3 * DA + DS + 2 * D
    U0, GA0, GB0 = 3 * DA, 3 * DA + DS, 3 * DA + DS + D
    NB = G // GROUPS_PER_BLOCK
    chip = 2 * lax.axis_index("x") + lax.axis_index("y")
    core = lax.axis_index("c")

    big = ["w_in", "w_glu", "w_attn_o", "w_out", "w_up", "w_down"]
    local = {n: args[n][0] for n in big}
    chip_idx = chip.reshape(1).astype(jnp.int32)
    slots = {n: _cast_slot(local[n], chip_idx, "cast_" + n) for n in big}
    gathered = {"w_in": _gather_chips([slots["w_in"]], "gather_w_in")[0]}
    tiny_shapes = [conv_w.shape[1:], meta.shape]
    tiny_rows = sum(_rows_for(int(np.prod(sh)), LANES) for sh in tiny_shapes)
    tiny = _gather_all(_pack_small([conv_w[0], meta], tiny_rows), "gather_small_weights")[0::2]
    conv_w_c, meta_c = _unpack_small(tiny, tiny_shapes)
    conv_w_f = _chips_to_cols(conv_w_c)
    meta_full = _chips_to_cols(meta_c)
    g_in = gathered["w_in"]
    n_in = g_in.shape[2]
    cols = lambda lo, hi: [g_in[q, :, a:b] for q, a, b in _chip_pieces(lo, hi, n_in)]
    w_zf = jnp.concatenate(cols(0, 3 * DA) + cols(3 * DA + H, 4 * n_in) + cols(3 * DA, 3 * DA + H)
                           + [jnp.zeros((D, LANES - H), BF16)], axis=1)
    N_GLU, N_AO, N_UP = (slots[n].shape[2] for n in ("w_glu", "w_attn_o", "w_up"))

    col = lambda a: a.reshape(GP, 1)
    lr_c, li_c = col(lam_re[0]), col(lam_im[0])
    ldt_c = jnp.repeat(log_dt[0], P).reshape(GP, 1)
    br2, bi2 = b_re[0].reshape(GP, C), b_im[0].reshape(GP, C)
    SL = _seg_len(LP)
    a_re, a_im, bb_re, bb_im, pw_re, pw_im, pa_re, pa_im, pb_re, pb_im = _ssm_prep(
        lr_c, li_c, ldt_c, br2, bi2, SL, "ssm_prep")
    S = GROUPS_PER_BLOCK * P
    CB = GROUPS_PER_BLOCK * C
    pw_r = pw_re.T.reshape(8, NB, S).transpose(1, 0, 2)
    pw_i = pw_im.T.reshape(8, NB, S).transpose(1, 0, 2)
    row8 = jnp.arange(8)[None, :, None]

    def masked_power(pw, k, keep):
        return jnp.where(keep, pw[:, k - 1][:, None, :], 0.0)

    pb_r = pb_re.T.reshape(8, NB, S).transpose(1, 0, 2)
    pb_i = pb_im.T.reshape(8, NB, S).transpose(1, 0, 2)
    coef = jnp.stack(
        [masked_power(pw, k, row8 >= k) for k in (1, 2, 4) for pw in (pb_r, pb_i)] + [pb_r, pb_i], axis=1)
    to_rows = lambda p: jnp.repeat(p[:, :SL].T.reshape(SL, NB, S).transpose(1, 0, 2), 8, axis=1)
    pw_t = jnp.stack([to_rows(pa_re), to_rows(pa_im)], axis=1)
    coef_rev = jnp.stack(
        [masked_power(pw, k, row8 < 8 - k) for k in (1, 2, 4) for pw in (pw_r, -pw_i)]
        + [pw_r[:, ::-1], -pw_i[:, ::-1]], axis=1)
    bd = lambda m: _block_diag(m, GROUPS_PER_BLOCK)
    bbr3, bbi3 = bb_re.reshape(G, P, C), bb_im.reshape(G, P, C)
    bbr_cs = bd(bbr3.transpose(0, 2, 1)).astype(BF16)
    bbi_cs = bd(bbi3.transpose(0, 2, 1)).astype(BF16)
    bbr_sc = bd(bbr3).astype(BF16)
    bbi_sc = bd(bbi3).astype(BF16)
    ccr_sc = bd(c_re[0].transpose(0, 2, 1)).astype(BF16)
    cci_sc = bd(c_im[0].transpose(0, 2, 1)).astype(BF16)
    ccr_cs = bd(c_re[0]).astype(BF16)
    cci_cs = bd(c_im[0]).astype(BF16)

    h0 = jnp.concatenate([jnp.zeros((PAD, D), F32), meta_full, x[0]], axis=0)
    n1 = _rms_fwd(h0, g_mix, "rms_mix")
    z = _mm(n1, w_zf, "nn", LP, NZ, D, BF16, "in_proj")
    fpre = _mm(n1, w_zf, "nn", LP, LANES, D, F32, "in_proj_f", b_off=(0, NZ))
    bf_pad = jnp.pad(b_f, ((0, 0), (0, LANES - H)))
    fcum = _fgate_fwd(fpre, bf_pad, PAD, "fgate_fwd")
    key_bias = jnp.where(jnp.arange(LP)[:, None] >= PAD, -fcum, NEG)
    bias_t = key_bias.T[:H].reshape(H, 1, LP)
    attn, attn_f32, lse_t, *rest = _attn_fwd(z, bias_t, H, PAD, "attn_fwd", gather=[slots[n] for n in big[1:]])
    gathered.update(zip(big[1:], rest))
    w_glu_c, w_ao_c, w_up_c = gathered["w_glu"], gathered["w_attn_o"], gathered["w_up"]
    w_out_f = gathered["w_out"].reshape(D, D)
    w_down_f = gathered["w_down"].reshape(DFF, D)
    ao = _mm(attn, w_ao_c, "nn", LP, D, DA, BF16, "attn_out", b_chips=N_AO)
    y, yg, hs_re, hs_im = _ssm_fwd(z, U0, coef, pw_t, bbr_cs, bbi_cs, ccr_sc, cci_sc, d_skip, "ssm_fwd")
    yab = _mm(yg, w_glu_c, "nn", LP, 2 * D, DS, BF16, "glu_proj", b_chips=N_GLU)
    merged = _merge_fwd(yab, z, ao, D, GA0, GB0, "merge_fwd")
    h1 = _mm(merged, w_out_f, "nn", LP, D, D, F32, "out_proj", res=h0)
    n2 = _rms_fwd(h1, g_ffn, "rms_ffn")
    gu = _mm(n2, w_up_c, "nn", LP, 2 * DFF, D, BF16, "up_proj", tn=1408, b_chips=N_UP)
    act = _convact_fwd(gu, conv_w_f, conv_b, DFF, "convact_fwd")
    h2 = _mm(act, w_down_f, "nn", LP, D, DFF, F32, "down_proj", res=h1, tn=512, tk=DFF)
    dh2, dg_final, loss_v = _final_loss(h2, g_final.reshape(1, D), loss_target[0], OFF, "final_loss")
    loss = lax.psum(loss_v[0, 0], ("x", "y", "c"))

    KW = dict(tm=512, tn=512, tk=LP)
    dact = _mm(dh2, w_down_f, "nt", LP, DFF, D, BF16, "down_bwd_x")
    dw_down = _mm(act, dh2, "tn", DFF, D, LP, F32, "down_bwd_w", **KW)
    dgc, du_ffn = _convact_bwd(dact, gu, conv_w_f, conv_b, DFF, "convact_bwd")
    dg_ffn_in, dconv_w, dconv_b = _conv_bwd(dgc, gu, conv_w_f, DFF, PAD, "conv_bwd")
    dn2 = _mm(dg_ffn_in, w_up_c, "nt", LP, D, DFF, F32, "up_bwd_x_g", tn=512, tk=N_UP, b_chips=N_UP)
    dn2 = _mm(du_ffn, w_up_c, "nt", LP, D, DFF, F32, "up_bwd_x_u", res=dn2, b_off=(0, DFF), tn=512, tk=N_UP, b_chips=N_UP)
    dw_up = _mm(n2, dg_ffn_in, "tn", D, DFF, LP, F32, "up_bwd_w_g", tm=512, tn=256, tk=LP, out_chips=N_UP,
                out_into=(lax.empty((4, D, N_UP), F32), 0))
    dw_up = _mm(n2, du_ffn, "tn", D, DFF, LP, F32, "up_bwd_w_u", tm=512, tn=256, tk=LP, out_chips=N_UP,
                out_into=(dw_up, DFF // N_UP))
    dh1, dg_ffn = _rms_bwd(h1, g_ffn, dn2, dh2, "rms_ffn_bwd")
    c_idx = core.reshape(1).astype(jnp.int32)
    chip_c = jnp.stack([chip, core]).astype(jnp.int32)

    dmerged = _mm(dh1, w_out_f, "nt", LP, D, D, F32, "out_bwd_x")
    dw_out = _mm(merged, dh1, "tn", D, D, LP, F32, "out_bwd_w", **KW)
    dya, dyb, dga, dgb, dao = _merge_bwd(dmerged, yab, z, ao, D, GA0, GB0, "merge_bwd")
    dattn = _mm(dao, w_ao_c, "nt", LP, DA, D, BF16, "attn_out_bwd_x", b_chips=N_AO)
    dw_ao = _mm(attn, dao, "tn", DA, D, LP, F32, "attn_out_bwd_w", out_chips=N_AO, **KW)
    dyg = _mm(dya, w_glu_c, "nt", LP, DS, D, F32, "glu_bwd_x_a", b_chips=N_GLU)
    dyg = _mm(dyb, w_glu_c, "nt", LP, DS, D, F32, "glu_bwd_x_b", res=dyg, b_off=(0, D), b_chips=N_GLU)
    dw_glu = _mm(yg, dya, "tn", DS, D, LP, F32, "glu_bwd_w_a", out_chips=N_GLU,
                 out_into=(lax.empty((4, DS, N_GLU), F32), 0), **KW)
    dw_glu = _mm(yg, dyb, "tn", DS, D, LP, F32, "glu_bwd_w_b", out_chips=N_GLU, out_into=(dw_glu, D // N_GLU), **KW)
    early = ["w_glu", "w_attn_o", "w_out", "w_up", "w_down"]
    early_grads = [dw_glu, dw_ao, dw_out.reshape(4, D // 4, D), dw_up, dw_down.reshape(4, DFF // 4, D)]
    (du_ssm, dbbr_d, dbbi_d, dccr_d, dcci_d, dar_b, dai_b, dd_skip, *early_got) = _ssm_bwd(
        z, U0, dyg, y, hs_re, hs_im, coef_rev, bbr_sc, bbi_sc, ccr_cs, cci_cs, d_skip, "ssm_bwd",
        ride=_sibling_ride(early_grads))
    delta_t = _attn_delta(dattn, attn_f32, H, "attn_delta")
    early_part = [_add_half(g, t, c_idx, "rs_add_" + n) for n, g, t in zip(early, early_grads, early_got)]
    dq, dk, dv, dbias_t, *early_land = _attn_bwd(z, dattn, lse_t, delta_t, bias_t, H, "attn_bwd", scatter=early_part)
    dF = jnp.pad(-dbias_t[:, 0, :].T, ((0, 0), (0, LANES - H)))
    dfpre, db_f = _fgate_bwd(dF, fpre, bf_pad, PAD, "fgate_bwd")
    dz = jnp.concatenate([dq, dk, dv, du_ssm, dga, dgb, dfpre.astype(BF16)], axis=1)
    dw_zf = _mm(n1, dz, "tn", D, NZ + LANES, LP, F32, "in_bwd_w", tm=512, tn=640, tk=LP)
    def orig_cols(lo, hi):
        parts_ = []
        for a, b, shift in ((0, 3 * DA, 0), (3 * DA, 3 * DA + H, NZ - 3 * DA), (3 * DA + H, 4 * n_in, -H)):
            a, b = max(a, lo), min(b, hi)
            if a < b:
                parts_.append(dw_zf[:, a + shift:b + shift])
        return jnp.concatenate(parts_, axis=1)

    late_grads = [jnp.stack([orig_cols(q * n_in, (q + 1) * n_in) for q in range(4)], axis=0)]
    late_got = _sibling_halves(late_grads, "rs_sibling_w_in")
    late_part = [_add_half(late_grads[0], late_got[0], c_idx, "rs_add_w_in")]
    dn1, *late_land = _mm(dz, w_zf, "nt", LP, D, NZ + LANES, F32, "in_bwd_x", tm=640, tn=256, tk=NZ + LANES,
                          ride=_scatter_ride(late_part))
    dh0, dg_mix = _rms_bwd(h0, g_mix, dn1, dh1, "rms_mix_bwd")
    grad_x = dh0[OFF:][None]
    dmeta_full = dh0[PAD:OFF]

    ext = lambda m, A, B: _block_diag_extract(m, GROUPS_PER_BLOCK, A, B)
    dbb_re = ext(dbbr_d, C, P).transpose(0, 2, 1).reshape(GP, C)
    dbb_im = ext(dbbi_d, C, P).transpose(0, 2, 1).reshape(GP, C)
    dc_re = ext(dccr_d, P, C).transpose(0, 2, 1)[None]
    dc_im = ext(dcci_d, P, C).transpose(0, 2, 1)[None]
    glr, gli, gldt, gbr, gbi = _ssm_prep_bwd(lr_c, li_c, ldt_c, br2, bi2, dar_b.reshape(GP, 1), dai_b.reshape(GP, 1),
                                             dbb_re, dbb_im, "ssm_prep_bwd")
    small_grads = {
        "g_mix": dg_mix, "b_f": db_f[:, :H], "lam_re": glr.reshape(1, G, P), "lam_im": gli.reshape(1, G, P),
        "log_dt": gldt.reshape(G, P).sum(axis=1)[None], "b_re": gbr.reshape(1, G, P, C), "b_im": gbi.reshape(1, G, P, C),
        "c_re": dc_re, "c_im": dc_im, "d_skip": dd_skip, "g_ffn": dg_ffn, "conv_b": dconv_b, "g_final": dg_final.reshape(D),
    }

    small = list(small_grads)
    rider_grads = [dconv_w, dmeta_full]
    small_shapes = [args[n].shape for n in small] + [g.shape for g in rider_grads]
    srows = sum(_rows_for(int(np.prod(sh)), LANES) for sh in small_shapes)
    srows = -(-srows // 256) * 256
    zeros_like_riders = [jnp.zeros(g.shape, F32) for g in rider_grads]
    pack = lambda arrs: _pack_small(arrs, srows)
    g_parts = _gather_all(pack([small_grads[n] for n in small] + rider_grads), "gather_small_grads")
    sm = _sum_adamw(g_parts, pack([args[n] for n in small] + zeros_like_riders),
                    pack([args["m_" + n] for n in small] + zeros_like_riders),
                    pack([args["v_" + n] for n in small] + zeros_like_riders), "small_adamw")
    unpacked = [_unpack_small(p, small_shapes) for p in sm]
    sg, sd, smm, svv = (dict(zip(small, u[:len(small)])) for u in unpacked)
    dconv_w_sum, dmeta_sum = unpacked[0][len(small):]
    n_cw, n_me = conv_w.shape[2], meta.shape[1]
    rider = {"conv_w": lax.dynamic_slice_in_dim(dconv_w_sum, chip * n_cw, n_cw, axis=1)[None],
             "meta": lax.dynamic_slice_in_dim(dmeta_sum, chip * n_me, n_me, axis=1)}

    big = ["w_in"] + early
    halves = [_sum_half(g, t, l_, chip_c, "rs_sum_" + n) for n, g, t, l_ in
              zip(big, late_grads + early_grads, list(late_got) + list(early_got), list(late_land) + list(early_land))]
    shard_grads = dict(zip(big, _sibling_join(halves, "rs_join")))
    shard_grads.update({n: g.reshape(g.shape[-2:]) for n, g in rider.items()})
    bg, bd_, bm, bv = {}, {}, {}, {}
    for n, g in shard_grads.items():
        bd_[n], bm[n], bv[n] = _adamw(args[n], g, args["m_" + n], args["v_" + n], "adamw_" + n)
        bg[n] = g.reshape(args[n].shape)

    order = ["meta", "g_mix", "w_in", "b_f", "lam_re", "lam_im", "log_dt", "b_re", "b_im", "c_re", "c_im", "d_skip",
             "w_glu", "w_attn_o", "w_out", "g_ffn", "w_up", "conv_w", "conv_b", "w_down", "g_final"]
    pick = lambda bigd, smalld, n: bigd[n] if n in bigd else smalld[n]
    outs = [loss, grad_x]
    for bigd, smalld in ((bg, sg), (bd_, sd), (bm, smm), (bv, svv)):
        outs += [pick(bigd, smalld, n) for n in order]
    return tuple(outs)
```

```python
import functools
import math

import jax
import jax.numpy as jnp
import numpy as np
from jax import lax
from jax.experimental import pallas as pl
from jax.experimental.pallas import tpu as pltpu

F32 = jnp.float32
BF16 = jnp.bfloat16
MESH = pl.DeviceIdType.MESH

EPS = 1e-6
HEAD_DIM = 128
LANES = 128
NEG = -1e30
GELU_C = math.sqrt(2.0 / math.pi)
GELU_A = 0.044715
ADAM_LR, ADAM_B1, ADAM_B2, ADAM_EPS, ADAM_WD, ADAM_STEP = 0.001, 0.9, 0.999, 1e-08, 0.01, 10
V7X_VMEM_BYTES = 64 << 20
GROUPS_PER_BLOCK = 8


def _tile(n, pref, mult=LANES):
    if n <= pref:
        return n
    t = (pref // mult) * mult
    while t >= mult:
        if n % t == 0:
            return t
        t -= mult
    raise ValueError(f"no tile for {n} <= {pref} (multiple of {mult})")


def _ctile(pref, *vals):
    g = 0
    for v in vals:
        g = math.gcd(g, v)
    return _tile(g, pref)


def _cparams(sem, est_bytes):
    limit = int(min(max(est_bytes * 1.25 + (4 << 20), 16 << 20), V7X_VMEM_BYTES - (8 << 20)))
    return pltpu.CompilerParams(dimension_semantics=sem, vmem_limit_bytes=limit)


def _sds(shape, dtype):
    return jax.ShapeDtypeStruct(tuple(shape), dtype)


def _sig(x):
    return 0.5 * jnp.tanh(0.5 * x) + 0.5


def _sig_tail(x):
    return 1.0 / (1.0 + jnp.exp(-x))


def _gelu(x):
    t = jnp.tanh(GELU_C * (x + GELU_A * x * x * x))
    return 0.5 * x * (1.0 + t)


def _gelu_grad(x):
    t = jnp.tanh(GELU_C * (x + GELU_A * x * x * x))
    return 0.5 * (1.0 + t) + 0.5 * x * (1.0 - t * t) * GELU_C * (1.0 + 3.0 * GELU_A * x * x)


def _mm(a, b, mode, M, N, K, out_dtype, name, *, res=None, a_off=(0, 0), b_off=(0, 0),
        tm=640, tn=1024, tk=2048, b_chips=None, out_chips=None, out_into=None, ride=None):
    tm, tn, tk = _tile(M, tm, 8 if mode != "tn" else LANES), _tile(N, tn), _tile(K, tk, LANES if mode != "tn" else 8)
    if b_chips is not None and mode == "nt":
        tk = _ctile(tk, tk, b_chips)
    if b_chips is not None and mode != "nt":
        tn = _ctile(tn, tn, b_chips)
    if out_chips is not None:
        tn = _ctile(tn, tn, out_chips)
    nk = K // tk
    ar, ac = a_off
    br, bc = b_off
    if mode == "tn":
        assert ar % tk == 0 and ac % tm == 0
        a_spec = pl.BlockSpec((tk, tm), lambda i, j, k: (k + ar // tk, i + ac // tm))
        a_dims = 0
    else:
        assert ar % tm == 0 and ac % tk == 0
        a_spec = pl.BlockSpec((tm, tk), lambda i, j, k: (i + ar // tm, k + ac // tk))
        a_dims = 1
    if mode == "nt":
        assert br % tn == 0 and bc % tk == 0
        if b_chips is None:
            b_spec = pl.BlockSpec((tn, tk), lambda i, j, k: (j + br // tn, k + bc // tk))
        else:
            per = b_chips // tk
            b_spec = pl.BlockSpec((None, tn, tk), lambda i, j, k: ((k + bc // tk) // per, j + br // tn, (k + bc // tk) % per))
        b_dims = 1
    else:
        assert br % tk == 0 and bc % tn == 0
        if b_chips is None:
            b_spec = pl.BlockSpec((tk, tn), lambda i, j, k: (k + br // tk, j + bc // tn))
        else:
            per = b_chips // tn
            b_spec = pl.BlockSpec((None, tk, tn), lambda i, j, k: ((j + bc // tn) // per, k + br // tk, (j + bc // tn) % per))
        b_dims = 0
    dims = (((a_dims,), (b_dims,)), ((), ()))
    if out_chips is None:
        o_spec = pl.BlockSpec((tm, tn), lambda i, j, k: (i, j))
        o_shape = _sds((M, N), out_dtype)
    else:
        per_o = out_chips // tn
        chip0 = 0 if out_into is None else out_into[1]
        o_spec = pl.BlockSpec((None, tm, tn), lambda i, j, k: (chip0 + j // per_o, i, j % per_o))
        o_shape = _sds((N // out_chips if out_into is None else 4, M, out_chips), out_dtype)
    has_res = res is not None
    has_into = out_into is not None

    r_ins, r_outs, r_sems = _ride_parts(ride)
    n_in = 2 + has_res + has_into
    steps = (M // tm, N // tn, nk)

    def body(*refs):
        a_ref, b_ref = refs[:2]
        r_ref = refs[2] if has_res else None
        o_ref = refs[n_in + len(r_ins)]
        if ride is not None:
            start, finish = ride["fn"](refs[n_in:n_in + len(r_ins)],
                                       refs[n_in + len(r_ins) + 1:n_in + len(r_ins) + 1 + len(r_outs)], *refs[-2:])
            pid = [pl.program_id(d) for d in range(3)]
            pl.when((pid[0] == 0) & (pid[1] == 0) & (pid[2] == 0))(start)
        part = lax.dot_general(a_ref[...].astype(BF16), b_ref[...].astype(BF16), dims, preferred_element_type=F32)

        def write_out(acc):
            if has_res:
                acc = r_ref[...] + acc
            o_ref[...] = acc.astype(o_ref.dtype)

        if nk == 1:
            write_out(part)
        else:
            acc_ref = refs[n_in + len(r_ins) + 1 + len(r_outs)]
            k = pl.program_id(2)

            @pl.when(k == 0)
            def _():
                acc_ref[...] = part

            @pl.when(k > 0)
            def _():
                acc_ref[...] += part

            @pl.when(k == nk - 1)
            def _():
                write_out(acc_ref[...])

        if ride is not None:
            pl.when((pid[0] == steps[0] - 1) & (pid[1] == steps[1] - 1) & (pid[2] == steps[2] - 1))(finish)

    in_specs = ([a_spec, b_spec] + ([o_spec] if has_res else []) + ([pl.BlockSpec(memory_space=pl.ANY)] if has_into else [])
                + [HBM_SPEC] * len(r_ins))
    args = (a, b) + ((res,) if has_res else ()) + ((out_into[0],) if has_into else ()) + tuple(r_ins)
    isz = lambda x: jnp.dtype(x.dtype).itemsize
    est = 2 * (tm * tk * isz(a) + tk * tn * isz(b) + tm * tn * jnp.dtype(out_dtype).itemsize) + tm * tn * 4 * (2 + 2 * has_res)
    sem = ("parallel", "parallel", "arbitrary") if ride is None else ("arbitrary",) * 3
    out = pl.pallas_call(
        body, grid=steps, in_specs=in_specs, out_specs=[o_spec] + [HBM_SPEC] * len(r_outs),
        out_shape=[o_shape] + r_outs,
        scratch_shapes=([pltpu.VMEM((tm, tn), F32)] if nk > 1 else []) + r_sems,
        input_output_aliases={2: 0} if has_into else {},
        compiler_params=_cparams(sem, est), name=name,
    )(*args)
    return out[0] if ride is None else out


def _rms_fwd(h, g, name):
    LP, D = h.shape
    tm = _tile(LP, 640, 8)

    def body(h_ref, g_ref, o_ref):
        x = h_ref[...]
        r = lax.rsqrt(jnp.mean(x * x, axis=-1, keepdims=True) + EPS)
        o_ref[...] = (x * r * g_ref[...]).astype(o_ref.dtype)

    row = pl.BlockSpec((tm, D), lambda i: (i, 0))
    return pl.pallas_call(
        body, grid=(LP // tm,), in_specs=[row, pl.BlockSpec((1, D), lambda i: (0, 0))], out_specs=row,
        out_shape=_sds((LP, D), BF16), compiler_params=_cparams(("parallel",), 2 * tm * D * 6), name=name,
    )(h, g)


def _rms_bwd(h, g, dn, dres, name):
    LP, D = h.shape
    tm = _tile(LP, 320, 8)
    nt = LP // tm

    def body(h_ref, g_ref, dn_ref, dres_ref, dh_ref, dg_ref):
        i = pl.program_id(0)
        x = h_ref[...]
        r = lax.rsqrt(jnp.mean(x * x, axis=-1, keepdims=True) + EPS)
        xh = x * r
        dn_v = dn_ref[...]
        dxh = dn_v * g_ref[...]
        dh_ref[...] = dres_ref[...] + r * (dxh - xh * jnp.mean(dxh * xh, axis=-1, keepdims=True))
        part = jnp.sum(dn_v * xh, axis=0, keepdims=True)

        @pl.when(i == 0)
        def _():
            dg_ref[...] = part

        @pl.when(i > 0)
        def _():
            dg_ref[...] += part

    row = pl.BlockSpec((tm, D), lambda i: (i, 0))
    vec = pl.BlockSpec((1, D), lambda i: (0, 0))
    return pl.pallas_call(
        body, grid=(nt,), in_specs=[row, vec, row, row], out_specs=[row, vec],
        out_shape=[_sds((LP, D), F32), _sds((1, D), F32)],
        compiler_params=_cparams(("arbitrary",), 2 * 4 * tm * D * 4), name=name,
    )(h, g, dn, dres)


def _final_loss(h, g, tgt, off, name):
    LP, D = h.shape
    tm = LANES
    assert off % tm == 0
    ob = off // tm
    nt = LP // tm

    def body(h_ref, g_ref, t_ref, dh_ref, dg_ref, loss_ref):
        i = pl.program_id(0)
        x = h_ref[...]
        r = lax.rsqrt(jnp.mean(x * x, axis=-1, keepdims=True) + EPS)
        xh = x * r
        gv = g_ref[...]
        e = xh * gv - t_ref[...]
        valid = i >= ob
        dy = jnp.where(valid, e * (1.0 / D), 0.0)
        lpart = jnp.where(valid, 0.5 * jnp.sum(jnp.mean(e * e, axis=-1, keepdims=True), axis=0, keepdims=True), 0.0)
        dxh = dy * gv
        dh_ref[...] = r * (dxh - xh * jnp.mean(dxh * xh, axis=-1, keepdims=True))
        gpart = jnp.sum(dy * xh, axis=0, keepdims=True)

        @pl.when(i == 0)
        def _():
            dg_ref[...] = gpart
            loss_ref[...] = jnp.broadcast_to(lpart, loss_ref.shape)

        @pl.when(i > 0)
        def _():
            dg_ref[...] += gpart
            loss_ref[...] += jnp.broadcast_to(lpart, loss_ref.shape)

    row = pl.BlockSpec((tm, D), lambda i: (i, 0))
    vec = pl.BlockSpec((1, D), lambda i: (0, 0))
    return pl.pallas_call(
        body, grid=(nt,),
        in_specs=[row, vec, pl.BlockSpec((tm, D), lambda i: (jnp.maximum(i - ob, 0), 0))],
        out_specs=[row, vec, pl.BlockSpec((1, LANES), lambda i: (0, 0))],
        out_shape=[_sds((LP, D), F32), _sds((1, D), F32), _sds((1, LANES), F32)],
        compiler_params=_cparams(("arbitrary",), 2 * 3 * tm * D * 4), name=name,
    )(h, g, tgt)


def _fgate_fwd(fpre, bias, pad, name):
    LP, W = fpre.shape

    def body(f_ref, b_ref, o_ref):
        row8 = lax.broadcasted_iota(jnp.int32, (8, W), 0)
        bv = b_ref[...]

        def step(g, carry):
            r0 = pl.multiple_of(g * 8, 8)
            x = f_ref[pl.ds(r0, 8), :] + bv
            lf = jnp.minimum(x, 0.0) - jnp.log(1.0 + jnp.exp(-jnp.abs(x)))
            lf = jnp.where(r0 + row8 >= pad, lf, 0.0)
            for k in (1, 2, 4):
                lf = lf + jnp.where(row8 >= k, pltpu.roll(lf, k, 0), 0.0)
            lf = lf + carry
            o_ref[pl.ds(r0, 8), :] = lf
            return jnp.broadcast_to(lf[7:8, :], (8, W))

        lax.fori_loop(0, LP // 8, step, jnp.zeros((8, W), F32))

    return pl.pallas_call(
        body, out_shape=_sds((LP, W), F32),
        compiler_params=_cparams(None, 3 * LP * W * 4), name=name,
    )(fpre, bias)


def _fgate_bwd(dF, fpre, bias, pad, name):
    LP, W = fpre.shape
    ng = LP // 8

    def body(d_ref, f_ref, b_ref, o_ref, db_ref):
        row8 = lax.broadcasted_iota(jnp.int32, (8, W), 0)
        bv = b_ref[...]

        def step(t, carry):
            run, acc = carry
            g = ng - 1 - t
            r0 = pl.multiple_of(g * 8, 8)
            x = d_ref[pl.ds(r0, 8), :]
            for k in (1, 2, 4):
                x = x + jnp.where(row8 < 8 - k, pltpu.roll(x, 8 - k, 0), 0.0)
            x = x + run
            df = x * _sig_tail(-(f_ref[pl.ds(r0, 8), :] + bv))
            df = jnp.where(r0 + row8 >= pad, df, 0.0)
            o_ref[pl.ds(r0, 8), :] = df
            return jnp.broadcast_to(x[0:1, :], (8, W)), acc + df

        _, acc = lax.fori_loop(0, ng, step, (jnp.zeros((8, W), F32), jnp.zeros((8, W), F32)))
        db_ref[...] = jnp.sum(acc, axis=0, keepdims=True)

    return pl.pallas_call(
        body, out_shape=[_sds((LP, W), F32), _sds((1, W), F32)],
        compiler_params=_cparams(None, 4 * LP * W * 4), name=name,
    )(dF, fpre, bias)


def _place():
    return lax.axis_index("x"), lax.axis_index("y"), lax.axis_index("c")


def _other_chips(x, y):
    return [(1 - x, y), (x, 1 - y), (1 - x, 1 - y)]


def _gather_phases(g_refs, send_sems, recv_sems):
    nw = len(g_refs)
    x, y, c = _place()
    chips = _other_chips(x, y)
    me = 2 * x + y

    def copy(i, k, chip, half, to):
        HR = g_refs[i].shape[1] // 2
        rows = g_refs[i].at[chip, pl.ds(pl.multiple_of(half * HR, 16), HR), :]
        return pltpu.make_async_remote_copy(
            src_ref=rows, dst_ref=rows, send_sem=send_sems.at[6 * i + k], recv_sem=recv_sems.at[6 * i + k],
            device_id=to, device_id_type=MESH)

    pairs = [(i, k, cx, cy) for i in range(nw) for k, (cx, cy) in enumerate(chips)]

    def start():
        for i, k, cx, cy in pairs:
            copy(i, k, me, c, (cx, cy, c)).start()

    def forward():
        for i, k, cx, cy in pairs:
            copy(i, k, 2 * cx + cy, c, (cx, cy, c)).wait_recv()
            copy(i, 3 + k, 2 * cx + cy, c, (x, y, 1 - c)).start()

    def finish():
        for i, k, cx, cy in pairs:
            copy(i, 3 + k, 2 * cx + cy, 1 - c, (x, y, 1 - c)).wait_recv()
        for i, k, cx, cy in pairs:
            copy(i, k, me, c, (cx, cy, c)).wait_send()
            copy(i, 3 + k, 2 * cx + cy, c, (x, y, 1 - c)).wait_send()

    return start, forward, finish


def _scatter_phases(s_refs, land_refs, send_sems, recv_sems):
    x, y, c = _place()
    chips = _other_chips(x, y)

    def copy(i, k, cx, cy):
        return pltpu.make_async_remote_copy(
            src_ref=s_refs[i].at[2 * cx + cy], dst_ref=land_refs[i].at[k],
            send_sem=send_sems.at[3 * i + k], recv_sem=recv_sems.at[3 * i + k],
            device_id=(cx, cy, c), device_id_type=MESH)

    pairs = [(i, k, cx, cy) for i in range(len(s_refs)) for k, (cx, cy) in enumerate(chips)]

    def start():
        for p in pairs:
            copy(*p).start()

    def finish():
        for p in pairs:
            copy(*p).wait_recv()
        for p in pairs:
            copy(*p).wait_send()

    return start, finish


def _sibling_phases(g_refs, land_refs, send_sems, recv_sems):
    x, y, c = _place()

    def copy(i):
        HR = land_refs[i].shape[1]
        q0 = pl.multiple_of((1 - c) * HR, 8)
        return pltpu.make_async_remote_copy(
            src_ref=g_refs[i].at[pl.ds(0, 4), pl.ds(q0, HR), :], dst_ref=land_refs[i],
            send_sem=send_sems.at[i], recv_sem=recv_sems.at[i], device_id=(x, y, 1 - c), device_id_type=MESH)

    def start():
        for i in range(len(g_refs)):
            copy(i).start()

    def finish():
        for i in range(len(g_refs)):
            copy(i).wait()

    return start, finish


def _sibling_ride(gs):
    return dict(fn=_sibling_phases, ins=list(gs), outs=[_sds((4, g.shape[1] // 2, g.shape[2]), g.dtype) for g in gs],
                sems=len(gs))


def _scatter_ride(ss):
    return dict(fn=_scatter_phases, ins=list(ss), outs=[_sds((3,) + p.shape[1:], p.dtype) for p in ss], sems=3 * len(ss))


def _ride_parts(ride):
    if ride is None:
        return [], [], []
    return ride["ins"], ride["outs"], [pltpu.SemaphoreType.DMA((ride["sems"],)), pltpu.SemaphoreType.DMA((ride["sems"],))]


HBM_SPEC = pl.BlockSpec(memory_space=pltpu.HBM)


def _col_to_row(col):
    n = col.shape[0]
    return jnp.transpose(jnp.broadcast_to(col, (n, LANES)))[0:1, :]


def _row_to_col(row):
    n = row.shape[1]
    return jnp.transpose(jnp.broadcast_to(row, (LANES, n)))[:, 0:1]


def _attn_fwd(z, bias_t, H, pad, name, gather=()):
    LP = z.shape[0]
    BQ = BK = _tile(LP, 640)
    scale = HEAD_DIM ** -0.5
    NT = (((1,), (1,)), ((), ()))

    nw = len(gather)
    nq = LP // BQ

    def body(*refs):
        q_ref, k_ref, v_ref, b_ref = refs[:4]
        o_ref, of_ref, lse_ref = refs[4 + nw:7 + nw]
        hd = pl.program_id(0)
        qi = pl.program_id(1)
        if nw:
            start, forward, finish = _gather_phases(refs[7 + nw:7 + 2 * nw], *refs[7 + 2 * nw:])
            pl.when((hd == 0) & (qi == 0))(start)
            pl.when((hd == H // 2) & (qi == 0))(forward)
        q = q_ref[...]

        def tile(kb, carry, masked):
            m, l, acc = carry
            k0 = pl.multiple_of(kb * BK, BK)
            s = lax.dot_general(q, k_ref[pl.ds(k0, BK), :], NT, preferred_element_type=F32) * scale
            s = s + b_ref[:, pl.ds(k0, BK)]
            if masked:
                ri = lax.broadcasted_iota(jnp.int32, (BQ, BK), 0)
                ci = lax.broadcasted_iota(jnp.int32, (BQ, BK), 1)
                s = jnp.where(ri >= ci, s, NEG)
            mn = jnp.maximum(m, jnp.max(s, axis=-1, keepdims=True))
            p = jnp.exp(s - mn)
            alpha = jnp.exp(m - mn)
            l = alpha * l + jnp.sum(p, axis=-1, keepdims=True)
            vk = v_ref[pl.ds(k0, BK), :]
            p_hi = p.astype(BF16)
            p_lo = (p - p_hi.astype(F32)).astype(BF16)
            pv = jnp.dot(p_hi, vk, preferred_element_type=F32) + jnp.dot(p_lo, vk, preferred_element_type=F32)
            return mn, l, alpha * acc + pv

        carry = (jnp.full((BQ, 1), NEG, F32), jnp.zeros((BQ, 1), F32), jnp.zeros((BQ, HEAD_DIM), F32))
        carry = lax.fori_loop(0, qi, lambda kb, c: tile(kb, c, False), carry)
        m, l, acc = tile(qi, carry, True)
        rows = qi * BQ + lax.broadcasted_iota(jnp.int32, (BQ, 1), 0)
        o = jnp.where(rows >= pad, acc / l, 0.0)
        o_ref[...] = o.astype(o_ref.dtype)
        of_ref[...] = o
        lse_ref[...] = _col_to_row(m + jnp.log(l))
        if nw:
            pl.when((hd == H - 1) & (qi == nq - 1))(finish)

    in_specs = [
        pl.BlockSpec((BQ, HEAD_DIM), lambda h, i: (i, h)),
        pl.BlockSpec((LP, HEAD_DIM), lambda h, i: (0, H + h)),
        pl.BlockSpec((LP, HEAD_DIM), lambda h, i: (0, 2 * H + h)),
        pl.BlockSpec((None, 1, LP), lambda h, i: (h, 0, 0)),
    ]
    out_specs = [
        pl.BlockSpec((BQ, HEAD_DIM), lambda h, i: (i, h)),
        pl.BlockSpec((BQ, HEAD_DIM), lambda h, i: (i, h)),
        pl.BlockSpec((None, 1, BQ), lambda h, i: (h, 0, i)),
    ]
    est = 2 * (2 * LP * HEAD_DIM * 2 + 8 * LP * 4) + 20 * BQ * LANES * 4 + 8 * BQ * BK * 4
    sems = [pltpu.SemaphoreType.DMA((6 * nw,)), pltpu.SemaphoreType.DMA((6 * nw,))] if nw else []
    return pl.pallas_call(
        body, grid=(H, nq), in_specs=in_specs + [HBM_SPEC] * nw, out_specs=out_specs + [HBM_SPEC] * nw,
        out_shape=[_sds((LP, H * HEAD_DIM), BF16), _sds((LP, H * HEAD_DIM), F32), _sds((H, 1, LP), F32)]
        + [_sds(g.shape, g.dtype) for g in gather],
        scratch_shapes=sems, input_output_aliases={4 + i: 3 + i for i in range(nw)},
        compiler_params=_cparams(("arbitrary", "arbitrary"), est), name=name,
    )(z, z, z, bias_t, *gather)


def _attn_delta(do, o, H, name):
    LP = do.shape[0]
    tm = _tile(LP, 640)

    def body(do_ref, o_ref, d_ref):
        d_ref[...] = _col_to_row(jnp.sum(do_ref[...].astype(F32) * o_ref[...].astype(F32), axis=-1, keepdims=True))

    blk = pl.BlockSpec((tm, HEAD_DIM), lambda h, i: (i, h))
    return pl.pallas_call(
        body, grid=(H, LP // tm), in_specs=[blk, blk],
        out_specs=pl.BlockSpec((None, 1, tm), lambda h, i: (h, 0, i)),
        out_shape=_sds((H, 1, LP), F32),
        compiler_params=_cparams(("parallel", "parallel"), 8 * tm * LANES * 4), name=name,
    )(do, o)


def _attn_bwd(z, do, lse_t, delta_t, bias_t, H, name, scatter=()):
    LP = z.shape[0]
    BK = BQ = _tile(LP, 640)
    nk = nq = LP // BK
    scale = HEAD_DIM ** -0.5
    NT = (((1,), (1,)), ((), ()))
    TN = (((0,), (0,)), ((), ()))

    nw = len(scatter)

    def body(*refs):
        q_ref, k_ref, v_ref, do_ref, lse_ref, dl_ref, b_ref = refs[:7]
        dq_ref, dk_ref, dv_ref, db_ref = refs[7 + nw:11 + nw]
        dq_acc = refs[11 + 2 * nw]
        hd = pl.program_id(0)
        kj = pl.program_id(1)
        if nw:
            start, finish = _scatter_phases(refs[7:7 + nw], refs[11 + nw:11 + 2 * nw], *refs[12 + 2 * nw:])
            pl.when((hd == 0) & (kj == 0))(start)

        @pl.when(kj == 0)
        def _():
            dq_acc[...] = jnp.zeros_like(dq_acc)

        k = k_ref[...]
        v = v_ref[...]
        bcol = _row_to_col(b_ref[:, pl.ds(pl.multiple_of(kj * BK, BK), BK)])

        def tile(qc, carry, masked):
            dk, dv, db = carry
            q0 = pl.multiple_of(qc * BQ, BQ)
            q = q_ref[pl.ds(q0, BQ), :]
            dout = do_ref[pl.ds(q0, BQ), :]
            st = lax.dot_general(k, q, NT, preferred_element_type=F32) * scale + bcol
            if masked:
                ri = lax.broadcasted_iota(jnp.int32, (BK, BQ), 0)
                ci = lax.broadcasted_iota(jnp.int32, (BK, BQ), 1)
                st = jnp.where(ci >= ri, st, NEG)
            pt = jnp.exp(st - lse_ref[:, pl.ds(q0, BQ)])
            dv = dv + jnp.dot(pt.astype(BF16), dout, preferred_element_type=F32)
            dpt = lax.dot_general(v, dout, NT, preferred_element_type=F32)
            dst = pt * (dpt - dl_ref[:, pl.ds(q0, BQ)])
            db = db + jnp.sum(dst, axis=-1, keepdims=True)
            dsb = (dst * scale).astype(BF16)
            dk = dk + jnp.dot(dsb, q, preferred_element_type=F32)
            dq_acc[pl.ds(q0, BQ), :] += lax.dot_general(dsb, k, TN, preferred_element_type=F32)
            return dk, dv, db

        carry = (jnp.zeros((BK, HEAD_DIM), F32), jnp.zeros((BK, HEAD_DIM), F32), jnp.zeros((BK, 1), F32))
        carry = tile(kj, carry, True)
        dk, dv, db = lax.fori_loop(kj + 1, nq, lambda qc, c: tile(qc, c, False), carry)
        dk_ref[...] = dk.astype(dk_ref.dtype)
        dv_ref[...] = dv.astype(dv_ref.dtype)
        db_ref[...] = _col_to_row(db)

        @pl.when(kj == nk - 1)
        def _():
            dq_ref[...] = dq_acc[...].astype(dq_ref.dtype)

        if nw:
            pl.when((hd == H - 1) & (kj == nk - 1))(finish)

    full = lambda c0: pl.BlockSpec((LP, HEAD_DIM), lambda h, j: (0, c0 + h))
    blk = lambda c0: pl.BlockSpec((BK, HEAD_DIM), lambda h, j: (j, c0 + h))
    vec = pl.BlockSpec((None, 1, LP), lambda h, j: (h, 0, 0))
    in_specs = [full(0), blk(H), blk(2 * H), full(0), vec, vec, vec]
    out_specs = [full(0), blk(0), blk(0), pl.BlockSpec((None, 1, BK), lambda h, j: (h, 0, j))]
    est = 2 * (3 * LP * HEAD_DIM * 2 + 16 * LP * 4) + LP * HEAD_DIM * 4 + 24 * BK * LANES * 4 + 10 * BK * BQ * 4
    sems = [pltpu.SemaphoreType.DMA((3 * nw,)), pltpu.SemaphoreType.DMA((3 * nw,))] if nw else []
    return pl.pallas_call(
        body, grid=(H, nk), in_specs=in_specs + [HBM_SPEC] * nw, out_specs=out_specs + [HBM_SPEC] * nw,
        out_shape=[_sds((LP, H * HEAD_DIM), BF16)] * 3 + [_sds((H, 1, LP), F32)]
        + [_sds((3,) + p.shape[1:], p.dtype) for p in scatter],
        scratch_shapes=[pltpu.VMEM((LP, HEAD_DIM), F32)] + sems,
        compiler_params=_cparams(("arbitrary", "arbitrary"), est), name=name,
    )(z, z, z, do, lse_t, delta_t, bias_t, *scatter)


def _ssm_disc(lr, li, ldt, br, bi):
    dt = jnp.exp(ldt)
    mag = jnp.exp(lr * dt)
    a_re = mag * jnp.cos(li * dt)
    a_im = mag * jnp.sin(li * dt)
    den = lr * lr + li * li
    nr = a_re - 1.0
    z_re = (nr * lr + a_im * li) / den
    z_im = (a_im * lr - nr * li) / den
    return a_re, a_im, z_re * br - z_im * bi, z_re * bi + z_im * br


def _ssm_prep(lr, li, ldt, br, bi, name):
    GP, C = br.shape

    def body(lr_ref, li_ref, ldt_ref, br_ref, bi_ref, ar_ref, ai_ref, bbr_ref, bbi_ref, pr_ref, pi_ref):
        a_re, a_im, bb_re, bb_im = _ssm_disc(lr_ref[...], li_ref[...], ldt_ref[...], br_ref[...], bi_ref[...])
        ar_ref[...] = a_re
        ai_ref[...] = a_im
        bbr_ref[...] = bb_re
        bbi_ref[...] = bb_im
        lane = lax.broadcasted_iota(jnp.int32, (tg, 8), 1)
        pr, pi_ = a_re, a_im
        accr = jnp.zeros((tg, 8), F32)
        acci = jnp.zeros((tg, 8), F32)
        for k in range(8):
            accr = jnp.where(lane == k, pr, accr)
            acci = jnp.where(lane == k, pi_, acci)
            pr, pi_ = pr * a_re - pi_ * a_im, pr * a_im + pi_ * a_re
        pr_ref[...] = accr
        pi_ref[...] = acci

    tg = _tile(GP, 512, 8)
    blk = lambda w: pl.BlockSpec((tg, w), lambda i: (i, 0))
    col = _sds((GP, 1), F32)
    return pl.pallas_call(
        body, grid=(GP // tg,), in_specs=[blk(1), blk(1), blk(1), blk(C), blk(C)],
        out_specs=[blk(1), blk(1), blk(C), blk(C), blk(8), blk(8)],
        out_shape=[col, col, _sds((GP, C), F32), _sds((GP, C), F32), _sds((GP, 8), F32), _sds((GP, 8), F32)],
        compiler_params=_cparams(("parallel",), 48 * tg * LANES * 4), name=name,
    )(lr, li, ldt, br, bi)


def _ssm_prep_bwd(lr, li, ldt, br, bi, dar, dai, dbbr, dbbi, name):
    GP, C = br.shape

    def body(lr_ref, li_ref, ldt_ref, br_ref, bi_ref, dar_ref, dai_ref, dbbr_ref, dbbi_ref,
             glr_ref, gli_ref, gldt_ref, gbr_ref, gbi_ref):
        _, vjp = jax.vjp(_ssm_disc, lr_ref[...], li_ref[...], ldt_ref[...], br_ref[...], bi_ref[...])
        glr, gli, gldt, gbr, gbi = vjp((dar_ref[...], dai_ref[...], dbbr_ref[...], dbbi_ref[...]))
        glr_ref[...] = glr
        gli_ref[...] = gli
        gldt_ref[...] = gldt
        gbr_ref[...] = gbr
        gbi_ref[...] = gbi

    tg = _tile(GP, 512, 8)
    blk = lambda w: pl.BlockSpec((tg, w), lambda i: (i, 0))
    col = _sds((GP, 1), F32)
    return pl.pallas_call(
        body, grid=(GP // tg,), in_specs=[blk(1), blk(1), blk(1), blk(C), blk(C), blk(1), blk(1), blk(C), blk(C)],
        out_specs=[blk(1), blk(1), blk(1), blk(C), blk(C)],
        out_shape=[col, col, col, _sds((GP, C), F32), _sds((GP, C), F32)],
        compiler_params=_cparams(("parallel",), 96 * tg * LANES * 4), name=name,
    )(lr, li, ldt, br, bi, dar, dai, dbbr, dbbi)


def _cmul_add(xr, xi, mr, mi, sr, si):
    return xr + mr * sr - mi * si, xi + mr * si + mi * sr


def _ssm_fwd(z, u_col0, coef, bbr, bbi, ccr, cci, dskip, name):
    LP = z.shape[0]
    NB, CB, S = bbr.shape
    TS = _tile(LP, 640, 8)
    nt = LP // TS

    def body(u_ref, coef_ref, bbr_ref, bbi_ref, ccr_ref, cci_ref, ds_ref, y_ref, yg_ref, hr_ref, hi_ref, bur, bui, carry):
        i = pl.program_id(1)

        @pl.when(i == 0)
        def _():
            carry[...] = jnp.zeros_like(carry)

        u = u_ref[...]
        bur[...] = jnp.dot(u, bbr_ref[...], preferred_element_type=F32)
        bui[...] = jnp.dot(u, bbi_ref[...], preferred_element_type=F32)

        def step(g, c):
            cr, ci = c
            r0 = pl.multiple_of(g * 8, 8)
            xr = bur[pl.ds(r0, 8), :]
            xi = bui[pl.ds(r0, 8), :]
            for n, k in enumerate((1, 2, 4)):
                xr, xi = _cmul_add(xr, xi, coef_ref[2 * n], coef_ref[2 * n + 1], pltpu.roll(xr, k, 0), pltpu.roll(xi, k, 0))
            xr, xi = _cmul_add(xr, xi, coef_ref[6], coef_ref[7], cr, ci)
            hr_ref[pl.ds(r0, 8), :] = xr
            hi_ref[pl.ds(r0, 8), :] = xi
            return jnp.broadcast_to(xr[7:8, :], (8, S)), jnp.broadcast_to(xi[7:8, :], (8, S))

        cr, ci = lax.fori_loop(0, TS // 8, step, (carry[0], carry[1]))
        carry[0] = cr
        carry[1] = ci
        y = (jnp.dot(hr_ref[...].astype(BF16), ccr_ref[...], preferred_element_type=F32)
             - jnp.dot(hi_ref[...].astype(BF16), cci_ref[...], preferred_element_type=F32)
             + ds_ref[...] * u.astype(F32))
        y_ref[...] = y
        yg_ref[...] = _gelu(y).astype(yg_ref.dtype)

    ucb = u_col0 // CB
    in_specs = [
        pl.BlockSpec((TS, CB), lambda j, i: (i, ucb + j)),
        pl.BlockSpec((None, 8, 8, S), lambda j, i: (j, 0, 0, 0)),
        pl.BlockSpec((None, CB, S), lambda j, i: (j, 0, 0)),
        pl.BlockSpec((None, CB, S), lambda j, i: (j, 0, 0)),
        pl.BlockSpec((None, S, CB), lambda j, i: (j, 0, 0)),
        pl.BlockSpec((None, S, CB), lambda j, i: (j, 0, 0)),
        pl.BlockSpec((1, CB), lambda j, i: (0, j)),
    ]
    yb = pl.BlockSpec((TS, CB), lambda j, i: (i, j))
    hb = pl.BlockSpec((TS, S), lambda j, i: (i, j))
    est = 2 * (2 * TS * S * 4 + 3 * TS * CB * 4 + 8 * 8 * S * 4 + 4 * CB * S * 2) + 3 * TS * S * 4
    return pl.pallas_call(
        body, grid=(NB, nt), in_specs=in_specs, out_specs=[yb, yb, hb, hb],
        out_shape=[_sds((LP, NB * CB), F32), _sds((LP, NB * CB), BF16), _sds((LP, NB * S), F32), _sds((LP, NB * S), F32)],
        scratch_shapes=[pltpu.VMEM((TS, S), F32), pltpu.VMEM((TS, S), F32), pltpu.VMEM((2, 8, S), F32)],
        compiler_params=_cparams(("parallel", "arbitrary"), est), name=name,
    )(z, coef, bbr, bbi, ccr, cci, dskip)


def _ssm_bwd(z, u_col0, dyg, y, hr, hi, coef_rev, bbr_t, bbi_t, ccr_t, cci_t, dskip, name, ride=None):
    LP = z.shape[0]
    NB, S, CB = bbr_t.shape
    TS = _tile(LP, 640, 8)
    nt = LP // TS
    ng = TS // 8
    TN = (((0,), (0,)), ((), ()))

    r_ins, r_outs, r_sems = _ride_parts(ride)

    def body(*refs):
        n_in, n_out = 13 + len(r_ins), 8 + len(r_outs)
        if ride is not None:
            start, finish = ride["fn"](refs[13:n_in], refs[n_in + 8:n_in + n_out], *refs[-2:])
            pl.when((pl.program_id(0) == 0) & (pl.program_id(1) == 0))(start)
        step(*refs[:13], *refs[n_in:n_in + 8], *refs[n_in + n_out:n_in + n_out + 9])
        if ride is not None:
            pl.when((pl.program_id(0) == NB - 1) & (pl.program_id(1) == nt - 1))(finish)

    def step(u_ref, dyg_ref, y_ref, hr_ref, hi_ref, tr_ref, ti_ref, coef_ref, bbr_ref, bbi_ref, ccr_ref, cci_ref, ds_ref,
             du_ref, dbbr_ref, dbbi_ref, dccr_ref, dcci_ref, dar_ref, dai_ref, dd_ref,
             gr, gi, carry, acc_bbr, acc_bbi, acc_ccr, acc_cci, acc_a, acc_d):
        i = pl.program_id(1)

        @pl.when(i == 0)
        def _():
            for ref in (carry, acc_bbr, acc_bbi, acc_ccr, acc_cci, acc_a, acc_d):
                ref[...] = jnp.zeros_like(ref)

        u = u_ref[...]
        dy = dyg_ref[...] * _gelu_grad(y_ref[...])
        dyb = dy.astype(BF16)
        gr[...] = jnp.dot(dyb, ccr_ref[...], preferred_element_type=F32)
        gi[...] = -jnp.dot(dyb, cci_ref[...], preferred_element_type=F32)
        row8 = lax.broadcasted_iota(jnp.int32, (8, S), 0)
        first_chunk = i == nt - 1
        tail_r = jnp.where(first_chunk, 0.0, tr_ref[...])
        tail_i = jnp.where(first_chunk, 0.0, ti_ref[...])

        def scan(t, c):
            cr, ci, sar, sai = c
            g = ng - 1 - t
            r0 = pl.multiple_of(g * 8, 8)
            xr = gr[pl.ds(r0, 8), :]
            xi = gi[pl.ds(r0, 8), :]
            for n, k in enumerate((1, 2, 4)):
                xr, xi = _cmul_add(xr, xi, coef_ref[2 * n], coef_ref[2 * n + 1], pltpu.roll(xr, 8 - k, 0), pltpu.roll(xi, 8 - k, 0))
            xr, xi = _cmul_add(xr, xi, coef_ref[6], coef_ref[7], cr, ci)
            gr[pl.ds(r0, 8), :] = xr
            gi[pl.ds(r0, 8), :] = xi
            p0 = pl.multiple_of(jnp.maximum(g - 1, 0) * 8, 8)
            pr = jnp.where(g > 0, hr_ref[pl.ds(p0, 8), :], tail_r)
            pi_ = jnp.where(g > 0, hi_ref[pl.ds(p0, 8), :], tail_i)
            hpr = pltpu.roll(jnp.where(row8 == 7, pr, hr_ref[pl.ds(r0, 8), :]), 1, 0)
            hpi = pltpu.roll(jnp.where(row8 == 7, pi_, hi_ref[pl.ds(r0, 8), :]), 1, 0)
            sar = sar + xr * hpr + xi * hpi
            sai = sai + xi * hpr - xr * hpi
            return jnp.broadcast_to(xr[0:1, :], (8, S)), jnp.broadcast_to(xi[0:1, :], (8, S)), sar, sai

        zero = jnp.zeros((8, S), F32)
        cr, ci, sar, sai = lax.fori_loop(0, ng, scan, (carry[0], carry[1], zero, zero))
        carry[0] = cr
        carry[1] = ci
        acc_a[0] += sar
        acc_a[1] += sai
        grb = gr[...].astype(BF16)
        gib = gi[...].astype(BF16)
        du = (jnp.dot(grb, bbr_ref[...], preferred_element_type=F32) + jnp.dot(gib, bbi_ref[...], preferred_element_type=F32)
              + ds_ref[...] * dy)
        du_ref[...] = du.astype(du_ref.dtype)
        acc_bbr[...] += lax.dot_general(u, grb, TN, preferred_element_type=F32)
        acc_bbi[...] += lax.dot_general(u, gib, TN, preferred_element_type=F32)
        acc_ccr[...] += lax.dot_general(hr_ref[...].astype(BF16), dyb, TN, preferred_element_type=F32)
        acc_cci[...] -= lax.dot_general(hi_ref[...].astype(BF16), dyb, TN, preferred_element_type=F32)
        acc_d[...] += jnp.sum(dy * u.astype(F32), axis=0, keepdims=True)

        @pl.when(i == nt - 1)
        def _():
            dbbr_ref[...] = acc_bbr[...]
            dbbi_ref[...] = acc_bbi[...]
            dccr_ref[...] = acc_ccr[...]
            dcci_ref[...] = acc_cci[...]
            dar_ref[...] = jnp.sum(acc_a[0], axis=0, keepdims=True)
            dai_ref[...] = jnp.sum(acc_a[1], axis=0, keepdims=True)
            dd_ref[...] = acc_d[...]

    ucb = u_col0 // CB
    rev = lambda i: nt - 1 - i
    tail = lambda j, i: (jnp.maximum(rev(i) * ng - 1, 0), j)
    yb = pl.BlockSpec((TS, CB), lambda j, i: (rev(i), j))
    hb = pl.BlockSpec((TS, S), lambda j, i: (rev(i), j))
    in_specs = [
        pl.BlockSpec((TS, CB), lambda j, i: (rev(i), ucb + j)), yb, yb, hb, hb,
        pl.BlockSpec((8, S), tail), pl.BlockSpec((8, S), tail),
        pl.BlockSpec((None, 8, 8, S), lambda j, i: (j, 0, 0, 0)),
        pl.BlockSpec((None, S, CB), lambda j, i: (j, 0, 0)),
        pl.BlockSpec((None, S, CB), lambda j, i: (j, 0, 0)),
        pl.BlockSpec((None, CB, S), lambda j, i: (j, 0, 0)),
        pl.BlockSpec((None, CB, S), lambda j, i: (j, 0, 0)),
        pl.BlockSpec((1, CB), lambda j, i: (0, j)),
    ]
    mat_cs = pl.BlockSpec((None, CB, S), lambda j, i: (j, 0, 0))
    mat_sc = pl.BlockSpec((None, S, CB), lambda j, i: (j, 0, 0))
    vec_s = pl.BlockSpec((None, 1, S), lambda j, i: (j, 0, 0))
    out_specs = [yb, mat_cs, mat_cs, mat_sc, mat_sc, vec_s, vec_s, pl.BlockSpec((1, CB), lambda j, i: (0, j))]
    out_shape = [_sds((LP, NB * CB), BF16), _sds((NB, CB, S), F32), _sds((NB, CB, S), F32), _sds((NB, S, CB), F32),
                 _sds((NB, S, CB), F32), _sds((NB, 1, S), F32), _sds((NB, 1, S), F32), _sds((1, NB * CB), F32)]
    scratch = [pltpu.VMEM((TS, S), F32), pltpu.VMEM((TS, S), F32), pltpu.VMEM((2, 8, S), F32),
               pltpu.VMEM((CB, S), F32), pltpu.VMEM((CB, S), F32), pltpu.VMEM((S, CB), F32), pltpu.VMEM((S, CB), F32),
               pltpu.VMEM((2, 8, S), F32), pltpu.VMEM((1, CB), F32)]
    est = 2 * (2 * TS * S * 4 + 4 * TS * CB * 4 + 8 * 8 * S * 4 + 12 * CB * S * 4) + 4 * TS * S * 4
    return pl.pallas_call(
        body, grid=(NB, nt), in_specs=in_specs + [HBM_SPEC] * len(r_ins), out_specs=out_specs + [HBM_SPEC] * len(r_outs),
        out_shape=out_shape + r_outs, scratch_shapes=scratch + r_sems,
        compiler_params=_cparams(("arbitrary", "arbitrary"), est), name=name,
    )(z, dyg, y, hr, hi, hr, hi, coef_rev, bbr_t, bbi_t, ccr_t, cci_t, dskip, *r_ins)


def _merge_fwd(yab, z, ao, D, ga0, gb0, name):
    LP = z.shape[0]
    tm = _tile(LP, 640, 8)
    tn = _ctile(512, D, ga0, gb0)
    nj = D // tn

    def body(ya_ref, yb_ref, ga_ref, gb_ref, ao_ref, o_ref):
        f = lambda r: r[...].astype(F32)
        ssm = f(ya_ref) * _sig(f(yb_ref))
        o_ref[...] = (_sig(f(ga_ref)) * ssm + _sig(f(gb_ref)) * f(ao_ref)).astype(o_ref.dtype)

    blk = lambda c0: pl.BlockSpec((tm, tn), lambda i, j: (i, c0 // tn + j))
    return pl.pallas_call(
        body, grid=(LP // tm, nj), in_specs=[blk(0), blk(D), blk(ga0), blk(gb0), blk(0)], out_specs=blk(0),
        out_shape=_sds((LP, D), BF16), compiler_params=_cparams(("parallel", "parallel"), 2 * 6 * tm * tn * 4), name=name,
    )(yab, yab, z, z, ao)


def _merge_bwd(dm, yab, z, ao, D, ga0, gb0, name):
    LP = z.shape[0]
    tm = _tile(LP, 640, 8)
    tn = _ctile(512, D, ga0, gb0)
    nj = D // tn

    def body(dm_ref, ya_ref, yb_ref, ga_ref, gb_ref, ao_ref, dya_ref, dyb_ref, dga_ref, dgb_ref, dao_ref):
        f = lambda r: r[...].astype(F32)
        dmv, ya, ao_v = f(dm_ref), f(ya_ref), f(ao_ref)
        sa, sb, sy = _sig(f(ga_ref)), _sig(f(gb_ref)), _sig(f(yb_ref))
        t = dmv * sa
        dya_ref[...] = (t * sy).astype(BF16)
        dyb_ref[...] = (t * ya * sy * (1.0 - sy)).astype(BF16)
        dga_ref[...] = (dmv * (ya * sy) * sa * (1.0 - sa)).astype(BF16)
        dgb_ref[...] = (dmv * ao_v * sb * (1.0 - sb)).astype(BF16)
        dao_ref[...] = (dmv * sb).astype(BF16)

    blk = lambda c0: pl.BlockSpec((tm, tn), lambda i, j: (i, c0 // tn + j))
    return pl.pallas_call(
        body, grid=(LP // tm, nj), in_specs=[blk(0), blk(0), blk(D), blk(ga0), blk(gb0), blk(0)], out_specs=[blk(0)] * 5,
        out_shape=[_sds((LP, D), BF16)] * 5, compiler_params=_cparams(("parallel", "parallel"), 2 * 11 * tm * tn * 4), name=name,
    )(dm, yab, yab, z, z, ao)


def _shift_down(x, halo, k, row8):
    s = pltpu.roll(x, k, 0)
    top = jnp.where(row8 < k, pltpu.roll(halo, k, 0), s[0:8])
    return jnp.concatenate([top, s[8:]], axis=0) if x.shape[0] > 8 else top


def _shift_up(x, halo, k, row8):
    tm = x.shape[0]
    s = pltpu.roll(x, tm - k, 0)
    bot = jnp.where(row8 >= 8 - k, pltpu.roll(halo, 8 - k, 0), s[tm - 8:])
    return jnp.concatenate([s[:tm - 8], bot], axis=0) if tm > 8 else bot


def _conv_gate(g, halo, w_ref, cb, row8):
    return cb + w_ref[0:1, :] * _shift_down(g, halo, 2, row8) + w_ref[1:2, :] * _shift_down(g, halo, 1, row8) + w_ref[2:3, :] * g


def _convact_fwd(gu, cw, cb, DFF, name):
    LP = gu.shape[0]
    tm = _tile(LP, 640, 8)
    tn = _tile(DFF, 512)
    nj = DFF // tn
    t8 = tm // 8

    def body(g_ref, h_ref, u_ref, w_ref, b_ref, o_ref):
        i = pl.program_id(0)
        row8 = lax.broadcasted_iota(jnp.int32, (8, tn), 0)
        g = g_ref[...].astype(F32)
        halo = jnp.where(i > 0, h_ref[...].astype(F32), 0.0)
        gc = _conv_gate(g, halo, w_ref, b_ref[...], row8)
        o_ref[...] = (gc * _sig(gc) * u_ref[...].astype(F32)).astype(o_ref.dtype)

    in_specs = [
        pl.BlockSpec((tm, tn), lambda i, j: (i, j)),
        pl.BlockSpec((8, tn), lambda i, j: (jnp.maximum(i * t8 - 1, 0), j)),
        pl.BlockSpec((tm, tn), lambda i, j: (i, nj + j)),
        pl.BlockSpec((3, tn), lambda i, j: (0, j)),
        pl.BlockSpec((1, tn), lambda i, j: (0, j)),
    ]
    return pl.pallas_call(
        body, grid=(LP // tm, nj), in_specs=in_specs, out_specs=pl.BlockSpec((tm, tn), lambda i, j: (i, j)),
        out_shape=_sds((LP, DFF), BF16), compiler_params=_cparams(("parallel", "parallel"), 2 * 8 * tm * tn * 4), name=name,
    )(gu, gu, gu, cw, cb)


def _convact_bwd(dact, gu, cw, cb, DFF, name):
    LP = gu.shape[0]
    tm = _tile(LP, 640, 8)
    tn = _tile(DFF, 512)
    nj = DFF // tn
    t8 = tm // 8

    def body(da_ref, g_ref, h_ref, u_ref, w_ref, b_ref, dgc_ref, du_ref):
        i = pl.program_id(0)
        row8 = lax.broadcasted_iota(jnp.int32, (8, tn), 0)
        g = g_ref[...].astype(F32)
        halo = jnp.where(i > 0, h_ref[...].astype(F32), 0.0)
        gc = _conv_gate(g, halo, w_ref, b_ref[...], row8)
        sg = _sig(gc)
        da = da_ref[...].astype(F32)
        du_ref[...] = (da * gc * sg).astype(du_ref.dtype)
        dgc_ref[...] = (da * u_ref[...].astype(F32) * sg * (1.0 + gc * (1.0 - sg))).astype(dgc_ref.dtype)

    blk = pl.BlockSpec((tm, tn), lambda i, j: (i, j))
    in_specs = [
        blk, blk,
        pl.BlockSpec((8, tn), lambda i, j: (jnp.maximum(i * t8 - 1, 0), j)),
        pl.BlockSpec((tm, tn), lambda i, j: (i, nj + j)),
        pl.BlockSpec((3, tn), lambda i, j: (0, j)),
        pl.BlockSpec((1, tn), lambda i, j: (0, j)),
    ]
    return pl.pallas_call(
        body, grid=(LP // tm, nj), in_specs=in_specs, out_specs=[blk, blk],
        out_shape=[_sds((LP, DFF), BF16), _sds((LP, DFF), BF16)],
        compiler_params=_cparams(("parallel", "parallel"), 2 * 10 * tm * tn * 4), name=name,
    )(dact, gu, gu, gu, cw, cb)


def _conv_bwd(dgc, gu, cw, DFF, pad, name):
    LP = gu.shape[0]
    tm = _tile(LP, 640, 8)
    tn = _tile(DFF, 512)
    nj = DFF // tn
    t8 = tm // 8
    nt = LP // tm

    def body(d_ref, dn_ref, g_ref, h_ref, w_ref, dg_ref, dw_ref, db_ref):
        i = pl.program_id(1)
        row8 = lax.broadcasted_iota(jnp.int32, (8, tn), 0)
        d = d_ref[...].astype(F32)
        nxt = jnp.where(i < nt - 1, dn_ref[...].astype(F32), 0.0)
        dg = w_ref[2:3, :] * d + w_ref[1:2, :] * _shift_up(d, nxt, 1, row8) + w_ref[0:1, :] * _shift_up(d, nxt, 2, row8)
        rows = i * tm + lax.broadcasted_iota(jnp.int32, (tm, 1), 0)
        dg_ref[...] = jnp.where(rows >= pad, dg, 0.0).astype(dg_ref.dtype)
        g = g_ref[...].astype(F32)
        halo = jnp.where(i > 0, h_ref[...].astype(F32), 0.0)
        row3 = lax.broadcasted_iota(jnp.int32, (3, tn), 0)
        s0 = jnp.sum(d * _shift_down(g, halo, 2, row8), axis=0, keepdims=True)
        s1 = jnp.sum(d * _shift_down(g, halo, 1, row8), axis=0, keepdims=True)
        s2 = jnp.sum(d * g, axis=0, keepdims=True)
        dw = jnp.where(row3 == 0, s0, jnp.where(row3 == 1, s1, s2))
        dbp = jnp.sum(d, axis=0, keepdims=True)

        @pl.when(i == 0)
        def _():
            dw_ref[...] = dw
            db_ref[...] = dbp

        @pl.when(i > 0)
        def _():
            dw_ref[...] += dw
            db_ref[...] += dbp

    blk = pl.BlockSpec((tm, tn), lambda j, i: (i, j))
    in_specs = [
        blk,
        pl.BlockSpec((8, tn), lambda j, i: (jnp.minimum((i + 1) * t8, LP // 8 - 1), j)),
        blk,
        pl.BlockSpec((8, tn), lambda j, i: (jnp.maximum(i * t8 - 1, 0), j)),
        pl.BlockSpec((3, tn), lambda j, i: (0, j)),
    ]
    out_specs = [blk, pl.BlockSpec((3, tn), lambda j, i: (0, j)), pl.BlockSpec((1, tn), lambda j, i: (0, j))]
    return pl.pallas_call(
        body, grid=(nj, nt), in_specs=in_specs, out_specs=out_specs,
        out_shape=[_sds((LP, DFF), BF16), _sds((3, DFF), F32), _sds((1, DFF), F32)],
        compiler_params=_cparams(("parallel", "arbitrary"), 2 * 10 * tm * tn * 4), name=name,
    )(dgc, dgc, gu, gu, cw)


def _adamw_math(w, g, m, v):
    m = ADAM_B1 * m + (1.0 - ADAM_B1) * g
    v = ADAM_B2 * v + (1.0 - ADAM_B2) * (g * g)
    m_hat = m / (1.0 - ADAM_B1 ** ADAM_STEP)
    v_hat = v / (1.0 - ADAM_B2 ** ADAM_STEP)
    delta = -ADAM_LR * (m_hat / (jnp.sqrt(v_hat) + ADAM_EPS) + ADAM_WD * w)
    return delta, m, v


def _adamw(w, g, m, v, name):
    R, C = g.shape
    tm = R if R * C * 4 <= (1 << 20) else _tile(R, max(8, ((1 << 20) // (C * 4)) // 8 * 8), 8)

    def body(w_ref, g_ref, m_ref, v_ref, d_ref, mo_ref, vo_ref):
        d_ref[...], mo_ref[...], vo_ref[...] = _adamw_math(w_ref[...], g_ref[...], m_ref[...], v_ref[...])

    blk = pl.BlockSpec((tm, C), lambda i: (i, 0))
    wblk = blk if w.ndim == 2 else pl.BlockSpec((None, tm, C), lambda i: (0, i, 0))
    return pl.pallas_call(
        body, grid=(R // tm,), in_specs=[wblk, blk, wblk, wblk], out_specs=[wblk] * 3, out_shape=[_sds(w.shape, F32)] * 3,
        compiler_params=_cparams(("parallel",), 2 * 7 * tm * (C + LANES) * 4), name=name,
    )(w, g, m, v)


def _sum_adamw(parts, w, m, v, name):
    n, R, C = parts.shape
    tm = _tile(R, 256, 8)

    def body(p_ref, w_ref, m_ref, v_ref, g_ref, d_ref, mo_ref, vo_ref):
        g = p_ref[0]
        for k in range(1, n):
            g = g + p_ref[k]
        g_ref[...] = g
        d_ref[...], mo_ref[...], vo_ref[...] = _adamw_math(w_ref[...], g, m_ref[...], v_ref[...])

    blk = pl.BlockSpec((tm, C), lambda i: (i, 0))
    return pl.pallas_call(
        body, grid=(R // tm,), in_specs=[pl.BlockSpec((n, tm, C), lambda i: (0, i, 0))] + [blk] * 3, out_specs=[blk] * 4,
        out_shape=[_sds((R, C), F32)] * 4,
        compiler_params=_cparams(("parallel",), 2 * (n + 7) * tm * C * 4), name=name,
    )(parts, w, m, v)


def _add_half(g, got, c_idx, name):
    n, R, C = g.shape
    HR = R // 2
    tm = _tile(HR, max(8, ((1 << 20) // (C * 4)) // 8 * 8), 8)
    nb = HR // tm

    def body(c_ref, g_ref, t_ref, o_ref):
        o_ref[...] = (g_ref[...] + t_ref[...]).astype(o_ref.dtype)

    grid_spec = pltpu.PrefetchScalarGridSpec(
        num_scalar_prefetch=1, grid=(n, nb),
        in_specs=[pl.BlockSpec((None, tm, C), lambda k, i, c: (k, c[0] * nb + i, 0)),
                  pl.BlockSpec((None, tm, C), lambda k, i, c: (k, i, 0))],
        out_specs=pl.BlockSpec((None, tm, C), lambda k, i, c: (k, i, 0)))
    return pl.pallas_call(
        body, grid_spec=grid_spec, out_shape=_sds((n, HR, C), BF16),
        compiler_params=_cparams(("parallel", "parallel"), 2 * 3 * tm * (C + LANES) * 4), name=name,
    )(c_idx, g, got)


def _sum_half(g, got, land, chip_c, name):
    n, R, C = g.shape
    HR = R // 2
    tm = _tile(HR, max(8, ((1 << 20) // (C * 4)) // 8 * 8), 8)
    nb = HR // tm

    def body(s_ref, g_ref, t_ref, l_ref, o_ref):
        acc = g_ref[...] + t_ref[...]
        for k in range(3):
            acc = acc + l_ref[k].astype(F32)
        o_ref[...] = acc

    grid_spec = pltpu.PrefetchScalarGridSpec(
        num_scalar_prefetch=1, grid=(nb,),
        in_specs=[pl.BlockSpec((None, tm, C), lambda i, sc: (sc[0], sc[1] * nb + i, 0)),
                  pl.BlockSpec((None, tm, C), lambda i, sc: (sc[0], i, 0)),
                  pl.BlockSpec((3, tm, C), lambda i, sc: (0, i, 0))],
        out_specs=pl.BlockSpec((tm, C), lambda i, sc: (sc[1] * nb + i, 0)))
    return pl.pallas_call(
        body, grid_spec=grid_spec, out_shape=_sds((R, C), F32),
        compiler_params=_cparams(("parallel",), 2 * 6 * tm * (C + LANES) * 4), name=name,
    )(chip_c, g, got, land)


def _cast_slot(w, chip_idx, name):
    R, C = w.shape
    tm = _tile(R, max(16, ((1 << 20) // (C * 4)) // 16 * 16), 16)

    def body(s_ref, w_ref, o_ref):
        o_ref[...] = w_ref[...].astype(o_ref.dtype)

    grid_spec = pltpu.PrefetchScalarGridSpec(
        num_scalar_prefetch=1, grid=(R // tm,),
        in_specs=[pl.BlockSpec((tm, C), lambda i, sc: (i, 0))],
        out_specs=pl.BlockSpec((None, tm, C), lambda i, sc: (sc[0], i, 0)))
    return pl.pallas_call(
        body, grid_spec=grid_spec, out_shape=_sds((4, R, C), BF16),
        compiler_params=_cparams(("parallel",), 2 * 2 * tm * (C + LANES) * 4), name=name,
    )(chip_idx, w)


def _gather_chips(gs, name):
    nw = len(gs)

    def body(*refs):
        start, forward, finish = _gather_phases(refs[nw:2 * nw], *refs[2 * nw:])
        start()
        forward()
        finish()

    return pl.pallas_call(
        body, out_shape=[_sds(g.shape, g.dtype) for g in gs], in_specs=[HBM_SPEC] * nw, out_specs=[HBM_SPEC] * nw,
        scratch_shapes=[pltpu.SemaphoreType.DMA((6 * nw,)), pltpu.SemaphoreType.DMA((6 * nw,))],
        input_output_aliases={i: i for i in range(nw)}, name=name,
    )(*gs)


def _sibling_halves(gs, name):
    nw = len(gs)
    assert all(g.shape[1] % 16 == 0 for g in gs)

    def body(*refs):
        start, finish = _sibling_phases(refs[:nw], refs[nw:2 * nw], *refs[2 * nw:])
        start()
        finish()

    return pl.pallas_call(
        body, out_shape=[_sds((4, g.shape[1] // 2, g.shape[2]), g.dtype) for g in gs],
        in_specs=[HBM_SPEC] * nw, out_specs=[HBM_SPEC] * nw,
        scratch_shapes=[pltpu.SemaphoreType.DMA((nw,)), pltpu.SemaphoreType.DMA((nw,))], name=name,
    )(*gs)


def _sibling_join(fs, name):
    nw = len(fs)
    assert all(f.shape[0] % 16 == 0 for f in fs)

    def body(*refs):
        o_refs = refs[nw:2 * nw]
        send_sems, recv_sems = refs[2 * nw:]
        x, y, c = _place()

        def copy(i, half):
            HR = o_refs[i].shape[0] // 2
            rows = o_refs[i].at[pl.ds(pl.multiple_of(half * HR, 8), HR), :]
            return pltpu.make_async_remote_copy(
                src_ref=rows, dst_ref=rows, send_sem=send_sems.at[i], recv_sem=recv_sems.at[i],
                device_id=(x, y, 1 - c), device_id_type=MESH)

        sends = [copy(i, c) for i in range(nw)]
        for cp in sends:
            cp.start()
        for i in range(nw):
            copy(i, 1 - c).wait_recv()
        for cp in sends:
            cp.wait_send()

    return pl.pallas_call(
        body, out_shape=[_sds(f.shape, f.dtype) for f in fs], in_specs=[HBM_SPEC] * nw, out_specs=[HBM_SPEC] * nw,
        scratch_shapes=[pltpu.SemaphoreType.DMA((nw,)), pltpu.SemaphoreType.DMA((nw,))],
        input_output_aliases={i: i for i in range(nw)}, name=name,
    )(*fs)


def _gather_all(v, name):
    M, W = v.shape

    def body(v_ref, o_ref, send_sems, recv_sems, local_sem):
        x, y, c = _place()
        me, sibling = (x, y, c), (x, y, 1 - c)
        chips = _other_chips(x, y)

        def slot(px, py, pc):
            return o_ref.at[4 * px + 2 * py + pc]

        def copy(k, block, to, src=None):
            return pltpu.make_async_remote_copy(
                src_ref=slot(*block) if src is None else src, dst_ref=slot(*block),
                send_sem=send_sems.at[k], recv_sem=recv_sems.at[k], device_id=to, device_id_type=MESH)

        mine = pltpu.make_async_copy(v_ref, slot(*me), local_sem)
        mine.start()
        first = [copy(0, me, sibling, src=v_ref)]
        first += [copy(1 + j, me, (*chip, c), src=v_ref) for j, chip in enumerate(chips)]
        for cp in first:
            cp.start()
        passed = [copy(4 + j, (*chip, c), sibling) for j, chip in enumerate(chips)]
        for j, chip in enumerate(chips):
            copy(1 + j, (*chip, c), me).wait_recv()
            passed[j].start()
        copy(0, sibling, me).wait_recv()
        for j, chip in enumerate(chips):
            copy(4 + j, (*chip, 1 - c), me).wait_recv()
        for cp in first + passed:
            cp.wait_send()
        mine.wait()

    vm = pl.BlockSpec(memory_space=pltpu.VMEM)
    return pl.pallas_call(
        body, out_shape=_sds((8, M, W), v.dtype), in_specs=[vm], out_specs=vm,
        scratch_shapes=[pltpu.SemaphoreType.DMA((7,)), pltpu.SemaphoreType.DMA((7,)), pltpu.SemaphoreType.DMA(())],
        compiler_params=pltpu.CompilerParams(vmem_limit_bytes=int(min(10 * M * W * 4 + (8 << 20), V7X_VMEM_BYTES - (8 << 20)))),
        name=name,
    )(v)


def _rows_for(n_elems, width, mult=8):
    rows = -(-n_elems // width)
    return -(-rows // mult) * mult


def _pack_small(arrs, total_rows):
    parts = []
    used = 0
    for a in arrs:
        rows = _rows_for(a.size, LANES)
        parts.append(jnp.pad(a.reshape(-1), (0, rows * LANES - a.size)).reshape(rows, LANES))
        used += rows
    if total_rows > used:
        parts.append(jnp.zeros((total_rows - used, LANES), F32))
    return jnp.concatenate(parts, axis=0)


def _unpack_small(p, shapes):
    outs, r = [], 0
    lead = p.shape[:-2]
    for shp in shapes:
        n = int(np.prod(shp))
        rows = _rows_for(n, LANES)
        outs.append(p[..., r:r + rows, :].reshape(lead + (rows * LANES,))[..., :n].reshape(lead + tuple(shp)))
        r += rows
    return outs


def _chip_pieces(lo, hi, n):
    out = []
    while lo < hi:
        q = lo // n
        b = min(hi, (q + 1) * n)
        out.append((q, lo - q * n, b - q * n))
        lo = b
    return out


def _chips_to_cols(w):
    n4, K, n = w.shape
    return w.transpose(1, 0, 2).reshape(K, n4 * n)


def _block_diag(m, gpb):
    G, A, B = m.shape
    nb = G // gpb
    eye = jnp.eye(gpb, dtype=m.dtype)
    t = m.reshape(nb, gpb, A, B)[:, :, :, None, :] * eye[None, :, None, :, None]
    return t.reshape(nb, gpb * A, gpb * B)


def _block_diag_extract(m, gpb, A, B):
    nb = m.shape[0]
    t = m.reshape(nb, gpb, A, gpb, B)
    eye = jnp.eye(gpb, dtype=m.dtype)
    d = jnp.sum(t * eye[None, :, None, :, None], axis=3)
    return d.reshape(nb * gpb, A, B)


def kernel(x, meta, g_mix, w_in, b_f, lam_re, lam_im, log_dt, b_re, b_im, c_re, c_im, d_skip, w_glu, w_attn_o, w_out, g_ffn, w_up, conv_w, conv_b, w_down, g_final, loss_target, m_meta, m_g_mix, m_w_in, m_b_f, m_lam_re, m_lam_im, m_log_dt, m_b_re, m_b_im, m_c_re, m_c_im, m_d_skip, m_w_glu, m_w_attn_o, m_w_out, m_g_ffn, m_w_up, m_conv_w, m_conv_b, m_w_down, m_g_final, v_meta, v_g_mix, v_w_in, v_b_f, v_lam_re, v_lam_im, v_log_dt, v_b_re, v_b_im, v_c_re, v_c_im, v_d_skip, v_w_glu, v_w_attn_o, v_w_out, v_g_ffn, v_w_up, v_conv_w, v_conv_b, v_w_down, v_g_final):
    args = dict(locals())
    L, D = x.shape[1], x.shape[2]
    NM = meta.shape[0]
    H = b_f.shape[1]
    DA = H * HEAD_DIM
    G, P, C = b_re.shape[1:]
    DS, GP = G * C, G * P
    DFF = conv_b.shape[1]
    PAD = (-NM) % LANES
    OFF = PAD + NM
    LP = OFF + L
    NZ = 3 * DA + DS + 2 * D
    U0, GA0, GB0 = 3 * DA, 3 * DA + DS, 3 * DA + DS + D
    NB = G // GROUPS_PER_BLOCK
    chip = 2 * lax.axis_index("x") + lax.axis_index("y")
    core = lax.axis_index("c")

    big = ["w_in", "w_glu", "w_attn_o", "w_out", "w_up", "w_down"]
    local = {n: args[n][0] for n in big}
    chip_idx = chip.reshape(1).astype(jnp.int32)
    slots = {n: _cast_slot(local[n], chip_idx, "cast_" + n) for n in big}
    gathered = {"w_in": _gather_chips([slots["w_in"]], "gather_w_in")[0]}
    tiny_shapes = [conv_w.shape[1:], meta.shape]
    tiny_rows = sum(_rows_for(int(np.prod(sh)), LANES) for sh in tiny_shapes)
    tiny = _gather_all(_pack_small([conv_w[0], meta], tiny_rows), "gather_small_weights")[0::2]
    conv_w_c, meta_c = _unpack_small(tiny, tiny_shapes)
    conv_w_f = _chips_to_cols(conv_w_c)
    meta_full = _chips_to_cols(meta_c)
    g_in = gathered["w_in"]
    n_in = g_in.shape[2]
    cols = lambda lo, hi: [g_in[q, :, a:b] for q, a, b in _chip_pieces(lo, hi, n_in)]
    w_zf = jnp.concatenate(cols(0, 3 * DA) + cols(3 * DA + H, 4 * n_in) + cols(3 * DA, 3 * DA + H)
                           + [jnp.zeros((D, LANES - H), BF16)], axis=1)
    N_GLU, N_AO, N_UP = (slots[n].shape[2] for n in ("w_glu", "w_attn_o", "w_up"))

    col = lambda a: a.reshape(GP, 1)
    lr_c, li_c = col(lam_re[0]), col(lam_im[0])
    ldt_c = jnp.repeat(log_dt[0], P).reshape(GP, 1)
    br2, bi2 = b_re[0].reshape(GP, C), b_im[0].reshape(GP, C)
    a_re, a_im, bb_re, bb_im, pw_re, pw_im = _ssm_prep(lr_c, li_c, ldt_c, br2, bi2, "ssm_prep")
    S = GROUPS_PER_BLOCK * P
    CB = GROUPS_PER_BLOCK * C
    pw_r = pw_re.T.reshape(8, NB, S).transpose(1, 0, 2)
    pw_i = pw_im.T.reshape(8, NB, S).transpose(1, 0, 2)
    row8 = jnp.arange(8)[None, :, None]

    def masked_power(pw, k, keep):
        return jnp.where(keep, pw[:, k - 1][:, None, :], 0.0)

    coef = jnp.stack(
        [masked_power(pw, k, row8 >= k) for k in (1, 2, 4) for pw in (pw_r, pw_i)] + [pw_r, pw_i], axis=1)
    coef_rev = jnp.stack(
        [masked_power(pw, k, row8 < 8 - k) for k in (1, 2, 4) for pw in (pw_r, -pw_i)]
        + [pw_r[:, ::-1], -pw_i[:, ::-1]], axis=1)
    bd = lambda m: _block_diag(m, GROUPS_PER_BLOCK)
    bbr3, bbi3 = bb_re.reshape(G, P, C), bb_im.reshape(G, P, C)
    bbr_cs = bd(bbr3.transpose(0, 2, 1)).astype(BF16)
    bbi_cs = bd(bbi3.transpose(0, 2, 1)).astype(BF16)
    bbr_sc = bd(bbr3).astype(BF16)
    bbi_sc = bd(bbi3).astype(BF16)
    ccr_sc = bd(c_re[0].transpose(0, 2, 1)).astype(BF16)
    cci_sc = bd(c_im[0].transpose(0, 2, 1)).astype(BF16)
    ccr_cs = bd(c_re[0]).astype(BF16)
    cci_cs = bd(c_im[0]).astype(BF16)

    h0 = jnp.concatenate([jnp.zeros((PAD, D), F32), meta_full, x[0]], axis=0)
    n1 = _rms_fwd(h0, g_mix, "rms_mix")
    z = _mm(n1, w_zf, "nn", LP, NZ, D, BF16, "in_proj")
    fpre = _mm(n1, w_zf, "nn", LP, LANES, D, F32, "in_proj_f", b_off=(0, NZ))
    bf_pad = jnp.pad(b_f, ((0, 0), (0, LANES - H)))
    fcum = _fgate_fwd(fpre, bf_pad, PAD, "fgate_fwd")
    key_bias = jnp.where(jnp.arange(LP)[:, None] >= PAD, -fcum, NEG)
    bias_t = key_bias.T[:H].reshape(H, 1, LP)
    attn, attn_f32, lse_t, *rest = _attn_fwd(z, bias_t, H, PAD, "attn_fwd", gather=[slots[n] for n in big[1:]])
    gathered.update(zip(big[1:], rest))
    w_glu_c, w_ao_c, w_up_c = gathered["w_glu"], gathered["w_attn_o"], gathered["w_up"]
    w_out_f = gathered["w_out"].reshape(D, D)
    w_down_f = gathered["w_down"].reshape(DFF, D)
    ao = _mm(attn, w_ao_c, "nn", LP, D, DA, BF16, "attn_out", b_chips=N_AO)
    y, yg, hs_re, hs_im = _ssm_fwd(z, U0, coef, bbr_cs, bbi_cs, ccr_sc, cci_sc, d_skip, "ssm_fwd")
    yab = _mm(yg, w_glu_c, "nn", LP, 2 * D, DS, BF16, "glu_proj", b_chips=N_GLU)
    merged = _merge_fwd(yab, z, ao, D, GA0, GB0, "merge_fwd")
    h1 = _mm(merged, w_out_f, "nn", LP, D, D, F32, "out_proj", res=h0)
    n2 = _rms_fwd(h1, g_ffn, "rms_ffn")
    gu = _mm(n2, w_up_c, "nn", LP, 2 * DFF, D, BF16, "up_proj", tn=2816, b_chips=N_UP)
    act = _convact_fwd(gu, conv_w_f, conv_b, DFF, "convact_fwd")
    h2 = _mm(act, w_down_f, "nn", LP, D, DFF, F32, "down_proj", res=h1, tn=512, tk=DFF)
    dh2, dg_final, loss_v = _final_loss(h2, g_final.reshape(1, D), loss_target[0], OFF, "final_loss")
    loss = lax.psum(loss_v[0, 0], ("x", "y", "c"))

    KW = dict(tm=512, tn=512, tk=LP)
    dact = _mm(dh2, w_down_f, "nt", LP, DFF, D, BF16, "down_bwd_x")
    dw_down = _mm(act, dh2, "tn", DFF, D, LP, F32, "down_bwd_w", **KW)
    dgc, du_ffn = _convact_bwd(dact, gu, conv_w_f, conv_b, DFF, "convact_bwd")
    dg_ffn_in, dconv_w, dconv_b = _conv_bwd(dgc, gu, conv_w_f, DFF, PAD, "conv_bwd")
    dn2 = _mm(dg_ffn_in, w_up_c, "nt", LP, D, DFF, F32, "up_bwd_x_g", tn=512, tk=N_UP, b_chips=N_UP)
    dn2 = _mm(du_ffn, w_up_c, "nt", LP, D, DFF, F32, "up_bwd_x_u", res=dn2, b_off=(0, DFF), tn=512, tk=N_UP, b_chips=N_UP)
    dw_up = _mm(n2, dg_ffn_in, "tn", D, DFF, LP, F32, "up_bwd_w_g", tm=512, tn=256, tk=LP, out_chips=N_UP,
                out_into=(lax.empty((4, D, N_UP), F32), 0))
    dw_up = _mm(n2, du_ffn, "tn", D, DFF, LP, F32, "up_bwd_w_u", tm=512, tn=256, tk=LP, out_chips=N_UP,
                out_into=(dw_up, DFF // N_UP))
    dh1, dg_ffn = _rms_bwd(h1, g_ffn, dn2, dh2, "rms_ffn_bwd")
    c_idx = core.reshape(1).astype(jnp.int32)
    chip_c = jnp.stack([chip, core]).astype(jnp.int32)

    dmerged = _mm(dh1, w_out_f, "nt", LP, D, D, F32, "out_bwd_x")
    dw_out = _mm(merged, dh1, "tn", D, D, LP, F32, "out_bwd_w", **KW)
    dya, dyb, dga, dgb, dao = _merge_bwd(dmerged, yab, z, ao, D, GA0, GB0, "merge_bwd")
    dattn = _mm(dao, w_ao_c, "nt", LP, DA, D, BF16, "attn_out_bwd_x", b_chips=N_AO)
    dw_ao = _mm(attn, dao, "tn", DA, D, LP, F32, "attn_out_bwd_w", out_chips=N_AO, **KW)
    dyg = _mm(dya, w_glu_c, "nt", LP, DS, D, F32, "glu_bwd_x_a", b_chips=N_GLU)
    dyg = _mm(dyb, w_glu_c, "nt", LP, DS, D, F32, "glu_bwd_x_b", res=dyg, b_off=(0, D), b_chips=N_GLU)
    dw_glu = _mm(yg, dya, "tn", DS, D, LP, F32, "glu_bwd_w_a", out_chips=N_GLU,
                 out_into=(lax.empty((4, DS, N_GLU), F32), 0), **KW)
    dw_glu = _mm(yg, dyb, "tn", DS, D, LP, F32, "glu_bwd_w_b", out_chips=N_GLU, out_into=(dw_glu, D // N_GLU), **KW)
    early = ["w_glu", "w_attn_o", "w_out", "w_up", "w_down"]
    early_grads = [dw_glu, dw_ao, dw_out.reshape(4, D // 4, D), dw_up, dw_down.reshape(4, DFF // 4, D)]
    (du_ssm, dbbr_d, dbbi_d, dccr_d, dcci_d, dar_b, dai_b, dd_skip, *early_got) = _ssm_bwd(
        z, U0, dyg, y, hs_re, hs_im, coef_rev, bbr_sc, bbi_sc, ccr_cs, cci_cs, d_skip, "ssm_bwd",
        ride=_sibling_ride(early_grads))
    delta_t = _attn_delta(dattn, attn_f32, H, "attn_delta")
    early_part = [_add_half(g, t, c_idx, "rs_add_" + n) for n, g, t in zip(early, early_grads, early_got)]
    dq, dk, dv, dbias_t, *early_land = _attn_bwd(z, dattn, lse_t, delta_t, bias_t, H, "attn_bwd", scatter=early_part)
    dF = jnp.pad(-dbias_t[:, 0, :].T, ((0, 0), (0, LANES - H)))
    dfpre, db_f = _fgate_bwd(dF, fpre, bf_pad, PAD, "fgate_bwd")
    dz = jnp.concatenate([dq, dk, dv, du_ssm, dga, dgb, dfpre.astype(BF16)], axis=1)
    dw_zf = _mm(n1, dz, "tn", D, NZ + LANES, LP, F32, "in_bwd_w", tm=512, tn=640, tk=LP)
    def orig_cols(lo, hi):
        parts_ = []
        for a, b, shift in ((0, 3 * DA, 0), (3 * DA, 3 * DA + H, NZ - 3 * DA), (3 * DA + H, 4 * n_in, -H)):
            a, b = max(a, lo), min(b, hi)
            if a < b:
                parts_.append(dw_zf[:, a + shift:b + shift])
        return jnp.concatenate(parts_, axis=1)

    late_grads = [jnp.stack([orig_cols(q * n_in, (q + 1) * n_in) for q in range(4)], axis=0)]
    late_got = _sibling_halves(late_grads, "rs_sibling_w_in")
    late_part = [_add_half(late_grads[0], late_got[0], c_idx, "rs_add_w_in")]
    dn1, *late_land = _mm(dz, w_zf, "nt", LP, D, NZ + LANES, F32, "in_bwd_x", tm=640, tn=256, tk=NZ + LANES,
                          ride=_scatter_ride(late_part))
    dh0, dg_mix = _rms_bwd(h0, g_mix, dn1, dh1, "rms_mix_bwd")
    grad_x = dh0[OFF:][None]
    dmeta_full = dh0[PAD:OFF]

    ext = lambda m, A, B: _block_diag_extract(m, GROUPS_PER_BLOCK, A, B)
    dbb_re = ext(dbbr_d, C, P).transpose(0, 2, 1).reshape(GP, C)
    dbb_im = ext(dbbi_d, C, P).transpose(0, 2, 1).reshape(GP, C)
    dc_re = ext(dccr_d, P, C).transpose(0, 2, 1)[None]
    dc_im = ext(dcci_d, P, C).transpose(0, 2, 1)[None]
    glr, gli, gldt, gbr, gbi = _ssm_prep_bwd(lr_c, li_c, ldt_c, br2, bi2, dar_b.reshape(GP, 1), dai_b.reshape(GP, 1),
                                             dbb_re, dbb_im, "ssm_prep_bwd")
    small_grads = {
        "g_mix": dg_mix, "b_f": db_f[:, :H], "lam_re": glr.reshape(1, G, P), "lam_im": gli.reshape(1, G, P),
        "log_dt": gldt.reshape(G, P).sum(axis=1)[None], "b_re": gbr.reshape(1, G, P, C), "b_im": gbi.reshape(1, G, P, C),
        "c_re": dc_re, "c_im": dc_im, "d_skip": dd_skip, "g_ffn": dg_ffn, "conv_b": dconv_b, "g_final": dg_final.reshape(D),
    }

    small = list(small_grads)
    rider_grads = [dconv_w, dmeta_full]
    small_shapes = [args[n].shape for n in small] + [g.shape for g in rider_grads]
    srows = sum(_rows_for(int(np.prod(sh)), LANES) for sh in small_shapes)
    srows = -(-srows // 256) * 256
    zeros_like_riders = [jnp.zeros(g.shape, F32) for g in rider_grads]
    pack = lambda arrs: _pack_small(arrs, srows)
    g_parts = _gather_all(pack([small_grads[n] for n in small] + rider_grads), "gather_small_grads")
    sm = _sum_adamw(g_parts, pack([args[n] for n in small] + zeros_like_riders),
                    pack([args["m_" + n] for n in small] + zeros_like_riders),
                    pack([args["v_" + n] for n in small] + zeros_like_riders), "small_adamw")
    unpacked = [_unpack_small(p, small_shapes) for p in sm]
    sg, sd, smm, svv = (dict(zip(small, u[:len(small)])) for u in unpacked)
    dconv_w_sum, dmeta_sum = unpacked[0][len(small):]
    n_cw, n_me = conv_w.shape[2], meta.shape[1]
    rider = {"conv_w": lax.dynamic_slice_in_dim(dconv_w_sum, chip * n_cw, n_cw, axis=1)[None],
             "meta": lax.dynamic_slice_in_dim(dmeta_sum, chip * n_me, n_me, axis=1)}

    big = ["w_in"] + early
    halves = [_sum_half(g, t, l_, chip_c, "rs_sum_" + n) for n, g, t, l_ in
              zip(big, late_grads + early_grads, list(late_got) + list(early_got), list(late_land) + list(early_land))]
    shard_grads = dict(zip(big, _sibling_join(halves, "rs_join")))
    shard_grads.update({n: g.reshape(g.shape[-2:]) for n, g in rider.items()})
    bg, bd_, bm, bv = {}, {}, {}, {}
    for n, g in shard_grads.items():
        bd_[n], bm[n], bv[n] = _adamw(args[n], g, args["m_" + n], args["v_" + n], "adamw_" + n)
        bg[n] = g.reshape(args[n].shape)

    order = ["meta", "g_mix", "w_in", "b_f", "lam_re", "lam_im", "log_dt", "b_re", "b_im", "c_re", "c_im", "d_skip",
             "w_glu", "w_attn_o", "w_out", "g_ffn", "w_up", "conv_w", "conv_b", "w_down", "g_final"]
    pick = lambda bigd, smalld, n: bigd[n] if n in bigd else smalld[n]
    outs = [loss, grad_x]
    for bigd, smalld in ((bg, sg), (bd_, sd), (bm, smm), (bv, svv)):
        outs += [pick(bigd, smalld, n) for n in order]
    return tuple(outs)
```
